```python
import math
import jax, jax.numpy as jnp
from jax import lax
import numpy as np

D_MODEL = 4096
BATCH = 4
SEQ = 2048
DEPTH = 2
DEC_BATCH = 8
DEC_SEQ = 1
PAST_LEN = 16384
PAGE_SIZE = 128

HEAD_DIM = 128
N_A_LAYERS = (DEPTH + 1) // 2
N_C_LAYERS = DEPTH // 2
CONV_W = 4
BAND_BLOCK = 128
A_PATTERNS = ((128, 1), (512, 4), (2048, 16))
A_N_GROUPS = len(A_PATTERNS)
A_HEADS = D_MODEL // 512
A_WIN_MAX = max(w for w, _ in A_PATTERNS)
A_Q = A_N_GROUPS * A_HEADS * HEAD_DIM
A_KV = A_HEADS * HEAD_DIM
A_OUT = A_HEADS * HEAD_DIM
B_WIDTH = 3 * D_MODEL // 4
B_BLOCKS = B_WIDTH // HEAD_DIM
B_BLOCK_DIM = B_WIDTH // B_BLOCKS
RG_C = 8.0
C_HEADS = D_MODEL // 256
C_KV_HEADS = C_HEADS // 4
C_GROUP = C_HEADS // C_KV_HEADS
C_BLOCK = 64
C_N_SEL = 16
C_WIN = 512
C_Q_CHUNK = 32
C_Q = C_HEADS * HEAD_DIM
C_KV = C_KV_HEADS * HEAD_DIM
D_HEADS = D_MODEL // 256
D_DK = HEAD_DIM
D_DV = HEAD_DIM
D_QK = D_HEADS * D_DK
D_V = D_HEADS * D_DV
D_CONV = 2 * D_QK + D_V
D_CHUNK = 64
EVEN_IN = A_Q + 2 * A_KV + 2 * B_WIDTH
EVEN_MIX = A_OUT + B_WIDTH
ODD_IN = C_Q + 6 * C_KV + 3 * C_HEADS + 2 * D_QK + 2 * D_V + 2 * D_HEADS
ODD_MIX = C_Q + D_V
N_GROUPS = 8
EXPERTS_PER_GROUP = 8
N_EXPERTS = N_GROUPS * EXPERTS_PER_GROUP
TOP_K = 2
D_EXPERT = D_MODEL // 8
MOE_BLOCK = 128
EPS = 1e-6
NEG = -1e30
FORCE = 1e4

kernel_name = 'hybrid_dilated_rglru_nsa_gdn_hmoe_step'


def rmsnorm(x, g):
    xf = x.astype(jnp.float32)
    y = xf * lax.rsqrt(jnp.mean(xf * xf, axis=-1, keepdims=True) + EPS)
    return (y * g.astype(jnp.float32)).astype(x.dtype)


def l2norm(x):
    return x * lax.rsqrt(jnp.sum(x * x, axis=-1, keepdims=True) + EPS)


def alibi_slopes(n):
    return 2.0 ** (-8.0 * (jnp.arange(n, dtype=jnp.float32) + 1.0) / n)


def split_cols(u, sizes):
    cuts = [int(c) for c in np.cumsum(sizes)[:-1]]
    return jnp.split(u, cuts, axis=-1)


def masked_stats(s, mask):
    s = jnp.where(mask, s, NEG)
    m = jnp.max(s, axis=-1, keepdims=True)
    p = jnp.where(mask, jnp.exp(s - m), 0.0)
    return p, m[..., 0], jnp.sum(p, axis=-1)


def merge_by_denominator(stats):
    big_m = stats[0][1]
    for _, m, _ in stats[1:]:
        big_m = jnp.maximum(big_m, m)
    num, den = 0.0, 0.0
    for o, m, l in stats:
        w = jnp.exp(m - big_m)
        num = num + w[..., None] * o
        den = den + w * l
    return num / den[..., None]


def causal_conv(x, buf, w, b=None):
    t = x.shape[1]
    xx = jnp.concatenate([buf.astype(x.dtype), x], axis=1)
    y = xx[:, CONV_W - 1:] * w[CONV_W - 1]
    for j in range(CONV_W - 1):
        y = y + xx[:, j:j + t] * w[j]
    if b is not None:
        y = y + b
    return y, xx[:, -(CONV_W - 1):]


def band_attn(q, k, v, n_back, slopes, dist_scale):
    bsz, t, h, g, dh = q.shape
    nbb = -(-n_back // BAND_BLOCK)
    nq = -(-t // BAND_BLOCK)
    pad = nq * BAND_BLOCK - t
    front = nbb * BAND_BLOCK
    qp = jnp.pad(q, ((0, 0), (0, pad), (0, 0), (0, 0), (0, 0)))
    kp = jnp.pad(k, ((0, 0), (front, pad), (0, 0), (0, 0)))
    vp = jnp.pad(v, ((0, 0), (front, pad), (0, 0), (0, 0)))
    kw = (nbb + 1) * BAND_BLOCK
    idx = jnp.arange(nq)[:, None] * BAND_BLOCK + jnp.arange(kw)[None, :]
    k_win, v_win = kp[:, idx], vp[:, idx]
    qr = qp.reshape(bsz, nq, BAND_BLOCK, h, g, dh)
    s = jnp.einsum('bnqhgd,bnkhd->bnhgqk', qr, k_win).astype(jnp.float32) * dh ** -0.5
    t_pos = jnp.arange(nq)[:, None] * BAND_BLOCK + jnp.arange(BAND_BLOCK)[None, :]
    s_pos = idx - front
    dist = t_pos[:, :, None] - s_pos[:, None, :]
    mask = (dist >= 0) & (dist <= n_back) & (s_pos[:, None, :] >= 0)
    s = s - slopes[None, None, :, :, None, None] * (dist * dist_scale).astype(jnp.float32)[None, :, None, None]
    p, m, l = masked_stats(s, mask[None, :, None, None])
    o = jnp.einsum('bnhgqk,bnkhd->bnqhgd', p, v_win.astype(jnp.float32))
    o = o.reshape(bsz, nq * BAND_BLOCK, h, g, dh)[:, :t]
    m = jnp.moveaxis(m, 4, 2).reshape(bsz, nq * BAND_BLOCK, h, g)[:, :t]
    l = jnp.moveaxis(l, 4, 2).reshape(bsz, nq * BAND_BLOCK, h, g)[:, :t]
    return o, m, l


def dilated_attn_prompt(q, k, v, slopes):
    bsz, s = q.shape[:2]
    stats = []
    for gi, (w, d) in enumerate(A_PATTERNS):
        n_res = s // d
        def to_res(x):
            return jnp.swapaxes(x.reshape(bsz, n_res, d, *x.shape[2:]), 1, 2).reshape(bsz * d, n_res, *x.shape[2:])
        def from_res(x):
            return jnp.swapaxes(x.reshape(bsz, d, n_res, *x.shape[2:]), 1, 2).reshape(bsz, s, *x.shape[2:])
        o, m, l = band_attn(to_res(q[:, :, gi])[:, :, :, None], to_res(k), to_res(v), w // d, slopes[gi][:, None], d)
        stats.append((from_res(o[:, :, :, 0]), from_res(m[:, :, :, 0]), from_res(l[:, :, :, 0])))
    return merge_by_denominator(stats)


def dilated_attn_sample(q, kv_ctx, slopes):
    t = q.shape[1]
    wb = kv_ctx.shape[1] - t
    stats = []
    for gi, (w, d) in enumerate(A_PATTERNS):
        j = jnp.arange(w // d + 1)
        idx = wb + jnp.arange(t)[:, None] - d * j[None, :]
        kvg = kv_ctx[:, jnp.maximum(idx, 0)]
        s = jnp.einsum('bthd,btjhd->bthj', q[:, :, gi], kvg[:, :, :, 0]).astype(jnp.float32) * HEAD_DIM ** -0.5
        s = s - slopes[gi][None, None, :, None] * (d * j).astype(jnp.float32)[None, None, None, :]
        p, m, l = masked_stats(s, (idx >= 0)[None, :, None, :])
        o = jnp.einsum('bthj,btjhd->bthd', p, kvg[:, :, :, 1].astype(jnp.float32))
        stats.append((o, m, l))
    return merge_by_denominator(stats)


def rglru_mixer(gate_in, x_in, conv_buf, h0, conv_w, conv_b, wa, ba, wx, bx, lam):
    xc, conv_new = causal_conv(x_in, conv_buf, conv_w, conv_b)
    bsz, t, _ = xc.shape
    xb = xc.reshape(bsz, t, B_BLOCKS, B_BLOCK_DIM)
    r = jax.nn.sigmoid(jnp.einsum('btni,nij->btnj', xb, wa).reshape(bsz, t, B_WIDTH) + ba).astype(jnp.float32)
    ig = jax.nn.sigmoid(jnp.einsum('btni,nij->btnj', xb, wx).reshape(bsz, t, B_WIDTH) + bx).astype(jnp.float32)
    log_a = -RG_C * r * jax.nn.softplus(-lam.astype(jnp.float32))
    a = jnp.exp(log_a)
    bterm = jnp.sqrt(-jnp.expm1(2.0 * log_a)) * (ig * xc.astype(jnp.float32))
    bterm = bterm.at[:, 0].add(a[:, 0] * h0.astype(jnp.float32))
    def comb(e1, e2):
        return e1[0] * e2[0], e2[0] * e1[1] + e2[1]
    _, h = lax.associative_scan(comb, (a, bterm), axis=1)
    y = h.astype(x_in.dtype) * jax.nn.gelu(gate_in)
    return y, conv_new, h[:, -1].astype(h0.dtype)


def gated_delta_chunked(q, k, v, g, beta, s0):
    bsz, t, h, _ = q.shape
    dv = v.shape[-1]
    n, c = t // D_CHUNK, D_CHUNK
    def chunks(x):
        return jnp.moveaxis(x.reshape(bsz, n, c, h, *x.shape[3:]), (1, 3), (0, 2))
    qc, kc, vc, gc, bc = chunks(q), chunks(k), chunks(v), chunks(g), chunks(beta)
    gam = jnp.cumsum(gc, axis=-1)
    lower = jnp.tril(jnp.ones((c, c), bool))
    strict = jnp.tril(jnp.ones((c, c), bool), -1)
    diff = gam[..., :, None] - gam[..., None, :]
    decay = jnp.where(lower, jnp.exp(jnp.where(lower, diff, 0.0)), 0.0)
    kk = jnp.einsum('nbhid,nbhjd->nbhij', kc, kc)
    a_mat = jnp.where(strict, bc[..., :, None] * kk * decay, 0.0) + jnp.eye(c, dtype=jnp.float32)
    rhs = jnp.concatenate([vc * bc[..., None], kc * (bc * jnp.exp(gam))[..., None]], axis=-1)
    sol = lax.linalg.triangular_solve(a_mat, rhs, left_side=True, lower=True)
    u, w = sol[..., :dv], sol[..., dv:]
    qk = jnp.where(lower, jnp.einsum('nbhid,nbhjd->nbhij', qc, kc) * decay, 0.0)
    q_dec = qc * jnp.exp(gam)[..., None]
    k_dec = kc * jnp.exp(gam[..., -1:] - gam)[..., None]
    last = jnp.exp(gam[..., -1])[..., None, None]
    def step(s, xs):
        u_n, w_n, q_n, k_n, qk_n, last_n = xs
        v_new = u_n - jnp.einsum('bhck,bhkv->bhcv', w_n, s)
        o = jnp.einsum('bhck,bhkv->bhcv', q_n, s) + jnp.einsum('bhij,bhjv->bhiv', qk_n, v_new)
        s = last_n * s + jnp.einsum('bhck,bhcv->bhkv', k_n, v_new)
        return s, o
    s_fin, o = lax.scan(step, s0, (u, w, q_dec, k_dec, qk, last))
    o = jnp.moveaxis(o, (0, 2), (1, 3)).reshape(bsz, t, h, dv)
    return o, s_fin


def gdn_mixer(dq, dk, dv, dz, db, da, conv_buf, s0, conv_w, a_log, dt_bias, out_gain):
    bsz, t, _ = dq.shape
    qkv, conv_new = causal_conv(jnp.concatenate([dq, dk, dv], axis=-1), conv_buf, conv_w)
    qkv = jax.nn.silu(qkv).astype(jnp.float32)
    q = l2norm(qkv[..., :D_QK].reshape(bsz, t, D_HEADS, D_DK)) * D_DK ** -0.5
    k = l2norm(qkv[..., D_QK:2 * D_QK].reshape(bsz, t, D_HEADS, D_DK))
    v = qkv[..., 2 * D_QK:].reshape(bsz, t, D_HEADS, D_DV)
    beta = jax.nn.sigmoid(db.astype(jnp.float32))
    g = -jnp.exp(a_log.astype(jnp.float32)) * jax.nn.softplus(da.astype(jnp.float32) + dt_bias.astype(jnp.float32))
    pad = (-t) % D_CHUNK
    def padt(x):
        return jnp.pad(x, [(0, 0), (0, pad)] + [(0, 0)] * (x.ndim - 2))
    o, s_fin = gated_delta_chunked(padt(q), padt(k), padt(v), padt(g), padt(beta), s0.astype(jnp.float32))
    o = rmsnorm(o[:, :t], out_gain) * jax.nn.silu(dz.reshape(bsz, t, D_HEADS, D_DV).astype(jnp.float32))
    return o.reshape(bsz, t, D_V).astype(dq.dtype), conv_new, s_fin.astype(s0.dtype)


def nsa_compress(q, q_pos, kv_rows, phi_k, phi_v, slopes):
    bsz, length = kv_rows.shape[:2]
    nbc = length // C_BLOCK
    blk = kv_rows[:, :nbc * C_BLOCK].reshape(bsz, nbc, C_BLOCK, 2, C_KV_HEADS, HEAD_DIM)
    kc = jnp.einsum('bnjhd,j->bnhd', blk[:, :, :, 0], phi_k)
    vc = jnp.einsum('bnjhd,j->bnhd', blk[:, :, :, 1], phi_v)
    s = jnp.einsum('bthgd,bnhd->bthgn', q, kc).astype(jnp.float32) * HEAD_DIM ** -0.5
    nidx = jnp.arange(nbc)
    centre = nidx * C_BLOCK + (C_BLOCK - 1) / 2.0
    dist = jnp.abs(q_pos[:, None].astype(jnp.float32) - centre[None, :].astype(jnp.float32))
    s = s - slopes[None, None, :, :, None] * dist[None, :, None, None, :]
    mask = ((nidx[None, :] + 1) * C_BLOCK - 1 <= q_pos[:, None])[None, :, None, None, :]
    p, _, l = masked_stats(s, mask)
    p = p / jnp.maximum(l, 1e-30)[..., None]
    out = jnp.einsum('bthgn,bnhd->bthgd', p, vc.astype(jnp.float32))
    return out, jnp.sum(p, axis=3)


def nsa_select_idx(imp, q_pos, n_blocks):
    nbc = imp.shape[-1]
    imp = jnp.pad(imp, ((0, 0), (0, 0), (0, 0), (0, n_blocks - nbc)))
    nidx = jnp.arange(n_blocks)[None, :]
    cur = (q_pos // C_BLOCK)[:, None]
    forced = (nidx == 0) | (nidx == cur) | (nidx == cur - 1)
    score = imp + jnp.where(forced, FORCE, 0.0)[None, :, None, :]
    score = jnp.where((nidx <= cur)[None, :, None, :], score, NEG)
    vals, idx = lax.top_k(score, min(C_N_SEL, n_blocks))
    return idx, vals > NEG / 2


def nsa_select_attend(q, q_pos, kv_g, idx, valid, slopes):
    bsz, t, hk, g, dh = q.shape
    n_keys = idx.shape[-1] * C_BLOCK
    kpos = idx[..., None] * C_BLOCK + jnp.arange(C_BLOCK)
    dist = q_pos[None, :, None, None, None] - kpos
    mask = (valid[..., None] & (dist >= 0)).reshape(bsz, t, hk, 1, n_keys)
    dist = dist.reshape(bsz, t, hk, 1, n_keys).astype(jnp.float32)
    kg = kv_g[..., 0, :].reshape(bsz, t, hk, n_keys, dh)
    vg = kv_g[..., 1, :].reshape(bsz, t, hk, n_keys, dh)
    s = jnp.einsum('bthgd,bthkd->bthgk', q, kg).astype(jnp.float32) * dh ** -0.5
    s = s - slopes[None, None, :, :, None] * dist
    p, _, l = masked_stats(s, mask)
    return jnp.einsum('bthgk,bthkd->bthgd', p / l[..., None], vg.astype(jnp.float32))


def nsa_select_prompt(q, kv_rows, idx, valid, slopes):
    bsz, s = q.shape[:2]
    nch = s // C_Q_CHUNK
    bi = jnp.arange(bsz)[:, None, None, None, None]
    hi = jnp.arange(C_KV_HEADS)[None, None, :, None, None]
    off = jnp.arange(C_BLOCK)
    def one_chunk(args):
        qc, ic, vc, pc = args
        rows = ic[..., None] * C_BLOCK + off
        kv_g = kv_rows[bi, rows, :, hi]
        return nsa_select_attend(qc, pc, kv_g, ic, vc, slopes)
    def split(x):
        return jnp.moveaxis(x.reshape(bsz, nch, C_Q_CHUNK, *x.shape[2:]), 1, 0)
    o = lax.map(one_chunk, (split(q), split(idx), split(valid), jnp.arange(s).reshape(nch, C_Q_CHUNK)))
    return jnp.moveaxis(o, 0, 1).reshape(bsz, s, *o.shape[3:])


def nsa_select_sample(q, q_pos, pool, page_table, kv_new, idx, valid, slopes):
    bd, t = q.shape[:2]
    nb_past = page_table.shape[1] * (PAGE_SIZE // C_BLOCK)
    bi = jnp.arange(bd)[:, None, None, None, None]
    hi = jnp.arange(C_KV_HEADS)[None, None, :, None, None]
    off = jnp.arange(C_BLOCK)
    tok_p = jnp.minimum(idx, nb_past - 1)[..., None] * C_BLOCK + off
    page = page_table[bi, tok_p // PAGE_SIZE]
    g_past = pool[page, tok_p % PAGE_SIZE, :, hi]
    nb_new = -(-t // C_BLOCK)
    new_pad = jnp.pad(kv_new, ((0, 0), (0, nb_new * C_BLOCK - t), (0, 0), (0, 0), (0, 0)))
    tok_n = jnp.clip(idx - nb_past, 0, nb_new - 1)[..., None] * C_BLOCK + off
    g_new = new_pad[bi, tok_n, :, hi]
    kv_g = jnp.where((idx < nb_past)[..., None, None, None], g_past, g_new.astype(g_past.dtype))
    return nsa_select_attend(q, q_pos, kv_g, idx, valid, slopes)


def window_attn_sample(q, q_pos, kv_ctx, slopes):
    n = kv_ctx.shape[1]
    k_pos = q_pos[-1] - (n - 1) + jnp.arange(n)
    dist = q_pos[:, None] - k_pos[None, :]
    mask = (dist >= 0) & (dist <= C_WIN)
    s = jnp.einsum('bthgd,bshd->bthgs', q, kv_ctx[:, :, 0]).astype(jnp.float32) * HEAD_DIM ** -0.5
    s = s - slopes[None, None, :, :, None] * dist.astype(jnp.float32)[None, :, None, None, :]
    p, _, l = masked_stats(s, mask[None, :, None, None, :])
    return jnp.einsum('bthgs,bshd->bthgd', p / l[..., None], kv_ctx[:, :, 1].astype(jnp.float32))


def even_mixer(xn, w_in, w_out, q_gain, k_gain, b_par, kv_buf, conv_buf, h0):
    bsz, t, _ = xn.shape
    u_q, u_k, u_v, u_gate, u_x = split_cols(xn @ w_in, (A_Q, A_KV, A_KV, B_WIDTH, B_WIDTH))
    q = rmsnorm(u_q.reshape(bsz, t, A_N_GROUPS, A_HEADS, HEAD_DIM), q_gain)
    k = rmsnorm(u_k.reshape(bsz, t, A_HEADS, HEAD_DIM), k_gain)
    v = u_v.reshape(bsz, t, A_HEADS, HEAD_DIM)
    kv_new = jnp.stack([k, v], axis=2)
    slopes = alibi_slopes(A_N_GROUPS * A_HEADS).reshape(A_N_GROUPS, A_HEADS)
    if kv_buf is None:
        a_out = dilated_attn_prompt(q, k, v, slopes)
        a_state = kv_new[:, -min(A_WIN_MAX, t):]
    else:
        a_out = dilated_attn_sample(q, jnp.concatenate([kv_buf.astype(kv_new.dtype), kv_new], axis=1), slopes)
        a_state = kv_new
    b_out, conv_new, h_new = rglru_mixer(u_gate, u_x, conv_buf, h0, *b_par)
    mixed = jnp.concatenate([a_out.reshape(bsz, t, A_OUT).astype(xn.dtype), b_out], axis=-1)
    return mixed @ w_out, (a_state, conv_new, h_new)


def odd_mixer(xn, w_in, w_out, q_gain, k_gain, phi_k, phi_v, d_par,
              cmp_pool, sel_pool, page_table, win_buf, d_conv_buf, d_s0):
    bsz, t, _ = xn.shape
    cq, ckv, cg, dq, dk, dv, dz, db, da = split_cols(
        xn @ w_in, (C_Q, 6 * C_KV, 3 * C_HEADS, D_QK, D_QK, D_V, D_V, D_HEADS, D_HEADS))
    q = rmsnorm(cq.reshape(bsz, t, C_KV_HEADS, C_GROUP, HEAD_DIM), q_gain)
    kv = ckv.reshape(bsz, t, 3, 2, C_KV_HEADS, HEAD_DIM)
    kv = jnp.stack([rmsnorm(kv[:, :, :, 0], k_gain[:, None, :]), kv[:, :, :, 1]], axis=3)
    cmp_rows, sel_rows, win_rows = kv[:, :, 0], kv[:, :, 1], kv[:, :, 2]
    slopes = alibi_slopes(C_HEADS).reshape(C_KV_HEADS, C_GROUP)
    if page_table is None:
        q_pos = jnp.arange(t)
        o_cmp, imp = nsa_compress(q, q_pos, cmp_rows, phi_k, phi_v, slopes)
        idx, valid = nsa_select_idx(imp, q_pos, t // C_BLOCK)
        o_sel = nsa_select_prompt(q, sel_rows, idx, valid, slopes)
        o_un, _, l = band_attn(q, win_rows[:, :, 0], win_rows[:, :, 1], C_WIN, slopes, 1)
        o_win = o_un / l[..., None]
        win_state = win_rows[:, -min(C_WIN, t):]
    else:
        past = page_table.shape[1] * PAGE_SIZE
        q_pos = past + jnp.arange(t)
        cmp_past = cmp_pool[page_table].reshape(bsz, past, 2, C_KV_HEADS, HEAD_DIM)
        cmp_ctx = jnp.concatenate([cmp_past, cmp_rows.astype(cmp_past.dtype)], axis=1)
        o_cmp, imp = nsa_compress(q, q_pos, cmp_ctx, phi_k, phi_v, slopes)
        idx, valid = nsa_select_idx(imp, q_pos, -(-(past + t) // C_BLOCK))
        o_sel = nsa_select_sample(q, q_pos, sel_pool, page_table, sel_rows, idx, valid, slopes)
        o_win = window_attn_sample(q, q_pos, jnp.concatenate([win_buf.astype(win_rows.dtype), win_rows], axis=1), slopes)
        win_state = win_rows
    gate = jax.nn.sigmoid(cg.reshape(bsz, t, 3, C_KV_HEADS, C_GROUP, 1).astype(jnp.float32))
    o_c = gate[:, :, 0] * o_cmp + gate[:, :, 1] * o_sel + gate[:, :, 2] * o_win
    d_out, d_conv_new, d_s = gdn_mixer(dq, dk, dv, dz, db, da, d_conv_buf, d_s0, *d_par)
    mixed = jnp.concatenate([o_c.reshape(bsz, t, C_Q).astype(xn.dtype), d_out], axis=-1)
    return mixed @ w_out, (cmp_rows, sel_rows, win_state, d_conv_new, d_s)


def grouped_experts(xf, experts, gates, w1, w3, w2):
    n_tok, d = xf.shape
    n_asg = n_tok * TOP_K
    blk = MOE_BLOCK if n_asg >= 4 * MOE_BLOCK else 8
    flat_e = experts.reshape(-1)
    order = jnp.argsort(flat_e)
    sorted_e = flat_e[order]
    counts = jnp.bincount(flat_e, length=N_EXPERTS)
    padded = (counts + blk - 1) // blk * blk
    pad_end = jnp.cumsum(padded)
    pad_start = pad_end - padded
    start = jnp.cumsum(counts) - counts
    dest = pad_start[sorted_e] + jnp.arange(n_asg) - start[sorted_e]
    n_rows = -(-(n_asg + N_EXPERTS * (blk - 1)) // blk) * blk
    n_blk = n_rows // blk
    tok = order // TOP_K
    buf = jnp.zeros((n_rows, d), xf.dtype).at[dest].set(xf[tok])
    blk_exp = jnp.minimum(jnp.searchsorted(pad_end, jnp.arange(n_blk) * blk, side='right'), N_EXPERTS - 1)
    def run(args):
        xb, e = args
        return (jax.nn.silu(xb @ w1[e]) * (xb @ w3[e])) @ w2[e]
    out = lax.map(run, (buf.reshape(n_blk, blk, d), blk_exp)).reshape(n_rows, d)
    contrib = out[dest] * gates.reshape(-1)[order][:, None].astype(out.dtype)
    return jax.ops.segment_sum(contrib, tok, num_segments=n_tok)


def hier_moe(xn, g_w, g_b, e_w, e_b, w1, w3, w2):
    bsz, t, d = xn.shape
    xf = xn.reshape(-1, d)
    g_logit = (xf @ g_w).astype(jnp.float32) + g_b
    g_prob = jax.nn.softmax(g_logit, axis=-1)
    grp = jnp.argmax(g_logit, axis=-1)
    p_grp = jnp.take_along_axis(g_prob, grp[:, None], axis=1)[:, 0]
    e_logit = ((xf @ e_w).astype(jnp.float32) + e_b).reshape(-1, N_GROUPS, EXPERTS_PER_GROUP)
    e_logit = jnp.take_along_axis(e_logit, grp[:, None, None], axis=1)[:, 0]
    e_val, e_idx = lax.top_k(e_logit, TOP_K)
    gates = p_grp[:, None] * jax.nn.softmax(e_val, axis=-1)
    experts = grp[:, None] * EXPERTS_PER_GROUP + e_idx
    return grouped_experts(xf, experts, gates, w1, w3, w2).reshape(bsz, t, d)


def setup_inputs(seed: int = 0) -> dict:
    key = jax.random.key(seed)
    ks = iter(jax.random.split(key, 64))
    f32 = jnp.float32
    def nrm(shape, scale):
        return jax.random.normal(next(ks), shape, f32) * scale
    def gain(shape):
        return 1.0 + nrm(shape, 0.02)
    n_pages = PAST_LEN // PAGE_SIZE
    n_phys = -(-5 * DEC_BATCH * n_pages // 4)
    a_buf = min(A_WIN_MAX, PAST_LEN)
    c_buf = min(C_WIN, PAST_LEN)
    page_table = jax.random.permutation(next(ks), n_phys)[:DEC_BATCH * n_pages].reshape(DEC_BATCH, n_pages).astype(jnp.int32)
    u = jax.random.uniform(next(ks), (N_A_LAYERS, B_WIDTH), f32, 0.9, 0.999)
    sa = u ** (1.0 / RG_C)
    b_lambda = jnp.log(sa) - jnp.log1p(-sa)
    dt = jnp.exp(jax.random.uniform(next(ks), (N_C_LAYERS, D_HEADS), f32, math.log(1e-3), math.log(1e-1)))
    d_dt_bias = dt + jnp.log(-jnp.expm1(-dt))
    d_a_log = jnp.log(jax.random.uniform(next(ks), (N_C_LAYERS, D_HEADS), f32, 1.0, 16.0))
    return {
        'x_prompt': nrm((BATCH, SEQ, D_MODEL), 1.0),
        'x_sample': nrm((DEC_BATCH, DEC_SEQ, D_MODEL), 1.0),
        'cache_a_kv': nrm((N_A_LAYERS, DEC_BATCH, a_buf, 2, A_HEADS, HEAD_DIM), 1.0),
        'state_b_conv': nrm((N_A_LAYERS, DEC_BATCH, CONV_W - 1, B_WIDTH), 1.0),
        'state_b_h': nrm((N_A_LAYERS, DEC_BATCH, B_WIDTH), 0.5),
        'cache_c_cmp_kv': nrm((N_C_LAYERS, n_phys, PAGE_SIZE, 2, C_KV_HEADS, HEAD_DIM), 1.0),
        'cache_c_sel_kv': nrm((N_C_LAYERS, n_phys, PAGE_SIZE, 2, C_KV_HEADS, HEAD_DIM), 1.0),
        'cache_c_win_kv': nrm((N_C_LAYERS, DEC_BATCH, c_buf, 2, C_KV_HEADS, HEAD_DIM), 1.0),
        'state_d_conv': nrm((N_C_LAYERS, DEC_BATCH, CONV_W - 1, D_CONV), 1.0),
        'state_d_S': nrm((N_C_LAYERS, DEC_BATCH, D_HEADS, D_DK, D_DV), 0.1),
        'page_table': page_table,
        'norm_mix': gain((DEPTH, D_MODEL)),
        'norm_ffn': gain((DEPTH, D_MODEL)),
        'even_w_in': nrm((N_A_LAYERS, D_MODEL, EVEN_IN), D_MODEL ** -0.5),
        'even_w_out': nrm((N_A_LAYERS, EVEN_MIX, D_MODEL), EVEN_MIX ** -0.5),
        'a_q_norm': gain((N_A_LAYERS, HEAD_DIM)),
        'a_k_norm': gain((N_A_LAYERS, HEAD_DIM)),
        'b_conv_w': nrm((N_A_LAYERS, CONV_W, B_WIDTH), CONV_W ** -0.5),
        'b_conv_b': nrm((N_A_LAYERS, B_WIDTH), 0.01),
        'b_gate_a_w': nrm((N_A_LAYERS, B_BLOCKS, B_BLOCK_DIM, B_BLOCK_DIM), B_BLOCK_DIM ** -0.5),
        'b_gate_a_b': nrm((N_A_LAYERS, B_WIDTH), 0.01),
        'b_gate_x_w': nrm((N_A_LAYERS, B_BLOCKS, B_BLOCK_DIM, B_BLOCK_DIM), B_BLOCK_DIM ** -0.5),
        'b_gate_x_b': nrm((N_A_LAYERS, B_WIDTH), 0.01),
        'b_lambda': b_lambda,
        'odd_w_in': nrm((N_C_LAYERS, D_MODEL, ODD_IN), D_MODEL ** -0.5),
        'odd_w_out': nrm((N_C_LAYERS, ODD_MIX, D_MODEL), ODD_MIX ** -0.5),
        'c_q_norm': gain((N_C_LAYERS, HEAD_DIM)),
        'c_k_norm': gain((N_C_LAYERS, 3, HEAD_DIM)),
        'c_phi_k': (1.0 + nrm((N_C_LAYERS, C_BLOCK), 0.1)) / C_BLOCK,
        'c_phi_v': (1.0 + nrm((N_C_LAYERS, C_BLOCK), 0.1)) / C_BLOCK,
        'd_conv_w': nrm((N_C_LAYERS, CONV_W, D_CONV), CONV_W ** -0.5),
        'd_a_log': d_a_log,
        'd_dt_bias': d_dt_bias,
        'd_out_norm': gain((N_C_LAYERS, D_DV)),
        'moe_group_w': nrm((DEPTH, D_MODEL, N_GROUPS), D_MODEL ** -0.5),
        'moe_group_b': nrm((DEPTH, N_GROUPS), 0.01),
        'moe_expert_w': nrm((DEPTH, D_MODEL, N_EXPERTS), D_MODEL ** -0.5),
        'moe_expert_b': nrm((DEPTH, N_EXPERTS), 0.01),
        'moe_w1': nrm((DEPTH, N_EXPERTS, D_MODEL, D_EXPERT), D_MODEL ** -0.5),
        'moe_w3': nrm((DEPTH, N_EXPERTS, D_MODEL, D_EXPERT), D_MODEL ** -0.5),
        'moe_w2': nrm((DEPTH, N_EXPERTS, D_EXPERT, D_MODEL), D_EXPERT ** -0.5),
    }


def reference(x_prompt, x_sample, cache_a_kv, state_b_conv, state_b_h, cache_c_cmp_kv, cache_c_sel_kv,
              cache_c_win_kv, state_d_conv, state_d_S, page_table,
              norm_mix, norm_ffn, even_w_in, even_w_out, a_q_norm, a_k_norm,
              b_conv_w, b_conv_b, b_gate_a_w, b_gate_a_b, b_gate_x_w, b_gate_x_b, b_lambda,
              odd_w_in, odd_w_out, c_q_norm, c_k_norm, c_phi_k, c_phi_v,
              d_conv_w, d_a_log, d_dt_bias, d_out_norm,
              moe_group_w, moe_group_b, moe_expert_w, moe_expert_b, moe_w1, moe_w3, moe_w2):
    hp, hs = x_prompt, x_sample
    bp = x_prompt.shape[0]
    ak_p, ak_s, bc_p, bc_s, bh_p, bh_s = [], [], [], [], [], []
    cc_p, cc_s, cs_p, cs_s, cw_p, cw_s = [], [], [], [], [], []
    dc_p, dc_s, ds_p, ds_s = [], [], [], []
    for l in range(DEPTH):
        i = l // 2
        xp, xs = rmsnorm(hp, norm_mix[l]), rmsnorm(hs, norm_mix[l])
        if l % 2 == 0:
            b_par = (b_conv_w[i], b_conv_b[i], b_gate_a_w[i], b_gate_a_b[i], b_gate_x_w[i], b_gate_x_b[i], b_lambda[i])
            yp, st_p = even_mixer(xp, even_w_in[i], even_w_out[i], a_q_norm[i], a_k_norm[i], b_par, None,
                                  jnp.zeros((bp, CONV_W - 1, B_WIDTH), xp.dtype), jnp.zeros((bp, B_WIDTH), xp.dtype))
            ys, st_s = even_mixer(xs, even_w_in[i], even_w_out[i], a_q_norm[i], a_k_norm[i], b_par,
                                  cache_a_kv[i], state_b_conv[i], state_b_h[i])
            ak_p.append(st_p[0]); bc_p.append(st_p[1]); bh_p.append(st_p[2])
            ak_s.append(st_s[0]); bc_s.append(st_s[1]); bh_s.append(st_s[2])
        else:
            d_par = (d_conv_w[i], d_a_log[i], d_dt_bias[i], d_out_norm[i])
            yp, st_p = odd_mixer(xp, odd_w_in[i], odd_w_out[i], c_q_norm[i], c_k_norm[i], c_phi_k[i], c_phi_v[i], d_par,
                                 None, None, None, None,
                                 jnp.zeros((bp, CONV_W - 1, D_CONV), xp.dtype),
                                 jnp.zeros((bp, D_HEADS, D_DK, D_DV), xp.dtype))
            ys, st_s = odd_mixer(xs, odd_w_in[i], odd_w_out[i], c_q_norm[i], c_k_norm[i], c_phi_k[i], c_phi_v[i], d_par,
                                 cache_c_cmp_kv[i], cache_c_sel_kv[i], page_table, cache_c_win_kv[i],
                                 state_d_conv[i], state_d_S[i])
            cc_p.append(st_p[0]); cs_p.append(st_p[1]); cw_p.append(st_p[2]); dc_p.append(st_p[3]); ds_p.append(st_p[4])
            cc_s.append(st_s[0]); cs_s.append(st_s[1]); cw_s.append(st_s[2]); dc_s.append(st_s[3]); ds_s.append(st_s[4])
        hp = hp + yp
        hs = hs + ys
        moe_par = (moe_group_w[l], moe_group_b[l], moe_expert_w[l], moe_expert_b[l], moe_w1[l], moe_w3[l], moe_w2[l])
        hp = hp + hier_moe(rmsnorm(hp, norm_ffn[l]), *moe_par)
        hs = hs + hier_moe(rmsnorm(hs, norm_ffn[l]), *moe_par)
    return (hp, hs,
            jnp.stack(ak_p), jnp.stack(ak_s), jnp.stack(bc_p), jnp.stack(bc_s), jnp.stack(bh_p), jnp.stack(bh_s),
            jnp.stack(cc_p), jnp.stack(cc_s), jnp.stack(cs_p), jnp.stack(cs_s), jnp.stack(cw_p), jnp.stack(cw_s),
            jnp.stack(dc_p), jnp.stack(dc_s), jnp.stack(ds_p), jnp.stack(ds_s))
```

```python
import functools
import math

import jax
import jax.numpy as jnp
import numpy as np
from jax import lax
from jax.experimental import pallas as pl
from jax.experimental.pallas import tpu as pltpu

D_MODEL = 4096
HEAD_DIM = 128
CONV_W = 4
BAND_BLOCK = 128
A_PATTERNS = ((128, 1), (512, 4), (2048, 16))
A_N_GROUPS = len(A_PATTERNS)
A_HEADS = D_MODEL // 512
A_WIN_MAX = max(w for w, _ in A_PATTERNS)
A_Q = A_N_GROUPS * A_HEADS * HEAD_DIM
A_KV = A_HEADS * HEAD_DIM
A_OUT = A_HEADS * HEAD_DIM
B_WIDTH = 3 * D_MODEL // 4
B_BLOCKS = B_WIDTH // HEAD_DIM
B_BLOCK_DIM = B_WIDTH // B_BLOCKS
RG_C = 8.0
C_HEADS = D_MODEL // 256
C_KV_HEADS = C_HEADS // 4
C_GROUP = C_HEADS // C_KV_HEADS
C_BLOCK = 64
C_N_SEL = 16
C_WIN = 512
C_Q_CHUNK = 32
C_Q = C_HEADS * HEAD_DIM
C_KV = C_KV_HEADS * HEAD_DIM
D_HEADS = D_MODEL // 256
D_DK = HEAD_DIM
D_DV = HEAD_DIM
D_QK = D_HEADS * D_DK
D_V = D_HEADS * D_DV
D_CONV = 2 * D_QK + D_V
D_CHUNK = 64
N_GROUPS = 8
EXPERTS_PER_GROUP = 8
N_EXPERTS = N_GROUPS * EXPERTS_PER_GROUP
TOP_K = 2
D_EXPERT = D_MODEL // 8
PAGE_SIZE = 128
EPS = 1e-6
NEG = -1e30
FORCE = 1e4

LANES = 128
VMEM_LIMIT = 56 * 1024 * 1024
ROUTER_PAD = LANES

HI = lax.Precision.HIGHEST


def _params(sem):
    return pltpu.CompilerParams(dimension_semantics=sem, vmem_limit_bytes=VMEM_LIMIT)


def _rmsnorm_kernel(x_ref, g_ref, o_ref):
    x = x_ref[...]
    y = x * lax.rsqrt(jnp.mean(x * x, axis=-1, keepdims=True) + EPS)
    o_ref[...] = (y * g_ref[...]).astype(o_ref.dtype)


def _rmsnorm(x2d, gain, out_dtype):
    m, d = x2d.shape
    tm = min(m, 512)
    return pl.pallas_call(
        _rmsnorm_kernel,
        out_shape=jax.ShapeDtypeStruct((m, d), out_dtype),
        grid=(m // tm,),
        in_specs=[pl.BlockSpec((tm, d), lambda i: (i, 0)), pl.BlockSpec((1, d), lambda i: (0, 0))],
        out_specs=pl.BlockSpec((tm, d), lambda i: (i, 0)),
        compiler_params=_params(("parallel",)),
        name="rmsnorm",
    )(x2d, gain.reshape(1, d))


def _matmul_kernel(x_ref, w_ref, *rest, exact, has_res):
    if has_res:
        r_ref, o_ref, acc_ref = rest
    else:
        o_ref, acc_ref = rest
    k = pl.program_id(2)

    @pl.when(k == 0)
    def _():
        acc_ref[...] = jnp.zeros_like(acc_ref)

    if exact:
        acc_ref[...] += jnp.dot(x_ref[...], w_ref[...], preferred_element_type=jnp.float32, precision=HI)
    else:
        acc_ref[...] += jnp.dot(x_ref[...], w_ref[...].astype(jnp.bfloat16), preferred_element_type=jnp.float32)

    @pl.when(k == pl.num_programs(2) - 1)
    def _():
        out = acc_ref[...]
        if has_res:
            out = out + r_ref[...]
        o_ref[...] = out


def _matmul(x, w3d, layer, col0, n, res=None, tn=1024, tk=512):
    m, kdim = x.shape
    exact = x.dtype == jnp.float32
    tm = min(m, 1024)
    tn = min(tn, n)
    assert m % tm == 0 and n % tn == 0 and kdim % tk == 0 and col0 % tn == 0
    jb = col0 // tn
    in_specs = [pl.BlockSpec((tm, tk), lambda i, j, k: (i, k)),
                pl.BlockSpec((None, tk, tn), lambda i, j, k: (layer, k, j + jb))]
    args = [x, w3d]
    if res is not None:
        in_specs.append(pl.BlockSpec((tm, tn), lambda i, j, k: (i, j)))
        args.append(res)
    return pl.pallas_call(
        functools.partial(_matmul_kernel, exact=exact, has_res=res is not None),
        out_shape=jax.ShapeDtypeStruct((m, n), jnp.float32),
        grid=(m // tm, n // tn, kdim // tk),
        in_specs=in_specs,
        out_specs=pl.BlockSpec((tm, tn), lambda i, j, k: (i, j)),
        scratch_shapes=[pltpu.VMEM((tm, tn), jnp.float32)],
        compiler_params=_params(("parallel", "parallel", "arbitrary")),
        name="proj",
    )(*args)


def _router_kernel(h_ref, g_ref, w_ref, b_ref, xn_ref, logit_ref):
    x = h_ref[...]
    y = x * lax.rsqrt(jnp.mean(x * x, axis=-1, keepdims=True) + EPS) * g_ref[...]
    xn_ref[...] = y.astype(xn_ref.dtype)
    logit_ref[...] = jnp.dot(y, w_ref[...], preferred_element_type=jnp.float32, precision=HI) + b_ref[...]


def _router(h2d, gain, w_router, b_router, xn_dtype):
    m, d = h2d.shape
    tm = min(m, 256)
    return pl.pallas_call(
        _router_kernel,
        out_shape=(jax.ShapeDtypeStruct((m, d), xn_dtype), jax.ShapeDtypeStruct((m, ROUTER_PAD), jnp.float32)),
        grid=(m // tm,),
        in_specs=[pl.BlockSpec((tm, d), lambda i: (i, 0)), pl.BlockSpec((1, d), lambda i: (0, 0)),
                  pl.BlockSpec((d, ROUTER_PAD), lambda i: (0, 0)), pl.BlockSpec((1, ROUTER_PAD), lambda i: (0, 0))],
        out_specs=(pl.BlockSpec((tm, d), lambda i: (i, 0)), pl.BlockSpec((tm, ROUTER_PAD), lambda i: (i, 0))),
        compiler_params=_params(("parallel",)),
        name="ffn_norm_router",
    )(h2d, gain.reshape(1, d), w_router, b_router)


MOE_K_CHUNK = 512


def _moe_up_kernel(be_ref, x_ref, w1_ref, w3_ref, o_ref, *, exact):
    tm = x_ref.shape[0]
    a = jnp.zeros((tm, D_EXPERT), jnp.float32)
    b = jnp.zeros((tm, D_EXPERT), jnp.float32)
    for c in range(D_MODEL // MOE_K_CHUNK):
        sl = slice(c * MOE_K_CHUNK, (c + 1) * MOE_K_CHUNK)
        if exact:
            a += jnp.dot(x_ref[:, sl], w1_ref[sl, :], preferred_element_type=jnp.float32, precision=HI)
            b += jnp.dot(x_ref[:, sl], w3_ref[sl, :], preferred_element_type=jnp.float32, precision=HI)
        else:
            a += jnp.dot(x_ref[:, sl], w1_ref[sl, :].astype(jnp.bfloat16), preferred_element_type=jnp.float32)
            b += jnp.dot(x_ref[:, sl], w3_ref[sl, :].astype(jnp.bfloat16), preferred_element_type=jnp.float32)
    o_ref[...] = (a * jax.nn.sigmoid(a) * b).astype(o_ref.dtype)


def _moe_down_kernel(be_ref, h_ref, w2_ref, g_ref, o_ref, *, exact):
    if exact:
        out = jnp.dot(h_ref[...], w2_ref[...], preferred_element_type=jnp.float32, precision=HI)
    else:
        out = jnp.dot(h_ref[...], w2_ref[...].astype(jnp.bfloat16), preferred_element_type=jnp.float32)
    o_ref[...] = out * g_ref[...]


def _moe_experts(x_rows, row_gate, blk_exp, w1, w3, w2, layer, tm):
    rows, d = x_rows.shape
    n_blk = rows // tm
    exact = x_rows.dtype == jnp.float32
    up = pl.pallas_call(
        functools.partial(_moe_up_kernel, exact=exact),
        out_shape=jax.ShapeDtypeStruct((rows, D_EXPERT), x_rows.dtype),
        grid_spec=pltpu.PrefetchScalarGridSpec(
            num_scalar_prefetch=1, grid=(n_blk,),
            in_specs=[pl.BlockSpec((tm, d), lambda i, be: (i, 0)),
                      pl.BlockSpec((None, None, d, D_EXPERT), lambda i, be: (layer, be[i], 0, 0)),
                      pl.BlockSpec((None, None, d, D_EXPERT), lambda i, be: (layer, be[i], 0, 0))],
            out_specs=pl.BlockSpec((tm, D_EXPERT), lambda i, be: (i, 0))),
        compiler_params=_params(("arbitrary",)),
        name="moe_up",
    )(blk_exp, x_rows, w1, w3)
    return pl.pallas_call(
        functools.partial(_moe_down_kernel, exact=exact),
        out_shape=jax.ShapeDtypeStruct((rows, d), jnp.float32),
        grid_spec=pltpu.PrefetchScalarGridSpec(
            num_scalar_prefetch=1, grid=(n_blk,),
            in_specs=[pl.BlockSpec((tm, D_EXPERT), lambda i, be: (i, 0)),
                      pl.BlockSpec((None, None, D_EXPERT, d), lambda i, be: (layer, be[i], 0, 0)),
                      pl.BlockSpec((tm, 1), lambda i, be: (i, 0))],
            out_specs=pl.BlockSpec((tm, d), lambda i, be: (i, 0))),
        compiler_params=_params(("arbitrary",)),
        name="moe_down",
    )(blk_exp, up, w2, row_gate)


def _hier_moe(h2d, gain, w_router, b_router, w1, w3, w2, layer, tm, xn_dtype):
    n_tok, d = h2d.shape
    xn, logits = _router(h2d, gain, w_router, b_router, xn_dtype)
    g_logit = logits[:, :N_GROUPS]
    g_prob = jax.nn.softmax(g_logit, axis=-1)
    grp = jnp.argmax(g_logit, axis=-1)
    p_grp = jnp.take_along_axis(g_prob, grp[:, None], axis=1)[:, 0]
    e_logit = logits[:, N_GROUPS:N_GROUPS + N_EXPERTS].reshape(-1, N_GROUPS, EXPERTS_PER_GROUP)
    e_logit = jnp.take_along_axis(e_logit, grp[:, None, None], axis=1)[:, 0]
    e_val, e_idx = lax.top_k(e_logit, TOP_K)
    gates = p_grp[:, None] * jax.nn.softmax(e_val, axis=-1)
    experts = grp[:, None] * EXPERTS_PER_GROUP + e_idx
    n_asg = n_tok * TOP_K
    flat_e = experts.reshape(-1).astype(jnp.int32)
    order = jnp.argsort(flat_e)
    sorted_e = flat_e[order]
    counts = jnp.bincount(flat_e, length=N_EXPERTS)
    padded = (counts + tm - 1) // tm * tm
    pad_end = jnp.cumsum(padded)
    pad_start = pad_end - padded
    start = jnp.cumsum(counts) - counts
    dest = (pad_start[sorted_e] + jnp.arange(n_asg) - start[sorted_e]).astype(jnp.int32)
    n_rows = -(-(n_asg + N_EXPERTS * (tm - 1)) // tm) * tm
    n_blk = n_rows // tm
    tok = (order // TOP_K).astype(jnp.int32)
    row_tok = jnp.zeros((n_rows,), jnp.int32).at[dest].set(tok)
    row_gate = jnp.zeros((n_rows,), jnp.float32).at[dest].set(gates.reshape(-1)[order])
    blk_exp = jnp.minimum(jnp.searchsorted(pad_end, jnp.arange(n_blk) * tm, side='right'),
                          N_EXPERTS - 1).astype(jnp.int32)
    x_rows = xn[row_tok]
    out = _moe_experts(x_rows, row_gate[:, None], blk_exp, w1, w3, w2, layer, tm)
    asg_row = jnp.zeros((n_asg,), jnp.int32).at[order].set(dest)
    y = out[asg_row].reshape(n_tok, TOP_K, d).sum(axis=1)
    return h2d + y


def _rms(x, g):
    xf = x.astype(jnp.float32)
    y = xf * lax.rsqrt(jnp.mean(xf * xf, axis=-1, keepdims=True) + EPS)
    return (y * g.astype(jnp.float32)).astype(x.dtype)


def _l2norm(x):
    return x * lax.rsqrt(jnp.sum(x * x, axis=-1, keepdims=True) + EPS)


def _alibi_slopes(n):
    return 2.0 ** (-8.0 * (jnp.arange(n, dtype=jnp.float32) + 1.0) / n)


def _split_cols(u, sizes):
    cuts = [int(c) for c in np.cumsum(sizes)[:-1]]
    return jnp.split(u, cuts, axis=-1)


def _masked_stats(s, mask):
    s = jnp.where(mask, s, NEG)
    m = jnp.max(s, axis=-1, keepdims=True)
    p = jnp.where(mask, jnp.exp(s - m), 0.0)
    return p, m[..., 0], jnp.sum(p, axis=-1)


def _merge_by_denominator(stats):
    big_m = stats[0][1]
    for _, m, _ in stats[1:]:
        big_m = jnp.maximum(big_m, m)
    num, den = 0.0, 0.0
    for o, m, l in stats:
        w = jnp.exp(m - big_m)
        num = num + w[..., None] * o
        den = den + w * l
    return num / den[..., None]


def _causal_conv(x, buf, w, b=None):
    t = x.shape[1]
    xx = jnp.concatenate([buf.astype(x.dtype), x], axis=1)
    y = xx[:, CONV_W - 1:] * w[CONV_W - 1]
    for j in range(CONV_W - 1):
        y = y + xx[:, j:j + t] * w[j]
    if b is not None:
        y = y + b
    return y, xx[:, -(CONV_W - 1):]


def _band_attn(q, k, v, n_back, slopes, dist_scale):
    bsz, t, h, g, dh = q.shape
    nbb = -(-n_back // BAND_BLOCK)
    nq = -(-t // BAND_BLOCK)
    pad = nq * BAND_BLOCK - t
    front = nbb * BAND_BLOCK
    qp = jnp.pad(q, ((0, 0), (0, pad), (0, 0), (0, 0), (0, 0)))
    kp = jnp.pad(k, ((0, 0), (front, pad), (0, 0), (0, 0)))
    vp = jnp.pad(v, ((0, 0), (front, pad), (0, 0), (0, 0)))
    kw = (nbb + 1) * BAND_BLOCK
    idx = jnp.arange(nq)[:, None] * BAND_BLOCK + jnp.arange(kw)[None, :]
    k_win, v_win = kp[:, idx], vp[:, idx]
    qr = qp.reshape(bsz, nq, BAND_BLOCK, h, g, dh)
    s = jnp.einsum('bnqhgd,bnkhd->bnhgqk', qr, k_win).astype(jnp.float32) * dh ** -0.5
    t_pos = jnp.arange(nq)[:, None] * BAND_BLOCK + jnp.arange(BAND_BLOCK)[None, :]
    s_pos = idx - front
    dist = t_pos[:, :, None] - s_pos[:, None, :]
    mask = (dist >= 0) & (dist <= n_back) & (s_pos[:, None, :] >= 0)
    s = s - slopes[None, None, :, :, None, None] * (dist * dist_scale).astype(jnp.float32)[None, :, None, None]
    p, m, l = _masked_stats(s, mask[None, :, None, None])
    o = jnp.einsum('bnhgqk,bnkhd->bnqhgd', p, v_win.astype(jnp.float32))
    o = o.reshape(bsz, nq * BAND_BLOCK, h, g, dh)[:, :t]
    m = jnp.moveaxis(m, 4, 2).reshape(bsz, nq * BAND_BLOCK, h, g)[:, :t]
    l = jnp.moveaxis(l, 4, 2).reshape(bsz, nq * BAND_BLOCK, h, g)[:, :t]
    return o, m, l


def _dilated_attn_prompt(q, k, v, slopes):
    bsz, s = q.shape[:2]
    stats = []
    for gi, (w, d) in enumerate(A_PATTERNS):
        n_res = s // d

        def to_res(x):
            return jnp.swapaxes(x.reshape(bsz, n_res, d, *x.shape[2:]), 1, 2).reshape(bsz * d, n_res, *x.shape[2:])

        def from_res(x):
            return jnp.swapaxes(x.reshape(bsz, d, n_res, *x.shape[2:]), 1, 2).reshape(bsz, s, *x.shape[2:])

        o, m, l = _band_attn(to_res(q[:, :, gi])[:, :, :, None], to_res(k), to_res(v), w // d, slopes[gi][:, None], d)
        stats.append((from_res(o[:, :, :, 0]), from_res(m[:, :, :, 0]), from_res(l[:, :, :, 0])))
    return _merge_by_denominator(stats)


def _dilated_attn_sample(q, kv_ctx, slopes):
    t = q.shape[1]
    wb = kv_ctx.shape[1] - t
    stats = []
    for gi, (w, d) in enumerate(A_PATTERNS):
        j = jnp.arange(w // d + 1)
        idx = wb + jnp.arange(t)[:, None] - d * j[None, :]
        kvg = kv_ctx[:, jnp.maximum(idx, 0)]
        s = jnp.einsum('bthd,btjhd->bthj', q[:, :, gi], kvg[:, :, :, 0]).astype(jnp.float32) * HEAD_DIM ** -0.5
        s = s - slopes[gi][None, None, :, None] * (d * j).astype(jnp.float32)[None, None, None, :]
        p, m, l = _masked_stats(s, (idx >= 0)[None, :, None, :])
        o = jnp.einsum('bthj,btjhd->bthd', p, kvg[:, :, :, 1].astype(jnp.float32))
        stats.append((o, m, l))
    return _merge_by_denominator(stats)


def _rglru_mixer(gate_in, x_in, conv_buf, h0, conv_w, conv_b, wa, ba, wx, bx, lam):
    xc, conv_new = _causal_conv(x_in, conv_buf, conv_w, conv_b)
    bsz, t, _ = xc.shape
    xb = xc.reshape(bsz, t, B_BLOCKS, B_BLOCK_DIM)
    r = jax.nn.sigmoid(jnp.einsum('btni,nij->btnj', xb, wa).reshape(bsz, t, B_WIDTH) + ba).astype(jnp.float32)
    ig = jax.nn.sigmoid(jnp.einsum('btni,nij->btnj', xb, wx).reshape(bsz, t, B_WIDTH) + bx).astype(jnp.float32)
    log_a = -RG_C * r * jax.nn.softplus(-lam.astype(jnp.float32))
    a = jnp.exp(log_a)
    bterm = jnp.sqrt(-jnp.expm1(2.0 * log_a)) * (ig * xc.astype(jnp.float32))
    bterm = bterm.at[:, 0].add(a[:, 0] * h0.astype(jnp.float32))

    def comb(e1, e2):
        return e1[0] * e2[0], e2[0] * e1[1] + e2[1]

    _, h = lax.associative_scan(comb, (a, bterm), axis=1)
    y = h.astype(x_in.dtype) * jax.nn.gelu(gate_in)
    return y, conv_new, h[:, -1].astype(h0.dtype)


def _gated_delta_chunked(q, k, v, g, beta, s0):
    bsz, t, h, _ = q.shape
    dv = v.shape[-1]
    n, c = t // D_CHUNK, D_CHUNK

    def chunks(x):
        return jnp.moveaxis(x.reshape(bsz, n, c, h, *x.shape[3:]), (1, 3), (0, 2))

    qc, kc, vc, gc, bc = chunks(q), chunks(k), chunks(v), chunks(g), chunks(beta)
    gam = jnp.cumsum(gc, axis=-1)
    lower = jnp.tril(jnp.ones((c, c), bool))
    strict = jnp.tril(jnp.ones((c, c), bool), -1)
    diff = gam[..., :, None] - gam[..., None, :]
    decay = jnp.where(lower, jnp.exp(jnp.where(lower, diff, 0.0)), 0.0)
    kk = jnp.einsum('nbhid,nbhjd->nbhij', kc, kc)
    a_mat = jnp.where(strict, bc[..., :, None] * kk * decay, 0.0) + jnp.eye(c, dtype=jnp.float32)
    rhs = jnp.concatenate([vc * bc[..., None], kc * (bc * jnp.exp(gam))[..., None]], axis=-1)
    sol = lax.linalg.triangular_solve(a_mat, rhs, left_side=True, lower=True)
    u, w = sol[..., :dv], sol[..., dv:]
    qk = jnp.where(lower, jnp.einsum('nbhid,nbhjd->nbhij', qc, kc) * decay, 0.0)
    q_dec = qc * jnp.exp(gam)[..., None]
    k_dec = kc * jnp.exp(gam[..., -1:] - gam)[..., None]
    last = jnp.exp(gam[..., -1])[..., None, None]

    def step(s, xs):
        u_n, w_n, q_n, k_n, qk_n, last_n = xs
        v_new = u_n - jnp.einsum('bhck,bhkv->bhcv', w_n, s)
        o = jnp.einsum('bhck,bhkv->bhcv', q_n, s) + jnp.einsum('bhij,bhjv->bhiv', qk_n, v_new)
        s = last_n * s + jnp.einsum('bhck,bhcv->bhkv', k_n, v_new)
        return s, o

    s_fin, o = lax.scan(step, s0, (u, w, q_dec, k_dec, qk, last))
    o = jnp.moveaxis(o, (0, 2), (1, 3)).reshape(bsz, t, h, dv)
    return o, s_fin


def _gdn_mixer(dq, dk, dv, dz, db, da, conv_buf, s0, conv_w, a_log, dt_bias, out_gain):
    bsz, t, _ = dq.shape
    qkv, conv_new = _causal_conv(jnp.concatenate([dq, dk, dv], axis=-1), conv_buf, conv_w)
    qkv = jax.nn.silu(qkv).astype(jnp.float32)
    q = _l2norm(qkv[..., :D_QK].reshape(bsz, t, D_HEADS, D_DK)) * D_DK ** -0.5
    k = _l2norm(qkv[..., D_QK:2 * D_QK].reshape(bsz, t, D_HEADS, D_DK))
    v = qkv[..., 2 * D_QK:].reshape(bsz, t, D_HEADS, D_DV)
    beta = jax.nn.sigmoid(db.astype(jnp.float32))
    g = -jnp.exp(a_log.astype(jnp.float32)) * jax.nn.softplus(da.astype(jnp.float32) + dt_bias.astype(jnp.float32))
    pad = (-t) % D_CHUNK

    def padt(x):
        return jnp.pad(x, [(0, 0), (0, pad)] + [(0, 0)] * (x.ndim - 2))

    o, s_fin = _gated_delta_chunked(padt(q), padt(k), padt(v), padt(g), padt(beta), s0.astype(jnp.float32))
    o = _rms(o[:, :t], out_gain) * jax.nn.silu(dz.reshape(bsz, t, D_HEADS, D_DV).astype(jnp.float32))
    return o.reshape(bsz, t, D_V).astype(dq.dtype), conv_new, s_fin.astype(s0.dtype)


def _nsa_compress(q, q_pos, kv_rows, phi_k, phi_v, slopes):
    bsz, length = kv_rows.shape[:2]
    nbc = length // C_BLOCK
    blk = kv_rows[:, :nbc * C_BLOCK].reshape(bsz, nbc, C_BLOCK, 2, C_KV_HEADS, HEAD_DIM)
    kc = jnp.einsum('bnjhd,j->bnhd', blk[:, :, :, 0], phi_k)
    vc = jnp.einsum('bnjhd,j->bnhd', blk[:, :, :, 1], phi_v)
    s = jnp.einsum('bthgd,bnhd->bthgn', q, kc).astype(jnp.float32) * HEAD_DIM ** -0.5
    nidx = jnp.arange(nbc)
    centre = nidx * C_BLOCK + (C_BLOCK - 1) / 2.0
    dist = jnp.abs(q_pos[:, None].astype(jnp.float32) - centre[None, :].astype(jnp.float32))
    s = s - slopes[None, None, :, :, None] * dist[None, :, None, None, :]
    mask = ((nidx[None, :] + 1) * C_BLOCK - 1 <= q_pos[:, None])[None, :, None, None, :]
    p, _, l = _masked_stats(s, mask)
    p = p / jnp.maximum(l, 1e-30)[..., None]
    out = jnp.einsum('bthgn,bnhd->bthgd', p, vc.astype(jnp.float32))
    return out, jnp.sum(p, axis=3)


def _nsa_select_idx(imp, q_pos, n_blocks):
    nbc = imp.shape[-1]
    imp = jnp.pad(imp, ((0, 0), (0, 0), (0, 0), (0, n_blocks - nbc)))
    nidx = jnp.arange(n_blocks)[None, :]
    cur = (q_pos // C_BLOCK)[:, None]
    forced = (nidx == 0) | (nidx == cur) | (nidx == cur - 1)
    score = imp + jnp.where(forced, FORCE, 0.0)[None, :, None, :]
    score = jnp.where((nidx <= cur)[None, :, None, :], score, NEG)
    vals, idx = lax.top_k(score, min(C_N_SEL, n_blocks))
    return idx, vals > NEG / 2


def _nsa_select_attend(q, q_pos, kv_g, idx, valid, slopes):
    bsz, t, hk, g, dh = q.shape
    n_keys = idx.shape[-1] * C_BLOCK
    kpos = idx[..., None] * C_BLOCK + jnp.arange(C_BLOCK)
    dist = q_pos[None, :, None, None, None] - kpos
    mask = (valid[..., None] & (dist >= 0)).reshape(bsz, t, hk, 1, n_keys)
    dist = dist.reshape(bsz, t, hk, 1, n_keys).astype(jnp.float32)
    kg = kv_g[..., 0, :].reshape(bsz, t, hk, n_keys, dh)
    vg = kv_g[..., 1, :].reshape(bsz, t, hk, n_keys, dh)
    s = jnp.einsum('bthgd,bthkd->bthgk', q, kg).astype(jnp.float32) * dh ** -0.5
    s = s - slopes[None, None, :, :, None] * dist
    p, _, l = _masked_stats(s, mask)
    return jnp.einsum('bthgk,bthkd->bthgd', p / l[..., None], vg.astype(jnp.float32))


def _nsa_select_prompt(q, kv_rows, idx, valid, slopes):
    bsz, s = q.shape[:2]
    nch = s // C_Q_CHUNK
    bi = jnp.arange(bsz)[:, None, None, None, None]
    hi = jnp.arange(C_KV_HEADS)[None, None, :, None, None]
    off = jnp.arange(C_BLOCK)

    def one_chunk(args):
        qc, ic, vc, pc = args
        rows = ic[..., None] * C_BLOCK + off
        kv_g = kv_rows[bi, rows, :, hi]
        return _nsa_select_attend(qc, pc, kv_g, ic, vc, slopes)

    def split(x):
        return jnp.moveaxis(x.reshape(bsz, nch, C_Q_CHUNK, *x.shape[2:]), 1, 0)

    o = lax.map(one_chunk, (split(q), split(idx), split(valid), jnp.arange(s).reshape(nch, C_Q_CHUNK)))
    return jnp.moveaxis(o, 0, 1).reshape(bsz, s, *o.shape[3:])


def _nsa_select_sample(q, q_pos, pool, page_table, kv_new, idx, valid, slopes):
    bd, t = q.shape[:2]
    nb_past = page_table.shape[1] * (PAGE_SIZE // C_BLOCK)
    bi = jnp.arange(bd)[:, None, None, None, None]
    hi = jnp.arange(C_KV_HEADS)[None, None, :, None, None]
    off = jnp.arange(C_BLOCK)
    tok_p = jnp.minimum(idx, nb_past - 1)[..., None] * C_BLOCK + off
    page = page_table[bi, tok_p // PAGE_SIZE]
    g_past = pool[page, tok_p % PAGE_SIZE, :, hi]
    nb_new = -(-t // C_BLOCK)
    new_pad = jnp.pad(kv_new, ((0, 0), (0, nb_new * C_BLOCK - t), (0, 0), (0, 0), (0, 0)))
    tok_n = jnp.clip(idx - nb_past, 0, nb_new - 1)[..., None] * C_BLOCK + off
    g_new = new_pad[bi, tok_n, :, hi]
    kv_g = jnp.where((idx < nb_past)[..., None, None, None], g_past, g_new.astype(g_past.dtype))
    return _nsa_select_attend(q, q_pos, kv_g, idx, valid, slopes)


def _window_attn_sample(q, q_pos, kv_ctx, slopes):
    n = kv_ctx.shape[1]
    k_pos = q_pos[-1] - (n - 1) + jnp.arange(n)
    dist = q_pos[:, None] - k_pos[None, :]
    mask = (dist >= 0) & (dist <= C_WIN)
    s = jnp.einsum('bthgd,bshd->bthgs', q, kv_ctx[:, :, 0]).astype(jnp.float32) * HEAD_DIM ** -0.5
    s = s - slopes[None, None, :, :, None] * dist.astype(jnp.float32)[None, :, None, None, :]
    p, _, l = _masked_stats(s, mask[None, :, None, None, :])
    return jnp.einsum('bthgs,bshd->bthgd', p / l[..., None], kv_ctx[:, :, 1].astype(jnp.float32))


def _even_mixer(u, w_out3d, layer, res2d, q_gain, k_gain, b_par, kv_buf, conv_buf, h0, bsz, t):
    u = u.reshape(bsz, t, -1)
    u_q, u_k, u_v, u_gate, u_x = _split_cols(u, (A_Q, A_KV, A_KV, B_WIDTH, B_WIDTH))
    q = _rms(u_q.reshape(bsz, t, A_N_GROUPS, A_HEADS, HEAD_DIM), q_gain)
    k = _rms(u_k.reshape(bsz, t, A_HEADS, HEAD_DIM), k_gain)
    v = u_v.reshape(bsz, t, A_HEADS, HEAD_DIM)
    kv_new = jnp.stack([k, v], axis=2)
    slopes = _alibi_slopes(A_N_GROUPS * A_HEADS).reshape(A_N_GROUPS, A_HEADS)
    if kv_buf is None:
        a_out = _dilated_attn_prompt(q, k, v, slopes)
        a_state = kv_new[:, -min(A_WIN_MAX, t):]
    else:
        a_out = _dilated_attn_sample(q, jnp.concatenate([kv_buf.astype(kv_new.dtype), kv_new], axis=1), slopes)
        a_state = kv_new
    b_out, conv_new, h_new = _rglru_mixer(u_gate, u_x, conv_buf, h0, *b_par)
    mixed = jnp.concatenate([a_out.reshape(bsz, t, A_OUT), b_out], axis=-1).reshape(bsz * t, -1)
    mixed = mixed.astype(jnp.bfloat16 if t > 1 else jnp.float32)
    return _matmul(mixed, w_out3d, layer, 0, D_MODEL, res=res2d), (a_state, conv_new, h_new)


def _odd_mixer(u_parts, w_out3d, layer, res2d, q_gain, k_gain, phi_k, phi_v, d_par,
               cmp_pool, sel_pool, page_table, win_buf, d_conv_buf, d_s0, bsz, t):
    u_c, u_d, u_s = [x.reshape(bsz, t, -1) for x in u_parts]
    cq, ckv = _split_cols(u_c, (C_Q, 6 * C_KV))
    dq, dk, dv, dz = _split_cols(u_d, (D_QK, D_QK, D_V, D_V))
    cg, db, da = _split_cols(u_s[..., :3 * C_HEADS + 2 * D_HEADS], (3 * C_HEADS, D_HEADS, D_HEADS))
    q = _rms(cq.reshape(bsz, t, C_KV_HEADS, C_GROUP, HEAD_DIM), q_gain)
    kv = ckv.reshape(bsz, t, 3, 2, C_KV_HEADS, HEAD_DIM)
    kv = jnp.stack([_rms(kv[:, :, :, 0], k_gain[:, None, :]), kv[:, :, :, 1]], axis=3)
    cmp_rows, sel_rows, win_rows = kv[:, :, 0], kv[:, :, 1], kv[:, :, 2]
    slopes = _alibi_slopes(C_HEADS).reshape(C_KV_HEADS, C_GROUP)
    if page_table is None:
        q_pos = jnp.arange(t)
        o_cmp, imp = _nsa_compress(q, q_pos, cmp_rows, phi_k, phi_v, slopes)
        idx, valid = _nsa_select_idx(imp, q_pos, t // C_BLOCK)
        o_sel = _nsa_select_prompt(q, sel_rows, idx, valid, slopes)
        o_un, _, l = _band_attn(q, win_rows[:, :, 0], win_rows[:, :, 1], C_WIN, slopes, 1)
        o_win = o_un / l[..., None]
        win_state = win_rows[:, -min(C_WIN, t):]
    else:
        past = page_table.shape[1] * PAGE_SIZE
        q_pos = past + jnp.arange(t)
        cmp_past = cmp_pool[page_table].reshape(bsz, past, 2, C_KV_HEADS, HEAD_DIM)
        cmp_ctx = jnp.concatenate([cmp_past, cmp_rows.astype(cmp_past.dtype)], axis=1)
        o_cmp, imp = _nsa_compress(q, q_pos, cmp_ctx, phi_k, phi_v, slopes)
        idx, valid = _nsa_select_idx(imp, q_pos, -(-(past + t) // C_BLOCK))
        o_sel = _nsa_select_sample(q, q_pos, sel_pool, page_table, sel_rows, idx, valid, slopes)
        o_win = _window_attn_sample(q, q_pos, jnp.concatenate([win_buf.astype(win_rows.dtype), win_rows], axis=1), slopes)
        win_state = win_rows
    gate = jax.nn.sigmoid(cg.reshape(bsz, t, 3, C_KV_HEADS, C_GROUP, 1).astype(jnp.float32))
    o_c = gate[:, :, 0] * o_cmp + gate[:, :, 1] * o_sel + gate[:, :, 2] * o_win
    d_out, d_conv_new, d_s = _gdn_mixer(dq, dk, dv, dz, db, da, d_conv_buf, d_s0, *d_par)
    mixed = jnp.concatenate([o_c.reshape(bsz, t, C_Q), d_out], axis=-1).reshape(bsz * t, -1)
    mixed = mixed.astype(jnp.bfloat16 if t > 1 else jnp.float32)
    return _matmul(mixed, w_out3d, layer, 0, D_MODEL, res=res2d), (cmp_rows, sel_rows, win_state, d_conv_new, d_s)


ODD_C = C_Q + 6 * C_KV
ODD_G0 = ODD_C
ODD_D0 = ODD_G0 + 3 * C_HEADS
ODD_D = 2 * D_QK + 2 * D_V
ODD_S0 = ODD_D0 + ODD_D
ODD_IN = ODD_S0 + 2 * D_HEADS


def kernel(x_prompt, x_sample, cache_a_kv, state_b_conv, state_b_h, cache_c_cmp_kv, cache_c_sel_kv, cache_c_win_kv, state_d_conv, state_d_S, page_table, norm_mix, norm_ffn, even_w_in, even_w_out, a_q_norm, a_k_norm, b_conv_w, b_conv_b, b_gate_a_w, b_gate_a_b, b_gate_x_w, b_gate_x_b, b_lambda, odd_w_in, odd_w_out, c_q_norm, c_k_norm, c_phi_k, c_phi_v, d_conv_w, d_a_log, d_dt_bias, d_out_norm, moe_group_w, moe_group_b, moe_expert_w, moe_expert_b, moe_w1, moe_w3, moe_w2):
    bp, sp, d = x_prompt.shape
    bs, ss, _ = x_sample.shape
    depth = norm_mix.shape[0]
    hp = x_prompt.reshape(bp * sp, d)
    hs = x_sample.reshape(bs * ss, d)
    outs = {k: [] for k in ("ak", "bc", "bh", "cc", "cs", "cw", "dc", "ds")}
    outs_s = {k: [] for k in outs}
    for l in range(depth):
        i = l // 2
        xp = _rmsnorm(hp, norm_mix[l], jnp.bfloat16)
        xs = _rmsnorm(hs, norm_mix[l], jnp.float32)
        if l % 2 == 0:
            b_par = (b_conv_w[i], b_conv_b[i], b_gate_a_w[i], b_gate_a_b[i], b_gate_x_w[i], b_gate_x_b[i], b_lambda[i])
            n_in = even_w_in.shape[-1]
            up = _matmul(xp, even_w_in, i, 0, n_in)
            us = _matmul(xs, even_w_in, i, 0, n_in)
            hp, st_p = _even_mixer(up, even_w_out, i, hp, a_q_norm[i], a_k_norm[i], b_par, None,
                                   jnp.zeros((bp, CONV_W - 1, B_WIDTH), jnp.float32),
                                   jnp.zeros((bp, B_WIDTH), jnp.float32), bp, sp)
            with jax.default_matmul_precision("float32"):
                hs, st_s = _even_mixer(us, even_w_out, i, hs, a_q_norm[i], a_k_norm[i], b_par,
                                       cache_a_kv[i], state_b_conv[i], state_b_h[i], bs, ss)
            for dst, st in ((outs, st_p), (outs_s, st_s)):
                dst["ak"].append(st[0]); dst["bc"].append(st[1]); dst["bh"].append(st[2])
        else:
            d_par = (d_conv_w[i], d_a_log[i], d_dt_bias[i], d_out_norm[i])
            w_d = odd_w_in[:, :, ODD_D0:ODD_S0]
            w_s = jnp.concatenate([odd_w_in[:, :, ODD_G0:ODD_D0], odd_w_in[:, :, ODD_S0:],
                                   jnp.zeros((odd_w_in.shape[0], d, LANES - 3 * C_HEADS - 2 * D_HEADS), jnp.float32)], axis=-1)
            ups = (_matmul(xp, odd_w_in, i, 0, ODD_C), _matmul(xp, w_d, i, 0, ODD_D), _matmul(xp, w_s, i, 0, LANES))
            uss = (_matmul(xs, odd_w_in, i, 0, ODD_C), _matmul(xs, w_d, i, 0, ODD_D), _matmul(xs, w_s, i, 0, LANES))
            hp, st_p = _odd_mixer(ups, odd_w_out, i, hp, c_q_norm[i], c_k_norm[i], c_phi_k[i], c_phi_v[i], d_par,
                                  None, None, None, None,
                                  jnp.zeros((bp, CONV_W - 1, D_CONV), jnp.float32),
                                  jnp.zeros((bp, D_HEADS, D_DK, D_DV), jnp.float32), bp, sp)
            with jax.default_matmul_precision("float32"):
                hs, st_s = _odd_mixer(uss, odd_w_out, i, hs, c_q_norm[i], c_k_norm[i], c_phi_k[i], c_phi_v[i], d_par,
                                      cache_c_cmp_kv[i], cache_c_sel_kv[i], page_table, cache_c_win_kv[i],
                                      state_d_conv[i], state_d_S[i], bs, ss)
            for dst, st in ((outs, st_p), (outs_s, st_s)):
                dst["cc"].append(st[0]); dst["cs"].append(st[1]); dst["cw"].append(st[2])
                dst["dc"].append(st[3]); dst["ds"].append(st[4])
        w_router = jnp.concatenate([moe_group_w[l], moe_expert_w[l],
                                    jnp.zeros((d, ROUTER_PAD - N_GROUPS - N_EXPERTS), jnp.float32)], axis=-1)
        b_router = jnp.concatenate([moe_group_b[l], moe_expert_b[l],
                                    jnp.zeros((ROUTER_PAD - N_GROUPS - N_EXPERTS,), jnp.float32)])[None, :]
        hp = _hier_moe(hp, norm_ffn[l], w_router, b_router, moe_w1, moe_w3, moe_w2, l, 256, jnp.bfloat16)
        hs = _hier_moe(hs, norm_ffn[l], w_router, b_router, moe_w1, moe_w3, moe_w2, l, 8, jnp.float32)
    res = [hp.reshape(bp, sp, d), hs.reshape(bs, ss, d)]
    for key in ("ak", "bc", "bh", "cc", "cs", "cw", "dc", "ds"):
        res.append(jnp.stack(outs[key]))
        res.append(jnp.stack(outs_s[key]))
    return tuple(res)
```

```python
import functools
import math

import jax
import jax.numpy as jnp
import numpy as np
from jax import lax
from jax.experimental import pallas as pl
from jax.experimental.pallas import tpu as pltpu

D_MODEL = 4096
HEAD_DIM = 128
CONV_W = 4
BAND_BLOCK = 128
A_PATTERNS = ((128, 1), (512, 4), (2048, 16))
A_N_GROUPS = len(A_PATTERNS)
A_HEADS = D_MODEL // 512
A_WIN_MAX = max(w for w, _ in A_PATTERNS)
A_Q = A_N_GROUPS * A_HEADS * HEAD_DIM
A_KV = A_HEADS * HEAD_DIM
A_OUT = A_HEADS * HEAD_DIM
B_WIDTH = 3 * D_MODEL // 4
B_BLOCKS = B_WIDTH // HEAD_DIM
B_BLOCK_DIM = B_WIDTH // B_BLOCKS
RG_C = 8.0
C_HEADS = D_MODEL // 256
C_KV_HEADS = C_HEADS // 4
C_GROUP = C_HEADS // C_KV_HEADS
C_BLOCK = 64
C_N_SEL = 16
C_WIN = 512
C_Q_CHUNK = 32
C_Q = C_HEADS * HEAD_DIM
C_KV = C_KV_HEADS * HEAD_DIM
D_HEADS = D_MODEL // 256
D_DK = HEAD_DIM
D_DV = HEAD_DIM
D_QK = D_HEADS * D_DK
D_V = D_HEADS * D_DV
D_CONV = 2 * D_QK + D_V
D_CHUNK = 64
N_GROUPS = 8
EXPERTS_PER_GROUP = 8
N_EXPERTS = N_GROUPS * EXPERTS_PER_GROUP
TOP_K = 2
D_EXPERT = D_MODEL // 8
PAGE_SIZE = 128
EPS = 1e-6
NEG = -1e30
FORCE = 1e4

LANES = 128
VMEM_LIMIT = 56 * 1024 * 1024
ROUTER_PAD = LANES

HI = lax.Precision.HIGHEST


def _params(sem):
    return pltpu.CompilerParams(dimension_semantics=sem, vmem_limit_bytes=VMEM_LIMIT)


def _rmsnorm_kernel(x_ref, g_ref, o_ref):
    x = x_ref[...]
    y = x * lax.rsqrt(jnp.mean(x * x, axis=-1, keepdims=True) + EPS)
    o_ref[...] = (y * g_ref[...]).astype(o_ref.dtype)


def _rmsnorm(x2d, gain, out_dtype):
    m, d = x2d.shape
    tm = min(m, 512)
    return pl.pallas_call(
        _rmsnorm_kernel,
        out_shape=jax.ShapeDtypeStruct((m, d), out_dtype),
        grid=(m // tm,),
        in_specs=[pl.BlockSpec((tm, d), lambda i: (i, 0)), pl.BlockSpec((1, d), lambda i: (0, 0))],
        out_specs=pl.BlockSpec((tm, d), lambda i: (i, 0)),
        compiler_params=_params(("parallel",)),
        name="rmsnorm",
    )(x2d, gain.reshape(1, d))


def _matmul_kernel(*refs, exact, has_res, k_bounds):
    n_x = len(k_bounds) - 1
    x_refs, w_ref = refs[:n_x], refs[n_x]
    if has_res:
        r_ref, o_ref, acc_ref = refs[n_x + 1:]
    else:
        o_ref, acc_ref = refs[n_x + 1:]
    k = pl.program_id(2)

    @pl.when(k == 0)
    def _():
        acc_ref[...] = jnp.zeros_like(acc_ref)

    for p, x_ref in enumerate(x_refs):
        @pl.when((k >= k_bounds[p]) & (k < k_bounds[p + 1]))
        def _(x_ref=x_ref):
            if exact:
                acc_ref[...] += jnp.dot(x_ref[...], w_ref[...], preferred_element_type=jnp.float32, precision=HI)
            else:
                acc_ref[...] += jnp.dot(x_ref[...], w_ref[...].astype(jnp.bfloat16), preferred_element_type=jnp.float32)

    @pl.when(k == pl.num_programs(2) - 1)
    def _():
        out = acc_ref[...]
        if has_res:
            out = out + r_ref[...]
        o_ref[...] = out


def _matmul(xs, w3d, layer, col0, n, res=None, tn=1024, tk=512):
    if not isinstance(xs, (list, tuple)):
        xs = [xs]
    m = xs[0].shape[0]
    exact = xs[0].dtype == jnp.float32
    tm = min(m, 1024)
    tn = min(tn, n)
    assert m % tm == 0 and n % tn == 0 and col0 % tn == 0 and all(x.shape[1] % tk == 0 for x in xs)
    jb = col0 // tn
    k_bounds = [0]
    for x in xs:
        k_bounds.append(k_bounds[-1] + x.shape[1] // tk)
    in_specs = []
    for p in range(len(xs)):
        lo, hi = k_bounds[p], k_bounds[p + 1]
        in_specs.append(pl.BlockSpec((tm, tk), lambda i, j, k, lo=lo, hi=hi: (i, jnp.clip(k, lo, hi - 1) - lo)))
    in_specs.append(pl.BlockSpec((None, tk, tn), lambda i, j, k: (layer, k, j + jb)))
    args = list(xs) + [w3d]
    if res is not None:
        in_specs.append(pl.BlockSpec((tm, tn), lambda i, j, k: (i, j)))
        args.append(res)
    return pl.pallas_call(
        functools.partial(_matmul_kernel, exact=exact, has_res=res is not None, k_bounds=tuple(k_bounds)),
        out_shape=jax.ShapeDtypeStruct((m, n), jnp.float32),
        grid=(m // tm, n // tn, k_bounds[-1]),
        in_specs=in_specs,
        out_specs=pl.BlockSpec((tm, tn), lambda i, j, k: (i, j)),
        scratch_shapes=[pltpu.VMEM((tm, tn), jnp.float32)],
        compiler_params=_params(("parallel", "parallel", "arbitrary")),
        name="proj",
    )(*args)


def _router_kernel(h_ref, g_ref, w_ref, b_ref, xn_ref, logit_ref):
    x = h_ref[...]
    y = x * lax.rsqrt(jnp.mean(x * x, axis=-1, keepdims=True) + EPS) * g_ref[...]
    xn_ref[...] = y.astype(xn_ref.dtype)
    logit_ref[...] = jnp.dot(y, w_ref[...], preferred_element_type=jnp.float32, precision=HI) + b_ref[...]


def _router(h2d, gain, w_router, b_router, xn_dtype):
    m, d = h2d.shape
    tm = min(m, 256)
    return pl.pallas_call(
        _router_kernel,
        out_shape=(jax.ShapeDtypeStruct((m, d), xn_dtype), jax.ShapeDtypeStruct((m, ROUTER_PAD), jnp.float32)),
        grid=(m // tm,),
        in_specs=[pl.BlockSpec((tm, d), lambda i: (i, 0)), pl.BlockSpec((1, d), lambda i: (0, 0)),
                  pl.BlockSpec((d, ROUTER_PAD), lambda i: (0, 0)), pl.BlockSpec((1, ROUTER_PAD), lambda i: (0, 0))],
        out_specs=(pl.BlockSpec((tm, d), lambda i: (i, 0)), pl.BlockSpec((tm, ROUTER_PAD), lambda i: (i, 0))),
        compiler_params=_params(("parallel",)),
        name="ffn_norm_router",
    )(h2d, gain.reshape(1, d), w_router, b_router)


MOE_K_CHUNK = 512


def _moe_up_kernel(be_ref, x_ref, w1_ref, w3_ref, o_ref, *, exact):
    tm = x_ref.shape[0]
    a = jnp.zeros((tm, D_EXPERT), jnp.float32)
    b = jnp.zeros((tm, D_EXPERT), jnp.float32)
    for c in range(D_MODEL // MOE_K_CHUNK):
        sl = slice(c * MOE_K_CHUNK, (c + 1) * MOE_K_CHUNK)
        if exact:
            a += jnp.dot(x_ref[:, sl], w1_ref[sl, :], preferred_element_type=jnp.float32, precision=HI)
            b += jnp.dot(x_ref[:, sl], w3_ref[sl, :], preferred_element_type=jnp.float32, precision=HI)
        else:
            a += jnp.dot(x_ref[:, sl], w1_ref[sl, :].astype(jnp.bfloat16), preferred_element_type=jnp.float32)
            b += jnp.dot(x_ref[:, sl], w3_ref[sl, :].astype(jnp.bfloat16), preferred_element_type=jnp.float32)
    o_ref[...] = (a * jax.nn.sigmoid(a) * b).astype(o_ref.dtype)


def _moe_down_kernel(be_ref, h_ref, w2_ref, g_ref, o_ref, *, exact):
    if exact:
        out = jnp.dot(h_ref[...], w2_ref[...], preferred_element_type=jnp.float32, precision=HI)
    else:
        out = jnp.dot(h_ref[...], w2_ref[...].astype(jnp.bfloat16), preferred_element_type=jnp.float32)
    o_ref[...] = out * g_ref[...]


def _moe_experts(x_rows, row_gate, blk_exp, w1, w3, w2, layer, tm):
    rows, d = x_rows.shape
    n_blk = rows // tm
    exact = x_rows.dtype == jnp.float32
    up = pl.pallas_call(
        functools.partial(_moe_up_kernel, exact=exact),
        out_shape=jax.ShapeDtypeStruct((rows, D_EXPERT), x_rows.dtype),
        grid_spec=pltpu.PrefetchScalarGridSpec(
            num_scalar_prefetch=1, grid=(n_blk,),
            in_specs=[pl.BlockSpec((tm, d), lambda i, be: (i, 0)),
                      pl.BlockSpec((None, None, d, D_EXPERT), lambda i, be: (layer, be[i], 0, 0)),
                      pl.BlockSpec((None, None, d, D_EXPERT), lambda i, be: (layer, be[i], 0, 0))],
            out_specs=pl.BlockSpec((tm, D_EXPERT), lambda i, be: (i, 0))),
        compiler_params=_params(("arbitrary",)),
        name="moe_up",
    )(blk_exp, x_rows, w1, w3)
    return pl.pallas_call(
        functools.partial(_moe_down_kernel, exact=exact),
        out_shape=jax.ShapeDtypeStruct((rows, d), jnp.float32),
        grid_spec=pltpu.PrefetchScalarGridSpec(
            num_scalar_prefetch=1, grid=(n_blk,),
            in_specs=[pl.BlockSpec((tm, D_EXPERT), lambda i, be: (i, 0)),
                      pl.BlockSpec((None, None, D_EXPERT, d), lambda i, be: (layer, be[i], 0, 0)),
                      pl.BlockSpec((tm, 1), lambda i, be: (i, 0))],
            out_specs=pl.BlockSpec((tm, d), lambda i, be: (i, 0))),
        compiler_params=_params(("arbitrary",)),
        name="moe_down",
    )(blk_exp, up, w2, row_gate)


def _hier_moe(h2d, gain, w_router, b_router, w1, w3, w2, layer, tm, xn_dtype):
    n_tok, d = h2d.shape
    xn, logits = _router(h2d, gain, w_router, b_router, xn_dtype)
    g_logit = logits[:, :N_GROUPS]
    g_prob = jax.nn.softmax(g_logit, axis=-1)
    grp = jnp.argmax(g_logit, axis=-1)
    p_grp = jnp.take_along_axis(g_prob, grp[:, None], axis=1)[:, 0]
    e_logit = logits[:, N_GROUPS:N_GROUPS + N_EXPERTS].reshape(-1, N_GROUPS, EXPERTS_PER_GROUP)
    e_logit = jnp.take_along_axis(e_logit, grp[:, None, None], axis=1)[:, 0]
    e_val, e_idx = lax.top_k(e_logit, TOP_K)
    gates = p_grp[:, None] * jax.nn.softmax(e_val, axis=-1)
    experts = grp[:, None] * EXPERTS_PER_GROUP + e_idx
    n_asg = n_tok * TOP_K
    flat_e = experts.reshape(-1).astype(jnp.int32)
    order = jnp.argsort(flat_e)
    sorted_e = flat_e[order]
    counts = jnp.bincount(flat_e, length=N_EXPERTS)
    padded = (counts + tm - 1) // tm * tm
    pad_end = jnp.cumsum(padded)
    pad_start = pad_end - padded
    start = jnp.cumsum(counts) - counts
    dest = (pad_start[sorted_e] + jnp.arange(n_asg) - start[sorted_e]).astype(jnp.int32)
    n_rows = -(-(n_asg + N_EXPERTS * (tm - 1)) // tm) * tm
    n_blk = n_rows // tm
    tok = (order // TOP_K).astype(jnp.int32)
    row_tok = jnp.zeros((n_rows,), jnp.int32).at[dest].set(tok)
    row_gate = jnp.zeros((n_rows,), jnp.float32).at[dest].set(gates.reshape(-1)[order])
    blk_exp = jnp.minimum(jnp.searchsorted(pad_end, jnp.arange(n_blk) * tm, side='right'),
                          N_EXPERTS - 1).astype(jnp.int32)
    x_rows = xn[row_tok]
    out = _moe_experts(x_rows, row_gate[:, None], blk_exp, w1, w3, w2, layer, tm)
    asg_row = jnp.zeros((n_asg,), jnp.int32).at[order].set(dest)
    y = out[asg_row].reshape(n_tok, TOP_K, d).sum(axis=1)
    return h2d + y


NSA_PREP_ROWS = 256
NSA_SUM_ROWS = 512
NSA_TQ = 128
NSA_GATE_LANES = LANES
TQ_SHIFT = NSA_TQ.bit_length() - 1
C_BLOCK_SHIFT = C_BLOCK.bit_length() - 1


def _chunk_rms(x, gain):
    return x * lax.rsqrt(jnp.mean(x * x, axis=-1, keepdims=True) + EPS) * gain


def _nsa_prep_kernel(uc_ref, us_ref, qg_ref, kg_ref, perm_ref, q_ref, cmp_ref, sel_ref, win_ref, gate_ref):
    for c in range(C_Q // HEAD_DIM):
        sl = slice(c * HEAD_DIM, (c + 1) * HEAD_DIM)
        q_ref[:, sl] = _chunk_rms(uc_ref[:, sl], qg_ref[...])
    for br, o_ref in enumerate((cmp_ref, sel_ref, win_ref)):
        base = C_Q + br * 2 * C_KV
        for c in range(C_KV_HEADS):
            sl = slice(c * HEAD_DIM, (c + 1) * HEAD_DIM)
            o_ref[:, sl] = _chunk_rms(uc_ref[:, base + c * HEAD_DIM: base + (c + 1) * HEAD_DIM], kg_ref[br:br + 1, :])
        o_ref[:, C_KV:] = uc_ref[:, base + C_KV: base + 2 * C_KV]
    gate_ref[...] = jnp.dot(jax.nn.sigmoid(us_ref[...]), perm_ref[...], preferred_element_type=jnp.float32, precision=HI)


def _nsa_prep(u_c, u_s, q_gain, k_gain):
    p = u_c.shape[0]
    tm = NSA_PREP_ROWS
    perm = np.zeros((LANES, C_KV_HEADS * NSA_GATE_LANES), np.float32)
    for br in range(3):
        for h in range(C_KV_HEADS):
            for g in range(C_GROUP):
                perm[br * C_HEADS + h * C_GROUP + g, h * NSA_GATE_LANES + br * C_GROUP + g] = 1.0
    row = lambda i: (i, 0)
    fixed = lambda i: (0, 0)
    return pl.pallas_call(
        _nsa_prep_kernel,
        out_shape=(jax.ShapeDtypeStruct((p, C_Q), jnp.float32),) + (jax.ShapeDtypeStruct((p, 2 * C_KV), jnp.float32),) * 3
        + (jax.ShapeDtypeStruct((p, C_KV_HEADS * NSA_GATE_LANES), jnp.float32),),
        grid=(p // tm,),
        in_specs=[pl.BlockSpec((tm, ODD_C), row), pl.BlockSpec((tm, LANES), row), pl.BlockSpec((1, HEAD_DIM), fixed),
                  pl.BlockSpec((3, HEAD_DIM), fixed), pl.BlockSpec(perm.shape, fixed)],
        out_specs=(pl.BlockSpec((tm, C_Q), row),) + (pl.BlockSpec((tm, 2 * C_KV), row),) * 3
        + (pl.BlockSpec((tm, C_KV_HEADS * NSA_GATE_LANES), row),),
        compiler_params=_params(("parallel",)),
        name="nsa_prep",
    )(u_c, u_s, q_gain.reshape(1, HEAD_DIM), k_gain, jnp.asarray(perm))


def _nsa_sum_kernel(cmp_ref, phik_ref, phiv_ref, kc_ref, vc_ref):
    kc_ref[...] = jnp.dot(phik_ref[...], cmp_ref[:, :C_KV], preferred_element_type=jnp.float32, precision=HI)
    vc_ref[...] = jnp.dot(phiv_ref[...], cmp_ref[:, C_KV:], preferred_element_type=jnp.float32, precision=HI)


def _nsa_summaries(cmp_rows, phi_k, phi_v):
    p = cmp_rows.shape[0]
    tm = NSA_SUM_ROWS
    nb = tm // C_BLOCK
    eye = jnp.eye(nb, dtype=jnp.float32)
    big_k = jnp.kron(eye, phi_k[None, :])
    big_v = jnp.kron(eye, phi_v[None, :])
    return pl.pallas_call(
        _nsa_sum_kernel,
        out_shape=(jax.ShapeDtypeStruct((p // C_BLOCK, C_KV), jnp.float32),) * 2,
        grid=(p // tm,),
        in_specs=[pl.BlockSpec((tm, 2 * C_KV), lambda i: (i, 0)), pl.BlockSpec((nb, tm), lambda i: (0, 0)),
                  pl.BlockSpec((nb, tm), lambda i: (0, 0))],
        out_specs=(pl.BlockSpec((nb, C_KV), lambda i: (i, 0)),) * 2,
        compiler_params=_params(("parallel",)),
        name="nsa_summaries",
    )(cmp_rows, big_k, big_v)


def _nsa_prompt_kernel(q_ref, kc_ref, vc_ref, sk_ref, sv_ref, wk_ref, wv_ref, gate_ref, o_ref,
                       m_scr, l_scr, acc_scr, *, n_blocks):
    h = pl.program_id(1)
    qb = pl.program_id(2)
    tq = NSA_TQ
    rows = C_GROUP * tq
    scale = HEAD_DIM ** -0.5
    q = jnp.concatenate([q_ref[:, g * HEAD_DIM:(g + 1) * HEAD_DIM] for g in range(C_GROUP)], axis=0)
    row = lax.broadcasted_iota(jnp.int32, (rows, 1), 0)
    t_row = qb * tq + (row & (tq - 1))
    slope = jnp.exp2(-0.5 * (h * C_GROUP + (row >> TQ_SHIFT) + 1).astype(jnp.float32))

    nidx = lax.broadcasted_iota(jnp.int32, (1, n_blocks), 1)
    s = lax.dot_general(q, kc_ref[...], (((1,), (1,)), ((), ())), preferred_element_type=jnp.float32, precision=HI) * scale
    centre = nidx.astype(jnp.float32) * C_BLOCK + (C_BLOCK - 1) / 2.0
    s = s - slope * jnp.abs(t_row.astype(jnp.float32) - centre)
    cmask = (nidx + 1) * C_BLOCK - 1 <= t_row
    s = jnp.where(cmask, s, NEG)
    m = jnp.max(s, axis=-1, keepdims=True)
    p = jnp.where(cmask, jnp.exp(s - m), 0.0)
    l = jnp.sum(p, axis=-1, keepdims=True)
    p = p / jnp.maximum(l, 1e-30)
    o_cmp = jnp.dot(p, vc_ref[...], preferred_element_type=jnp.float32, precision=HI)
    imp = p[0:tq]
    for g in range(1, C_GROUP):
        imp = imp + p[g * tq:(g + 1) * tq]

    t_q = t_row[0:tq]
    cur = t_q >> C_BLOCK_SHIFT
    forced = (nidx == 0) | (nidx == cur) | (nidx == cur - 1)
    causal = nidx <= cur
    score = jnp.where(causal, imp + jnp.where(forced, FORCE, 0.0), NEG)
    rank = jnp.zeros((tq, n_blocks), jnp.int32)
    for mcol in range(n_blocks):
        cm = score[:, mcol:mcol + 1]
        ahead = (cm > score) | ((cm == score) & (mcol < nidx))
        rank = rank + ahead.astype(jnp.int32)
    selm = ((rank < min(C_N_SEL, n_blocks)) & causal).astype(jnp.bfloat16)

    q16 = q.astype(jnp.bfloat16)
    kcol = lax.broadcasted_iota(jnp.int32, (1, tq), 1)

    def attend(k_ref, v_ref, c, mask):
        start = pl.multiple_of(c * tq, tq)
        k = k_ref[pl.ds(start, tq), :].astype(jnp.bfloat16)
        v = v_ref[pl.ds(start, tq), :].astype(jnp.bfloat16)
        dist = t_row - (c * tq + kcol)
        sc = lax.dot_general(q16, k, (((1,), (1,)), ((), ())), preferred_element_type=jnp.float32) * scale
        sc = sc - slope * dist.astype(jnp.float32)
        ok = mask(dist)
        sc = jnp.where(ok, sc, NEG)
        m_old = m_scr[...]
        m_new = jnp.maximum(m_old, jnp.max(sc, axis=-1, keepdims=True))
        pc = jnp.where(ok, jnp.exp(sc - m_new), 0.0)
        alpha = jnp.exp(m_old - m_new)
        l_scr[...] = alpha * l_scr[...] + jnp.sum(pc, axis=-1, keepdims=True)
        acc_scr[...] = alpha * acc_scr[...] + jnp.dot(pc.astype(jnp.bfloat16), v, preferred_element_type=jnp.float32)
        m_scr[...] = m_new

    def reset():
        m_scr[...] = jnp.full_like(m_scr, NEG)
        l_scr[...] = jnp.zeros_like(l_scr)
        acc_scr[...] = jnp.zeros_like(acc_scr)

    reset()
    brow = lax.broadcasted_iota(jnp.int32, (n_blocks, tq), 0)
    bcol = lax.broadcasted_iota(jnp.int32, (n_blocks, tq), 1)

    def sel_body(c, carry):
        expand = (brow == c * (tq // C_BLOCK) + (bcol >> C_BLOCK_SHIFT)).astype(jnp.bfloat16)
        sel_keys = jnp.dot(selm, expand, preferred_element_type=jnp.float32)
        sel_keys = jnp.concatenate([sel_keys] * C_GROUP, axis=0) > 0.5
        attend(sk_ref, sv_ref, c, lambda dist: sel_keys & (dist >= 0))
        return carry

    lax.fori_loop(0, qb + 1, sel_body, 0)
    o_sel = acc_scr[...] / l_scr[...]

    reset()

    def win_body(c, carry):
        attend(wk_ref, wv_ref, c, lambda dist: (dist >= 0) & (dist <= C_WIN))
        return carry

    lax.fori_loop(jnp.maximum(qb - C_WIN // tq, 0), qb + 1, win_body, 0)
    o_win = acc_scr[...] / l_scr[...]

    gt = gate_ref[...]
    outs = []
    for g in range(C_GROUP):
        rs = slice(g * tq, (g + 1) * tq)
        outs.append(gt[:, g:g + 1] * o_cmp[rs] + gt[:, C_GROUP + g:C_GROUP + g + 1] * o_sel[rs]
                    + gt[:, 2 * C_GROUP + g:2 * C_GROUP + g + 1] * o_win[rs])
    o_ref[...] = jnp.concatenate(outs, axis=1).astype(o_ref.dtype)


def _nsa_prompt(qn, kc, vc, sel_rows, win_rows, gates, bsz, t, out_dtype):
    tq = NSA_TQ
    nq = t // tq
    n_blocks = t // C_BLOCK
    kv_k = pl.BlockSpec((t, HEAD_DIM), lambda b, h, i: (b, h))
    kv_v = pl.BlockSpec((t, HEAD_DIM), lambda b, h, i: (b, C_KV_HEADS + h))
    return pl.pallas_call(
        functools.partial(_nsa_prompt_kernel, n_blocks=n_blocks),
        out_shape=jax.ShapeDtypeStruct((bsz * t, C_Q), out_dtype),
        grid=(bsz, C_KV_HEADS, nq),
        in_specs=[pl.BlockSpec((tq, C_GROUP * HEAD_DIM), lambda b, h, i: (b * nq + i, h)),
                  pl.BlockSpec((n_blocks, HEAD_DIM), lambda b, h, i: (b, h)),
                  pl.BlockSpec((n_blocks, HEAD_DIM), lambda b, h, i: (b, h)),
                  kv_k, kv_v, kv_k, kv_v,
                  pl.BlockSpec((tq, NSA_GATE_LANES), lambda b, h, i: (b * nq + i, h))],
        out_specs=pl.BlockSpec((tq, C_GROUP * HEAD_DIM), lambda b, h, i: (b * nq + i, h)),
        scratch_shapes=[pltpu.VMEM((C_GROUP * tq, 1), jnp.float32), pltpu.VMEM((C_GROUP * tq, 1), jnp.float32),
                        pltpu.VMEM((C_GROUP * tq, HEAD_DIM), jnp.float32)],
        compiler_params=_params(("parallel", "parallel", "arbitrary")),
        name="nsa_prompt",
    )(qn, kc, vc, sel_rows, sel_rows, win_rows, win_rows, gates)


def _rms(x, g):
    xf = x.astype(jnp.float32)
    y = xf * lax.rsqrt(jnp.mean(xf * xf, axis=-1, keepdims=True) + EPS)
    return (y * g.astype(jnp.float32)).astype(x.dtype)


def _l2norm(x):
    return x * lax.rsqrt(jnp.sum(x * x, axis=-1, keepdims=True) + EPS)


def _alibi_slopes(n):
    return 2.0 ** (-8.0 * (jnp.arange(n, dtype=jnp.float32) + 1.0) / n)


def _split_cols(u, sizes):
    cuts = [int(c) for c in np.cumsum(sizes)[:-1]]
    return jnp.split(u, cuts, axis=-1)


def _masked_stats(s, mask):
    s = jnp.where(mask, s, NEG)
    m = jnp.max(s, axis=-1, keepdims=True)
    p = jnp.where(mask, jnp.exp(s - m), 0.0)
    return p, m[..., 0], jnp.sum(p, axis=-1)


def _merge_by_denominator(stats):
    big_m = stats[0][1]
    for _, m, _ in stats[1:]:
        big_m = jnp.maximum(big_m, m)
    num, den = 0.0, 0.0
    for o, m, l in stats:
        w = jnp.exp(m - big_m)
        num = num + w[..., None] * o
        den = den + w * l
    return num / den[..., None]


def _causal_conv(x, buf, w, b=None):
    t = x.shape[1]
    xx = jnp.concatenate([buf.astype(x.dtype), x], axis=1)
    y = xx[:, CONV_W - 1:] * w[CONV_W - 1]
    for j in range(CONV_W - 1):
        y = y + xx[:, j:j + t] * w[j]
    if b is not None:
        y = y + b
    return y, xx[:, -(CONV_W - 1):]


def _band_attn(q, k, v, n_back, slopes, dist_scale):
    bsz, t, h, g, dh = q.shape
    nbb = -(-n_back // BAND_BLOCK)
    nq = -(-t // BAND_BLOCK)
    pad = nq * BAND_BLOCK - t
    front = nbb * BAND_BLOCK
    qp = jnp.pad(q, ((0, 0), (0, pad), (0, 0), (0, 0), (0, 0)))
    kp = jnp.pad(k, ((0, 0), (front, pad), (0, 0), (0, 0)))
    vp = jnp.pad(v, ((0, 0), (front, pad), (0, 0), (0, 0)))
    kw = (nbb + 1) * BAND_BLOCK
    idx = jnp.arange(nq)[:, None] * BAND_BLOCK + jnp.arange(kw)[None, :]
    k_win, v_win = kp[:, idx], vp[:, idx]
    qr = qp.reshape(bsz, nq, BAND_BLOCK, h, g, dh)
    s = jnp.einsum('bnqhgd,bnkhd->bnhgqk', qr, k_win).astype(jnp.float32) * dh ** -0.5
    t_pos = jnp.arange(nq)[:, None] * BAND_BLOCK + jnp.arange(BAND_BLOCK)[None, :]
    s_pos = idx - front
    dist = t_pos[:, :, None] - s_pos[:, None, :]
    mask = (dist >= 0) & (dist <= n_back) & (s_pos[:, None, :] >= 0)
    s = s - slopes[None, None, :, :, None, None] * (dist * dist_scale).astype(jnp.float32)[None, :, None, None]
    p, m, l = _masked_stats(s, mask[None, :, None, None])
    o = jnp.einsum('bnhgqk,bnkhd->bnqhgd', p, v_win.astype(jnp.float32))
    o = o.reshape(bsz, nq * BAND_BLOCK, h, g, dh)[:, :t]
    m = jnp.moveaxis(m, 4, 2).reshape(bsz, nq * BAND_BLOCK, h, g)[:, :t]
    l = jnp.moveaxis(l, 4, 2).reshape(bsz, nq * BAND_BLOCK, h, g)[:, :t]
    return o, m, l


def _dilated_attn_prompt(q, k, v, slopes):
    bsz, s = q.shape[:2]
    stats = []
    for gi, (w, d) in enumerate(A_PATTERNS):
        n_res = s // d

        def to_res(x):
            return jnp.swapaxes(x.reshape(bsz, n_res, d, *x.shape[2:]), 1, 2).reshape(bsz * d, n_res, *x.shape[2:])

        def from_res(x):
            return jnp.swapaxes(x.reshape(bsz, d, n_res, *x.shape[2:]), 1, 2).reshape(bsz, s, *x.shape[2:])

        o, m, l = _band_attn(to_res(q[:, :, gi])[:, :, :, None], to_res(k), to_res(v), w // d, slopes[gi][:, None], d)
        stats.append((from_res(o[:, :, :, 0]), from_res(m[:, :, :, 0]), from_res(l[:, :, :, 0])))
    return _merge_by_denominator(stats)


def _dilated_attn_sample(q, kv_ctx, slopes):
    t = q.shape[1]
    wb = kv_ctx.shape[1] - t
    stats = []
    for gi, (w, d) in enumerate(A_PATTERNS):
        j = jnp.arange(w // d + 1)
        idx = wb + jnp.arange(t)[:, None] - d * j[None, :]
        kvg = kv_ctx[:, jnp.maximum(idx, 0)]
        s = jnp.einsum('bthd,btjhd->bthj', q[:, :, gi], kvg[:, :, :, 0]).astype(jnp.float32) * HEAD_DIM ** -0.5
        s = s - slopes[gi][None, None, :, None] * (d * j).astype(jnp.float32)[None, None, None, :]
        p, m, l = _masked_stats(s, (idx >= 0)[None, :, None, :])
        o = jnp.einsum('bthj,btjhd->bthd', p, kvg[:, :, :, 1].astype(jnp.float32))
        stats.append((o, m, l))
    return _merge_by_denominator(stats)


def _rglru_mixer(gate_in, x_in, conv_buf, h0, conv_w, conv_b, wa, ba, wx, bx, lam):
    xc, conv_new = _causal_conv(x_in, conv_buf, conv_w, conv_b)
    bsz, t, _ = xc.shape
    xb = xc.reshape(bsz, t, B_BLOCKS, B_BLOCK_DIM)
    r = jax.nn.sigmoid(jnp.einsum('btni,nij->btnj', xb, wa).reshape(bsz, t, B_WIDTH) + ba).astype(jnp.float32)
    ig = jax.nn.sigmoid(jnp.einsum('btni,nij->btnj', xb, wx).reshape(bsz, t, B_WIDTH) + bx).astype(jnp.float32)
    log_a = -RG_C * r * jax.nn.softplus(-lam.astype(jnp.float32))
    a = jnp.exp(log_a)
    bterm = jnp.sqrt(-jnp.expm1(2.0 * log_a)) * (ig * xc.astype(jnp.float32))
    bterm = bterm.at[:, 0].add(a[:, 0] * h0.astype(jnp.float32))

    def comb(e1, e2):
        return e1[0] * e2[0], e2[0] * e1[1] + e2[1]

    _, h = lax.associative_scan(comb, (a, bterm), axis=1)
    y = h.astype(x_in.dtype) * jax.nn.gelu(gate_in)
    return y, conv_new, h[:, -1].astype(h0.dtype)


def _gated_delta_chunked(q, k, v, g, beta, s0):
    bsz, t, h, _ = q.shape
    dv = v.shape[-1]
    n, c = t // D_CHUNK, D_CHUNK

    def chunks(x):
        return jnp.moveaxis(x.reshape(bsz, n, c, h, *x.shape[3:]), (1, 3), (0, 2))

    qc, kc, vc, gc, bc = chunks(q), chunks(k), chunks(v), chunks(g), chunks(beta)
    gam = jnp.cumsum(gc, axis=-1)
    lower = jnp.tril(jnp.ones((c, c), bool))
    strict = jnp.tril(jnp.ones((c, c), bool), -1)
    diff = gam[..., :, None] - gam[..., None, :]
    decay = jnp.where(lower, jnp.exp(jnp.where(lower, diff, 0.0)), 0.0)
    kk = jnp.einsum('nbhid,nbhjd->nbhij', kc, kc)
    a_mat = jnp.where(strict, bc[..., :, None] * kk * decay, 0.0) + jnp.eye(c, dtype=jnp.float32)
    rhs = jnp.concatenate([vc * bc[..., None], kc * (bc * jnp.exp(gam))[..., None]], axis=-1)
    sol = lax.linalg.triangular_solve(a_mat, rhs, left_side=True, lower=True)
    u, w = sol[..., :dv], sol[..., dv:]
    qk = jnp.where(lower, jnp.einsum('nbhid,nbhjd->nbhij', qc, kc) * decay, 0.0)
    q_dec = qc * jnp.exp(gam)[..., None]
    k_dec = kc * jnp.exp(gam[..., -1:] - gam)[..., None]
    last = jnp.exp(gam[..., -1])[..., None, None]

    def step(s, xs):
        u_n, w_n, q_n, k_n, qk_n, last_n = xs
        v_new = u_n - jnp.einsum('bhck,bhkv->bhcv', w_n, s)
        o = jnp.einsum('bhck,bhkv->bhcv', q_n, s) + jnp.einsum('bhij,bhjv->bhiv', qk_n, v_new)
        s = last_n * s + jnp.einsum('bhck,bhcv->bhkv', k_n, v_new)
        return s, o

    s_fin, o = lax.scan(step, s0, (u, w, q_dec, k_dec, qk, last))
    o = jnp.moveaxis(o, (0, 2), (1, 3)).reshape(bsz, t, h, dv)
    return o, s_fin


def _gdn_mixer(dq, dk, dv, dz, db, da, conv_buf, s0, conv_w, a_log, dt_bias, out_gain):
    bsz, t, _ = dq.shape
    qkv, conv_new = _causal_conv(jnp.concatenate([dq, dk, dv], axis=-1), conv_buf, conv_w)
    qkv = jax.nn.silu(qkv).astype(jnp.float32)
    q = _l2norm(qkv[..., :D_QK].reshape(bsz, t, D_HEADS, D_DK)) * D_DK ** -0.5
    k = _l2norm(qkv[..., D_QK:2 * D_QK].reshape(bsz, t, D_HEADS, D_DK))
    v = qkv[..., 2 * D_QK:].reshape(bsz, t, D_HEADS, D_DV)
    beta = jax.nn.sigmoid(db.astype(jnp.float32))
    g = -jnp.exp(a_log.astype(jnp.float32)) * jax.nn.softplus(da.astype(jnp.float32) + dt_bias.astype(jnp.float32))
    pad = (-t) % D_CHUNK

    def padt(x):
        return jnp.pad(x, [(0, 0), (0, pad)] + [(0, 0)] * (x.ndim - 2))

    o, s_fin = _gated_delta_chunked(padt(q), padt(k), padt(v), padt(g), padt(beta), s0.astype(jnp.float32))
    o = _rms(o[:, :t], out_gain) * jax.nn.silu(dz.reshape(bsz, t, D_HEADS, D_DV).astype(jnp.float32))
    return o.reshape(bsz, t, D_V).astype(dq.dtype), conv_new, s_fin.astype(s0.dtype)


def _nsa_compress(q, q_pos, kv_rows, phi_k, phi_v, slopes):
    bsz, length = kv_rows.shape[:2]
    nbc = length // C_BLOCK
    blk = kv_rows[:, :nbc * C_BLOCK].reshape(bsz, nbc, C_BLOCK, 2, C_KV_HEADS, HEAD_DIM)
    kc = jnp.einsum('bnjhd,j->bnhd', blk[:, :, :, 0], phi_k)
    vc = jnp.einsum('bnjhd,j->bnhd', blk[:, :, :, 1], phi_v)
    s = jnp.einsum('bthgd,bnhd->bthgn', q, kc).astype(jnp.float32) * HEAD_DIM ** -0.5
    nidx = jnp.arange(nbc)
    centre = nidx * C_BLOCK + (C_BLOCK - 1) / 2.0
    dist = jnp.abs(q_pos[:, None].astype(jnp.float32) - centre[None, :].astype(jnp.float32))
    s = s - slopes[None, None, :, :, None] * dist[None, :, None, None, :]
    mask = ((nidx[None, :] + 1) * C_BLOCK - 1 <= q_pos[:, None])[None, :, None, None, :]
    p, _, l = _masked_stats(s, mask)
    p = p / jnp.maximum(l, 1e-30)[..., None]
    out = jnp.einsum('bthgn,bnhd->bthgd', p, vc.astype(jnp.float32))
    return out, jnp.sum(p, axis=3)


def _nsa_select_idx(imp, q_pos, n_blocks):
    nbc = imp.shape[-1]
    imp = jnp.pad(imp, ((0, 0), (0, 0), (0, 0), (0, n_blocks - nbc)))
    nidx = jnp.arange(n_blocks)[None, :]
    cur = (q_pos // C_BLOCK)[:, None]
    forced = (nidx == 0) | (nidx == cur) | (nidx == cur - 1)
    score = imp + jnp.where(forced, FORCE, 0.0)[None, :, None, :]
    score = jnp.where((nidx <= cur)[None, :, None, :], score, NEG)
    vals, idx = lax.top_k(score, min(C_N_SEL, n_blocks))
    return idx, vals > NEG / 2


def _nsa_select_attend(q, q_pos, kv_g, idx, valid, slopes):
    bsz, t, hk, g, dh = q.shape
    n_keys = idx.shape[-1] * C_BLOCK
    kpos = idx[..., None] * C_BLOCK + jnp.arange(C_BLOCK)
    dist = q_pos[None, :, None, None, None] - kpos
    mask = (valid[..., None] & (dist >= 0)).reshape(bsz, t, hk, 1, n_keys)
    dist = dist.reshape(bsz, t, hk, 1, n_keys).astype(jnp.float32)
    kg = kv_g[..., 0, :].reshape(bsz, t, hk, n_keys, dh)
    vg = kv_g[..., 1, :].reshape(bsz, t, hk, n_keys, dh)
    s = jnp.einsum('bthgd,bthkd->bthgk', q, kg).astype(jnp.float32) * dh ** -0.5
    s = s - slopes[None, None, :, :, None] * dist
    p, _, l = _masked_stats(s, mask)
    return jnp.einsum('bthgk,bthkd->bthgd', p / l[..., None], vg.astype(jnp.float32))


def _nsa_select_sample(q, q_pos, pool, page_table, kv_new, idx, valid, slopes):
    bd, t = q.shape[:2]
    nb_past = page_table.shape[1] * (PAGE_SIZE // C_BLOCK)
    bi = jnp.arange(bd)[:, None, None, None, None]
    hi = jnp.arange(C_KV_HEADS)[None, None, :, None, None]
    off = jnp.arange(C_BLOCK)
    tok_p = jnp.minimum(idx, nb_past - 1)[..., None] * C_BLOCK + off
    page = page_table[bi, tok_p // PAGE_SIZE]
    g_past = pool[page, tok_p % PAGE_SIZE, :, hi]
    nb_new = -(-t // C_BLOCK)
    new_pad = jnp.pad(kv_new, ((0, 0), (0, nb_new * C_BLOCK - t), (0, 0), (0, 0), (0, 0)))
    tok_n = jnp.clip(idx - nb_past, 0, nb_new - 1)[..., None] * C_BLOCK + off
    g_new = new_pad[bi, tok_n, :, hi]
    kv_g = jnp.where((idx < nb_past)[..., None, None, None], g_past, g_new.astype(g_past.dtype))
    return _nsa_select_attend(q, q_pos, kv_g, idx, valid, slopes)


def _window_attn_sample(q, q_pos, kv_ctx, slopes):
    n = kv_ctx.shape[1]
    k_pos = q_pos[-1] - (n - 1) + jnp.arange(n)
    dist = q_pos[:, None] - k_pos[None, :]
    mask = (dist >= 0) & (dist <= C_WIN)
    s = jnp.einsum('bthgd,bshd->bthgs', q, kv_ctx[:, :, 0]).astype(jnp.float32) * HEAD_DIM ** -0.5
    s = s - slopes[None, None, :, :, None] * dist.astype(jnp.float32)[None, :, None, None, :]
    p, _, l = _masked_stats(s, mask[None, :, None, None, :])
    return jnp.einsum('bthgs,bshd->bthgd', p / l[..., None], kv_ctx[:, :, 1].astype(jnp.float32))


def _even_mixer(u, w_out3d, layer, res2d, q_gain, k_gain, b_par, kv_buf, conv_buf, h0, bsz, t):
    u = u.reshape(bsz, t, -1)
    u_q, u_k, u_v, u_gate, u_x = _split_cols(u, (A_Q, A_KV, A_KV, B_WIDTH, B_WIDTH))
    q = _rms(u_q.reshape(bsz, t, A_N_GROUPS, A_HEADS, HEAD_DIM), q_gain)
    k = _rms(u_k.reshape(bsz, t, A_HEADS, HEAD_DIM), k_gain)
    v = u_v.reshape(bsz, t, A_HEADS, HEAD_DIM)
    kv_new = jnp.stack([k, v], axis=2)
    slopes = _alibi_slopes(A_N_GROUPS * A_HEADS).reshape(A_N_GROUPS, A_HEADS)
    if kv_buf is None:
        a_out = _dilated_attn_prompt(q, k, v, slopes)
        a_state = kv_new[:, -min(A_WIN_MAX, t):]
    else:
        a_out = _dilated_attn_sample(q, jnp.concatenate([kv_buf.astype(kv_new.dtype), kv_new], axis=1), slopes)
        a_state = kv_new
    b_out, conv_new, h_new = _rglru_mixer(u_gate, u_x, conv_buf, h0, *b_par)
    mixed = jnp.concatenate([a_out.reshape(bsz, t, A_OUT), b_out], axis=-1).reshape(bsz * t, -1)
    mixed = mixed.astype(jnp.bfloat16 if t > 1 else jnp.float32)
    return _matmul(mixed, w_out3d, layer, 0, D_MODEL, res=res2d), (a_state, conv_new, h_new)


def _odd_mixer(u_parts, w_out3d, layer, res2d, q_gain, k_gain, phi_k, phi_v, d_par,
               cmp_pool, sel_pool, page_table, win_buf, d_conv_buf, d_s0, bsz, t):
    u_c2d, u_d2d, u_s2d = u_parts
    u_d, u_s = u_d2d.reshape(bsz, t, -1), u_s2d.reshape(bsz, t, -1)
    dq, dk, dv, dz = _split_cols(u_d, (D_QK, D_QK, D_V, D_V))
    cg, db, da = _split_cols(u_s[..., :3 * C_HEADS + 2 * D_HEADS], (3 * C_HEADS, D_HEADS, D_HEADS))
    kv_shape = (bsz, t, 2, C_KV_HEADS, HEAD_DIM)
    if page_table is None:
        qn, cmp2d, sel2d, win2d, gates = _nsa_prep(u_c2d, u_s2d, q_gain, k_gain)
        kc, vc = _nsa_summaries(cmp2d, phi_k, phi_v)
        o_c = _nsa_prompt(qn, kc, vc, sel2d, win2d, gates, bsz, t, jnp.bfloat16)
        cmp_rows, sel_rows = cmp2d.reshape(kv_shape), sel2d.reshape(kv_shape)
        win_state = win2d.reshape(kv_shape)[:, -min(C_WIN, t):]
    else:
        u_c = u_c2d.reshape(bsz, t, -1)
        cq, ckv = _split_cols(u_c, (C_Q, 6 * C_KV))
        q = _rms(cq.reshape(bsz, t, C_KV_HEADS, C_GROUP, HEAD_DIM), q_gain)
        kv = ckv.reshape(bsz, t, 3, 2, C_KV_HEADS, HEAD_DIM)
        kv = jnp.stack([_rms(kv[:, :, :, 0], k_gain[:, None, :]), kv[:, :, :, 1]], axis=3)
        cmp_rows, sel_rows, win_rows = kv[:, :, 0], kv[:, :, 1], kv[:, :, 2]
        slopes = _alibi_slopes(C_HEADS).reshape(C_KV_HEADS, C_GROUP)
        past = page_table.shape[1] * PAGE_SIZE
        q_pos = past + jnp.arange(t)
        cmp_past = cmp_pool[page_table].reshape(bsz, past, 2, C_KV_HEADS, HEAD_DIM)
        cmp_ctx = jnp.concatenate([cmp_past, cmp_rows.astype(cmp_past.dtype)], axis=1)
        o_cmp, imp = _nsa_compress(q, q_pos, cmp_ctx, phi_k, phi_v, slopes)
        idx, valid = _nsa_select_idx(imp, q_pos, -(-(past + t) // C_BLOCK))
        o_sel = _nsa_select_sample(q, q_pos, sel_pool, page_table, sel_rows, idx, valid, slopes)
        o_win = _window_attn_sample(q, q_pos, jnp.concatenate([win_buf.astype(win_rows.dtype), win_rows], axis=1), slopes)
        win_state = win_rows
        gate = jax.nn.sigmoid(cg.reshape(bsz, t, 3, C_KV_HEADS, C_GROUP, 1).astype(jnp.float32))
        o_c = (gate[:, :, 0] * o_cmp + gate[:, :, 1] * o_sel + gate[:, :, 2] * o_win).reshape(bsz * t, C_Q)
    d_out, d_conv_new, d_s = _gdn_mixer(dq, dk, dv, dz, db, da, d_conv_buf, d_s0, *d_par)
    mix_dtype = jnp.bfloat16 if t > 1 else jnp.float32
    parts = [o_c.astype(mix_dtype), d_out.reshape(bsz * t, D_V).astype(mix_dtype)]
    return _matmul(parts, w_out3d, layer, 0, D_MODEL, res=res2d), (cmp_rows, sel_rows, win_state, d_conv_new, d_s)


ODD_C = C_Q + 6 * C_KV
ODD_G0 = ODD_C
ODD_D0 = ODD_G0 + 3 * C_HEADS
ODD_D = 2 * D_QK + 2 * D_V
ODD_S0 = ODD_D0 + ODD_D
ODD_IN = ODD_S0 + 2 * D_HEADS


def kernel(x_prompt, x_sample, cache_a_kv, state_b_conv, state_b_h, cache_c_cmp_kv, cache_c_sel_kv, cache_c_win_kv, state_d_conv, state_d_S, page_table, norm_mix, norm_ffn, even_w_in, even_w_out, a_q_norm, a_k_norm, b_conv_w, b_conv_b, b_gate_a_w, b_gate_a_b, b_gate_x_w, b_gate_x_b, b_lambda, odd_w_in, odd_w_out, c_q_norm, c_k_norm, c_phi_k, c_phi_v, d_conv_w, d_a_log, d_dt_bias, d_out_norm, moe_group_w, moe_group_b, moe_expert_w, moe_expert_b, moe_w1, moe_w3, moe_w2):
    bp, sp, d = x_prompt.shape
    bs, ss, _ = x_sample.shape
    depth = norm_mix.shape[0]
    hp = x_prompt.reshape(bp * sp, d)
    hs = x_sample.reshape(bs * ss, d)
    outs = {k: [] for k in ("ak", "bc", "bh", "cc", "cs", "cw", "dc", "ds")}
    outs_s = {k: [] for k in outs}
    for l in range(depth):
        i = l // 2
        xp = _rmsnorm(hp, norm_mix[l], jnp.bfloat16)
        xs = _rmsnorm(hs, norm_mix[l], jnp.float32)
        if l % 2 == 0:
            b_par = (b_conv_w[i], b_conv_b[i], b_gate_a_w[i], b_gate_a_b[i], b_gate_x_w[i], b_gate_x_b[i], b_lambda[i])
            n_in = even_w_in.shape[-1]
            up = _matmul(xp, even_w_in, i, 0, n_in)
            us = _matmul(xs, even_w_in, i, 0, n_in)
            hp, st_p = _even_mixer(up, even_w_out, i, hp, a_q_norm[i], a_k_norm[i], b_par, None,
                                   jnp.zeros((bp, CONV_W - 1, B_WIDTH), jnp.float32),
                                   jnp.zeros((bp, B_WIDTH), jnp.float32), bp, sp)
            with jax.default_matmul_precision("float32"):
                hs, st_s = _even_mixer(us, even_w_out, i, hs, a_q_norm[i], a_k_norm[i], b_par,
                                       cache_a_kv[i], state_b_conv[i], state_b_h[i], bs, ss)
            for dst, st in ((outs, st_p), (outs_s, st_s)):
                dst["ak"].append(st[0]); dst["bc"].append(st[1]); dst["bh"].append(st[2])
        else:
            d_par = (d_conv_w[i], d_a_log[i], d_dt_bias[i], d_out_norm[i])
            w_d = odd_w_in[:, :, ODD_D0:ODD_S0]
            w_s = jnp.concatenate([odd_w_in[:, :, ODD_G0:ODD_D0], odd_w_in[:, :, ODD_S0:],
                                   jnp.zeros((odd_w_in.shape[0], d, LANES - 3 * C_HEADS - 2 * D_HEADS), jnp.float32)], axis=-1)
            ups = (_matmul(xp, odd_w_in, i, 0, ODD_C), _matmul(xp, w_d, i, 0, ODD_D), _matmul(xp, w_s, i, 0, LANES))
            uss = (_matmul(xs, odd_w_in, i, 0, ODD_C), _matmul(xs, w_d, i, 0, ODD_D), _matmul(xs, w_s, i, 0, LANES))
            hp, st_p = _odd_mixer(ups, odd_w_out, i, hp, c_q_norm[i], c_k_norm[i], c_phi_k[i], c_phi_v[i], d_par,
                                  None, None, None, None,
                                  jnp.zeros((bp, CONV_W - 1, D_CONV), jnp.float32),
                                  jnp.zeros((bp, D_HEADS, D_DK, D_DV), jnp.float32), bp, sp)
            with jax.default_matmul_precision("float32"):
                hs, st_s = _odd_mixer(uss, odd_w_out, i, hs, c_q_norm[i], c_k_norm[i], c_phi_k[i], c_phi_v[i], d_par,
                                      cache_c_cmp_kv[i], cache_c_sel_kv[i], page_table, cache_c_win_kv[i],
                                      state_d_conv[i], state_d_S[i], bs, ss)
            for dst, st in ((outs, st_p), (outs_s, st_s)):
                dst["cc"].append(st[0]); dst["cs"].append(st[1]); dst["cw"].append(st[2])
                dst["dc"].append(st[3]); dst["ds"].append(st[4])
        w_router = jnp.concatenate([moe_group_w[l], moe_expert_w[l],
                                    jnp.zeros((d, ROUTER_PAD - N_GROUPS - N_EXPERTS), jnp.float32)], axis=-1)
        b_router = jnp.concatenate([moe_group_b[l], moe_expert_b[l],
                                    jnp.zeros((ROUTER_PAD - N_GROUPS - N_EXPERTS,), jnp.float32)])[None, :]
        hp = _hier_moe(hp, norm_ffn[l], w_router, b_router, moe_w1, moe_w3, moe_w2, l, 256, jnp.bfloat16)
        hs = _hier_moe(hs, norm_ffn[l], w_router, b_router, moe_w1, moe_w3, moe_w2, l, 8, jnp.float32)
    res = [hp.reshape(bp, sp, d), hs.reshape(bs, ss, d)]
    for key in ("ak", "bc", "bh", "cc", "cs", "cw", "dc", "ds"):
        res.append(jnp.stack(outs[key]))
        res.append(jnp.stack(outs_s[key]))
    return tuple(res)
```

```python
import functools
import math

import jax
import jax.numpy as jnp
import numpy as np
from jax import lax
from jax.experimental import pallas as pl
from jax.experimental.pallas import tpu as pltpu

D_MODEL = 4096
HEAD_DIM = 128
CONV_W = 4
BAND_BLOCK = 128
A_PATTERNS = ((128, 1), (512, 4), (2048, 16))
A_N_GROUPS = len(A_PATTERNS)
A_HEADS = D_MODEL // 512
A_WIN_MAX = max(w for w, _ in A_PATTERNS)
A_Q = A_N_GROUPS * A_HEADS * HEAD_DIM
A_KV = A_HEADS * HEAD_DIM
A_OUT = A_HEADS * HEAD_DIM
B_WIDTH = 3 * D_MODEL // 4
B_BLOCKS = B_WIDTH // HEAD_DIM
B_BLOCK_DIM = B_WIDTH // B_BLOCKS
RG_C = 8.0
C_HEADS = D_MODEL // 256
C_KV_HEADS = C_HEADS // 4
C_GROUP = C_HEADS // C_KV_HEADS
C_BLOCK = 64
C_N_SEL = 16
C_WIN = 512
C_Q_CHUNK = 32
C_Q = C_HEADS * HEAD_DIM
C_KV = C_KV_HEADS * HEAD_DIM
D_HEADS = D_MODEL // 256
D_DK = HEAD_DIM
D_DV = HEAD_DIM
D_QK = D_HEADS * D_DK
D_V = D_HEADS * D_DV
D_CONV = 2 * D_QK + D_V
D_CHUNK = 64
N_GROUPS = 8
EXPERTS_PER_GROUP = 8
N_EXPERTS = N_GROUPS * EXPERTS_PER_GROUP
TOP_K = 2
D_EXPERT = D_MODEL // 8
PAGE_SIZE = 128
EPS = 1e-6
NEG = -1e30
FORCE = 1e4

LANES = 128
VMEM_LIMIT = 56 * 1024 * 1024
ROUTER_PAD = LANES

HI = lax.Precision.HIGHEST


def _params(sem):
    return pltpu.CompilerParams(dimension_semantics=sem, vmem_limit_bytes=VMEM_LIMIT)


def _rmsnorm_kernel(x_ref, g_ref, o_ref):
    x = x_ref[...]
    y = x * lax.rsqrt(jnp.mean(x * x, axis=-1, keepdims=True) + EPS)
    o_ref[...] = (y * g_ref[...]).astype(o_ref.dtype)


def _rmsnorm(x2d, gain, out_dtype):
    m, d = x2d.shape
    tm = min(m, 512)
    return pl.pallas_call(
        _rmsnorm_kernel,
        out_shape=jax.ShapeDtypeStruct((m, d), out_dtype),
        grid=(m // tm,),
        in_specs=[pl.BlockSpec((tm, d), lambda i: (i, 0)), pl.BlockSpec((1, d), lambda i: (0, 0))],
        out_specs=pl.BlockSpec((tm, d), lambda i: (i, 0)),
        compiler_params=_params(("parallel",)),
        name="rmsnorm",
    )(x2d, gain.reshape(1, d))


def _matmul_kernel(*refs, exact, has_res, k_bounds):
    n_x = len(k_bounds) - 1
    x_refs, w_ref = refs[:n_x], refs[n_x]
    if has_res:
        r_ref, o_ref, acc_ref = refs[n_x + 1:]
    else:
        o_ref, acc_ref = refs[n_x + 1:]
    k = pl.program_id(2)

    @pl.when(k == 0)
    def _():
        acc_ref[...] = jnp.zeros_like(acc_ref)

    for p, x_ref in enumerate(x_refs):
        @pl.when((k >= k_bounds[p]) & (k < k_bounds[p + 1]))
        def _(x_ref=x_ref):
            if exact:
                acc_ref[...] += jnp.dot(x_ref[...], w_ref[...], preferred_element_type=jnp.float32, precision=HI)
            else:
                acc_ref[...] += jnp.dot(x_ref[...].astype(jnp.bfloat16), w_ref[...].astype(jnp.bfloat16),
                                        preferred_element_type=jnp.float32)

    @pl.when(k == pl.num_programs(2) - 1)
    def _():
        out = acc_ref[...]
        if has_res:
            out = out + r_ref[...]
        o_ref[...] = out


def _matmul(xs, w3d, layer, col0, n, exact, res=None, tn=1024, tk=512):
    if not isinstance(xs, (list, tuple)):
        xs = [xs]
    m = xs[0].shape[0]
    tm = min(m, 1024)
    tn = min(tn, n)
    assert m % tm == 0 and n % tn == 0 and col0 % tn == 0 and all(x.shape[1] % tk == 0 for x in xs)
    jb = col0 // tn
    k_bounds = [0]
    for x in xs:
        k_bounds.append(k_bounds[-1] + x.shape[1] // tk)
    in_specs = []
    for p in range(len(xs)):
        lo, hi = k_bounds[p], k_bounds[p + 1]
        in_specs.append(pl.BlockSpec((tm, tk), lambda i, j, k, lo=lo, hi=hi: (i, jnp.clip(k, lo, hi - 1) - lo)))
    in_specs.append(pl.BlockSpec((None, tk, tn), lambda i, j, k: (layer, k, j + jb)))
    args = list(xs) + [w3d]
    if res is not None:
        in_specs.append(pl.BlockSpec((tm, tn), lambda i, j, k: (i, j)))
        args.append(res)
    return pl.pallas_call(
        functools.partial(_matmul_kernel, exact=exact, has_res=res is not None, k_bounds=tuple(k_bounds)),
        out_shape=jax.ShapeDtypeStruct((m, n), jnp.float32),
        grid=(m // tm, n // tn, k_bounds[-1]),
        in_specs=in_specs,
        out_specs=pl.BlockSpec((tm, tn), lambda i, j, k: (i, j)),
        scratch_shapes=[pltpu.VMEM((tm, tn), jnp.float32)],
        compiler_params=_params(("parallel", "parallel", "arbitrary")),
        name="proj",
    )(*args)


def _router_kernel(h_ref, g_ref, w_ref, b_ref, xn_ref, logit_ref):
    x = h_ref[...]
    y = x * lax.rsqrt(jnp.mean(x * x, axis=-1, keepdims=True) + EPS) * g_ref[...]
    xn_ref[...] = y.astype(xn_ref.dtype)
    logit_ref[...] = jnp.dot(y, w_ref[...], preferred_element_type=jnp.float32, precision=HI) + b_ref[...]


def _router(h2d, gain, w_router, b_router, xn_dtype):
    m, d = h2d.shape
    tm = min(m, 256)
    return pl.pallas_call(
        _router_kernel,
        out_shape=(jax.ShapeDtypeStruct((m, d), xn_dtype), jax.ShapeDtypeStruct((m, ROUTER_PAD), jnp.float32)),
        grid=(m // tm,),
        in_specs=[pl.BlockSpec((tm, d), lambda i: (i, 0)), pl.BlockSpec((1, d), lambda i: (0, 0)),
                  pl.BlockSpec((d, ROUTER_PAD), lambda i: (0, 0)), pl.BlockSpec((1, ROUTER_PAD), lambda i: (0, 0))],
        out_specs=(pl.BlockSpec((tm, d), lambda i: (i, 0)), pl.BlockSpec((tm, ROUTER_PAD), lambda i: (i, 0))),
        compiler_params=_params(("parallel",)),
        name="ffn_norm_router",
    )(h2d, gain.reshape(1, d), w_router, b_router)


MOE_K_CHUNK = 512


def _moe_up_kernel(be_ref, x_ref, w1_ref, w3_ref, o_ref, *, exact):
    tm = x_ref.shape[0]
    a = jnp.zeros((tm, D_EXPERT), jnp.float32)
    b = jnp.zeros((tm, D_EXPERT), jnp.float32)
    for c in range(D_MODEL // MOE_K_CHUNK):
        sl = slice(c * MOE_K_CHUNK, (c + 1) * MOE_K_CHUNK)
        if exact:
            a += jnp.dot(x_ref[:, sl], w1_ref[sl, :], preferred_element_type=jnp.float32, precision=HI)
            b += jnp.dot(x_ref[:, sl], w3_ref[sl, :], preferred_element_type=jnp.float32, precision=HI)
        else:
            a += jnp.dot(x_ref[:, sl], w1_ref[sl, :].astype(jnp.bfloat16), preferred_element_type=jnp.float32)
            b += jnp.dot(x_ref[:, sl], w3_ref[sl, :].astype(jnp.bfloat16), preferred_element_type=jnp.float32)
    o_ref[...] = (a * jax.nn.sigmoid(a) * b).astype(o_ref.dtype)


def _moe_down_kernel(be_ref, h_ref, w2_ref, g_ref, o_ref, *, exact):
    if exact:
        out = jnp.dot(h_ref[...], w2_ref[...], preferred_element_type=jnp.float32, precision=HI)
    else:
        out = jnp.dot(h_ref[...], w2_ref[...].astype(jnp.bfloat16), preferred_element_type=jnp.float32)
    o_ref[...] = out * g_ref[...]


def _moe_experts(x_rows, row_gate, blk_exp, w1, w3, w2, layer, tm):
    rows, d = x_rows.shape
    n_blk = rows // tm
    exact = x_rows.dtype == jnp.float32
    up = pl.pallas_call(
        functools.partial(_moe_up_kernel, exact=exact),
        out_shape=jax.ShapeDtypeStruct((rows, D_EXPERT), x_rows.dtype),
        grid_spec=pltpu.PrefetchScalarGridSpec(
            num_scalar_prefetch=1, grid=(n_blk,),
            in_specs=[pl.BlockSpec((tm, d), lambda i, be: (i, 0)),
                      pl.BlockSpec((None, None, d, D_EXPERT), lambda i, be: (layer, be[i], 0, 0)),
                      pl.BlockSpec((None, None, d, D_EXPERT), lambda i, be: (layer, be[i], 0, 0))],
            out_specs=pl.BlockSpec((tm, D_EXPERT), lambda i, be: (i, 0))),
        compiler_params=_params(("arbitrary",)),
        name="moe_up",
    )(blk_exp, x_rows, w1, w3)
    return pl.pallas_call(
        functools.partial(_moe_down_kernel, exact=exact),
        out_shape=jax.ShapeDtypeStruct((rows, d), jnp.float32),
        grid_spec=pltpu.PrefetchScalarGridSpec(
            num_scalar_prefetch=1, grid=(n_blk,),
            in_specs=[pl.BlockSpec((tm, D_EXPERT), lambda i, be: (i, 0)),
                      pl.BlockSpec((None, None, D_EXPERT, d), lambda i, be: (layer, be[i], 0, 0)),
                      pl.BlockSpec((tm, 1), lambda i, be: (i, 0))],
            out_specs=pl.BlockSpec((tm, d), lambda i, be: (i, 0))),
        compiler_params=_params(("arbitrary",)),
        name="moe_down",
    )(blk_exp, up, w2, row_gate)


def _hier_moe(h2d, gain, w_router, b_router, w1, w3, w2, layer, tm, xn_dtype):
    n_tok, d = h2d.shape
    xn, logits = _router(h2d, gain, w_router, b_router, xn_dtype)
    g_logit = logits[:, :N_GROUPS]
    g_prob = jax.nn.softmax(g_logit, axis=-1)
    grp = jnp.argmax(g_logit, axis=-1)
    p_grp = jnp.take_along_axis(g_prob, grp[:, None], axis=1)[:, 0]
    e_logit = logits[:, N_GROUPS:N_GROUPS + N_EXPERTS].reshape(-1, N_GROUPS, EXPERTS_PER_GROUP)
    e_logit = jnp.take_along_axis(e_logit, grp[:, None, None], axis=1)[:, 0]
    e_val, e_idx = lax.top_k(e_logit, TOP_K)
    gates = p_grp[:, None] * jax.nn.softmax(e_val, axis=-1)
    experts = grp[:, None] * EXPERTS_PER_GROUP + e_idx
    n_asg = n_tok * TOP_K
    flat_e = experts.reshape(-1).astype(jnp.int32)
    order = jnp.argsort(flat_e)
    sorted_e = flat_e[order]
    counts = jnp.bincount(flat_e, length=N_EXPERTS)
    padded = (counts + tm - 1) // tm * tm
    pad_end = jnp.cumsum(padded)
    pad_start = pad_end - padded
    start = jnp.cumsum(counts) - counts
    dest = (pad_start[sorted_e] + jnp.arange(n_asg) - start[sorted_e]).astype(jnp.int32)
    n_rows = -(-(n_asg + N_EXPERTS * (tm - 1)) // tm) * tm
    n_blk = n_rows // tm
    tok = (order // TOP_K).astype(jnp.int32)
    row_tok = jnp.zeros((n_rows,), jnp.int32).at[dest].set(tok)
    row_gate = jnp.zeros((n_rows,), jnp.float32).at[dest].set(gates.reshape(-1)[order])
    blk_exp = jnp.minimum(jnp.searchsorted(pad_end, jnp.arange(n_blk) * tm, side='right'),
                          N_EXPERTS - 1).astype(jnp.int32)
    x_rows = xn[row_tok]
    out = _moe_experts(x_rows, row_gate[:, None], blk_exp, w1, w3, w2, layer, tm)
    asg_row = jnp.zeros((n_asg,), jnp.int32).at[order].set(dest)
    y = out[asg_row].reshape(n_tok, TOP_K, d).sum(axis=1)
    return h2d + y


NSA_PREP_ROWS = 256
NSA_SUM_ROWS = 512
NSA_TQ = 128
NSA_GATE_LANES = LANES
TQ_SHIFT = NSA_TQ.bit_length() - 1
C_BLOCK_SHIFT = C_BLOCK.bit_length() - 1


def _chunk_rms(x, gain):
    return x * lax.rsqrt(jnp.mean(x * x, axis=-1, keepdims=True) + EPS) * gain


def _nsa_prep_kernel(uc_ref, us_ref, qg_ref, kg_ref, perm_ref, q_ref, cmp_ref, sel_ref, win_ref, gate_ref):
    for c in range(C_Q // HEAD_DIM):
        sl = slice(c * HEAD_DIM, (c + 1) * HEAD_DIM)
        q_ref[:, sl] = _chunk_rms(uc_ref[:, sl], qg_ref[...])
    for br, o_ref in enumerate((cmp_ref, sel_ref, win_ref)):
        base = C_Q + br * 2 * C_KV
        for c in range(C_KV_HEADS):
            sl = slice(c * HEAD_DIM, (c + 1) * HEAD_DIM)
            o_ref[:, sl] = _chunk_rms(uc_ref[:, base + c * HEAD_DIM: base + (c + 1) * HEAD_DIM], kg_ref[br:br + 1, :])
        o_ref[:, C_KV:] = uc_ref[:, base + C_KV: base + 2 * C_KV]
    gate_ref[...] = jnp.dot(jax.nn.sigmoid(us_ref[...]), perm_ref[...], preferred_element_type=jnp.float32, precision=HI)


def _nsa_prep(u_c, u_s, q_gain, k_gain):
    p = u_c.shape[0]
    tm = NSA_PREP_ROWS
    perm = np.zeros((LANES, C_KV_HEADS * NSA_GATE_LANES), np.float32)
    for br in range(3):
        for h in range(C_KV_HEADS):
            for g in range(C_GROUP):
                perm[br * C_HEADS + h * C_GROUP + g, h * NSA_GATE_LANES + br * C_GROUP + g] = 1.0
    row = lambda i: (i, 0)
    fixed = lambda i: (0, 0)
    return pl.pallas_call(
        _nsa_prep_kernel,
        out_shape=(jax.ShapeDtypeStruct((p, C_Q), jnp.float32),) + (jax.ShapeDtypeStruct((p, 2 * C_KV), jnp.float32),) * 3
        + (jax.ShapeDtypeStruct((p, C_KV_HEADS * NSA_GATE_LANES), jnp.float32),),
        grid=(p // tm,),
        in_specs=[pl.BlockSpec((tm, ODD_C), row), pl.BlockSpec((tm, LANES), row), pl.BlockSpec((1, HEAD_DIM), fixed),
                  pl.BlockSpec((3, HEAD_DIM), fixed), pl.BlockSpec(perm.shape, fixed)],
        out_specs=(pl.BlockSpec((tm, C_Q), row),) + (pl.BlockSpec((tm, 2 * C_KV), row),) * 3
        + (pl.BlockSpec((tm, C_KV_HEADS * NSA_GATE_LANES), row),),
        compiler_params=_params(("parallel",)),
        name="nsa_prep",
    )(u_c, u_s, q_gain.reshape(1, HEAD_DIM), k_gain, jnp.asarray(perm))


def _nsa_sum_kernel(cmp_ref, phik_ref, phiv_ref, kc_ref, vc_ref):
    kc_ref[...] = jnp.dot(phik_ref[...], cmp_ref[:, :C_KV], preferred_element_type=jnp.float32, precision=HI)
    vc_ref[...] = jnp.dot(phiv_ref[...], cmp_ref[:, C_KV:], preferred_element_type=jnp.float32, precision=HI)


def _nsa_summaries(cmp_rows, phi_k, phi_v):
    p = cmp_rows.shape[0]
    tm = NSA_SUM_ROWS
    nb = tm // C_BLOCK
    eye = jnp.eye(nb, dtype=jnp.float32)
    big_k = jnp.kron(eye, phi_k[None, :])
    big_v = jnp.kron(eye, phi_v[None, :])
    return pl.pallas_call(
        _nsa_sum_kernel,
        out_shape=(jax.ShapeDtypeStruct((p // C_BLOCK, C_KV), jnp.float32),) * 2,
        grid=(p // tm,),
        in_specs=[pl.BlockSpec((tm, 2 * C_KV), lambda i: (i, 0)), pl.BlockSpec((nb, tm), lambda i: (0, 0)),
                  pl.BlockSpec((nb, tm), lambda i: (0, 0))],
        out_specs=(pl.BlockSpec((nb, C_KV), lambda i: (i, 0)),) * 2,
        compiler_params=_params(("parallel",)),
        name="nsa_summaries",
    )(cmp_rows, big_k, big_v)


def _nsa_prompt_kernel(q_ref, kc_ref, vc_ref, sk_ref, sv_ref, wk_ref, wv_ref, gate_ref, o_ref,
                       m_scr, l_scr, acc_scr, *, n_blocks):
    h = pl.program_id(1)
    qb = pl.program_id(2)
    tq = NSA_TQ
    rows = C_GROUP * tq
    scale = HEAD_DIM ** -0.5
    q = jnp.concatenate([q_ref[:, g * HEAD_DIM:(g + 1) * HEAD_DIM] for g in range(C_GROUP)], axis=0)
    row = lax.broadcasted_iota(jnp.int32, (rows, 1), 0)
    t_row = qb * tq + (row & (tq - 1))
    slope = jnp.exp2(-0.5 * (h * C_GROUP + (row >> TQ_SHIFT) + 1).astype(jnp.float32))

    nidx = lax.broadcasted_iota(jnp.int32, (1, n_blocks), 1)
    s = lax.dot_general(q, kc_ref[...], (((1,), (1,)), ((), ())), preferred_element_type=jnp.float32, precision=HI) * scale
    centre = nidx.astype(jnp.float32) * C_BLOCK + (C_BLOCK - 1) / 2.0
    s = s - slope * jnp.abs(t_row.astype(jnp.float32) - centre)
    cmask = (nidx + 1) * C_BLOCK - 1 <= t_row
    s = jnp.where(cmask, s, NEG)
    m = jnp.max(s, axis=-1, keepdims=True)
    p = jnp.where(cmask, jnp.exp(s - m), 0.0)
    l = jnp.sum(p, axis=-1, keepdims=True)
    p = p / jnp.maximum(l, 1e-30)
    o_cmp = jnp.dot(p, vc_ref[...], preferred_element_type=jnp.float32, precision=HI)
    imp = p[0:tq]
    for g in range(1, C_GROUP):
        imp = imp + p[g * tq:(g + 1) * tq]

    t_q = t_row[0:tq]
    cur = t_q >> C_BLOCK_SHIFT
    forced = (nidx == 0) | (nidx == cur) | (nidx == cur - 1)
    causal = nidx <= cur
    score = jnp.where(causal, imp + jnp.where(forced, FORCE, 0.0), NEG)
    rank = jnp.zeros((tq, n_blocks), jnp.int32)
    for mcol in range(n_blocks):
        cm = score[:, mcol:mcol + 1]
        ahead = (cm > score) | ((cm == score) & (mcol < nidx))
        rank = rank + ahead.astype(jnp.int32)
    selm = ((rank < min(C_N_SEL, n_blocks)) & causal).astype(jnp.bfloat16)

    q16 = q.astype(jnp.bfloat16)
    kcol = lax.broadcasted_iota(jnp.int32, (1, tq), 1)

    def attend(k_ref, v_ref, c, mask):
        start = pl.multiple_of(c * tq, tq)
        k = k_ref[pl.ds(start, tq), :].astype(jnp.bfloat16)
        v = v_ref[pl.ds(start, tq), :].astype(jnp.bfloat16)
        dist = t_row - (c * tq + kcol)
        sc = lax.dot_general(q16, k, (((1,), (1,)), ((), ())), preferred_element_type=jnp.float32) * scale
        sc = sc - slope * dist.astype(jnp.float32)
        ok = mask(dist)
        sc = jnp.where(ok, sc, NEG)
        m_old = m_scr[...]
        m_new = jnp.maximum(m_old, jnp.max(sc, axis=-1, keepdims=True))
        pc = jnp.where(ok, jnp.exp(sc - m_new), 0.0)
        alpha = jnp.exp(m_old - m_new)
        l_scr[...] = alpha * l_scr[...] + jnp.sum(pc, axis=-1, keepdims=True)
        acc_scr[...] = alpha * acc_scr[...] + jnp.dot(pc.astype(jnp.bfloat16), v, preferred_element_type=jnp.float32)
        m_scr[...] = m_new

    def reset():
        m_scr[...] = jnp.full_like(m_scr, NEG)
        l_scr[...] = jnp.zeros_like(l_scr)
        acc_scr[...] = jnp.zeros_like(acc_scr)

    reset()
    brow = lax.broadcasted_iota(jnp.int32, (n_blocks, tq), 0)
    bcol = lax.broadcasted_iota(jnp.int32, (n_blocks, tq), 1)

    def sel_body(c, carry):
        expand = (brow == c * (tq // C_BLOCK) + (bcol >> C_BLOCK_SHIFT)).astype(jnp.bfloat16)
        sel_keys = jnp.dot(selm, expand, preferred_element_type=jnp.float32)
        sel_keys = jnp.concatenate([sel_keys] * C_GROUP, axis=0) > 0.5
        attend(sk_ref, sv_ref, c, lambda dist: sel_keys & (dist >= 0))
        return carry

    lax.fori_loop(0, qb + 1, sel_body, 0)
    o_sel = acc_scr[...] / l_scr[...]

    reset()

    def win_body(c, carry):
        attend(wk_ref, wv_ref, c, lambda dist: (dist >= 0) & (dist <= C_WIN))
        return carry

    lax.fori_loop(jnp.maximum(qb - C_WIN // tq, 0), qb + 1, win_body, 0)
    o_win = acc_scr[...] / l_scr[...]

    gt = gate_ref[...]
    outs = []
    for g in range(C_GROUP):
        rs = slice(g * tq, (g + 1) * tq)
        outs.append(gt[:, g:g + 1] * o_cmp[rs] + gt[:, C_GROUP + g:C_GROUP + g + 1] * o_sel[rs]
                    + gt[:, 2 * C_GROUP + g:2 * C_GROUP + g + 1] * o_win[rs])
    o_ref[...] = jnp.concatenate(outs, axis=1).astype(o_ref.dtype)


def _nsa_prompt(qn, kc, vc, sel_rows, win_rows, gates, bsz, t, out_dtype):
    tq = NSA_TQ
    nq = t // tq
    n_blocks = t // C_BLOCK
    kv_k = pl.BlockSpec((t, HEAD_DIM), lambda b, h, i: (b, h))
    kv_v = pl.BlockSpec((t, HEAD_DIM), lambda b, h, i: (b, C_KV_HEADS + h))
    return pl.pallas_call(
        functools.partial(_nsa_prompt_kernel, n_blocks=n_blocks),
        out_shape=jax.ShapeDtypeStruct((bsz * t, C_Q), out_dtype),
        grid=(bsz, C_KV_HEADS, nq),
        in_specs=[pl.BlockSpec((tq, C_GROUP * HEAD_DIM), lambda b, h, i: (b * nq + i, h)),
                  pl.BlockSpec((n_blocks, HEAD_DIM), lambda b, h, i: (b, h)),
                  pl.BlockSpec((n_blocks, HEAD_DIM), lambda b, h, i: (b, h)),
                  kv_k, kv_v, kv_k, kv_v,
                  pl.BlockSpec((tq, NSA_GATE_LANES), lambda b, h, i: (b * nq + i, h))],
        out_specs=pl.BlockSpec((tq, C_GROUP * HEAD_DIM), lambda b, h, i: (b * nq + i, h)),
        scratch_shapes=[pltpu.VMEM((C_GROUP * tq, 1), jnp.float32), pltpu.VMEM((C_GROUP * tq, 1), jnp.float32),
                        pltpu.VMEM((C_GROUP * tq, HEAD_DIM), jnp.float32)],
        compiler_params=_params(("parallel", "parallel", "arbitrary")),
        name="nsa_prompt",
    )(qn, kc, vc, sel_rows, sel_rows, win_rows, win_rows, gates)


A_PREP_ROWS = 256
A_TQ = BAND_BLOCK
A_SLOPES = [[2.0 ** (-8.0 * (gi * A_HEADS + h + 1.0) / (A_N_GROUPS * A_HEADS)) for h in range(A_HEADS)]
            for gi in range(A_N_GROUPS)]


def _a_prep_kernel(q_in, k_in, v_in, qg_ref, kg_ref, q_ref, kv_ref):
    for c in range(A_Q // HEAD_DIM):
        sl = slice(c * HEAD_DIM, (c + 1) * HEAD_DIM)
        q_ref[:, sl] = _chunk_rms(q_in[:, sl], qg_ref[...])
    for c in range(A_HEADS):
        sl = slice(c * HEAD_DIM, (c + 1) * HEAD_DIM)
        kv_ref[:, sl] = _chunk_rms(k_in[:, sl], kg_ref[...])
    kv_ref[:, A_KV:] = v_in[...]


def _a_prep(u, q_gain, k_gain):
    p = u.shape[0]
    tm = A_PREP_ROWS
    fixed = lambda i: (0, 0)
    return pl.pallas_call(
        _a_prep_kernel,
        out_shape=(jax.ShapeDtypeStruct((p, A_Q), jnp.float32), jax.ShapeDtypeStruct((p, 2 * A_KV), jnp.float32)),
        grid=(p // tm,),
        in_specs=[pl.BlockSpec((tm, A_Q), lambda i: (i, 0)),
                  pl.BlockSpec((tm, A_KV), lambda i: (i, A_Q // A_KV)),
                  pl.BlockSpec((tm, A_KV), lambda i: (i, A_Q // A_KV + 1)),
                  pl.BlockSpec((1, HEAD_DIM), fixed), pl.BlockSpec((1, HEAD_DIM), fixed)],
        out_specs=(pl.BlockSpec((tm, A_Q), lambda i: (i, 0)), pl.BlockSpec((tm, 2 * A_KV), lambda i: (i, 0))),
        compiler_params=_params(("parallel",)),
        name="a_prep",
    )(u, u, u, q_gain.reshape(1, HEAD_DIM), k_gain.reshape(1, HEAD_DIM))


def _a_band_kernel(*refs, gi, dil, first, last):
    if first:
        q_ref, kp_ref, kc_ref, vp_ref, vc_ref = refs[:5]
        outs = refs[5:]
    else:
        q_ref, kp_ref, kc_ref, vp_ref, vc_ref, m_in, l_in, acc_in = refs[:8]
        outs = refs[8:]
    qb = pl.program_id(2)
    tq = A_TQ
    row = lax.broadcasted_iota(jnp.int32, (tq, 2 * tq), 0)
    col = lax.broadcasted_iota(jnp.int32, (tq, 2 * tq), 1)
    dist = tq + row - col
    ok = (dist >= 0) & (dist <= tq) & ((col >= tq) | (qb > 0))
    dist_f = (dist * dil).astype(jnp.float32)
    scale = HEAD_DIM ** -0.5
    for h in range(A_HEADS):
        sl = slice(h * HEAD_DIM, (h + 1) * HEAD_DIM)
        q = q_ref[:, sl].astype(jnp.bfloat16)
        k = jnp.concatenate([kp_ref[:, sl], kc_ref[:, sl]], axis=0).astype(jnp.bfloat16)
        v = jnp.concatenate([vp_ref[:, sl], vc_ref[:, sl]], axis=0).astype(jnp.bfloat16)
        s = lax.dot_general(q, k, (((1,), (1,)), ((), ())), preferred_element_type=jnp.float32) * scale
        s = jnp.where(ok, s - A_SLOPES[gi][h] * dist_f, NEG)
        m_row = jnp.max(s, axis=-1, keepdims=True)
        if first:
            m_new = jnp.broadcast_to(m_row, (tq, HEAD_DIM))
        else:
            m_old = m_in[:, sl]
            m_new = jnp.maximum(m_old, m_row)
        p = jnp.where(ok, jnp.exp(s - m_new[:, 0:1]), 0.0)
        l_new = jnp.broadcast_to(jnp.sum(p, axis=-1, keepdims=True), (tq, HEAD_DIM))
        acc = jnp.dot(p.astype(jnp.bfloat16), v, preferred_element_type=jnp.float32)
        if not first:
            alpha = jnp.exp(m_old - m_new)
            l_new = alpha * l_in[:, sl] + l_new
            acc = alpha * acc_in[:, sl] + acc
        if last:
            outs[0][:, sl] = (acc / l_new).astype(outs[0].dtype)
        else:
            outs[0][:, sl] = m_new
            outs[1][:, sl] = l_new
            outs[2][:, sl] = acc


def _a_band_group(gi, qn, akv, stats, bsz, t, out_dtype):
    w, dil = A_PATTERNS[gi]
    assert w // dil == A_TQ
    p = bsz * t
    n_res = t // dil
    nq = n_res // A_TQ
    first, last = gi == 0, gi == A_N_GROUPS - 1
    rows = p // dil
    qv = qn.reshape(rows, dil * A_Q)
    kvv = akv.reshape(rows, dil * 2 * A_KV)
    blk = (A_TQ, A_KV)
    cur = lambda b, r, i: b * nq + i
    prev = lambda b, r, i: b * nq + jnp.maximum(i - 1, 0)
    in_specs = [pl.BlockSpec(blk, lambda b, r, i: (cur(b, r, i), r * A_N_GROUPS + gi)),
                pl.BlockSpec(blk, lambda b, r, i: (prev(b, r, i), r * 2)),
                pl.BlockSpec(blk, lambda b, r, i: (cur(b, r, i), r * 2)),
                pl.BlockSpec(blk, lambda b, r, i: (prev(b, r, i), r * 2 + 1)),
                pl.BlockSpec(blk, lambda b, r, i: (cur(b, r, i), r * 2 + 1))]
    args = [qv, kvv, kvv, kvv, kvv]
    stat_spec = pl.BlockSpec(blk, lambda b, r, i: (cur(b, r, i), r))
    if not first:
        in_specs += [stat_spec] * 3
        args += [s.reshape(rows, dil * A_KV) for s in stats]
    if last:
        out_shape = jax.ShapeDtypeStruct((rows, dil * A_OUT), out_dtype)
        out_specs = stat_spec
    else:
        out_shape = (jax.ShapeDtypeStruct((rows, dil * A_KV), jnp.float32),) * 3
        out_specs = (stat_spec,) * 3
    out = pl.pallas_call(
        functools.partial(_a_band_kernel, gi=gi, dil=dil, first=first, last=last),
        out_shape=out_shape,
        grid=(bsz, dil, nq),
        in_specs=in_specs,
        out_specs=out_specs,
        compiler_params=_params(("parallel", "parallel", "arbitrary")),
        name="a_band_g%d" % gi,
    )(*args)
    if last:
        return out.reshape(p, A_OUT)
    return tuple(o.reshape(p, A_KV) for o in out)


def _a_attn_prompt(qn, akv, bsz, t, out_dtype):
    stats = None
    for gi in range(A_N_GROUPS):
        stats = _a_band_group(gi, qn, akv, stats, bsz, t, out_dtype)
    return stats


B_SCAN_T = 64


def _b_gates_kernel(g_ref, x_ref, cw_ref, cb_ref, wa_ref, ba_ref, wx_ref, bx_ref, lam_ref, a_ref, b_ref, gg_ref):
    t = x_ref.shape[0]
    x = x_ref[...]
    xx = jnp.concatenate([jnp.zeros((8, x.shape[1]), x.dtype), x], axis=0)
    xc = x * cw_ref[CONV_W - 1:CONV_W, :] + cb_ref[...]
    for j in range(CONV_W - 1):
        shift = CONV_W - 1 - j
        xc = xc + pltpu.roll(xx, shift, axis=0)[8:8 + t] * cw_ref[j:j + 1, :]
    x16 = xc.astype(jnp.bfloat16)
    r = jax.nn.sigmoid(jnp.dot(x16, wa_ref[...].astype(jnp.bfloat16), preferred_element_type=jnp.float32) + ba_ref[...])
    ig = jax.nn.sigmoid(jnp.dot(x16, wx_ref[...].astype(jnp.bfloat16), preferred_element_type=jnp.float32) + bx_ref[...])
    log_a = r * lam_ref[...]
    a = jnp.exp(log_a)
    a_ref[...] = a
    b_ref[...] = jnp.sqrt(1.0 - jnp.exp(2.0 * log_a)) * (ig * xc)
    g = g_ref[...]
    gg_ref[...] = 0.5 * g * (1.0 + jnp.tanh(math.sqrt(2.0 / math.pi) * (g + 0.044715 * (g * g * g))))


def _b_gates(u, b_par, bsz, t):
    conv_w, conv_b, wa, ba, wx, bx, lam = b_par
    lam_c = (-RG_C * jax.nn.softplus(-lam)).reshape(1, B_WIDTH)
    p = bsz * t
    g0 = (A_Q + 2 * A_KV) // B_BLOCK_DIM
    x0 = g0 + B_BLOCKS
    vec = lambda b, n: (0, n)
    blk = pl.BlockSpec((t, B_BLOCK_DIM), lambda b, n: (b, n))
    wspec = pl.BlockSpec((None, B_BLOCK_DIM, B_BLOCK_DIM), lambda b, n: (n, 0, 0))
    return pl.pallas_call(
        _b_gates_kernel,
        out_shape=(jax.ShapeDtypeStruct((p, B_WIDTH), jnp.float32),) * 3,
        grid=(bsz, B_BLOCKS),
        in_specs=[pl.BlockSpec((t, B_BLOCK_DIM), lambda b, n: (b, g0 + n)),
                  pl.BlockSpec((t, B_BLOCK_DIM), lambda b, n: (b, x0 + n)),
                  pl.BlockSpec((CONV_W, B_BLOCK_DIM), vec), pl.BlockSpec((1, B_BLOCK_DIM), vec),
                  wspec, pl.BlockSpec((1, B_BLOCK_DIM), vec), wspec, pl.BlockSpec((1, B_BLOCK_DIM), vec),
                  pl.BlockSpec((1, B_BLOCK_DIM), vec)],
        out_specs=(blk,) * 3,
        compiler_params=_params(("parallel", "parallel")),
        name="b_gates",
    )(u, u, conv_w, conv_b.reshape(1, B_WIDTH), wa, ba.reshape(1, B_WIDTH), wx, bx.reshape(1, B_WIDTH), lam_c)


def _b_scan_kernel(a_ref, b_ref, gg_ref, y_ref, hl_ref, h_scr):
    c = pl.program_id(0)

    @pl.when(c == 0)
    def _():
        h_scr[...] = jnp.zeros_like(h_scr)

    def body(t, h):
        h = a_ref[:, t] * h + b_ref[:, t]
        y_ref[:, t] = (h * gg_ref[:, t]).astype(y_ref.dtype)
        return h

    h = lax.fori_loop(0, a_ref.shape[1], body, h_scr[...], unroll=8)
    h_scr[...] = h
    hl_ref[...] = h


def _b_scan(a, b, gg, bsz, t, out_dtype):
    shp = (bsz, t, B_BLOCKS, B_BLOCK_DIM)
    blk = pl.BlockSpec((bsz, B_SCAN_T, B_BLOCKS, B_BLOCK_DIM), lambda c: (0, c, 0, 0))
    y, h_last = pl.pallas_call(
        _b_scan_kernel,
        out_shape=(jax.ShapeDtypeStruct(shp, out_dtype), jax.ShapeDtypeStruct((bsz, B_BLOCKS, B_BLOCK_DIM), jnp.float32)),
        grid=(t // B_SCAN_T,),
        in_specs=[blk] * 3,
        out_specs=(blk, pl.BlockSpec((bsz, B_BLOCKS, B_BLOCK_DIM), lambda c: (0, 0, 0))),
        scratch_shapes=[pltpu.VMEM((bsz, B_BLOCKS, B_BLOCK_DIM), jnp.float32)],
        compiler_params=_params(("arbitrary",)),
        name="b_scan",
    )(a.reshape(shp), b.reshape(shp), gg.reshape(shp))
    return y.reshape(bsz * t, B_WIDTH), h_last.reshape(bsz, B_WIDTH)


GDN_CHUNK = 128
GDN_GATE_ROWS = 512
GDN_HEADS_PER_STEP = 2
GDN_INV_PASSES = 3
DB_LANE = 3 * C_HEADS
DA_LANE = DB_LANE + D_HEADS


def _softplus(x):
    return jnp.maximum(x, 0.0) + jnp.log1p(jnp.exp(-jnp.abs(x)))


def _d_gates_kernel(us_ref, dtb_ref, aneg_ref, tri_ref, beta_ref, gam_ref):
    tm = us_ref.shape[0]
    us = us_ref[...]
    g = aneg_ref[...] * _softplus(us + dtb_ref[...])
    gam = jnp.dot(tri_ref[...], g, preferred_element_type=jnp.float32, precision=HI)
    beta = jax.nn.sigmoid(us)
    for h in range(D_HEADS):
        sl = slice(h * HEAD_DIM, (h + 1) * HEAD_DIM)
        beta_ref[:, sl] = jnp.broadcast_to(beta[:, DB_LANE + h:DB_LANE + h + 1], (tm, HEAD_DIM))
        gam_ref[:, sl] = jnp.broadcast_to(gam[:, DA_LANE + h:DA_LANE + h + 1], (tm, HEAD_DIM))


def _d_gates(u_s, a_log, dt_bias):
    p = u_s.shape[0]
    tm = GDN_GATE_ROWS
    pad = lambda x: jnp.zeros((1, LANES), jnp.float32).at[0, DA_LANE:DA_LANE + D_HEADS].set(x)
    r = np.arange(tm)
    tri = ((r[:, None] >= r[None, :]) & (r[:, None] // GDN_CHUNK == r[None, :] // GDN_CHUNK)).astype(np.float32)
    fixed = lambda i: (0, 0)
    return pl.pallas_call(
        _d_gates_kernel,
        out_shape=(jax.ShapeDtypeStruct((p, D_V), jnp.float32),) * 2,
        grid=(p // tm,),
        in_specs=[pl.BlockSpec((tm, LANES), lambda i: (i, 0)), pl.BlockSpec((1, LANES), fixed),
                  pl.BlockSpec((1, LANES), fixed), pl.BlockSpec((tm, tm), fixed)],
        out_specs=(pl.BlockSpec((tm, D_V), lambda i: (i, 0)),) * 2,
        compiler_params=_params(("parallel",)),
        name="d_gates",
    )(u_s, pad(dt_bias), pad(-jnp.exp(a_log)), jnp.asarray(tri))


def _mm(a, b, passes=1):
    ah, bh = a.astype(jnp.bfloat16), b.astype(jnp.bfloat16)
    out = jnp.dot(ah, bh, preferred_element_type=jnp.float32)
    if passes == 3:
        al = (a - ah.astype(jnp.float32)).astype(jnp.bfloat16)
        bl = (b - bh.astype(jnp.float32)).astype(jnp.bfloat16)
        out = out + jnp.dot(ah, bl, preferred_element_type=jnp.float32) + jnp.dot(al, bh, preferred_element_type=jnp.float32)
    return out


def _mm_nt(a, b):
    return lax.dot_general(a.astype(jnp.bfloat16), b.astype(jnp.bfloat16), (((1,), (1,)), ((), ())),
                           preferred_element_type=jnp.float32)


def _d_delta_kernel(q_in, k_in, v_in, z_ref, beta_ref, gam_ref, cwq_ref, cwk_ref, cwv_ref, og_ref,
                    y_ref, s_ref, q_scr, k_scr, v_scr):
    t = q_in.shape[0]
    c = GDN_CHUNK

    def conv_silu(x_ref, cw_ref):
        x = x_ref[...]
        xx = jnp.concatenate([jnp.zeros((8, x.shape[1]), x.dtype), x], axis=0)
        acc = x * cw_ref[CONV_W - 1:CONV_W, :]
        for j in range(CONV_W - 1):
            acc = acc + pltpu.roll(xx, CONV_W - 1 - j, axis=0)[8:8 + t] * cw_ref[j:j + 1, :]
        return acc * jax.nn.sigmoid(acc)

    qa = conv_silu(q_in, cwq_ref)
    ka = conv_silu(k_in, cwk_ref)
    v_scr[...] = conv_silu(v_in, cwv_ref)
    for hh in range(GDN_HEADS_PER_STEP):
        sl = slice(hh * HEAD_DIM, (hh + 1) * HEAD_DIM)
        qh, kh = qa[:, sl], ka[:, sl]
        q_scr[:, sl] = qh * lax.rsqrt(jnp.sum(qh * qh, axis=-1, keepdims=True) + EPS) * D_DK ** -0.5
        k_scr[:, sl] = kh * lax.rsqrt(jnp.sum(kh * kh, axis=-1, keepdims=True) + EPS)
    s_ref[...] = jnp.zeros_like(s_ref)

    ri = lax.broadcasted_iota(jnp.int32, (c, c), 0)
    ci = lax.broadcasted_iota(jnp.int32, (c, c), 1)
    lower, strict = ri >= ci, ri > ci
    eye = (ri == ci).astype(jnp.float32)

    def chunk(n, carry):
        r0 = pl.multiple_of(n * c, c)
        rows = pl.ds(r0, c)
        for hh in range(GDN_HEADS_PER_STEP):
            sl = slice(hh * HEAD_DIM, (hh + 1) * HEAD_DIM)
            q, k, v = q_scr[rows, sl], k_scr[rows, sl], v_scr[rows, sl]
            beta, gam = beta_ref[rows, sl], gam_ref[rows, sl]
            gam_t = jnp.transpose(gam)
            decay = jnp.where(lower, jnp.exp(jnp.minimum(gam - gam_t, 0.0)), 0.0)
            lmat = jnp.where(strict, beta * _mm_nt(k, k) * decay, 0.0)
            npow = -lmat
            inv = eye + npow
            for _ in range(c.bit_length() - 2):
                npow = _mm(npow, npow, GDN_INV_PASSES)
                inv = inv + _mm(inv, npow, GDN_INV_PASSES)
            e_gam = jnp.exp(gam)
            u = _mm(inv, v * beta, GDN_INV_PASSES)
            w = _mm(inv, k * (beta * e_gam), GDN_INV_PASSES)
            qk = jnp.where(lower, _mm_nt(q, k) * decay, 0.0)
            gam_last = gam[c - 1:c, :]
            state = s_ref[0, hh]
            v_new = u - _mm(w, state)
            o = _mm(q * e_gam, state) + _mm(qk, v_new)
            k_dec_t = jnp.transpose(k * jnp.exp(gam_last - gam))
            s_ref[0, hh] = jnp.exp(gam_last) * state + _mm(k_dec_t, v_new)
            z = z_ref[rows, sl]
            o = o * lax.rsqrt(jnp.mean(o * o, axis=-1, keepdims=True) + EPS) * og_ref[...]
            y_ref[rows, sl] = (o * (z * jax.nn.sigmoid(z))).astype(y_ref.dtype)
        return carry

    lax.fori_loop(0, t // c, chunk, 0)


def _d_delta(u_d, beta_r, gam_r, conv_w, out_gain, bsz, t, out_dtype):
    hp = GDN_HEADS_PER_STEP
    w = hp * HEAD_DIM
    nb = D_V // w
    blk = lambda sec: pl.BlockSpec((t, w), lambda b, h: (b, sec * nb + h))
    cw = lambda sec: pl.BlockSpec((CONV_W, w), lambda b, h: (0, sec * nb + h))
    return pl.pallas_call(
        _d_delta_kernel,
        out_shape=(jax.ShapeDtypeStruct((bsz * t, D_V), out_dtype),
                   jax.ShapeDtypeStruct((bsz, D_HEADS, D_DK, D_DV), jnp.float32)),
        grid=(bsz, D_HEADS // hp),
        in_specs=[blk(0), blk(1), blk(2), blk(3), blk(0), blk(0), cw(0), cw(1), cw(2),
                  pl.BlockSpec((1, HEAD_DIM), lambda b, h: (0, 0))],
        out_specs=(blk(0), pl.BlockSpec((1, hp, D_DK, D_DV), lambda b, h: (b, h, 0, 0))),
        scratch_shapes=[pltpu.VMEM((t, w), jnp.float32)] * 3,
        compiler_params=_params(("parallel", "parallel")),
        name="d_delta",
    )(u_d, u_d, u_d, u_d, beta_r, gam_r, conv_w, conv_w, conv_w, out_gain.reshape(1, HEAD_DIM))


def _rms(x, g):
    xf = x.astype(jnp.float32)
    y = xf * lax.rsqrt(jnp.mean(xf * xf, axis=-1, keepdims=True) + EPS)
    return (y * g.astype(jnp.float32)).astype(x.dtype)


def _l2norm(x):
    return x * lax.rsqrt(jnp.sum(x * x, axis=-1, keepdims=True) + EPS)


def _alibi_slopes(n):
    return 2.0 ** (-8.0 * (jnp.arange(n, dtype=jnp.float32) + 1.0) / n)


def _split_cols(u, sizes):
    cuts = [int(c) for c in np.cumsum(sizes)[:-1]]
    return jnp.split(u, cuts, axis=-1)


def _masked_stats(s, mask):
    s = jnp.where(mask, s, NEG)
    m = jnp.max(s, axis=-1, keepdims=True)
    p = jnp.where(mask, jnp.exp(s - m), 0.0)
    return p, m[..., 0], jnp.sum(p, axis=-1)


def _merge_by_denominator(stats):
    big_m = stats[0][1]
    for _, m, _ in stats[1:]:
        big_m = jnp.maximum(big_m, m)
    num, den = 0.0, 0.0
    for o, m, l in stats:
        w = jnp.exp(m - big_m)
        num = num + w[..., None] * o
        den = den + w * l
    return num / den[..., None]


def _causal_conv(x, buf, w, b=None):
    t = x.shape[1]
    xx = jnp.concatenate([buf.astype(x.dtype), x], axis=1)
    y = xx[:, CONV_W - 1:] * w[CONV_W - 1]
    for j in range(CONV_W - 1):
        y = y + xx[:, j:j + t] * w[j]
    if b is not None:
        y = y + b
    return y, xx[:, -(CONV_W - 1):]


def _dilated_attn_sample(q, kv_ctx, slopes):
    t = q.shape[1]
    wb = kv_ctx.shape[1] - t
    stats = []
    for gi, (w, d) in enumerate(A_PATTERNS):
        j = jnp.arange(w // d + 1)
        idx = wb + jnp.arange(t)[:, None] - d * j[None, :]
        kvg = kv_ctx[:, jnp.maximum(idx, 0)]
        s = jnp.einsum('bthd,btjhd->bthj', q[:, :, gi], kvg[:, :, :, 0]).astype(jnp.float32) * HEAD_DIM ** -0.5
        s = s - slopes[gi][None, None, :, None] * (d * j).astype(jnp.float32)[None, None, None, :]
        p, m, l = _masked_stats(s, (idx >= 0)[None, :, None, :])
        o = jnp.einsum('bthj,btjhd->bthd', p, kvg[:, :, :, 1].astype(jnp.float32))
        stats.append((o, m, l))
    return _merge_by_denominator(stats)


def _rglru_mixer(gate_in, x_in, conv_buf, h0, conv_w, conv_b, wa, ba, wx, bx, lam):
    xc, conv_new = _causal_conv(x_in, conv_buf, conv_w, conv_b)
    bsz, t, _ = xc.shape
    xb = xc.reshape(bsz, t, B_BLOCKS, B_BLOCK_DIM)
    r = jax.nn.sigmoid(jnp.einsum('btni,nij->btnj', xb, wa).reshape(bsz, t, B_WIDTH) + ba).astype(jnp.float32)
    ig = jax.nn.sigmoid(jnp.einsum('btni,nij->btnj', xb, wx).reshape(bsz, t, B_WIDTH) + bx).astype(jnp.float32)
    log_a = -RG_C * r * jax.nn.softplus(-lam.astype(jnp.float32))
    a = jnp.exp(log_a)
    bterm = jnp.sqrt(-jnp.expm1(2.0 * log_a)) * (ig * xc.astype(jnp.float32))
    bterm = bterm.at[:, 0].add(a[:, 0] * h0.astype(jnp.float32))

    def comb(e1, e2):
        return e1[0] * e2[0], e2[0] * e1[1] + e2[1]

    _, h = lax.associative_scan(comb, (a, bterm), axis=1)
    y = h.astype(x_in.dtype) * jax.nn.gelu(gate_in)
    return y, conv_new, h[:, -1].astype(h0.dtype)


def _gated_delta_chunked(q, k, v, g, beta, s0):
    bsz, t, h, _ = q.shape
    dv = v.shape[-1]
    n, c = t // D_CHUNK, D_CHUNK

    def chunks(x):
        return jnp.moveaxis(x.reshape(bsz, n, c, h, *x.shape[3:]), (1, 3), (0, 2))

    qc, kc, vc, gc, bc = chunks(q), chunks(k), chunks(v), chunks(g), chunks(beta)
    gam = jnp.cumsum(gc, axis=-1)
    lower = jnp.tril(jnp.ones((c, c), bool))
    strict = jnp.tril(jnp.ones((c, c), bool), -1)
    diff = gam[..., :, None] - gam[..., None, :]
    decay = jnp.where(lower, jnp.exp(jnp.where(lower, diff, 0.0)), 0.0)
    kk = jnp.einsum('nbhid,nbhjd->nbhij', kc, kc)
    a_mat = jnp.where(strict, bc[..., :, None] * kk * decay, 0.0) + jnp.eye(c, dtype=jnp.float32)
    rhs = jnp.concatenate([vc * bc[..., None], kc * (bc * jnp.exp(gam))[..., None]], axis=-1)
    sol = lax.linalg.triangular_solve(a_mat, rhs, left_side=True, lower=True)
    u, w = sol[..., :dv], sol[..., dv:]
    qk = jnp.where(lower, jnp.einsum('nbhid,nbhjd->nbhij', qc, kc) * decay, 0.0)
    q_dec = qc * jnp.exp(gam)[..., None]
    k_dec = kc * jnp.exp(gam[..., -1:] - gam)[..., None]
    last = jnp.exp(gam[..., -1])[..., None, None]

    def step(s, xs):
        u_n, w_n, q_n, k_n, qk_n, last_n = xs
        v_new = u_n - jnp.einsum('bhck,bhkv->bhcv', w_n, s)
        o = jnp.einsum('bhck,bhkv->bhcv', q_n, s) + jnp.einsum('bhij,bhjv->bhiv', qk_n, v_new)
        s = last_n * s + jnp.einsum('bhck,bhcv->bhkv', k_n, v_new)
        return s, o

    s_fin, o = lax.scan(step, s0, (u, w, q_dec, k_dec, qk, last))
    o = jnp.moveaxis(o, (0, 2), (1, 3)).reshape(bsz, t, h, dv)
    return o, s_fin


def _gdn_mixer(dq, dk, dv, dz, db, da, conv_buf, s0, conv_w, a_log, dt_bias, out_gain):
    bsz, t, _ = dq.shape
    qkv, conv_new = _causal_conv(jnp.concatenate([dq, dk, dv], axis=-1), conv_buf, conv_w)
    qkv = jax.nn.silu(qkv).astype(jnp.float32)
    q = _l2norm(qkv[..., :D_QK].reshape(bsz, t, D_HEADS, D_DK)) * D_DK ** -0.5
    k = _l2norm(qkv[..., D_QK:2 * D_QK].reshape(bsz, t, D_HEADS, D_DK))
    v = qkv[..., 2 * D_QK:].reshape(bsz, t, D_HEADS, D_DV)
    beta = jax.nn.sigmoid(db.astype(jnp.float32))
    g = -jnp.exp(a_log.astype(jnp.float32)) * jax.nn.softplus(da.astype(jnp.float32) + dt_bias.astype(jnp.float32))
    pad = (-t) % D_CHUNK

    def padt(x):
        return jnp.pad(x, [(0, 0), (0, pad)] + [(0, 0)] * (x.ndim - 2))

    o, s_fin = _gated_delta_chunked(padt(q), padt(k), padt(v), padt(g), padt(beta), s0.astype(jnp.float32))
    o = _rms(o[:, :t], out_gain) * jax.nn.silu(dz.reshape(bsz, t, D_HEADS, D_DV).astype(jnp.float32))
    return o.reshape(bsz, t, D_V).astype(dq.dtype), conv_new, s_fin.astype(s0.dtype)


def _nsa_compress(q, q_pos, kv_rows, phi_k, phi_v, slopes):
    bsz, length = kv_rows.shape[:2]
    nbc = length // C_BLOCK
    blk = kv_rows[:, :nbc * C_BLOCK].reshape(bsz, nbc, C_BLOCK, 2, C_KV_HEADS, HEAD_DIM)
    kc = jnp.einsum('bnjhd,j->bnhd', blk[:, :, :, 0], phi_k)
    vc = jnp.einsum('bnjhd,j->bnhd', blk[:, :, :, 1], phi_v)
    s = jnp.einsum('bthgd,bnhd->bthgn', q, kc).astype(jnp.float32) * HEAD_DIM ** -0.5
    nidx = jnp.arange(nbc)
    centre = nidx * C_BLOCK + (C_BLOCK - 1) / 2.0
    dist = jnp.abs(q_pos[:, None].astype(jnp.float32) - centre[None, :].astype(jnp.float32))
    s = s - slopes[None, None, :, :, None] * dist[None, :, None, None, :]
    mask = ((nidx[None, :] + 1) * C_BLOCK - 1 <= q_pos[:, None])[None, :, None, None, :]
    p, _, l = _masked_stats(s, mask)
    p = p / jnp.maximum(l, 1e-30)[..., None]
    out = jnp.einsum('bthgn,bnhd->bthgd', p, vc.astype(jnp.float32))
    return out, jnp.sum(p, axis=3)


def _nsa_select_idx(imp, q_pos, n_blocks):
    nbc = imp.shape[-1]
    imp = jnp.pad(imp, ((0, 0), (0, 0), (0, 0), (0, n_blocks - nbc)))
    nidx = jnp.arange(n_blocks)[None, :]
    cur = (q_pos // C_BLOCK)[:, None]
    forced = (nidx == 0) | (nidx == cur) | (nidx == cur - 1)
    score = imp + jnp.where(forced, FORCE, 0.0)[None, :, None, :]
    score = jnp.where((nidx <= cur)[None, :, None, :], score, NEG)
    vals, idx = lax.top_k(score, min(C_N_SEL, n_blocks))
    return idx, vals > NEG / 2


def _nsa_select_attend(q, q_pos, kv_g, idx, valid, slopes):
    bsz, t, hk, g, dh = q.shape
    n_keys = idx.shape[-1] * C_BLOCK
    kpos = idx[..., None] * C_BLOCK + jnp.arange(C_BLOCK)
    dist = q_pos[None, :, None, None, None] - kpos
    mask = (valid[..., None] & (dist >= 0)).reshape(bsz, t, hk, 1, n_keys)
    dist = dist.reshape(bsz, t, hk, 1, n_keys).astype(jnp.float32)
    kg = kv_g[..., 0, :].reshape(bsz, t, hk, n_keys, dh)
    vg = kv_g[..., 1, :].reshape(bsz, t, hk, n_keys, dh)
    s = jnp.einsum('bthgd,bthkd->bthgk', q, kg).astype(jnp.float32) * dh ** -0.5
    s = s - slopes[None, None, :, :, None] * dist
    p, _, l = _masked_stats(s, mask)
    return jnp.einsum('bthgk,bthkd->bthgd', p / l[..., None], vg.astype(jnp.float32))


def _nsa_select_sample(q, q_pos, pool, page_table, kv_new, idx, valid, slopes):
    bd, t = q.shape[:2]
    nb_past = page_table.shape[1] * (PAGE_SIZE // C_BLOCK)
    bi = jnp.arange(bd)[:, None, None, None, None]
    hi = jnp.arange(C_KV_HEADS)[None, None, :, None, None]
    off = jnp.arange(C_BLOCK)
    tok_p = jnp.minimum(idx, nb_past - 1)[..., None] * C_BLOCK + off
    page = page_table[bi, tok_p // PAGE_SIZE]
    g_past = pool[page, tok_p % PAGE_SIZE, :, hi]
    nb_new = -(-t // C_BLOCK)
    new_pad = jnp.pad(kv_new, ((0, 0), (0, nb_new * C_BLOCK - t), (0, 0), (0, 0), (0, 0)))
    tok_n = jnp.clip(idx - nb_past, 0, nb_new - 1)[..., None] * C_BLOCK + off
    g_new = new_pad[bi, tok_n, :, hi]
    kv_g = jnp.where((idx < nb_past)[..., None, None, None], g_past, g_new.astype(g_past.dtype))
    return _nsa_select_attend(q, q_pos, kv_g, idx, valid, slopes)


def _window_attn_sample(q, q_pos, kv_ctx, slopes):
    n = kv_ctx.shape[1]
    k_pos = q_pos[-1] - (n - 1) + jnp.arange(n)
    dist = q_pos[:, None] - k_pos[None, :]
    mask = (dist >= 0) & (dist <= C_WIN)
    s = jnp.einsum('bthgd,bshd->bthgs', q, kv_ctx[:, :, 0]).astype(jnp.float32) * HEAD_DIM ** -0.5
    s = s - slopes[None, None, :, :, None] * dist.astype(jnp.float32)[None, :, None, None, :]
    p, _, l = _masked_stats(s, mask[None, :, None, None, :])
    return jnp.einsum('bthgs,bshd->bthgd', p / l[..., None], kv_ctx[:, :, 1].astype(jnp.float32))


def _even_mixer(u2d, w_out3d, layer, res2d, q_gain, k_gain, b_par, kv_buf, conv_buf, h0, bsz, t):
    if kv_buf is None:
        qn, akv = _a_prep(u2d, q_gain, k_gain)
        a_out = _a_attn_prompt(qn, akv, bsz, t, jnp.bfloat16)
        a_state = akv.reshape(bsz, t, 2, A_HEADS, HEAD_DIM)[:, -min(A_WIN_MAX, t):]
        a_dec, b_in, gg = _b_gates(u2d, b_par, bsz, t)
        b_out, h_new = _b_scan(a_dec, b_in, gg, bsz, t, jnp.float32)
        conv_new = u2d.reshape(bsz, t, -1)[:, -(CONV_W - 1):, -B_WIDTH:]
        return _matmul([a_out, b_out], w_out3d, layer, 0, D_MODEL, False, res=res2d), (a_state, conv_new, h_new)
    u = u2d.reshape(bsz, t, -1)
    u_q, u_k, u_v, u_gate, u_x = _split_cols(u, (A_Q, A_KV, A_KV, B_WIDTH, B_WIDTH))
    q = _rms(u_q.reshape(bsz, t, A_N_GROUPS, A_HEADS, HEAD_DIM), q_gain)
    k = _rms(u_k.reshape(bsz, t, A_HEADS, HEAD_DIM), k_gain)
    v = u_v.reshape(bsz, t, A_HEADS, HEAD_DIM)
    kv_new = jnp.stack([k, v], axis=2)
    slopes = _alibi_slopes(A_N_GROUPS * A_HEADS).reshape(A_N_GROUPS, A_HEADS)
    a_out = _dilated_attn_sample(q, jnp.concatenate([kv_buf.astype(kv_new.dtype), kv_new], axis=1), slopes)
    b_out, conv_new, h_new = _rglru_mixer(u_gate, u_x, conv_buf, h0, *b_par)
    parts = [a_out.reshape(bsz * t, A_OUT), b_out.reshape(bsz * t, B_WIDTH)]
    return _matmul(parts, w_out3d, layer, 0, D_MODEL, True, res=res2d), (kv_new, conv_new, h_new)


def _odd_mixer(u_parts, w_out3d, layer, res2d, q_gain, k_gain, phi_k, phi_v, d_par,
               cmp_pool, sel_pool, page_table, win_buf, d_conv_buf, d_s0, bsz, t):
    u_c2d, u_d2d, u_s2d = u_parts
    u_d, u_s = u_d2d.reshape(bsz, t, -1), u_s2d.reshape(bsz, t, -1)
    dq, dk, dv, dz = _split_cols(u_d, (D_QK, D_QK, D_V, D_V))
    cg, db, da = _split_cols(u_s[..., :3 * C_HEADS + 2 * D_HEADS], (3 * C_HEADS, D_HEADS, D_HEADS))
    kv_shape = (bsz, t, 2, C_KV_HEADS, HEAD_DIM)
    if page_table is None:
        qn, cmp2d, sel2d, win2d, gates = _nsa_prep(u_c2d, u_s2d, q_gain, k_gain)
        kc, vc = _nsa_summaries(cmp2d, phi_k, phi_v)
        o_c = _nsa_prompt(qn, kc, vc, sel2d, win2d, gates, bsz, t, jnp.bfloat16)
        cmp_rows, sel_rows = cmp2d.reshape(kv_shape), sel2d.reshape(kv_shape)
        win_state = win2d.reshape(kv_shape)[:, -min(C_WIN, t):]
    else:
        u_c = u_c2d.reshape(bsz, t, -1)
        cq, ckv = _split_cols(u_c, (C_Q, 6 * C_KV))
        q = _rms(cq.reshape(bsz, t, C_KV_HEADS, C_GROUP, HEAD_DIM), q_gain)
        kv = ckv.reshape(bsz, t, 3, 2, C_KV_HEADS, HEAD_DIM)
        kv = jnp.stack([_rms(kv[:, :, :, 0], k_gain[:, None, :]), kv[:, :, :, 1]], axis=3)
        cmp_rows, sel_rows, win_rows = kv[:, :, 0], kv[:, :, 1], kv[:, :, 2]
        slopes = _alibi_slopes(C_HEADS).reshape(C_KV_HEADS, C_GROUP)
        past = page_table.shape[1] * PAGE_SIZE
        q_pos = past + jnp.arange(t)
        cmp_past = cmp_pool[page_table].reshape(bsz, past, 2, C_KV_HEADS, HEAD_DIM)
        cmp_ctx = jnp.concatenate([cmp_past, cmp_rows.astype(cmp_past.dtype)], axis=1)
        o_cmp, imp = _nsa_compress(q, q_pos, cmp_ctx, phi_k, phi_v, slopes)
        idx, valid = _nsa_select_idx(imp, q_pos, -(-(past + t) // C_BLOCK))
        o_sel = _nsa_select_sample(q, q_pos, sel_pool, page_table, sel_rows, idx, valid, slopes)
        o_win = _window_attn_sample(q, q_pos, jnp.concatenate([win_buf.astype(win_rows.dtype), win_rows], axis=1), slopes)
        win_state = win_rows
        gate = jax.nn.sigmoid(cg.reshape(bsz, t, 3, C_KV_HEADS, C_GROUP, 1).astype(jnp.float32))
        o_c = (gate[:, :, 0] * o_cmp + gate[:, :, 1] * o_sel + gate[:, :, 2] * o_win).reshape(bsz * t, C_Q)
    if page_table is None:
        conv_w, a_log, dt_bias, out_gain = d_par
        beta_r, gam_r = _d_gates(u_s2d, a_log, dt_bias)
        d_out, d_s = _d_delta(u_d2d, beta_r, gam_r, conv_w, out_gain, bsz, t, jnp.bfloat16)
        d_conv_new = u_d[:, -(CONV_W - 1):, :D_CONV]
    else:
        d_out, d_conv_new, d_s = _gdn_mixer(dq, dk, dv, dz, db, da, d_conv_buf, d_s0, *d_par)
        d_out = d_out.reshape(bsz * t, D_V)
    return (_matmul([o_c, d_out], w_out3d, layer, 0, D_MODEL, t == 1, res=res2d),
            (cmp_rows, sel_rows, win_state, d_conv_new, d_s))


ODD_C = C_Q + 6 * C_KV
ODD_G0 = ODD_C
ODD_D0 = ODD_G0 + 3 * C_HEADS
ODD_D = 2 * D_QK + 2 * D_V
ODD_S0 = ODD_D0 + ODD_D
ODD_IN = ODD_S0 + 2 * D_HEADS


def kernel(x_prompt, x_sample, cache_a_kv, state_b_conv, state_b_h, cache_c_cmp_kv, cache_c_sel_kv, cache_c_win_kv, state_d_conv, state_d_S, page_table, norm_mix, norm_ffn, even_w_in, even_w_out, a_q_norm, a_k_norm, b_conv_w, b_conv_b, b_gate_a_w, b_gate_a_b, b_gate_x_w, b_gate_x_b, b_lambda, odd_w_in, odd_w_out, c_q_norm, c_k_norm, c_phi_k, c_phi_v, d_conv_w, d_a_log, d_dt_bias, d_out_norm, moe_group_w, moe_group_b, moe_expert_w, moe_expert_b, moe_w1, moe_w3, moe_w2):
    bp, sp, d = x_prompt.shape
    bs, ss, _ = x_sample.shape
    depth = norm_mix.shape[0]
    hp = x_prompt.reshape(bp * sp, d)
    hs = x_sample.reshape(bs * ss, d)
    outs = {k: [] for k in ("ak", "bc", "bh", "cc", "cs", "cw", "dc", "ds")}
    outs_s = {k: [] for k in outs}
    for l in range(depth):
        i = l // 2
        xp = _rmsnorm(hp, norm_mix[l], jnp.bfloat16)
        xs = _rmsnorm(hs, norm_mix[l], jnp.float32)
        if l % 2 == 0:
            b_par = (b_conv_w[i], b_conv_b[i], b_gate_a_w[i], b_gate_a_b[i], b_gate_x_w[i], b_gate_x_b[i], b_lambda[i])
            n_in = even_w_in.shape[-1]
            up = _matmul(xp, even_w_in, i, 0, n_in, False)
            us = _matmul(xs, even_w_in, i, 0, n_in, True)
            hp, st_p = _even_mixer(up, even_w_out, i, hp, a_q_norm[i], a_k_norm[i], b_par, None,
                                   jnp.zeros((bp, CONV_W - 1, B_WIDTH), jnp.float32),
                                   jnp.zeros((bp, B_WIDTH), jnp.float32), bp, sp)
            with jax.default_matmul_precision("float32"):
                hs, st_s = _even_mixer(us, even_w_out, i, hs, a_q_norm[i], a_k_norm[i], b_par,
                                       cache_a_kv[i], state_b_conv[i], state_b_h[i], bs, ss)
            for dst, st in ((outs, st_p), (outs_s, st_s)):
                dst["ak"].append(st[0]); dst["bc"].append(st[1]); dst["bh"].append(st[2])
        else:
            d_par = (d_conv_w[i], d_a_log[i], d_dt_bias[i], d_out_norm[i])
            w_d = odd_w_in[:, :, ODD_D0:ODD_S0]
            w_s = jnp.concatenate([odd_w_in[:, :, ODD_G0:ODD_D0], odd_w_in[:, :, ODD_S0:],
                                   jnp.zeros((odd_w_in.shape[0], d, LANES - 3 * C_HEADS - 2 * D_HEADS), jnp.float32)], axis=-1)
            ups = (_matmul(xp, odd_w_in, i, 0, ODD_C, False), _matmul(xp, w_d, i, 0, ODD_D, False), _matmul(xp, w_s, i, 0, LANES, False))
            uss = (_matmul(xs, odd_w_in, i, 0, ODD_C, True), _matmul(xs, w_d, i, 0, ODD_D, True), _matmul(xs, w_s, i, 0, LANES, True))
            hp, st_p = _odd_mixer(ups, odd_w_out, i, hp, c_q_norm[i], c_k_norm[i], c_phi_k[i], c_phi_v[i], d_par,
                                  None, None, None, None,
                                  jnp.zeros((bp, CONV_W - 1, D_CONV), jnp.float32),
                                  jnp.zeros((bp, D_HEADS, D_DK, D_DV), jnp.float32), bp, sp)
            with jax.default_matmul_precision("float32"):
                hs, st_s = _odd_mixer(uss, odd_w_out, i, hs, c_q_norm[i], c_k_norm[i], c_phi_k[i], c_phi_v[i], d_par,
                                      cache_c_cmp_kv[i], cache_c_sel_kv[i], page_table, cache_c_win_kv[i],
                                      state_d_conv[i], state_d_S[i], bs, ss)
            for dst, st in ((outs, st_p), (outs_s, st_s)):
                dst["cc"].append(st[0]); dst["cs"].append(st[1]); dst["cw"].append(st[2])
                dst["dc"].append(st[3]); dst["ds"].append(st[4])
        w_router = jnp.concatenate([moe_group_w[l], moe_expert_w[l],
                                    jnp.zeros((d, ROUTER_PAD - N_GROUPS - N_EXPERTS), jnp.float32)], axis=-1)
        b_router = jnp.concatenate([moe_group_b[l], moe_expert_b[l],
                                    jnp.zeros((ROUTER_PAD - N_GROUPS - N_EXPERTS,), jnp.float32)])[None, :]
        hp = _hier_moe(hp, norm_ffn[l], w_router, b_router, moe_w1, moe_w3, moe_w2, l, 256, jnp.bfloat16)
        hs = _hier_moe(hs, norm_ffn[l], w_router, b_router, moe_w1, moe_w3, moe_w2, l, 8, jnp.float32)
    res = [hp.reshape(bp, sp, d), hs.reshape(bs, ss, d)]
    for key in ("ak", "bc", "bh", "cc", "cs", "cw", "dc", "ds"):
        res.append(jnp.stack(outs[key]))
        res.append(jnp.stack(outs_s[key]))
    return tuple(res)
```

```python
import functools
import math

import jax
import jax.numpy as jnp
import numpy as np
from jax import lax
from jax.experimental import pallas as pl
from jax.experimental.pallas import tpu as pltpu

D_MODEL = 4096
HEAD_DIM = 128
CONV_W = 4
BAND_BLOCK = 128
A_PATTERNS = ((128, 1), (512, 4), (2048, 16))
A_N_GROUPS = len(A_PATTERNS)
A_HEADS = D_MODEL // 512
A_WIN_MAX = max(w for w, _ in A_PATTERNS)
A_Q = A_N_GROUPS * A_HEADS * HEAD_DIM
A_KV = A_HEADS * HEAD_DIM
A_OUT = A_HEADS * HEAD_DIM
B_WIDTH = 3 * D_MODEL // 4
B_BLOCKS = B_WIDTH // HEAD_DIM
B_BLOCK_DIM = B_WIDTH // B_BLOCKS
RG_C = 8.0
C_HEADS = D_MODEL // 256
C_KV_HEADS = C_HEADS // 4
C_GROUP = C_HEADS // C_KV_HEADS
C_BLOCK = 64
C_N_SEL = 16
C_WIN = 512
C_Q = C_HEADS * HEAD_DIM
C_KV = C_KV_HEADS * HEAD_DIM
D_HEADS = D_MODEL // 256
D_DK = HEAD_DIM
D_DV = HEAD_DIM
D_QK = D_HEADS * D_DK
D_V = D_HEADS * D_DV
D_CONV = 2 * D_QK + D_V
N_GROUPS = 8
EXPERTS_PER_GROUP = 8
N_EXPERTS = N_GROUPS * EXPERTS_PER_GROUP
TOP_K = 2
D_EXPERT = D_MODEL // 8
PAGE_SIZE = 128
EPS = 1e-6
NEG = -1e30
FORCE = 1e4

LANES = 128
VMEM_LIMIT = 56 * 1024 * 1024
ROUTER_PAD = LANES

HI = lax.Precision.HIGHEST


def _params(sem):
    return pltpu.CompilerParams(dimension_semantics=sem, vmem_limit_bytes=VMEM_LIMIT)


def _rmsnorm_kernel(x_ref, g_ref, o_ref):
    x = x_ref[...]
    y = x * lax.rsqrt(jnp.mean(x * x, axis=-1, keepdims=True) + EPS)
    o_ref[...] = (y * g_ref[...]).astype(o_ref.dtype)


def _rmsnorm(x2d, gain, out_dtype):
    m, d = x2d.shape
    tm = min(m, 512)
    return pl.pallas_call(
        _rmsnorm_kernel,
        out_shape=jax.ShapeDtypeStruct((m, d), out_dtype),
        grid=(m // tm,),
        in_specs=[pl.BlockSpec((tm, d), lambda i: (i, 0)), pl.BlockSpec((1, d), lambda i: (0, 0))],
        out_specs=pl.BlockSpec((tm, d), lambda i: (i, 0)),
        compiler_params=_params(("parallel",)),
        name="rmsnorm",
    )(x2d, gain.reshape(1, d))


def _matmul_kernel(*refs, exact, has_res, k_bounds):
    n_x = len(k_bounds) - 1
    x_refs, w_ref = refs[:n_x], refs[n_x]
    if has_res:
        r_ref, o_ref, acc_ref = refs[n_x + 1:]
    else:
        o_ref, acc_ref = refs[n_x + 1:]
    k = pl.program_id(2)

    @pl.when(k == 0)
    def _():
        acc_ref[...] = jnp.zeros_like(acc_ref)

    for p, x_ref in enumerate(x_refs):
        @pl.when((k >= k_bounds[p]) & (k < k_bounds[p + 1]))
        def _(x_ref=x_ref):
            if exact:
                acc_ref[...] += jnp.dot(x_ref[...], w_ref[...], preferred_element_type=jnp.float32, precision=HI)
            else:
                acc_ref[...] += jnp.dot(x_ref[...].astype(jnp.bfloat16), w_ref[...].astype(jnp.bfloat16),
                                        preferred_element_type=jnp.float32)

    @pl.when(k == pl.num_programs(2) - 1)
    def _():
        out = acc_ref[...]
        if has_res:
            out = out + r_ref[...]
        o_ref[...] = out


def _matmul(xs, w3d, layer, col0, n, exact, res=None, tn=1024, tk=512):
    if not isinstance(xs, (list, tuple)):
        xs = [xs]
    m = xs[0].shape[0]
    tm = min(m, 1024)
    tn = min(tn, n)
    assert m % tm == 0 and n % tn == 0 and col0 % tn == 0 and all(x.shape[1] % tk == 0 for x in xs)
    jb = col0 // tn
    k_bounds = [0]
    for x in xs:
        k_bounds.append(k_bounds[-1] + x.shape[1] // tk)
    in_specs = []
    for p in range(len(xs)):
        lo, hi = k_bounds[p], k_bounds[p + 1]
        in_specs.append(pl.BlockSpec((tm, tk), lambda i, j, k, lo=lo, hi=hi: (i, jnp.clip(k, lo, hi - 1) - lo)))
    in_specs.append(pl.BlockSpec((None, tk, tn), lambda i, j, k: (layer, k, j + jb)))
    args = list(xs) + [w3d]
    if res is not None:
        in_specs.append(pl.BlockSpec((tm, tn), lambda i, j, k: (i, j)))
        args.append(res)
    return pl.pallas_call(
        functools.partial(_matmul_kernel, exact=exact, has_res=res is not None, k_bounds=tuple(k_bounds)),
        out_shape=jax.ShapeDtypeStruct((m, n), jnp.float32),
        grid=(m // tm, n // tn, k_bounds[-1]),
        in_specs=in_specs,
        out_specs=pl.BlockSpec((tm, tn), lambda i, j, k: (i, j)),
        scratch_shapes=[pltpu.VMEM((tm, tn), jnp.float32)],
        compiler_params=_params(("parallel", "parallel", "arbitrary")),
        name="proj",
    )(*args)


def _router_kernel(h_ref, g_ref, w_ref, b_ref, xn_ref, logit_ref):
    x = h_ref[...]
    y = x * lax.rsqrt(jnp.mean(x * x, axis=-1, keepdims=True) + EPS) * g_ref[...]
    xn_ref[...] = y.astype(xn_ref.dtype)
    logit_ref[...] = jnp.dot(y, w_ref[...], preferred_element_type=jnp.float32, precision=HI) + b_ref[...]


def _router(h2d, gain, w_router, b_router, xn_dtype):
    m, d = h2d.shape
    tm = min(m, 256)
    return pl.pallas_call(
        _router_kernel,
        out_shape=(jax.ShapeDtypeStruct((m, d), xn_dtype), jax.ShapeDtypeStruct((m, ROUTER_PAD), jnp.float32)),
        grid=(m // tm,),
        in_specs=[pl.BlockSpec((tm, d), lambda i: (i, 0)), pl.BlockSpec((1, d), lambda i: (0, 0)),
                  pl.BlockSpec((d, ROUTER_PAD), lambda i: (0, 0)), pl.BlockSpec((1, ROUTER_PAD), lambda i: (0, 0))],
        out_specs=(pl.BlockSpec((tm, d), lambda i: (i, 0)), pl.BlockSpec((tm, ROUTER_PAD), lambda i: (i, 0))),
        compiler_params=_params(("parallel",)),
        name="ffn_norm_router",
    )(h2d, gain.reshape(1, d), w_router, b_router)


MOE_K_CHUNK = 512


def _moe_up_kernel(be_ref, nu_ref, x_ref, w1_ref, w3_ref, o_ref, *, exact):
    tm = x_ref.shape[0]
    used = pl.program_id(0) < nu_ref[0]

    @pl.when(used)
    def _():
        a = jnp.zeros((tm, D_EXPERT), jnp.float32)
        b = jnp.zeros((tm, D_EXPERT), jnp.float32)
        for c in range(D_MODEL // MOE_K_CHUNK):
            sl = slice(c * MOE_K_CHUNK, (c + 1) * MOE_K_CHUNK)
            if exact:
                a += jnp.dot(x_ref[:, sl], w1_ref[sl, :], preferred_element_type=jnp.float32, precision=HI)
                b += jnp.dot(x_ref[:, sl], w3_ref[sl, :], preferred_element_type=jnp.float32, precision=HI)
            else:
                a += jnp.dot(x_ref[:, sl], w1_ref[sl, :].astype(jnp.bfloat16), preferred_element_type=jnp.float32)
                b += jnp.dot(x_ref[:, sl], w3_ref[sl, :].astype(jnp.bfloat16), preferred_element_type=jnp.float32)
        o_ref[...] = (a * jax.nn.sigmoid(a) * b).astype(o_ref.dtype)

    @pl.when(jnp.logical_not(used))
    def _():
        o_ref[...] = jnp.zeros_like(o_ref)


def _moe_down_kernel(be_ref, nu_ref, h_ref, w2_ref, g_ref, o_ref, *, exact):
    used = pl.program_id(0) < nu_ref[0]

    @pl.when(used)
    def _():
        if exact:
            out = jnp.dot(h_ref[...], w2_ref[...], preferred_element_type=jnp.float32, precision=HI)
        else:
            out = jnp.dot(h_ref[...], w2_ref[...].astype(jnp.bfloat16), preferred_element_type=jnp.float32)
        o_ref[...] = out * g_ref[...]

    @pl.when(jnp.logical_not(used))
    def _():
        o_ref[...] = jnp.zeros_like(o_ref)


def _moe_experts(x_rows, row_gate, blk_exp, n_used, w1, w3, w2, layer, tm):
    rows, d = x_rows.shape
    n_blk = rows // tm
    exact = x_rows.dtype == jnp.float32
    up = pl.pallas_call(
        functools.partial(_moe_up_kernel, exact=exact),
        out_shape=jax.ShapeDtypeStruct((rows, D_EXPERT), x_rows.dtype),
        grid_spec=pltpu.PrefetchScalarGridSpec(
            num_scalar_prefetch=2, grid=(n_blk,),
            in_specs=[pl.BlockSpec((tm, d), lambda i, be, nu: (jnp.minimum(i, nu[0] - 1), 0)),
                      pl.BlockSpec((None, None, d, D_EXPERT), lambda i, be, nu: (layer, be[i], 0, 0)),
                      pl.BlockSpec((None, None, d, D_EXPERT), lambda i, be, nu: (layer, be[i], 0, 0))],
            out_specs=pl.BlockSpec((tm, D_EXPERT), lambda i, be, nu: (i, 0))),
        compiler_params=_params(("arbitrary",)),
        name="moe_up",
    )(blk_exp, n_used, x_rows, w1, w3)
    return pl.pallas_call(
        functools.partial(_moe_down_kernel, exact=exact),
        out_shape=jax.ShapeDtypeStruct((rows, d), jnp.float32),
        grid_spec=pltpu.PrefetchScalarGridSpec(
            num_scalar_prefetch=2, grid=(n_blk,),
            in_specs=[pl.BlockSpec((tm, D_EXPERT), lambda i, be, nu: (jnp.minimum(i, nu[0] - 1), 0)),
                      pl.BlockSpec((None, None, D_EXPERT, d), lambda i, be, nu: (layer, be[i], 0, 0)),
                      pl.BlockSpec((tm, 1), lambda i, be, nu: (jnp.minimum(i, nu[0] - 1), 0))],
            out_specs=pl.BlockSpec((tm, d), lambda i, be, nu: (i, 0))),
        compiler_params=_params(("arbitrary",)),
        name="moe_down",
    )(blk_exp, n_used, up, w2, row_gate)


def _hier_moe(h2d, gain, w_router, b_router, w1, w3, w2, layer, tm, xn_dtype):
    n_tok, d = h2d.shape
    xn, logits = _router(h2d, gain, w_router, b_router, xn_dtype)
    g_logit = logits[:, :N_GROUPS]
    g_prob = jax.nn.softmax(g_logit, axis=-1)
    grp = jnp.argmax(g_logit, axis=-1)
    p_grp = jnp.take_along_axis(g_prob, grp[:, None], axis=1)[:, 0]
    e_logit = logits[:, N_GROUPS:N_GROUPS + N_EXPERTS].reshape(-1, N_GROUPS, EXPERTS_PER_GROUP)
    e_logit = jnp.take_along_axis(e_logit, grp[:, None, None], axis=1)[:, 0]
    e_val, e_idx = lax.top_k(e_logit, TOP_K)
    gates = p_grp[:, None] * jax.nn.softmax(e_val, axis=-1)
    experts = grp[:, None] * EXPERTS_PER_GROUP + e_idx
    n_asg = n_tok * TOP_K
    flat_e = experts.reshape(-1).astype(jnp.int32)
    order = jnp.argsort(flat_e)
    sorted_e = flat_e[order]
    counts = jnp.bincount(flat_e, length=N_EXPERTS)
    padded = (counts + tm - 1) // tm * tm
    pad_end = jnp.cumsum(padded)
    pad_start = pad_end - padded
    start = jnp.cumsum(counts) - counts
    dest = (pad_start[sorted_e] + jnp.arange(n_asg) - start[sorted_e]).astype(jnp.int32)
    n_rows = -(-(n_asg + N_EXPERTS * (tm - 1)) // tm) * tm
    n_blk = n_rows // tm
    tok = (order // TOP_K).astype(jnp.int32)
    row_tok = jnp.zeros((n_rows,), jnp.int32).at[dest].set(tok)
    row_gate = jnp.zeros((n_rows,), jnp.float32).at[dest].set(gates.reshape(-1)[order])
    blk_exp = jnp.minimum(jnp.searchsorted(pad_end, jnp.arange(n_blk) * tm, side='right'),
                          N_EXPERTS - 1).astype(jnp.int32)
    x_rows = xn[row_tok]
    n_used = (pad_end[-1:] // tm).astype(jnp.int32)
    out = _moe_experts(x_rows, row_gate[:, None], blk_exp, n_used, w1, w3, w2, layer, tm)
    asg_row = jnp.zeros((n_asg,), jnp.int32).at[order].set(dest)
    y = out[asg_row].reshape(n_tok, TOP_K, d).sum(axis=1)
    return h2d + y


NSA_PREP_ROWS = 256
NSA_SUM_ROWS = 512
NSA_TQ = 128
NSA_GATE_LANES = LANES
TQ_SHIFT = NSA_TQ.bit_length() - 1
C_BLOCK_SHIFT = C_BLOCK.bit_length() - 1


def _chunk_rms(x, gain):
    return x * lax.rsqrt(jnp.mean(x * x, axis=-1, keepdims=True) + EPS) * gain


def _nsa_prep_kernel(uc_ref, us_ref, qg_ref, kg_ref, perm_ref, q_ref, cmp_ref, sel_ref, win_ref, gate_ref):
    for c in range(C_Q // HEAD_DIM):
        sl = slice(c * HEAD_DIM, (c + 1) * HEAD_DIM)
        q_ref[:, sl] = _chunk_rms(uc_ref[:, sl], qg_ref[...])
    for br, o_ref in enumerate((cmp_ref, sel_ref, win_ref)):
        base = C_Q + br * 2 * C_KV
        for c in range(C_KV_HEADS):
            sl = slice(c * HEAD_DIM, (c + 1) * HEAD_DIM)
            o_ref[:, sl] = _chunk_rms(uc_ref[:, base + c * HEAD_DIM: base + (c + 1) * HEAD_DIM], kg_ref[br:br + 1, :])
        o_ref[:, C_KV:] = uc_ref[:, base + C_KV: base + 2 * C_KV]
    gate_ref[...] = jnp.dot(jax.nn.sigmoid(us_ref[...]), perm_ref[...], preferred_element_type=jnp.float32, precision=HI)


def _nsa_prep(u_c, u_s, q_gain, k_gain):
    p = u_c.shape[0]
    tm = min(p, NSA_PREP_ROWS)
    perm = np.zeros((LANES, C_KV_HEADS * NSA_GATE_LANES), np.float32)
    for br in range(3):
        for h in range(C_KV_HEADS):
            for g in range(C_GROUP):
                perm[br * C_HEADS + h * C_GROUP + g, h * NSA_GATE_LANES + br * C_GROUP + g] = 1.0
    row = lambda i: (i, 0)
    fixed = lambda i: (0, 0)
    return pl.pallas_call(
        _nsa_prep_kernel,
        out_shape=(jax.ShapeDtypeStruct((p, C_Q), jnp.float32),) + (jax.ShapeDtypeStruct((p, 2 * C_KV), jnp.float32),) * 3
        + (jax.ShapeDtypeStruct((p, C_KV_HEADS * NSA_GATE_LANES), jnp.float32),),
        grid=(p // tm,),
        in_specs=[pl.BlockSpec((tm, ODD_C), row), pl.BlockSpec((tm, LANES), row), pl.BlockSpec((1, HEAD_DIM), fixed),
                  pl.BlockSpec((3, HEAD_DIM), fixed), pl.BlockSpec(perm.shape, fixed)],
        out_specs=(pl.BlockSpec((tm, C_Q), row),) + (pl.BlockSpec((tm, 2 * C_KV), row),) * 3
        + (pl.BlockSpec((tm, C_KV_HEADS * NSA_GATE_LANES), row),),
        compiler_params=_params(("parallel",)),
        name="nsa_prep",
    )(u_c, u_s, q_gain.reshape(1, HEAD_DIM), k_gain, jnp.asarray(perm))


def _nsa_sum_kernel(cmp_ref, phik_ref, phiv_ref, kc_ref, vc_ref):
    kc_ref[...] = jnp.dot(phik_ref[...], cmp_ref[:, :C_KV], preferred_element_type=jnp.float32, precision=HI)
    vc_ref[...] = jnp.dot(phiv_ref[...], cmp_ref[:, C_KV:], preferred_element_type=jnp.float32, precision=HI)


def _nsa_summaries(cmp_rows, phi_k, phi_v):
    p = cmp_rows.shape[0]
    tm = NSA_SUM_ROWS
    nb = tm // C_BLOCK
    eye = jnp.eye(nb, dtype=jnp.float32)
    big_k = jnp.kron(eye, phi_k[None, :])
    big_v = jnp.kron(eye, phi_v[None, :])
    return pl.pallas_call(
        _nsa_sum_kernel,
        out_shape=(jax.ShapeDtypeStruct((p // C_BLOCK, C_KV), jnp.float32),) * 2,
        grid=(p // tm,),
        in_specs=[pl.BlockSpec((tm, 2 * C_KV), lambda i: (i, 0)), pl.BlockSpec((nb, tm), lambda i: (0, 0)),
                  pl.BlockSpec((nb, tm), lambda i: (0, 0))],
        out_specs=(pl.BlockSpec((nb, C_KV), lambda i: (i, 0)),) * 2,
        compiler_params=_params(("parallel",)),
        name="nsa_summaries",
    )(cmp_rows, big_k, big_v)


def _nsa_prompt_kernel(q_ref, kc_ref, vc_ref, sk_ref, sv_ref, wk_ref, wv_ref, gate_ref, o_ref,
                       m_scr, l_scr, acc_scr, *, n_blocks):
    h = pl.program_id(1)
    qb = pl.program_id(2)
    tq = NSA_TQ
    rows = C_GROUP * tq
    scale = HEAD_DIM ** -0.5
    q = jnp.concatenate([q_ref[:, g * HEAD_DIM:(g + 1) * HEAD_DIM] for g in range(C_GROUP)], axis=0)
    row = lax.broadcasted_iota(jnp.int32, (rows, 1), 0)
    t_row = qb * tq + (row & (tq - 1))
    slope = jnp.exp2(-0.5 * (h * C_GROUP + (row >> TQ_SHIFT) + 1).astype(jnp.float32))

    nidx = lax.broadcasted_iota(jnp.int32, (1, n_blocks), 1)
    s = lax.dot_general(q, kc_ref[...], (((1,), (1,)), ((), ())), preferred_element_type=jnp.float32, precision=HI) * scale
    centre = nidx.astype(jnp.float32) * C_BLOCK + (C_BLOCK - 1) / 2.0
    s = s - slope * jnp.abs(t_row.astype(jnp.float32) - centre)
    cmask = (nidx + 1) * C_BLOCK - 1 <= t_row
    s = jnp.where(cmask, s, NEG)
    m = jnp.max(s, axis=-1, keepdims=True)
    p = jnp.where(cmask, jnp.exp(s - m), 0.0)
    l = jnp.sum(p, axis=-1, keepdims=True)
    p = p / jnp.maximum(l, 1e-30)
    o_cmp = jnp.dot(p, vc_ref[...], preferred_element_type=jnp.float32, precision=HI)
    imp = p[0:tq]
    for g in range(1, C_GROUP):
        imp = imp + p[g * tq:(g + 1) * tq]

    t_q = t_row[0:tq]
    cur = t_q >> C_BLOCK_SHIFT
    forced = (nidx == 0) | (nidx == cur) | (nidx == cur - 1)
    causal = nidx <= cur
    score = jnp.where(causal, imp + jnp.where(forced, FORCE, 0.0), NEG)
    rank = jnp.zeros((tq, n_blocks), jnp.int32)
    for mcol in range(n_blocks):
        cm = score[:, mcol:mcol + 1]
        ahead = (cm > score) | ((cm == score) & (mcol < nidx))
        rank = rank + ahead.astype(jnp.int32)
    selm = ((rank < min(C_N_SEL, n_blocks)) & causal).astype(jnp.bfloat16)

    q16 = q.astype(jnp.bfloat16)
    kcol = lax.broadcasted_iota(jnp.int32, (1, tq), 1)

    def attend(k_ref, v_ref, c, mask):
        start = pl.multiple_of(c * tq, tq)
        k = k_ref[pl.ds(start, tq), :].astype(jnp.bfloat16)
        v = v_ref[pl.ds(start, tq), :].astype(jnp.bfloat16)
        dist = t_row - (c * tq + kcol)
        sc = lax.dot_general(q16, k, (((1,), (1,)), ((), ())), preferred_element_type=jnp.float32) * scale
        sc = sc - slope * dist.astype(jnp.float32)
        ok = mask(dist)
        sc = jnp.where(ok, sc, NEG)
        m_old = m_scr[...]
        m_new = jnp.maximum(m_old, jnp.max(sc, axis=-1, keepdims=True))
        pc = jnp.where(ok, jnp.exp(sc - m_new), 0.0)
        alpha = jnp.exp(m_old - m_new)
        l_scr[...] = alpha * l_scr[...] + jnp.sum(pc, axis=-1, keepdims=True)
        acc_scr[...] = alpha * acc_scr[...] + jnp.dot(pc.astype(jnp.bfloat16), v, preferred_element_type=jnp.float32)
        m_scr[...] = m_new

    def reset():
        m_scr[...] = jnp.full_like(m_scr, NEG)
        l_scr[...] = jnp.zeros_like(l_scr)
        acc_scr[...] = jnp.zeros_like(acc_scr)

    reset()
    brow = lax.broadcasted_iota(jnp.int32, (n_blocks, tq), 0)
    bcol = lax.broadcasted_iota(jnp.int32, (n_blocks, tq), 1)

    def sel_body(c, carry):
        expand = (brow == c * (tq // C_BLOCK) + (bcol >> C_BLOCK_SHIFT)).astype(jnp.bfloat16)
        sel_keys = jnp.dot(selm, expand, preferred_element_type=jnp.float32)
        sel_keys = jnp.concatenate([sel_keys] * C_GROUP, axis=0) > 0.5
        attend(sk_ref, sv_ref, c, lambda dist: sel_keys & (dist >= 0))
        return carry

    lax.fori_loop(0, qb + 1, sel_body, 0)
    o_sel = acc_scr[...] / l_scr[...]

    reset()

    def win_body(c, carry):
        attend(wk_ref, wv_ref, c, lambda dist: (dist >= 0) & (dist <= C_WIN))
        return carry

    lax.fori_loop(jnp.maximum(qb - C_WIN // tq, 0), qb + 1, win_body, 0)
    o_win = acc_scr[...] / l_scr[...]

    gt = gate_ref[...]
    outs = []
    for g in range(C_GROUP):
        rs = slice(g * tq, (g + 1) * tq)
        outs.append(gt[:, g:g + 1] * o_cmp[rs] + gt[:, C_GROUP + g:C_GROUP + g + 1] * o_sel[rs]
                    + gt[:, 2 * C_GROUP + g:2 * C_GROUP + g + 1] * o_win[rs])
    o_ref[...] = jnp.concatenate(outs, axis=1).astype(o_ref.dtype)


def _nsa_prompt(qn, kc, vc, sel_rows, win_rows, gates, bsz, t, out_dtype):
    tq = NSA_TQ
    nq = t // tq
    n_blocks = t // C_BLOCK
    kv_k = pl.BlockSpec((t, HEAD_DIM), lambda b, h, i: (b, h))
    kv_v = pl.BlockSpec((t, HEAD_DIM), lambda b, h, i: (b, C_KV_HEADS + h))
    return pl.pallas_call(
        functools.partial(_nsa_prompt_kernel, n_blocks=n_blocks),
        out_shape=jax.ShapeDtypeStruct((bsz * t, C_Q), out_dtype),
        grid=(bsz, C_KV_HEADS, nq),
        in_specs=[pl.BlockSpec((tq, C_GROUP * HEAD_DIM), lambda b, h, i: (b * nq + i, h)),
                  pl.BlockSpec((n_blocks, HEAD_DIM), lambda b, h, i: (b, h)),
                  pl.BlockSpec((n_blocks, HEAD_DIM), lambda b, h, i: (b, h)),
                  kv_k, kv_v, kv_k, kv_v,
                  pl.BlockSpec((tq, NSA_GATE_LANES), lambda b, h, i: (b * nq + i, h))],
        out_specs=pl.BlockSpec((tq, C_GROUP * HEAD_DIM), lambda b, h, i: (b * nq + i, h)),
        scratch_shapes=[pltpu.VMEM((C_GROUP * tq, 1), jnp.float32), pltpu.VMEM((C_GROUP * tq, 1), jnp.float32),
                        pltpu.VMEM((C_GROUP * tq, HEAD_DIM), jnp.float32)],
        compiler_params=_params(("parallel", "parallel", "arbitrary")),
        name="nsa_prompt",
    )(qn, kc, vc, sel_rows, sel_rows, win_rows, win_rows, gates)


A_PREP_ROWS = 256
A_TQ = BAND_BLOCK
A_SLOPES = [[2.0 ** (-8.0 * (gi * A_HEADS + h + 1.0) / (A_N_GROUPS * A_HEADS)) for h in range(A_HEADS)]
            for gi in range(A_N_GROUPS)]


def _a_prep_kernel(q_in, k_in, v_in, qg_ref, kg_ref, q_ref, kv_ref):
    for c in range(A_Q // HEAD_DIM):
        sl = slice(c * HEAD_DIM, (c + 1) * HEAD_DIM)
        q_ref[:, sl] = _chunk_rms(q_in[:, sl], qg_ref[...])
    for c in range(A_HEADS):
        sl = slice(c * HEAD_DIM, (c + 1) * HEAD_DIM)
        kv_ref[:, sl] = _chunk_rms(k_in[:, sl], kg_ref[...])
    kv_ref[:, A_KV:] = v_in[...]


def _a_prep(u, q_gain, k_gain):
    p = u.shape[0]
    tm = A_PREP_ROWS
    fixed = lambda i: (0, 0)
    return pl.pallas_call(
        _a_prep_kernel,
        out_shape=(jax.ShapeDtypeStruct((p, A_Q), jnp.float32), jax.ShapeDtypeStruct((p, 2 * A_KV), jnp.float32)),
        grid=(p // tm,),
        in_specs=[pl.BlockSpec((tm, A_Q), lambda i: (i, 0)),
                  pl.BlockSpec((tm, A_KV), lambda i: (i, A_Q // A_KV)),
                  pl.BlockSpec((tm, A_KV), lambda i: (i, A_Q // A_KV + 1)),
                  pl.BlockSpec((1, HEAD_DIM), fixed), pl.BlockSpec((1, HEAD_DIM), fixed)],
        out_specs=(pl.BlockSpec((tm, A_Q), lambda i: (i, 0)), pl.BlockSpec((tm, 2 * A_KV), lambda i: (i, 0))),
        compiler_params=_params(("parallel",)),
        name="a_prep",
    )(u, u, u, q_gain.reshape(1, HEAD_DIM), k_gain.reshape(1, HEAD_DIM))


def _a_band_kernel(*refs, gi, dil, first, last):
    if first:
        q_ref, kp_ref, kc_ref, vp_ref, vc_ref = refs[:5]
        outs = refs[5:]
    else:
        q_ref, kp_ref, kc_ref, vp_ref, vc_ref, m_in, l_in, acc_in = refs[:8]
        outs = refs[8:]
    qb = pl.program_id(2)
    tq = A_TQ
    row = lax.broadcasted_iota(jnp.int32, (tq, 2 * tq), 0)
    col = lax.broadcasted_iota(jnp.int32, (tq, 2 * tq), 1)
    dist = tq + row - col
    ok = (dist >= 0) & (dist <= tq) & ((col >= tq) | (qb > 0))
    dist_f = (dist * dil).astype(jnp.float32)
    scale = HEAD_DIM ** -0.5
    for h in range(A_HEADS):
        sl = slice(h * HEAD_DIM, (h + 1) * HEAD_DIM)
        q = q_ref[:, sl].astype(jnp.bfloat16)
        k = jnp.concatenate([kp_ref[:, sl], kc_ref[:, sl]], axis=0).astype(jnp.bfloat16)
        v = jnp.concatenate([vp_ref[:, sl], vc_ref[:, sl]], axis=0).astype(jnp.bfloat16)
        s = lax.dot_general(q, k, (((1,), (1,)), ((), ())), preferred_element_type=jnp.float32) * scale
        s = jnp.where(ok, s - A_SLOPES[gi][h] * dist_f, NEG)
        m_row = jnp.max(s, axis=-1, keepdims=True)
        if first:
            m_new = jnp.broadcast_to(m_row, (tq, HEAD_DIM))
        else:
            m_old = m_in[:, sl]
            m_new = jnp.maximum(m_old, m_row)
        p = jnp.where(ok, jnp.exp(s - m_new[:, 0:1]), 0.0)
        l_new = jnp.broadcast_to(jnp.sum(p, axis=-1, keepdims=True), (tq, HEAD_DIM))
        acc = jnp.dot(p.astype(jnp.bfloat16), v, preferred_element_type=jnp.float32)
        if not first:
            alpha = jnp.exp(m_old - m_new)
            l_new = alpha * l_in[:, sl] + l_new
            acc = alpha * acc_in[:, sl] + acc
        if last:
            outs[0][:, sl] = (acc / l_new).astype(outs[0].dtype)
        else:
            outs[0][:, sl] = m_new
            outs[1][:, sl] = l_new
            outs[2][:, sl] = acc


def _a_band_group(gi, qn, akv, stats, bsz, t, out_dtype):
    w, dil = A_PATTERNS[gi]
    assert w // dil == A_TQ
    p = bsz * t
    n_res = t // dil
    nq = n_res // A_TQ
    first, last = gi == 0, gi == A_N_GROUPS - 1
    rows = p // dil
    qv = qn.reshape(rows, dil * A_Q)
    kvv = akv.reshape(rows, dil * 2 * A_KV)
    blk = (A_TQ, A_KV)
    cur = lambda b, r, i: b * nq + i
    prev = lambda b, r, i: b * nq + jnp.maximum(i - 1, 0)
    in_specs = [pl.BlockSpec(blk, lambda b, r, i: (cur(b, r, i), r * A_N_GROUPS + gi)),
                pl.BlockSpec(blk, lambda b, r, i: (prev(b, r, i), r * 2)),
                pl.BlockSpec(blk, lambda b, r, i: (cur(b, r, i), r * 2)),
                pl.BlockSpec(blk, lambda b, r, i: (prev(b, r, i), r * 2 + 1)),
                pl.BlockSpec(blk, lambda b, r, i: (cur(b, r, i), r * 2 + 1))]
    args = [qv, kvv, kvv, kvv, kvv]
    stat_spec = pl.BlockSpec(blk, lambda b, r, i: (cur(b, r, i), r))
    if not first:
        in_specs += [stat_spec] * 3
        args += [s.reshape(rows, dil * A_KV) for s in stats]
    if last:
        out_shape = jax.ShapeDtypeStruct((rows, dil * A_OUT), out_dtype)
        out_specs = stat_spec
    else:
        out_shape = (jax.ShapeDtypeStruct((rows, dil * A_KV), jnp.float32),) * 3
        out_specs = (stat_spec,) * 3
    out = pl.pallas_call(
        functools.partial(_a_band_kernel, gi=gi, dil=dil, first=first, last=last),
        out_shape=out_shape,
        grid=(bsz, dil, nq),
        in_specs=in_specs,
        out_specs=out_specs,
        compiler_params=_params(("parallel", "parallel", "arbitrary")),
        name="a_band_g%d" % gi,
    )(*args)
    if last:
        return out.reshape(p, A_OUT)
    return tuple(o.reshape(p, A_KV) for o in out)


def _a_attn_prompt(qn, akv, bsz, t, out_dtype):
    stats = None
    for gi in range(A_N_GROUPS):
        stats = _a_band_group(gi, qn, akv, stats, bsz, t, out_dtype)
    return stats


B_SCAN_T = 64


def _b_gates_kernel(g_ref, x_ref, cw_ref, cb_ref, wa_ref, ba_ref, wx_ref, bx_ref, lam_ref, a_ref, b_ref, gg_ref):
    t = x_ref.shape[0]
    x = x_ref[...]
    xx = jnp.concatenate([jnp.zeros((8, x.shape[1]), x.dtype), x], axis=0)
    xc = x * cw_ref[CONV_W - 1:CONV_W, :] + cb_ref[...]
    for j in range(CONV_W - 1):
        shift = CONV_W - 1 - j
        xc = xc + pltpu.roll(xx, shift, axis=0)[8:8 + t] * cw_ref[j:j + 1, :]
    x16 = xc.astype(jnp.bfloat16)
    r = jax.nn.sigmoid(jnp.dot(x16, wa_ref[...].astype(jnp.bfloat16), preferred_element_type=jnp.float32) + ba_ref[...])
    ig = jax.nn.sigmoid(jnp.dot(x16, wx_ref[...].astype(jnp.bfloat16), preferred_element_type=jnp.float32) + bx_ref[...])
    log_a = r * lam_ref[...]
    a = jnp.exp(log_a)
    a_ref[...] = a
    b_ref[...] = jnp.sqrt(1.0 - jnp.exp(2.0 * log_a)) * (ig * xc)
    g = g_ref[...]
    gg_ref[...] = 0.5 * g * (1.0 + jnp.tanh(math.sqrt(2.0 / math.pi) * (g + 0.044715 * (g * g * g))))


def _b_gates(u, b_par, bsz, t):
    conv_w, conv_b, wa, ba, wx, bx, lam = b_par
    lam_c = (-RG_C * jax.nn.softplus(-lam)).reshape(1, B_WIDTH)
    p = bsz * t
    g0 = (A_Q + 2 * A_KV) // B_BLOCK_DIM
    x0 = g0 + B_BLOCKS
    vec = lambda b, n: (0, n)
    blk = pl.BlockSpec((t, B_BLOCK_DIM), lambda b, n: (b, n))
    wspec = pl.BlockSpec((None, B_BLOCK_DIM, B_BLOCK_DIM), lambda b, n: (n, 0, 0))
    return pl.pallas_call(
        _b_gates_kernel,
        out_shape=(jax.ShapeDtypeStruct((p, B_WIDTH), jnp.float32),) * 3,
        grid=(bsz, B_BLOCKS),
        in_specs=[pl.BlockSpec((t, B_BLOCK_DIM), lambda b, n: (b, g0 + n)),
                  pl.BlockSpec((t, B_BLOCK_DIM), lambda b, n: (b, x0 + n)),
                  pl.BlockSpec((CONV_W, B_BLOCK_DIM), vec), pl.BlockSpec((1, B_BLOCK_DIM), vec),
                  wspec, pl.BlockSpec((1, B_BLOCK_DIM), vec), wspec, pl.BlockSpec((1, B_BLOCK_DIM), vec),
                  pl.BlockSpec((1, B_BLOCK_DIM), vec)],
        out_specs=(blk,) * 3,
        compiler_params=_params(("parallel", "parallel")),
        name="b_gates",
    )(u, u, conv_w, conv_b.reshape(1, B_WIDTH), wa, ba.reshape(1, B_WIDTH), wx, bx.reshape(1, B_WIDTH), lam_c)


def _b_scan_kernel(a_ref, b_ref, gg_ref, y_ref, hl_ref, h_scr):
    c = pl.program_id(0)

    @pl.when(c == 0)
    def _():
        h_scr[...] = jnp.zeros_like(h_scr)

    def body(t, h):
        h = a_ref[:, t] * h + b_ref[:, t]
        y_ref[:, t] = (h * gg_ref[:, t]).astype(y_ref.dtype)
        return h

    h = lax.fori_loop(0, a_ref.shape[1], body, h_scr[...], unroll=8)
    h_scr[...] = h
    hl_ref[...] = h


def _b_scan(a, b, gg, bsz, t, out_dtype):
    shp = (bsz, t, B_BLOCKS, B_BLOCK_DIM)
    blk = pl.BlockSpec((bsz, B_SCAN_T, B_BLOCKS, B_BLOCK_DIM), lambda c: (0, c, 0, 0))
    y, h_last = pl.pallas_call(
        _b_scan_kernel,
        out_shape=(jax.ShapeDtypeStruct(shp, out_dtype), jax.ShapeDtypeStruct((bsz, B_BLOCKS, B_BLOCK_DIM), jnp.float32)),
        grid=(t // B_SCAN_T,),
        in_specs=[blk] * 3,
        out_specs=(blk, pl.BlockSpec((bsz, B_BLOCKS, B_BLOCK_DIM), lambda c: (0, 0, 0))),
        scratch_shapes=[pltpu.VMEM((bsz, B_BLOCKS, B_BLOCK_DIM), jnp.float32)],
        compiler_params=_params(("arbitrary",)),
        name="b_scan",
    )(a.reshape(shp), b.reshape(shp), gg.reshape(shp))
    return y.reshape(bsz * t, B_WIDTH), h_last.reshape(bsz, B_WIDTH)


GDN_CHUNK = 128
GDN_GATE_ROWS = 512
GDN_HEADS_PER_STEP = 2
GDN_INV_PASSES = 3
DB_LANE = 3 * C_HEADS
DA_LANE = DB_LANE + D_HEADS


def _softplus(x):
    return jnp.maximum(x, 0.0) + jnp.log1p(jnp.exp(-jnp.abs(x)))


def _d_gates_kernel(us_ref, dtb_ref, aneg_ref, tri_ref, beta_ref, gam_ref):
    tm = us_ref.shape[0]
    us = us_ref[...]
    g = aneg_ref[...] * _softplus(us + dtb_ref[...])
    gam = jnp.dot(tri_ref[...], g, preferred_element_type=jnp.float32, precision=HI)
    beta = jax.nn.sigmoid(us)
    for h in range(D_HEADS):
        sl = slice(h * HEAD_DIM, (h + 1) * HEAD_DIM)
        beta_ref[:, sl] = jnp.broadcast_to(beta[:, DB_LANE + h:DB_LANE + h + 1], (tm, HEAD_DIM))
        gam_ref[:, sl] = jnp.broadcast_to(gam[:, DA_LANE + h:DA_LANE + h + 1], (tm, HEAD_DIM))


def _d_gates(u_s, a_log, dt_bias):
    p = u_s.shape[0]
    tm = GDN_GATE_ROWS
    pad = lambda x: jnp.zeros((1, LANES), jnp.float32).at[0, DA_LANE:DA_LANE + D_HEADS].set(x)
    r = np.arange(tm)
    tri = ((r[:, None] >= r[None, :]) & (r[:, None] // GDN_CHUNK == r[None, :] // GDN_CHUNK)).astype(np.float32)
    fixed = lambda i: (0, 0)
    return pl.pallas_call(
        _d_gates_kernel,
        out_shape=(jax.ShapeDtypeStruct((p, D_V), jnp.float32),) * 2,
        grid=(p // tm,),
        in_specs=[pl.BlockSpec((tm, LANES), lambda i: (i, 0)), pl.BlockSpec((1, LANES), fixed),
                  pl.BlockSpec((1, LANES), fixed), pl.BlockSpec((tm, tm), fixed)],
        out_specs=(pl.BlockSpec((tm, D_V), lambda i: (i, 0)),) * 2,
        compiler_params=_params(("parallel",)),
        name="d_gates",
    )(u_s, pad(dt_bias), pad(-jnp.exp(a_log)), jnp.asarray(tri))


def _mm(a, b, passes=1):
    ah, bh = a.astype(jnp.bfloat16), b.astype(jnp.bfloat16)
    out = jnp.dot(ah, bh, preferred_element_type=jnp.float32)
    if passes == 3:
        al = (a - ah.astype(jnp.float32)).astype(jnp.bfloat16)
        bl = (b - bh.astype(jnp.float32)).astype(jnp.bfloat16)
        out = out + jnp.dot(ah, bl, preferred_element_type=jnp.float32) + jnp.dot(al, bh, preferred_element_type=jnp.float32)
    return out


def _mm_nt(a, b):
    return lax.dot_general(a.astype(jnp.bfloat16), b.astype(jnp.bfloat16), (((1,), (1,)), ((), ())),
                           preferred_element_type=jnp.float32)


def _d_delta_kernel(q_in, k_in, v_in, z_ref, beta_ref, gam_ref, cwq_ref, cwk_ref, cwv_ref, og_ref,
                    y_ref, s_ref, q_scr, k_scr, v_scr):
    t = q_in.shape[0]
    c = GDN_CHUNK

    def conv_silu(x_ref, cw_ref):
        x = x_ref[...]
        xx = jnp.concatenate([jnp.zeros((8, x.shape[1]), x.dtype), x], axis=0)
        acc = x * cw_ref[CONV_W - 1:CONV_W, :]
        for j in range(CONV_W - 1):
            acc = acc + pltpu.roll(xx, CONV_W - 1 - j, axis=0)[8:8 + t] * cw_ref[j:j + 1, :]
        return acc * jax.nn.sigmoid(acc)

    qa = conv_silu(q_in, cwq_ref)
    ka = conv_silu(k_in, cwk_ref)
    v_scr[...] = conv_silu(v_in, cwv_ref)
    for hh in range(GDN_HEADS_PER_STEP):
        sl = slice(hh * HEAD_DIM, (hh + 1) * HEAD_DIM)
        qh, kh = qa[:, sl], ka[:, sl]
        q_scr[:, sl] = qh * lax.rsqrt(jnp.sum(qh * qh, axis=-1, keepdims=True) + EPS) * D_DK ** -0.5
        k_scr[:, sl] = kh * lax.rsqrt(jnp.sum(kh * kh, axis=-1, keepdims=True) + EPS)
    s_ref[...] = jnp.zeros_like(s_ref)

    ri = lax.broadcasted_iota(jnp.int32, (c, c), 0)
    ci = lax.broadcasted_iota(jnp.int32, (c, c), 1)
    lower, strict = ri >= ci, ri > ci
    eye = (ri == ci).astype(jnp.float32)

    def chunk(n, carry):
        r0 = pl.multiple_of(n * c, c)
        rows = pl.ds(r0, c)
        for hh in range(GDN_HEADS_PER_STEP):
            sl = slice(hh * HEAD_DIM, (hh + 1) * HEAD_DIM)
            q, k, v = q_scr[rows, sl], k_scr[rows, sl], v_scr[rows, sl]
            beta, gam = beta_ref[rows, sl], gam_ref[rows, sl]
            gam_t = jnp.transpose(gam)
            decay = jnp.where(lower, jnp.exp(jnp.minimum(gam - gam_t, 0.0)), 0.0)
            lmat = jnp.where(strict, beta * _mm_nt(k, k) * decay, 0.0)
            npow = -lmat
            inv = eye + npow
            for _ in range(c.bit_length() - 2):
                npow = _mm(npow, npow, GDN_INV_PASSES)
                inv = inv + _mm(inv, npow, GDN_INV_PASSES)
            e_gam = jnp.exp(gam)
            u = _mm(inv, v * beta, GDN_INV_PASSES)
            w = _mm(inv, k * (beta * e_gam), GDN_INV_PASSES)
            qk = jnp.where(lower, _mm_nt(q, k) * decay, 0.0)
            gam_last = gam[c - 1:c, :]
            state = s_ref[0, hh]
            v_new = u - _mm(w, state)
            o = _mm(q * e_gam, state) + _mm(qk, v_new)
            k_dec_t = jnp.transpose(k * jnp.exp(gam_last - gam))
            s_ref[0, hh] = jnp.exp(gam_last) * state + _mm(k_dec_t, v_new)
            z = z_ref[rows, sl]
            o = o * lax.rsqrt(jnp.mean(o * o, axis=-1, keepdims=True) + EPS) * og_ref[...]
            y_ref[rows, sl] = (o * (z * jax.nn.sigmoid(z))).astype(y_ref.dtype)
        return carry

    lax.fori_loop(0, t // c, chunk, 0)


def _d_delta(u_d, beta_r, gam_r, conv_w, out_gain, bsz, t, out_dtype):
    hp = GDN_HEADS_PER_STEP
    w = hp * HEAD_DIM
    nb = D_V // w
    blk = lambda sec: pl.BlockSpec((t, w), lambda b, h: (b, sec * nb + h))
    cw = lambda sec: pl.BlockSpec((CONV_W, w), lambda b, h: (0, sec * nb + h))
    return pl.pallas_call(
        _d_delta_kernel,
        out_shape=(jax.ShapeDtypeStruct((bsz * t, D_V), out_dtype),
                   jax.ShapeDtypeStruct((bsz, D_HEADS, D_DK, D_DV), jnp.float32)),
        grid=(bsz, D_HEADS // hp),
        in_specs=[blk(0), blk(1), blk(2), blk(3), blk(0), blk(0), cw(0), cw(1), cw(2),
                  pl.BlockSpec((1, HEAD_DIM), lambda b, h: (0, 0))],
        out_specs=(blk(0), pl.BlockSpec((1, hp, D_DK, D_DV), lambda b, h: (b, h, 0, 0))),
        scratch_shapes=[pltpu.VMEM((t, w), jnp.float32)] * 3,
        compiler_params=_params(("parallel", "parallel")),
        name="d_delta",
    )(u_d, u_d, u_d, u_d, beta_r, gam_r, conv_w, conv_w, conv_w, out_gain.reshape(1, HEAD_DIM))


def _dot_hi(a, b):
    return jnp.dot(a, b, preferred_element_type=jnp.float32, precision=HI)


def _dot_hi_nt(a, b):
    return lax.dot_general(a, b, (((1,), (1,)), ((), ())), preferred_element_type=jnp.float32, precision=HI)


def _gelu_tanh(g):
    return 0.5 * g * (1.0 + jnp.tanh(math.sqrt(2.0 / math.pi) * (g + 0.044715 * (g * g * g))))


def _rows_to_block(rows, n_rows):
    ri = lax.broadcasted_iota(jnp.int32, (n_rows, 1), 0)
    out = jnp.zeros((n_rows, rows[0].shape[1]), jnp.float32)
    for i, r in enumerate(rows):
        out = jnp.where(ri == i, r, out)
    return out


def _a_sample_kernel(q0_ref, q1_ref, q2_ref, kn_ref, vn_ref, kc_ref, vc_ref, qg_ref, kg_ref, o_ref, ko_ref):
    h = pl.program_id(0)
    n_ctx = kc_ref.shape[0]
    qs = [_chunk_rms(r[...], qg_ref[...]) for r in (q0_ref, q1_ref, q2_ref)]
    k_new = _chunk_rms(kn_ref[...], kg_ref[...])
    v_new = vn_ref[...]
    ko_ref[...] = k_new
    q = _rows_to_block(qs, 8)
    gi = lax.broadcasted_iota(jnp.int32, (8, 1), 0)
    live = gi < A_N_GROUPS
    dil = jnp.where(gi == 0, A_PATTERNS[0][1], jnp.where(gi == 1, A_PATTERNS[1][1], A_PATTERNS[2][1]))
    win = jnp.where(gi == 0, A_PATTERNS[0][0], jnp.where(gi == 1, A_PATTERNS[1][0], A_PATTERNS[2][0]))
    slope = jnp.exp2((-8.0 / (A_N_GROUPS * A_HEADS)) * (gi * A_HEADS + h + 1).astype(jnp.float32))
    dist = n_ctx - lax.broadcasted_iota(jnp.int32, (1, n_ctx), 1)
    ok = live & ((dist & (dil - 1)) == 0) & (dist <= win)
    scale = HEAD_DIM ** -0.5
    s = jnp.where(ok, _dot_hi_nt(q, kc_ref[...]) * scale - slope * dist.astype(jnp.float32), NEG)
    s_new = jnp.where(live, jnp.sum(q * k_new, axis=-1, keepdims=True) * scale, NEG)
    m = jnp.max(jnp.maximum(jnp.max(s, axis=-1, keepdims=True), s_new), axis=0, keepdims=True)
    p = jnp.where(ok, jnp.exp(s - m), 0.0)
    p_new = jnp.sum(jnp.where(live, jnp.exp(s_new - m), 0.0), axis=0, keepdims=True)
    l = jnp.sum(jnp.sum(p, axis=-1, keepdims=True), axis=0, keepdims=True) + p_new
    o = _dot_hi(jnp.sum(p, axis=0, keepdims=True), vc_ref[...]) + p_new * v_new
    o_ref[...] = o / l


def _a_sample(us, cache, layer, q_gain, k_gain):
    bd = us.shape[0]
    n_ctx = cache.shape[2]
    cview = cache.reshape(cache.shape[0], bd, n_ctx, 2 * A_KV)
    us3 = us.reshape(bd, 1, -1)
    col = lambda c: pl.BlockSpec((None, 1, HEAD_DIM), lambda h, b, c=c: (b, 0, c + h))
    fixed = pl.BlockSpec((1, HEAD_DIM), lambda h, b: (0, 0))
    o, k_new = pl.pallas_call(
        _a_sample_kernel,
        out_shape=(jax.ShapeDtypeStruct((bd, 1, A_OUT), jnp.float32), jax.ShapeDtypeStruct((bd, 1, A_KV), jnp.float32)),
        grid=(A_HEADS, bd),
        in_specs=[col(0), col(A_HEADS), col(2 * A_HEADS), col(3 * A_HEADS), col(4 * A_HEADS),
                  pl.BlockSpec((None, None, n_ctx, HEAD_DIM), lambda h, b: (layer, b, 0, h)),
                  pl.BlockSpec((None, None, n_ctx, HEAD_DIM), lambda h, b: (layer, b, 0, A_HEADS + h)),
                  fixed, fixed],
        out_specs=(col(0), col(0)),
        compiler_params=_params(("parallel", "parallel")),
        name="a_sample",
    )(us3, us3, us3, us3, us3, cview, cview, q_gain.reshape(1, HEAD_DIM), k_gain.reshape(1, HEAD_DIM))
    return o.reshape(bd, A_OUT), k_new.reshape(bd, A_KV)


def _b_sample_kernel(g_ref, x_ref, buf_ref, h0_ref, cw_ref, cb_ref, wa_ref, ba_ref, wx_ref, bx_ref, lam_ref, y_ref, h_ref):
    x = x_ref[...]
    xc = x * cw_ref[CONV_W - 1:CONV_W, :] + cb_ref[...]
    for j in range(CONV_W - 1):
        xc = xc + buf_ref[j] * cw_ref[j:j + 1, :]
    r = jax.nn.sigmoid(_dot_hi(xc, wa_ref[...]) + ba_ref[...])
    ig = jax.nn.sigmoid(_dot_hi(xc, wx_ref[...]) + bx_ref[...])
    log_a = r * lam_ref[...]
    h = jnp.exp(log_a) * h0_ref[...] + jnp.sqrt(1.0 - jnp.exp(2.0 * log_a)) * (ig * xc)
    h_ref[...] = h
    y_ref[...] = h * _gelu_tanh(g_ref[...])


def _b_sample(us, conv_buf, h0, b_par):
    conv_w, conv_b, wa, ba, wx, bx, lam = b_par
    bd = us.shape[0]
    lam_c = (-RG_C * jax.nn.softplus(-lam)).reshape(1, B_WIDTH)
    g0 = (A_Q + 2 * A_KV) // B_BLOCK_DIM
    vec = lambda n: (0, n)
    blk = pl.BlockSpec((bd, B_BLOCK_DIM), vec)
    one = pl.BlockSpec((1, B_BLOCK_DIM), vec)
    wspec = pl.BlockSpec((None, B_BLOCK_DIM, B_BLOCK_DIM), lambda n: (n, 0, 0))
    return pl.pallas_call(
        _b_sample_kernel,
        out_shape=(jax.ShapeDtypeStruct((bd, B_WIDTH), jnp.float32),) * 2,
        grid=(B_BLOCKS,),
        in_specs=[pl.BlockSpec((bd, B_BLOCK_DIM), lambda n: (0, g0 + n)),
                  pl.BlockSpec((bd, B_BLOCK_DIM), lambda n: (0, g0 + B_BLOCKS + n)),
                  pl.BlockSpec((CONV_W - 1, bd, B_BLOCK_DIM), lambda n: (0, 0, n)), blk,
                  pl.BlockSpec((CONV_W, B_BLOCK_DIM), vec), one, wspec, one, wspec, one, one],
        out_specs=(blk, blk),
        compiler_params=_params(("parallel",)),
        name="b_sample",
    )(us, us, jnp.swapaxes(conv_buf, 0, 1), h0, conv_w, conv_b.reshape(1, B_WIDTH), wa, ba.reshape(1, B_WIDTH),
      wx, bx.reshape(1, B_WIDTH), lam_c)


def _d_sample_kernel(q_in, k_in, v_in, z_ref, bq_ref, bk_ref, bv_ref, us_ref, dtb_ref, aneg_ref,
                     cwq_ref, cwk_ref, cwv_ref, og_ref, s0_ref, y_ref, s_ref, o_scr):
    h = pl.program_id(0)
    bd = q_in.shape[0]

    def conv_silu(x_ref, b_ref, cw_ref):
        acc = x_ref[...] * cw_ref[CONV_W - 1:CONV_W, :]
        for j in range(CONV_W - 1):
            acc = acc + b_ref[j] * cw_ref[j:j + 1, :]
        return acc * jax.nn.sigmoid(acc)

    q = conv_silu(q_in, bq_ref, cwq_ref)
    k = conv_silu(k_in, bk_ref, cwk_ref)
    v = conv_silu(v_in, bv_ref, cwv_ref)
    q = q * lax.rsqrt(jnp.sum(q * q, axis=-1, keepdims=True) + EPS) * D_DK ** -0.5
    k = k * lax.rsqrt(jnp.sum(k * k, axis=-1, keepdims=True) + EPS)
    us = us_ref[...]
    lane = lax.broadcasted_iota(jnp.int32, (1, LANES), 1)
    beta = jnp.sum(jnp.where(lane == DB_LANE + h, jax.nn.sigmoid(us), 0.0), axis=-1, keepdims=True)
    g = jnp.sum(jnp.where(lane == DA_LANE + h, aneg_ref[...] * _softplus(us + dtb_ref[...]), 0.0), axis=-1, keepdims=True)
    a = jnp.exp(g)
    k_t = jnp.transpose(k)
    q_t = jnp.transpose(q)
    for b in range(bd):
        rb = slice(b, b + 1)
        state = s0_ref[b]
        kcol, qcol = k_t[:, rb], q_t[:, rb]
        k_s = jnp.sum(kcol * state, axis=0, keepdims=True)
        v_new = beta[rb] * (v[rb] - a[rb] * k_s)
        state = a[rb] * state + kcol * v_new
        s_ref[b] = state
        o_scr[rb, :] = jnp.sum(qcol * state, axis=0, keepdims=True)
    o = o_scr[...]
    o = o * lax.rsqrt(jnp.mean(o * o, axis=-1, keepdims=True) + EPS) * og_ref[...]
    z = z_ref[...]
    y_ref[...] = o * (z * jax.nn.sigmoid(z))


def _d_sample(u_d, u_s, conv_buf, s0, layer, d_par):
    conv_w, a_log, dt_bias, out_gain = d_par
    bd = u_d.shape[0]
    pad = lambda x: jnp.zeros((1, LANES), jnp.float32).at[0, DA_LANE:DA_LANE + D_HEADS].set(x)
    sec = lambda s: pl.BlockSpec((bd, HEAD_DIM), lambda h, s=s: (0, s * D_HEADS + h))
    bsec = lambda s: pl.BlockSpec((CONV_W - 1, bd, HEAD_DIM), lambda h, s=s: (0, 0, s * D_HEADS + h))
    cw = lambda s: pl.BlockSpec((CONV_W, HEAD_DIM), lambda h, s=s: (0, s * D_HEADS + h))
    one = pl.BlockSpec((1, LANES), lambda h: (0, 0))
    buf_t = jnp.swapaxes(conv_buf, 0, 1)
    return pl.pallas_call(
        _d_sample_kernel,
        out_shape=(jax.ShapeDtypeStruct((bd, D_V), jnp.float32),
                   jax.ShapeDtypeStruct((bd, D_HEADS, D_DK, D_DV), jnp.float32)),
        grid=(D_HEADS,),
        in_specs=[sec(0), sec(1), sec(2), sec(3), bsec(0), bsec(1), bsec(2),
                  pl.BlockSpec((bd, LANES), lambda h: (0, 0)), one, one, cw(0), cw(1), cw(2), one,
                  pl.BlockSpec((None, bd, None, D_DK, D_DV), lambda h: (layer, 0, h, 0, 0))],
        out_specs=(sec(0), pl.BlockSpec((bd, None, D_DK, D_DV), lambda h: (0, h, 0, 0))),
        scratch_shapes=[pltpu.VMEM((bd, HEAD_DIM), jnp.float32)],
        compiler_params=_params(("parallel",)),
        name="d_sample",
    )(u_d, u_d, u_d, u_d, buf_t, buf_t, buf_t, u_s, pad(dt_bias), pad(-jnp.exp(a_log)),
      conv_w, conv_w, conv_w, out_gain.reshape(1, HEAD_DIM), s0)


NSA_S_PAGES = 4
NSA_S_LANES = 384


def _nsa_s_sum_kernel(pt_ref, *refs):
    pages, w_ref, o_ref = refs[:NSA_S_PAGES], refs[NSA_S_PAGES], refs[NSA_S_PAGES + 1]
    per_page = PAGE_SIZE // C_BLOCK
    for i, page in enumerate(pages):
        x = page[...] * w_ref[...]
        for j in range(per_page):
            o_ref[i * per_page + j: i * per_page + j + 1, :] = jnp.sum(x[j * C_BLOCK:(j + 1) * C_BLOCK], axis=0, keepdims=True)


def _nsa_s_summaries(pool, layer, page_table, phi_k, phi_v):
    bd, n_pages = page_table.shape
    n_phys = pool.shape[1]
    pview = pool.reshape(pool.shape[0], n_phys, PAGE_SIZE, 2 * C_KV)
    reps = PAGE_SIZE // C_BLOCK
    w = jnp.concatenate([jnp.broadcast_to(jnp.tile(phi_k, reps)[:, None], (PAGE_SIZE, C_KV)),
                         jnp.broadcast_to(jnp.tile(phi_v, reps)[:, None], (PAGE_SIZE, C_KV))], axis=1)
    steps = n_pages // NSA_S_PAGES
    rows = NSA_S_PAGES * reps
    page_spec = lambda i: pl.BlockSpec((None, None, PAGE_SIZE, 2 * C_KV),
                                       lambda b, s, pt, i=i: (layer, pt[b, s * NSA_S_PAGES + i], 0, 0))
    return pl.pallas_call(
        _nsa_s_sum_kernel,
        out_shape=jax.ShapeDtypeStruct((bd, steps * rows, 2 * C_KV), jnp.float32),
        grid_spec=pltpu.PrefetchScalarGridSpec(
            num_scalar_prefetch=1, grid=(bd, steps),
            in_specs=[page_spec(i) for i in range(NSA_S_PAGES)]
            + [pl.BlockSpec((PAGE_SIZE, 2 * C_KV), lambda b, s, pt: (0, 0))],
            out_specs=pl.BlockSpec((None, rows, 2 * C_KV), lambda b, s, pt: (b, s, 0))),
        compiler_params=_params(("parallel", "arbitrary")),
        name="nsa_s_summaries",
    )(page_table, *([pview] * NSA_S_PAGES), w)


def _nsa_s_query(q_ref, h):
    q = _rows_to_block([q_ref[:, g * HEAD_DIM:(g + 1) * HEAD_DIM] for g in range(C_GROUP)], 8)
    gi = lax.broadcasted_iota(jnp.int32, (8, 1), 0)
    slope = jnp.exp2(-0.5 * (h * C_GROUP + gi + 1).astype(jnp.float32))
    return q, gi, slope


def _nsa_s_gate(gate_ref, gi, br):
    gt = gate_ref[...]
    out = jnp.zeros((8, 1), jnp.float32)
    for g in range(C_GROUP):
        out = jnp.where(gi == g, gt[:, br * C_GROUP + g: br * C_GROUP + g + 1], out)
    return out


def _nsa_s_cmpwin_kernel(q_ref, kc_ref, vc_ref, wk_ref, wv_ref, wkn_ref, wvn_ref, gate_ref, part_ref, sel_ref, *, past):
    h = pl.program_id(1)
    q, gi, slope = _nsa_s_query(q_ref, h)
    live = gi < C_GROUP
    scale = HEAD_DIM ** -0.5
    nbc = kc_ref.shape[0]
    nidx = lax.broadcasted_iota(jnp.int32, (1, nbc), 1)
    centre = nidx.astype(jnp.float32) * C_BLOCK + (C_BLOCK - 1) / 2.0
    cmask = (nidx + 1) * C_BLOCK - 1 <= past
    s = jnp.where(cmask, _dot_hi_nt(q, kc_ref[...]) * scale - slope * jnp.abs(past - centre), NEG)
    m = jnp.max(s, axis=-1, keepdims=True)
    p = jnp.where(cmask, jnp.exp(s - m), 0.0)
    p = p / jnp.maximum(jnp.sum(p, axis=-1, keepdims=True), 1e-30)
    o_cmp = _dot_hi(p, vc_ref[...])
    imp = jnp.sum(jnp.where(live, p, 0.0), axis=0, keepdims=True)
    n_lanes = NSA_S_LANES
    n_blocks = -(-(past + 1) // C_BLOCK)
    cur = past // C_BLOCK
    lane = lax.broadcasted_iota(jnp.int32, (1, n_lanes), 1)
    imp = jnp.concatenate([imp, jnp.zeros((1, n_lanes - nbc), jnp.float32)], axis=1)
    forced = (lane == 0) | (lane == cur) | (lane == cur - 1)
    causal = (lane <= cur) & (lane < n_blocks)
    score = jnp.where(causal, imp + jnp.where(forced, FORCE, 0.0), NEG)
    score_col = jnp.transpose(jnp.broadcast_to(score, (8, n_lanes)))[:, 0:1]
    mi = lax.broadcasted_iota(jnp.int32, (n_lanes, 1), 0)
    ahead = (score_col > score) | ((score_col == score) & (mi < lane))
    rank = jnp.sum(ahead.astype(jnp.float32), axis=0, keepdims=True)
    sel = (rank < min(C_N_SEL, n_blocks)) & causal
    sel_ref[...] = jnp.broadcast_to(sel.astype(jnp.float32), (8, n_lanes))
    n_win = wk_ref.shape[0]
    dist = n_win - lax.broadcasted_iota(jnp.int32, (1, n_win), 1)
    wmask = dist <= C_WIN
    sw = jnp.where(wmask, _dot_hi_nt(q, wk_ref[...]) * scale - slope * dist.astype(jnp.float32), NEG)
    sw_new = jnp.sum(q * wkn_ref[...], axis=-1, keepdims=True) * scale
    mw = jnp.maximum(jnp.max(sw, axis=-1, keepdims=True), sw_new)
    pw = jnp.where(wmask, jnp.exp(sw - mw), 0.0)
    pw_new = jnp.exp(sw_new - mw)
    o_win = (_dot_hi(pw, wv_ref[...]) + pw_new * wvn_ref[...]) / (jnp.sum(pw, axis=-1, keepdims=True) + pw_new)
    part_ref[...] = _nsa_s_gate(gate_ref, gi, 0) * o_cmp + _nsa_s_gate(gate_ref, gi, 2) * o_win


def _nsa_s_cmpwin(qn, kvc, win_buf, layer, win_new, gates, past):
    bd = qn.shape[0]
    nbc = kvc.shape[1]
    n_win = win_buf.shape[2]
    wview = win_buf.reshape(win_buf.shape[0], bd, n_win, 2 * C_KV)
    rows = lambda c: pl.BlockSpec((None, 1, HEAD_DIM), lambda b, h, c=c: (b, 0, c + h))
    qn, win_new, gates = qn.reshape(bd, 1, -1), win_new.reshape(bd, 1, -1), gates.reshape(bd, 1, -1)
    return pl.pallas_call(
        functools.partial(_nsa_s_cmpwin_kernel, past=past),
        out_shape=(jax.ShapeDtypeStruct((bd, C_KV_HEADS, 8, HEAD_DIM), jnp.float32),
                   jax.ShapeDtypeStruct((bd, C_KV_HEADS, 8, NSA_S_LANES), jnp.float32)),
        grid=(bd, C_KV_HEADS),
        in_specs=[pl.BlockSpec((None, 1, C_GROUP * HEAD_DIM), lambda b, h: (b, 0, h)),
                  pl.BlockSpec((None, nbc, HEAD_DIM), lambda b, h: (b, 0, h)),
                  pl.BlockSpec((None, nbc, HEAD_DIM), lambda b, h: (b, 0, C_KV_HEADS + h)),
                  pl.BlockSpec((None, None, n_win, HEAD_DIM), lambda b, h: (layer, b, 0, h)),
                  pl.BlockSpec((None, None, n_win, HEAD_DIM), lambda b, h: (layer, b, 0, C_KV_HEADS + h)),
                  rows(0), rows(C_KV_HEADS), rows(0)],
        out_specs=(pl.BlockSpec((None, None, 8, HEAD_DIM), lambda b, h: (b, h, 0, 0)),
                   pl.BlockSpec((None, None, 8, NSA_S_LANES), lambda b, h: (b, h, 0, 0))),
        compiler_params=_params(("parallel", "parallel")),
        name="nsa_s_cmpwin",
    )(qn, kvc, kvc, wview, wview, win_new, win_new, gates)


def _nsa_s_sel_kernel(hp_ref, blk_ref, q_ref, k_ref, v_ref, kn_ref, vn_ref, gate_ref, part_ref, o_ref,
                      m_scr, l_scr, acc_scr, *, past):
    b = pl.program_id(0)
    h = pl.program_id(1)
    j = pl.program_id(2)
    q, gi, slope = _nsa_s_query(q_ref, h)
    scale = HEAD_DIM ** -0.5

    @pl.when(j == 0)
    def _():
        m_scr[...] = jnp.sum(q * kn_ref[...], axis=-1, keepdims=True) * scale
        l_scr[...] = jnp.ones_like(l_scr)
        acc_scr[...] = jnp.broadcast_to(vn_ref[...], acc_scr.shape)

    blk = blk_ref[(b * C_KV_HEADS + h) * pl.num_programs(2) + j]
    kpos = blk * C_BLOCK + lax.broadcasted_iota(jnp.int32, (1, C_BLOCK), 1)
    s = _dot_hi_nt(q, k_ref[...]) * scale - slope * (past - kpos).astype(jnp.float32)
    m_old = m_scr[...]
    m_new = jnp.maximum(m_old, jnp.max(s, axis=-1, keepdims=True))
    p = jnp.exp(s - m_new)
    alpha = jnp.exp(m_old - m_new)
    l_scr[...] = alpha * l_scr[...] + jnp.sum(p, axis=-1, keepdims=True)
    acc_scr[...] = alpha * acc_scr[...] + _dot_hi(p, v_ref[...])
    m_scr[...] = m_new

    @pl.when(j == pl.num_programs(2) - 1)
    def _():
        o_ref[...] = part_ref[...] + _nsa_s_gate(gate_ref, gi, 1) * (acc_scr[...] / l_scr[...])


def _nsa_s_select(qn, pool, layer, half_pages, blocks, sel_new, gates, part, past):
    bd = qn.shape[0]
    n_sel = blocks.shape[-1]
    n_phys = pool.shape[1]
    halves = PAGE_SIZE // C_BLOCK
    pview = pool.reshape(pool.shape[0], n_phys * halves, C_BLOCK, 2 * C_KV)
    rows = lambda c: pl.BlockSpec((None, 1, HEAD_DIM), lambda b, h, j, hp, bl, c=c: (b, 0, c + h))
    qn, sel_new, gates = qn.reshape(bd, 1, -1), sel_new.reshape(bd, 1, -1), gates.reshape(bd, 1, -1)
    flat = lambda b, h, j: (b * C_KV_HEADS + h) * n_sel + j
    blk4 = pl.BlockSpec((None, None, 8, HEAD_DIM), lambda b, h, j, hp, bl: (b, h, 0, 0))
    return pl.pallas_call(
        functools.partial(_nsa_s_sel_kernel, past=past),
        out_shape=jax.ShapeDtypeStruct((bd, C_KV_HEADS, 8, HEAD_DIM), jnp.float32),
        grid_spec=pltpu.PrefetchScalarGridSpec(
            num_scalar_prefetch=2, grid=(bd, C_KV_HEADS, n_sel),
            in_specs=[pl.BlockSpec((None, 1, C_GROUP * HEAD_DIM), lambda b, h, j, hp, bl: (b, 0, h)),
                      pl.BlockSpec((None, None, C_BLOCK, HEAD_DIM), lambda b, h, j, hp, bl: (layer, hp[flat(b, h, j)], 0, h)),
                      pl.BlockSpec((None, None, C_BLOCK, HEAD_DIM),
                                   lambda b, h, j, hp, bl: (layer, hp[flat(b, h, j)], 0, C_KV_HEADS + h)),
                      rows(0), rows(C_KV_HEADS), rows(0), blk4],
            out_specs=blk4,
            scratch_shapes=[pltpu.VMEM((8, 1), jnp.float32), pltpu.VMEM((8, 1), jnp.float32),
                            pltpu.VMEM((8, HEAD_DIM), jnp.float32)]),
        compiler_params=_params(("parallel", "parallel", "arbitrary")),
        name="nsa_s_select",
    )(half_pages.reshape(-1), blocks.reshape(-1), qn, pview, pview, sel_new, sel_new, gates, part)


def _nsa_sample(u_c, u_s, q_gain, k_gain, phi_k, phi_v, cmp_pool, sel_pool, win_buf, layer, page_table):
    bd = u_c.shape[0]
    past = page_table.shape[1] * PAGE_SIZE
    qn, cmp_new, sel_new, win_new, gates = _nsa_prep(u_c, u_s, q_gain, k_gain)
    kvc = _nsa_s_summaries(cmp_pool, layer, page_table, phi_k, phi_v)
    part, sel = _nsa_s_cmpwin(qn, kvc, win_buf, layer, win_new, gates, past)
    n_past_blocks = past // C_BLOCK
    n_sel = min(C_N_SEL, n_past_blocks + 1) - 1
    _, blocks = lax.top_k(sel[:, :, 0, :n_past_blocks], n_sel)
    blocks = blocks.astype(jnp.int32)
    halves = PAGE_SIZE // C_BLOCK
    pages = jnp.take_along_axis(page_table[:, None, :], blocks // halves, axis=2)
    half_pages = (pages * halves + blocks % halves).astype(jnp.int32)
    out = _nsa_s_select(qn, sel_pool, layer, half_pages, blocks, sel_new, gates, part, past)
    return out[:, :, :C_GROUP, :].reshape(bd, C_Q), cmp_new, sel_new, win_new


def _even_mixer(u2d, w_out3d, layer, res2d, q_gain, k_gain, b_par, kv_cache, conv_buf, h0, bsz, t):
    u_x = u2d.reshape(bsz, t, -1)[:, :, -B_WIDTH:]
    if kv_cache is None:
        qn, akv = _a_prep(u2d, q_gain, k_gain)
        a_out = _a_attn_prompt(qn, akv, bsz, t, jnp.bfloat16)
        a_state = akv.reshape(bsz, t, 2, A_HEADS, HEAD_DIM)[:, -min(A_WIN_MAX, t):]
        a_dec, b_in, gg = _b_gates(u2d, b_par, bsz, t)
        b_out, h_new = _b_scan(a_dec, b_in, gg, bsz, t, jnp.float32)
        conv_new = u_x[:, -(CONV_W - 1):]
    else:
        a_out, k_new = _a_sample(u2d, kv_cache, layer, q_gain, k_gain)
        a_state = jnp.stack([k_new, u2d[:, A_Q + A_KV:A_Q + 2 * A_KV]], axis=1).reshape(bsz, t, 2, A_HEADS, HEAD_DIM)
        b_out, h_new = _b_sample(u2d, conv_buf, h0, b_par)
        conv_new = jnp.concatenate([conv_buf[:, 1:], u_x], axis=1)
    out = _matmul([a_out, b_out], w_out3d, layer, 0, D_MODEL, kv_cache is not None, res=res2d)
    return out, (a_state, conv_new, h_new)


def _odd_mixer(u_parts, w_out3d, layer, res2d, q_gain, k_gain, phi_k, phi_v, d_par,
               cmp_pool, sel_pool, page_table, win_buf, d_conv_buf, d_s0, bsz, t):
    u_c2d, u_d2d, u_s2d = u_parts
    kv_shape = (bsz, t, 2, C_KV_HEADS, HEAD_DIM)
    d_in = u_d2d.reshape(bsz, t, -1)[:, :, :D_CONV]
    conv_w, a_log, dt_bias, out_gain = d_par
    if page_table is None:
        qn, cmp2d, sel2d, win2d, gates = _nsa_prep(u_c2d, u_s2d, q_gain, k_gain)
        kc, vc = _nsa_summaries(cmp2d, phi_k, phi_v)
        o_c = _nsa_prompt(qn, kc, vc, sel2d, win2d, gates, bsz, t, jnp.bfloat16)
        win_state = win2d.reshape(kv_shape)[:, -min(C_WIN, t):]
        beta_r, gam_r = _d_gates(u_s2d, a_log, dt_bias)
        d_out, d_s = _d_delta(u_d2d, beta_r, gam_r, conv_w, out_gain, bsz, t, jnp.bfloat16)
        d_conv_new = d_in[:, -(CONV_W - 1):]
    else:
        o_c, cmp2d, sel2d, win2d = _nsa_sample(u_c2d, u_s2d, q_gain, k_gain, phi_k, phi_v,
                                                cmp_pool, sel_pool, win_buf, layer, page_table)
        win_state = win2d.reshape(kv_shape)
        d_out, d_s = _d_sample(u_d2d, u_s2d, d_conv_buf, d_s0, layer, d_par)
        d_conv_new = jnp.concatenate([d_conv_buf[:, 1:], d_in], axis=1)
    out = _matmul([o_c, d_out], w_out3d, layer, 0, D_MODEL, page_table is not None, res=res2d)
    return out, (cmp2d.reshape(kv_shape), sel2d.reshape(kv_shape), win_state, d_conv_new, d_s)


ODD_C = C_Q + 6 * C_KV
ODD_G0 = ODD_C
ODD_D0 = ODD_G0 + 3 * C_HEADS
ODD_D = 2 * D_QK + 2 * D_V
ODD_S0 = ODD_D0 + ODD_D
ODD_IN = ODD_S0 + 2 * D_HEADS


def kernel(x_prompt, x_sample, cache_a_kv, state_b_conv, state_b_h, cache_c_cmp_kv, cache_c_sel_kv, cache_c_win_kv, state_d_conv, state_d_S, page_table, norm_mix, norm_ffn, even_w_in, even_w_out, a_q_norm, a_k_norm, b_conv_w, b_conv_b, b_gate_a_w, b_gate_a_b, b_gate_x_w, b_gate_x_b, b_lambda, odd_w_in, odd_w_out, c_q_norm, c_k_norm, c_phi_k, c_phi_v, d_conv_w, d_a_log, d_dt_bias, d_out_norm, moe_group_w, moe_group_b, moe_expert_w, moe_expert_b, moe_w1, moe_w3, moe_w2):
    bp, sp, d = x_prompt.shape
    bs, ss, _ = x_sample.shape
    depth = norm_mix.shape[0]
    hp = x_prompt.reshape(bp * sp, d)
    hs = x_sample.reshape(bs * ss, d)
    outs = {k: [] for k in ("ak", "bc", "bh", "cc", "cs", "cw", "dc", "ds")}
    outs_s = {k: [] for k in outs}
    for l in range(depth):
        i = l // 2
        xp = _rmsnorm(hp, norm_mix[l], jnp.bfloat16)
        xs = _rmsnorm(hs, norm_mix[l], jnp.float32)
        if l % 2 == 0:
            b_par = (b_conv_w[i], b_conv_b[i], b_gate_a_w[i], b_gate_a_b[i], b_gate_x_w[i], b_gate_x_b[i], b_lambda[i])
            n_in = even_w_in.shape[-1]
            up = _matmul(xp, even_w_in, i, 0, n_in, False)
            us = _matmul(xs, even_w_in, i, 0, n_in, True)
            hp, st_p = _even_mixer(up, even_w_out, i, hp, a_q_norm[i], a_k_norm[i], b_par, None, None, None, bp, sp)
            hs, st_s = _even_mixer(us, even_w_out, i, hs, a_q_norm[i], a_k_norm[i], b_par,
                                   cache_a_kv, state_b_conv[i], state_b_h[i], bs, ss)
            for dst, st in ((outs, st_p), (outs_s, st_s)):
                dst["ak"].append(st[0]); dst["bc"].append(st[1]); dst["bh"].append(st[2])
        else:
            d_par = (d_conv_w[i], d_a_log[i], d_dt_bias[i], d_out_norm[i])
            w_d = odd_w_in[:, :, ODD_D0:ODD_S0]
            w_s = jnp.concatenate([odd_w_in[:, :, ODD_G0:ODD_D0], odd_w_in[:, :, ODD_S0:],
                                   jnp.zeros((odd_w_in.shape[0], d, LANES - 3 * C_HEADS - 2 * D_HEADS), jnp.float32)], axis=-1)
            ups = (_matmul(xp, odd_w_in, i, 0, ODD_C, False), _matmul(xp, w_d, i, 0, ODD_D, False), _matmul(xp, w_s, i, 0, LANES, False))
            uss = (_matmul(xs, odd_w_in, i, 0, ODD_C, True), _matmul(xs, w_d, i, 0, ODD_D, True), _matmul(xs, w_s, i, 0, LANES, True))
            hp, st_p = _odd_mixer(ups, odd_w_out, i, hp, c_q_norm[i], c_k_norm[i], c_phi_k[i], c_phi_v[i], d_par,
                                  None, None, None, None, None, None, bp, sp)
            hs, st_s = _odd_mixer(uss, odd_w_out, i, hs, c_q_norm[i], c_k_norm[i], c_phi_k[i], c_phi_v[i], d_par,
                                  cache_c_cmp_kv, cache_c_sel_kv, page_table, cache_c_win_kv,
                                  state_d_conv[i], state_d_S, bs, ss)
            for dst, st in ((outs, st_p), (outs_s, st_s)):
                dst["cc"].append(st[0]); dst["cs"].append(st[1]); dst["cw"].append(st[2])
                dst["dc"].append(st[3]); dst["ds"].append(st[4])
        w_router = jnp.concatenate([moe_group_w[l], moe_expert_w[l],
                                    jnp.zeros((d, ROUTER_PAD - N_GROUPS - N_EXPERTS), jnp.float32)], axis=-1)
        b_router = jnp.concatenate([moe_group_b[l], moe_expert_b[l],
                                    jnp.zeros((ROUTER_PAD - N_GROUPS - N_EXPERTS,), jnp.float32)])[None, :]
        hp = _hier_moe(hp, norm_ffn[l], w_router, b_router, moe_w1, moe_w3, moe_w2, l, 256, jnp.bfloat16)
        hs = _hier_moe(hs, norm_ffn[l], w_router, b_router, moe_w1, moe_w3, moe_w2, l, 8, jnp.float32)
    res = [hp.reshape(bp, sp, d), hs.reshape(bs, ss, d)]
    for key in ("ak", "bc", "bh", "cc", "cs", "cw", "dc", "ds"):
        res.append(jnp.stack(outs[key]))
        res.append(jnp.stack(outs_s[key]))
    return tuple(res)
```

```python
import functools
import math

import jax
import jax.numpy as jnp
import numpy as np
from jax import lax
from jax.experimental import pallas as pl
from jax.experimental.pallas import tpu as pltpu

D_MODEL = 4096
HEAD_DIM = 128
CONV_W = 4
BAND_BLOCK = 128
A_PATTERNS = ((128, 1), (512, 4), (2048, 16))
A_N_GROUPS = len(A_PATTERNS)
A_HEADS = D_MODEL // 512
A_WIN_MAX = max(w for w, _ in A_PATTERNS)
A_Q = A_N_GROUPS * A_HEADS * HEAD_DIM
A_KV = A_HEADS * HEAD_DIM
A_OUT = A_HEADS * HEAD_DIM
B_WIDTH = 3 * D_MODEL // 4
B_BLOCKS = B_WIDTH // HEAD_DIM
B_BLOCK_DIM = B_WIDTH // B_BLOCKS
RG_C = 8.0
C_HEADS = D_MODEL // 256
C_KV_HEADS = C_HEADS // 4
C_GROUP = C_HEADS // C_KV_HEADS
C_BLOCK = 64
C_N_SEL = 16
C_WIN = 512
C_Q = C_HEADS * HEAD_DIM
C_KV = C_KV_HEADS * HEAD_DIM
D_HEADS = D_MODEL // 256
D_DK = HEAD_DIM
D_DV = HEAD_DIM
D_QK = D_HEADS * D_DK
D_V = D_HEADS * D_DV
D_CONV = 2 * D_QK + D_V
N_GROUPS = 8
EXPERTS_PER_GROUP = 8
N_EXPERTS = N_GROUPS * EXPERTS_PER_GROUP
TOP_K = 2
D_EXPERT = D_MODEL // 8
PAGE_SIZE = 128
EPS = 1e-6
NEG = -1e30
FORCE = 1e4

LANES = 128
VMEM_LIMIT = 56 * 1024 * 1024
ROUTER_PAD = LANES

HI = lax.Precision.HIGHEST


def _params(sem):
    return pltpu.CompilerParams(dimension_semantics=sem, vmem_limit_bytes=VMEM_LIMIT)


def _rmsnorm_kernel(x_ref, g_ref, o_ref):
    x = x_ref[...]
    y = x * lax.rsqrt(jnp.mean(x * x, axis=-1, keepdims=True) + EPS)
    o_ref[...] = (y * g_ref[...]).astype(o_ref.dtype)


def _rmsnorm(x2d, gain, out_dtype):
    m, d = x2d.shape
    tm = min(m, 512)
    return pl.pallas_call(
        _rmsnorm_kernel,
        out_shape=jax.ShapeDtypeStruct((m, d), out_dtype),
        grid=(m // tm,),
        in_specs=[pl.BlockSpec((tm, d), lambda i: (i, 0)), pl.BlockSpec((1, d), lambda i: (0, 0))],
        out_specs=pl.BlockSpec((tm, d), lambda i: (i, 0)),
        compiler_params=_params(("parallel",)),
        name="rmsnorm",
    )(x2d, gain.reshape(1, d))


def _matmul_kernel(*refs, exact, has_res, k_bounds):
    n_x = len(k_bounds) - 1
    x_refs, w_ref = refs[:n_x], refs[n_x]
    if has_res:
        r_ref, o_ref, acc_ref = refs[n_x + 1:]
    else:
        o_ref, acc_ref = refs[n_x + 1:]
    k = pl.program_id(2)

    @pl.when(k == 0)
    def _():
        acc_ref[...] = jnp.zeros_like(acc_ref)

    for p, x_ref in enumerate(x_refs):
        @pl.when((k >= k_bounds[p]) & (k < k_bounds[p + 1]))
        def _(x_ref=x_ref):
            if exact:
                acc_ref[...] += jnp.dot(x_ref[...], w_ref[...], preferred_element_type=jnp.float32, precision=HI)
            else:
                acc_ref[...] += jnp.dot(x_ref[...].astype(jnp.bfloat16), w_ref[...].astype(jnp.bfloat16),
                                        preferred_element_type=jnp.float32)

    @pl.when(k == pl.num_programs(2) - 1)
    def _():
        out = acc_ref[...]
        if has_res:
            out = out + r_ref[...]
        o_ref[...] = out


def _matmul(xs, w3d, layer, col0, n, exact, res=None, tn=1024, tk=512):
    if not isinstance(xs, (list, tuple)):
        xs = [xs]
    m = xs[0].shape[0]
    tm = min(m, 1024)
    tn = min(tn, n)
    assert m % tm == 0 and n % tn == 0 and col0 % tn == 0 and all(x.shape[1] % tk == 0 for x in xs)
    jb = col0 // tn
    k_bounds = [0]
    for x in xs:
        k_bounds.append(k_bounds[-1] + x.shape[1] // tk)
    in_specs = []
    for p in range(len(xs)):
        lo, hi = k_bounds[p], k_bounds[p + 1]
        in_specs.append(pl.BlockSpec((tm, tk), lambda i, j, k, lo=lo, hi=hi: (i, jnp.clip(k, lo, hi - 1) - lo)))
    in_specs.append(pl.BlockSpec((None, tk, tn), lambda i, j, k: (layer, k, j + jb)))
    args = list(xs) + [w3d]
    if res is not None:
        in_specs.append(pl.BlockSpec((tm, tn), lambda i, j, k: (i, j)))
        args.append(res)
    return pl.pallas_call(
        functools.partial(_matmul_kernel, exact=exact, has_res=res is not None, k_bounds=tuple(k_bounds)),
        out_shape=jax.ShapeDtypeStruct((m, n), jnp.float32),
        grid=(m // tm, n // tn, k_bounds[-1]),
        in_specs=in_specs,
        out_specs=pl.BlockSpec((tm, tn), lambda i, j, k: (i, j)),
        scratch_shapes=[pltpu.VMEM((tm, tn), jnp.float32)],
        compiler_params=_params(("parallel", "parallel", "arbitrary")),
        name="proj",
    )(*args)


def _router_kernel(h_ref, g_ref, w_ref, b_ref, xn_ref, logit_ref):
    x = h_ref[...]
    y = x * lax.rsqrt(jnp.mean(x * x, axis=-1, keepdims=True) + EPS) * g_ref[...]
    xn_ref[...] = y.astype(xn_ref.dtype)
    logit_ref[...] = jnp.dot(y, w_ref[...], preferred_element_type=jnp.float32, precision=HI) + b_ref[...]


def _router(h2d, gain, w_router, b_router, xn_dtype):
    m, d = h2d.shape
    tm = min(m, 256)
    return pl.pallas_call(
        _router_kernel,
        out_shape=(jax.ShapeDtypeStruct((m, d), xn_dtype), jax.ShapeDtypeStruct((m, ROUTER_PAD), jnp.float32)),
        grid=(m // tm,),
        in_specs=[pl.BlockSpec((tm, d), lambda i: (i, 0)), pl.BlockSpec((1, d), lambda i: (0, 0)),
                  pl.BlockSpec((d, ROUTER_PAD), lambda i: (0, 0)), pl.BlockSpec((1, ROUTER_PAD), lambda i: (0, 0))],
        out_specs=(pl.BlockSpec((tm, d), lambda i: (i, 0)), pl.BlockSpec((tm, ROUTER_PAD), lambda i: (i, 0))),
        compiler_params=_params(("parallel",)),
        name="ffn_norm_router",
    )(h2d, gain.reshape(1, d), w_router, b_router)


MOE_K_CHUNK = 512


def _moe_up_kernel(be_ref, nu_ref, x_ref, w1_ref, w3_ref, o_ref, *, exact):
    tm = x_ref.shape[0]
    used = pl.program_id(0) < nu_ref[0]

    @pl.when(used)
    def _():
        a = jnp.zeros((tm, D_EXPERT), jnp.float32)
        b = jnp.zeros((tm, D_EXPERT), jnp.float32)
        for c in range(D_MODEL // MOE_K_CHUNK):
            sl = slice(c * MOE_K_CHUNK, (c + 1) * MOE_K_CHUNK)
            if exact:
                a += jnp.dot(x_ref[:, sl], w1_ref[sl, :], preferred_element_type=jnp.float32, precision=HI)
                b += jnp.dot(x_ref[:, sl], w3_ref[sl, :], preferred_element_type=jnp.float32, precision=HI)
            else:
                a += jnp.dot(x_ref[:, sl], w1_ref[sl, :].astype(jnp.bfloat16), preferred_element_type=jnp.float32)
                b += jnp.dot(x_ref[:, sl], w3_ref[sl, :].astype(jnp.bfloat16), preferred_element_type=jnp.float32)
        o_ref[...] = (a * jax.nn.sigmoid(a) * b).astype(o_ref.dtype)

    @pl.when(jnp.logical_not(used))
    def _():
        o_ref[...] = jnp.zeros_like(o_ref)


def _moe_down_kernel(be_ref, nu_ref, h_ref, w2_ref, g_ref, o_ref, *, exact):
    used = pl.program_id(0) < nu_ref[0]

    @pl.when(used)
    def _():
        if exact:
            out = jnp.dot(h_ref[...], w2_ref[...], preferred_element_type=jnp.float32, precision=HI)
        else:
            out = jnp.dot(h_ref[...], w2_ref[...].astype(jnp.bfloat16), preferred_element_type=jnp.float32)
        o_ref[...] = out * g_ref[...]

    @pl.when(jnp.logical_not(used))
    def _():
        o_ref[...] = jnp.zeros_like(o_ref)


def _moe_experts(x_rows, row_gate, blk_exp, n_used, w1, w3, w2, layer, tm):
    rows, d = x_rows.shape
    n_blk = rows // tm
    exact = x_rows.dtype == jnp.float32
    up = pl.pallas_call(
        functools.partial(_moe_up_kernel, exact=exact),
        out_shape=jax.ShapeDtypeStruct((rows, D_EXPERT), x_rows.dtype),
        grid_spec=pltpu.PrefetchScalarGridSpec(
            num_scalar_prefetch=2, grid=(n_blk,),
            in_specs=[pl.BlockSpec((tm, d), lambda i, be, nu: (jnp.minimum(i, nu[0] - 1), 0)),
                      pl.BlockSpec((None, None, d, D_EXPERT), lambda i, be, nu: (layer, be[i], 0, 0)),
                      pl.BlockSpec((None, None, d, D_EXPERT), lambda i, be, nu: (layer, be[i], 0, 0))],
            out_specs=pl.BlockSpec((tm, D_EXPERT), lambda i, be, nu: (i, 0))),
        compiler_params=_params(("arbitrary",)),
        name="moe_up",
    )(blk_exp, n_used, x_rows, w1, w3)
    return pl.pallas_call(
        functools.partial(_moe_down_kernel, exact=exact),
        out_shape=jax.ShapeDtypeStruct((rows, d), jnp.float32),
        grid_spec=pltpu.PrefetchScalarGridSpec(
            num_scalar_prefetch=2, grid=(n_blk,),
            in_specs=[pl.BlockSpec((tm, D_EXPERT), lambda i, be, nu: (jnp.minimum(i, nu[0] - 1), 0)),
                      pl.BlockSpec((None, None, D_EXPERT, d), lambda i, be, nu: (layer, be[i], 0, 0)),
                      pl.BlockSpec((tm, 1), lambda i, be, nu: (jnp.minimum(i, nu[0] - 1), 0))],
            out_specs=pl.BlockSpec((tm, d), lambda i, be, nu: (i, 0))),
        compiler_params=_params(("arbitrary",)),
        name="moe_down",
    )(blk_exp, n_used, up, w2, row_gate)


def _hier_moe(h2d, gain, w_router, b_router, w1, w3, w2, layer, tm, xn_dtype):
    n_tok, d = h2d.shape
    xn, logits = _router(h2d, gain, w_router, b_router, xn_dtype)
    g_logit = logits[:, :N_GROUPS]
    g_prob = jax.nn.softmax(g_logit, axis=-1)
    grp = jnp.argmax(g_logit, axis=-1)
    p_grp = jnp.take_along_axis(g_prob, grp[:, None], axis=1)[:, 0]
    e_logit = logits[:, N_GROUPS:N_GROUPS + N_EXPERTS].reshape(-1, N_GROUPS, EXPERTS_PER_GROUP)
    e_logit = jnp.take_along_axis(e_logit, grp[:, None, None], axis=1)[:, 0]
    e_val, e_idx = lax.top_k(e_logit, TOP_K)
    gates = p_grp[:, None] * jax.nn.softmax(e_val, axis=-1)
    experts = grp[:, None] * EXPERTS_PER_GROUP + e_idx
    n_asg = n_tok * TOP_K
    flat_e = experts.reshape(-1).astype(jnp.int32)
    order = jnp.argsort(flat_e)
    sorted_e = flat_e[order]
    counts = jnp.bincount(flat_e, length=N_EXPERTS)
    padded = (counts + tm - 1) // tm * tm
    pad_end = jnp.cumsum(padded)
    pad_start = pad_end - padded
    start = jnp.cumsum(counts) - counts
    dest = (pad_start[sorted_e] + jnp.arange(n_asg) - start[sorted_e]).astype(jnp.int32)
    n_rows = -(-(n_asg + N_EXPERTS * (tm - 1)) // tm) * tm
    n_blk = n_rows // tm
    tok = (order // TOP_K).astype(jnp.int32)
    row_tok = jnp.zeros((n_rows,), jnp.int32).at[dest].set(tok)
    row_gate = jnp.zeros((n_rows,), jnp.float32).at[dest].set(gates.reshape(-1)[order])
    blk_exp = jnp.minimum(jnp.searchsorted(pad_end, jnp.arange(n_blk) * tm, side='right'),
                          N_EXPERTS - 1).astype(jnp.int32)
    x_rows = xn[row_tok]
    n_used = (pad_end[-1:] // tm).astype(jnp.int32)
    out = _moe_experts(x_rows, row_gate[:, None], blk_exp, n_used, w1, w3, w2, layer, tm)
    asg_row = jnp.zeros((n_asg,), jnp.int32).at[order].set(dest)
    y = out[asg_row].reshape(n_tok, TOP_K, d).sum(axis=1)
    return h2d + y


NSA_PREP_ROWS = 256
NSA_SUM_ROWS = 512
NSA_TQ = 128
NSA_GATE_LANES = LANES
TQ_SHIFT = NSA_TQ.bit_length() - 1
C_BLOCK_SHIFT = C_BLOCK.bit_length() - 1


def _chunk_rms(x, gain):
    return x * lax.rsqrt(jnp.mean(x * x, axis=-1, keepdims=True) + EPS) * gain


def _nsa_prep_kernel(uc_ref, us_ref, qg_ref, kg_ref, perm_ref, q_ref, cmp_ref, sel_ref, win_ref, gate_ref):
    for c in range(C_Q // HEAD_DIM):
        sl = slice(c * HEAD_DIM, (c + 1) * HEAD_DIM)
        q_ref[:, sl] = _chunk_rms(uc_ref[:, sl], qg_ref[...])
    for br, o_ref in enumerate((cmp_ref, sel_ref, win_ref)):
        base = C_Q + br * 2 * C_KV
        for c in range(C_KV_HEADS):
            sl = slice(c * HEAD_DIM, (c + 1) * HEAD_DIM)
            o_ref[:, sl] = _chunk_rms(uc_ref[:, base + c * HEAD_DIM: base + (c + 1) * HEAD_DIM], kg_ref[br:br + 1, :])
        o_ref[:, C_KV:] = uc_ref[:, base + C_KV: base + 2 * C_KV]
    gate_ref[...] = jnp.dot(jax.nn.sigmoid(us_ref[...]), perm_ref[...], preferred_element_type=jnp.float32, precision=HI)


def _nsa_prep(u_c, u_s, q_gain, k_gain):
    p = u_c.shape[0]
    tm = min(p, NSA_PREP_ROWS)
    perm = np.zeros((LANES, C_KV_HEADS * NSA_GATE_LANES), np.float32)
    for br in range(3):
        for h in range(C_KV_HEADS):
            for g in range(C_GROUP):
                perm[br * C_HEADS + h * C_GROUP + g, h * NSA_GATE_LANES + br * C_GROUP + g] = 1.0
    row = lambda i: (i, 0)
    fixed = lambda i: (0, 0)
    return pl.pallas_call(
        _nsa_prep_kernel,
        out_shape=(jax.ShapeDtypeStruct((p, C_Q), jnp.float32),) + (jax.ShapeDtypeStruct((p, 2 * C_KV), jnp.float32),) * 3
        + (jax.ShapeDtypeStruct((p, C_KV_HEADS * NSA_GATE_LANES), jnp.float32),),
        grid=(p // tm,),
        in_specs=[pl.BlockSpec((tm, ODD_C), row), pl.BlockSpec((tm, LANES), row), pl.BlockSpec((1, HEAD_DIM), fixed),
                  pl.BlockSpec((3, HEAD_DIM), fixed), pl.BlockSpec(perm.shape, fixed)],
        out_specs=(pl.BlockSpec((tm, C_Q), row),) + (pl.BlockSpec((tm, 2 * C_KV), row),) * 3
        + (pl.BlockSpec((tm, C_KV_HEADS * NSA_GATE_LANES), row),),
        compiler_params=_params(("parallel",)),
        name="nsa_prep",
    )(u_c, u_s, q_gain.reshape(1, HEAD_DIM), k_gain, jnp.asarray(perm))


def _nsa_sum_kernel(cmp_ref, phik_ref, phiv_ref, kc_ref, vc_ref):
    kc_ref[...] = jnp.dot(phik_ref[...], cmp_ref[:, :C_KV], preferred_element_type=jnp.float32, precision=HI)
    vc_ref[...] = jnp.dot(phiv_ref[...], cmp_ref[:, C_KV:], preferred_element_type=jnp.float32, precision=HI)


def _nsa_summaries(cmp_rows, phi_k, phi_v):
    p = cmp_rows.shape[0]
    tm = NSA_SUM_ROWS
    nb = tm // C_BLOCK
    eye = jnp.eye(nb, dtype=jnp.float32)
    big_k = jnp.kron(eye, phi_k[None, :])
    big_v = jnp.kron(eye, phi_v[None, :])
    return pl.pallas_call(
        _nsa_sum_kernel,
        out_shape=(jax.ShapeDtypeStruct((p // C_BLOCK, C_KV), jnp.float32),) * 2,
        grid=(p // tm,),
        in_specs=[pl.BlockSpec((tm, 2 * C_KV), lambda i: (i, 0)), pl.BlockSpec((nb, tm), lambda i: (0, 0)),
                  pl.BlockSpec((nb, tm), lambda i: (0, 0))],
        out_specs=(pl.BlockSpec((nb, C_KV), lambda i: (i, 0)),) * 2,
        compiler_params=_params(("parallel",)),
        name="nsa_summaries",
    )(cmp_rows, big_k, big_v)


def _nsa_prompt_kernel(q_ref, kc_ref, vc_ref, sk_ref, sv_ref, wk_ref, wv_ref, gate_ref, o_ref,
                       m_scr, l_scr, acc_scr, *, n_blocks):
    h = pl.program_id(1)
    qb = pl.program_id(2)
    tq = NSA_TQ
    rows = C_GROUP * tq
    scale = HEAD_DIM ** -0.5
    q = jnp.concatenate([q_ref[:, g * HEAD_DIM:(g + 1) * HEAD_DIM] for g in range(C_GROUP)], axis=0)
    row = lax.broadcasted_iota(jnp.int32, (rows, 1), 0)
    t_row = qb * tq + (row & (tq - 1))
    slope = jnp.exp2(-0.5 * (h * C_GROUP + (row >> TQ_SHIFT) + 1).astype(jnp.float32))

    nidx = lax.broadcasted_iota(jnp.int32, (1, n_blocks), 1)
    s = lax.dot_general(q, kc_ref[...], (((1,), (1,)), ((), ())), preferred_element_type=jnp.float32, precision=HI) * scale
    centre = nidx.astype(jnp.float32) * C_BLOCK + (C_BLOCK - 1) / 2.0
    s = s - slope * jnp.abs(t_row.astype(jnp.float32) - centre)
    cmask = (nidx + 1) * C_BLOCK - 1 <= t_row
    s = jnp.where(cmask, s, NEG)
    m = jnp.max(s, axis=-1, keepdims=True)
    p = jnp.where(cmask, jnp.exp(s - m), 0.0)
    l = jnp.sum(p, axis=-1, keepdims=True)
    p = p / jnp.maximum(l, 1e-30)
    o_cmp = jnp.dot(p, vc_ref[...], preferred_element_type=jnp.float32, precision=HI)
    imp = p[0:tq]
    for g in range(1, C_GROUP):
        imp = imp + p[g * tq:(g + 1) * tq]

    t_q = t_row[0:tq]
    cur = t_q >> C_BLOCK_SHIFT
    forced = (nidx == 0) | (nidx == cur) | (nidx == cur - 1)
    causal = nidx <= cur
    score = jnp.where(causal, imp + jnp.where(forced, FORCE, 0.0), NEG)
    rank = jnp.zeros((tq, n_blocks), jnp.int32)
    for mcol in range(n_blocks):
        cm = score[:, mcol:mcol + 1]
        ahead = (cm > score) | ((cm == score) & (mcol < nidx))
        rank = rank + ahead.astype(jnp.int32)
    selm = ((rank < min(C_N_SEL, n_blocks)) & causal).astype(jnp.bfloat16)

    q16 = q.astype(jnp.bfloat16)
    slope_b = jnp.broadcast_to(slope, (rows, tq))
    t_rel = jnp.broadcast_to(t_row, (rows, tq)) - lax.broadcasted_iota(jnp.int32, (rows, tq), 1)

    def attend(k_ref, v_ref, c, mask):
        start = pl.multiple_of(c * tq, tq)
        k = k_ref[pl.ds(start, tq), :].astype(jnp.bfloat16)
        v = v_ref[pl.ds(start, tq), :].astype(jnp.bfloat16)
        dist = t_rel - c * tq
        sc = lax.dot_general(q16, k, (((1,), (1,)), ((), ())), preferred_element_type=jnp.float32) * scale
        sc = sc - slope_b * dist.astype(jnp.float32)
        ok = mask(dist)
        sc = jnp.where(ok, sc, NEG)
        m_old = m_scr[...]
        m_new = jnp.maximum(m_old, jnp.max(sc, axis=-1, keepdims=True))
        pc = jnp.where(ok, jnp.exp(sc - m_new), 0.0)
        alpha = jnp.exp(m_old - m_new)
        l_scr[...] = alpha * l_scr[...] + jnp.sum(pc, axis=-1, keepdims=True)
        acc_scr[...] = alpha * acc_scr[...] + jnp.dot(pc.astype(jnp.bfloat16), v, preferred_element_type=jnp.float32)
        m_scr[...] = m_new

    def reset():
        m_scr[...] = jnp.full_like(m_scr, NEG)
        l_scr[...] = jnp.zeros_like(l_scr)
        acc_scr[...] = jnp.zeros_like(acc_scr)

    reset()
    brow = lax.broadcasted_iota(jnp.int32, (n_blocks, tq), 0)
    bcol = lax.broadcasted_iota(jnp.int32, (n_blocks, tq), 1)

    def sel_body(c, carry):
        expand = (brow == c * (tq // C_BLOCK) + (bcol >> C_BLOCK_SHIFT)).astype(jnp.bfloat16)
        sel_keys = jnp.dot(selm, expand, preferred_element_type=jnp.float32)
        sel_keys = jnp.concatenate([sel_keys] * C_GROUP, axis=0) > 0.5
        attend(sk_ref, sv_ref, c, lambda dist: sel_keys & (dist >= 0))
        return carry

    lax.fori_loop(0, qb + 1, sel_body, 0)
    o_sel = acc_scr[...] / l_scr[...]

    reset()

    def win_body(c, carry):
        attend(wk_ref, wv_ref, c, lambda dist: (dist >= 0) & (dist <= C_WIN))
        return carry

    lax.fori_loop(jnp.maximum(qb - C_WIN // tq, 0), qb + 1, win_body, 0)
    o_win = acc_scr[...] / l_scr[...]

    gt = gate_ref[...]
    outs = []
    for g in range(C_GROUP):
        rs = slice(g * tq, (g + 1) * tq)
        outs.append(gt[:, g:g + 1] * o_cmp[rs] + gt[:, C_GROUP + g:C_GROUP + g + 1] * o_sel[rs]
                    + gt[:, 2 * C_GROUP + g:2 * C_GROUP + g + 1] * o_win[rs])
    o_ref[...] = jnp.concatenate(outs, axis=1).astype(o_ref.dtype)


def _nsa_prompt(qn, kc, vc, sel_rows, win_rows, gates, bsz, t, out_dtype):
    tq = NSA_TQ
    assert tq == HEAD_DIM
    nq = t // tq
    n_blocks = t // C_BLOCK
    kv_k = pl.BlockSpec((t, HEAD_DIM), lambda b, h, i: (b, h))
    kv_v = pl.BlockSpec((t, HEAD_DIM), lambda b, h, i: (b, C_KV_HEADS + h))
    return pl.pallas_call(
        functools.partial(_nsa_prompt_kernel, n_blocks=n_blocks),
        out_shape=jax.ShapeDtypeStruct((bsz * t, C_Q), out_dtype),
        grid=(bsz, C_KV_HEADS, nq),
        in_specs=[pl.BlockSpec((tq, C_GROUP * HEAD_DIM), lambda b, h, i: (b * nq + i, h)),
                  pl.BlockSpec((n_blocks, HEAD_DIM), lambda b, h, i: (b, h)),
                  pl.BlockSpec((n_blocks, HEAD_DIM), lambda b, h, i: (b, h)),
                  kv_k, kv_v, kv_k, kv_v,
                  pl.BlockSpec((tq, NSA_GATE_LANES), lambda b, h, i: (b * nq + i, h))],
        out_specs=pl.BlockSpec((tq, C_GROUP * HEAD_DIM), lambda b, h, i: (b * nq + i, h)),
        scratch_shapes=[pltpu.VMEM((C_GROUP * tq, tq), jnp.float32), pltpu.VMEM((C_GROUP * tq, tq), jnp.float32),
                        pltpu.VMEM((C_GROUP * tq, HEAD_DIM), jnp.float32)],
        compiler_params=_params(("parallel", "parallel", "arbitrary")),
        name="nsa_prompt",
    )(qn, kc, vc, sel_rows, sel_rows, win_rows, win_rows, gates)


A_PREP_ROWS = 256
A_TQ = BAND_BLOCK
A_SLOPES = [[2.0 ** (-8.0 * (gi * A_HEADS + h + 1.0) / (A_N_GROUPS * A_HEADS)) for h in range(A_HEADS)]
            for gi in range(A_N_GROUPS)]


def _a_prep_kernel(q_in, k_in, v_in, qg_ref, kg_ref, q_ref, kv_ref):
    for c in range(A_Q // HEAD_DIM):
        sl = slice(c * HEAD_DIM, (c + 1) * HEAD_DIM)
        q_ref[:, sl] = _chunk_rms(q_in[:, sl], qg_ref[...])
    for c in range(A_HEADS):
        sl = slice(c * HEAD_DIM, (c + 1) * HEAD_DIM)
        kv_ref[:, sl] = _chunk_rms(k_in[:, sl], kg_ref[...])
    kv_ref[:, A_KV:] = v_in[...]


def _a_prep(u, q_gain, k_gain):
    p = u.shape[0]
    tm = A_PREP_ROWS
    fixed = lambda i: (0, 0)
    return pl.pallas_call(
        _a_prep_kernel,
        out_shape=(jax.ShapeDtypeStruct((p, A_Q), jnp.float32), jax.ShapeDtypeStruct((p, 2 * A_KV), jnp.float32)),
        grid=(p // tm,),
        in_specs=[pl.BlockSpec((tm, A_Q), lambda i: (i, 0)),
                  pl.BlockSpec((tm, A_KV), lambda i: (i, A_Q // A_KV)),
                  pl.BlockSpec((tm, A_KV), lambda i: (i, A_Q // A_KV + 1)),
                  pl.BlockSpec((1, HEAD_DIM), fixed), pl.BlockSpec((1, HEAD_DIM), fixed)],
        out_specs=(pl.BlockSpec((tm, A_Q), lambda i: (i, 0)), pl.BlockSpec((tm, 2 * A_KV), lambda i: (i, 0))),
        compiler_params=_params(("parallel",)),
        name="a_prep",
    )(u, u, u, q_gain.reshape(1, HEAD_DIM), k_gain.reshape(1, HEAD_DIM))


def _a_band_kernel(*refs, gi, dil, first, last):
    if first:
        q_ref, kp_ref, kc_ref, vp_ref, vc_ref = refs[:5]
        outs = refs[5:]
    else:
        q_ref, kp_ref, kc_ref, vp_ref, vc_ref, m_in, l_in, acc_in = refs[:8]
        outs = refs[8:]
    qb = pl.program_id(2)
    tq = A_TQ
    row = lax.broadcasted_iota(jnp.int32, (tq, 2 * tq), 0)
    col = lax.broadcasted_iota(jnp.int32, (tq, 2 * tq), 1)
    dist = tq + row - col
    ok = (dist >= 0) & (dist <= tq) & ((col >= tq) | (qb > 0))
    dist_f = (dist * dil).astype(jnp.float32)
    scale = HEAD_DIM ** -0.5
    for h in range(A_HEADS):
        sl = slice(h * HEAD_DIM, (h + 1) * HEAD_DIM)
        q = q_ref[:, sl].astype(jnp.bfloat16)
        k = jnp.concatenate([kp_ref[:, sl], kc_ref[:, sl]], axis=0).astype(jnp.bfloat16)
        v = jnp.concatenate([vp_ref[:, sl], vc_ref[:, sl]], axis=0).astype(jnp.bfloat16)
        s = lax.dot_general(q, k, (((1,), (1,)), ((), ())), preferred_element_type=jnp.float32) * scale
        s = jnp.where(ok, s - A_SLOPES[gi][h] * dist_f, NEG)
        m_row = jnp.max(s, axis=-1, keepdims=True)
        if first:
            m_new = jnp.broadcast_to(m_row, (tq, HEAD_DIM))
        else:
            m_old = m_in[:, sl]
            m_new = jnp.maximum(m_old, m_row)
        p = jnp.where(ok, jnp.exp(s - m_new[:, 0:1]), 0.0)
        l_new = jnp.broadcast_to(jnp.sum(p, axis=-1, keepdims=True), (tq, HEAD_DIM))
        acc = jnp.dot(p.astype(jnp.bfloat16), v, preferred_element_type=jnp.float32)
        if not first:
            alpha = jnp.exp(m_old - m_new)
            l_new = alpha * l_in[:, sl] + l_new
            acc = alpha * acc_in[:, sl] + acc
        if last:
            outs[0][:, sl] = (acc / l_new).astype(outs[0].dtype)
        else:
            outs[0][:, sl] = m_new
            outs[1][:, sl] = l_new
            outs[2][:, sl] = acc


def _a_band_group(gi, qn, akv, stats, bsz, t, out_dtype):
    w, dil = A_PATTERNS[gi]
    assert w // dil == A_TQ
    p = bsz * t
    n_res = t // dil
    nq = n_res // A_TQ
    first, last = gi == 0, gi == A_N_GROUPS - 1
    rows = p // dil
    qv = qn.reshape(rows, dil * A_Q)
    kvv = akv.reshape(rows, dil * 2 * A_KV)
    blk = (A_TQ, A_KV)
    cur = lambda b, r, i: b * nq + i
    prev = lambda b, r, i: b * nq + jnp.maximum(i - 1, 0)
    in_specs = [pl.BlockSpec(blk, lambda b, r, i: (cur(b, r, i), r * A_N_GROUPS + gi)),
                pl.BlockSpec(blk, lambda b, r, i: (prev(b, r, i), r * 2)),
                pl.BlockSpec(blk, lambda b, r, i: (cur(b, r, i), r * 2)),
                pl.BlockSpec(blk, lambda b, r, i: (prev(b, r, i), r * 2 + 1)),
                pl.BlockSpec(blk, lambda b, r, i: (cur(b, r, i), r * 2 + 1))]
    args = [qv, kvv, kvv, kvv, kvv]
    stat_spec = pl.BlockSpec(blk, lambda b, r, i: (cur(b, r, i), r))
    if not first:
        in_specs += [stat_spec] * 3
        args += [s.reshape(rows, dil * A_KV) for s in stats]
    if last:
        out_shape = jax.ShapeDtypeStruct((rows, dil * A_OUT), out_dtype)
        out_specs = stat_spec
    else:
        out_shape = (jax.ShapeDtypeStruct((rows, dil * A_KV), jnp.float32),) * 3
        out_specs = (stat_spec,) * 3
    out = pl.pallas_call(
        functools.partial(_a_band_kernel, gi=gi, dil=dil, first=first, last=last),
        out_shape=out_shape,
        grid=(bsz, dil, nq),
        in_specs=in_specs,
        out_specs=out_specs,
        compiler_params=_params(("parallel", "parallel", "arbitrary")),
        name="a_band_g%d" % gi,
    )(*args)
    if last:
        return out.reshape(p, A_OUT)
    return tuple(o.reshape(p, A_KV) for o in out)


def _a_attn_prompt(qn, akv, bsz, t, out_dtype):
    stats = None
    for gi in range(A_N_GROUPS):
        stats = _a_band_group(gi, qn, akv, stats, bsz, t, out_dtype)
    return stats


B_SCAN_T = 64


def _b_gates_kernel(g_ref, x_ref, cw_ref, cb_ref, wa_ref, ba_ref, wx_ref, bx_ref, lam_ref, a_ref, b_ref, gg_ref):
    t = x_ref.shape[0]
    x = x_ref[...]
    xx = jnp.concatenate([jnp.zeros((8, x.shape[1]), x.dtype), x], axis=0)
    xc = x * cw_ref[CONV_W - 1:CONV_W, :] + cb_ref[...]
    for j in range(CONV_W - 1):
        shift = CONV_W - 1 - j
        xc = xc + pltpu.roll(xx, shift, axis=0)[8:8 + t] * cw_ref[j:j + 1, :]
    x16 = xc.astype(jnp.bfloat16)
    r = jax.nn.sigmoid(jnp.dot(x16, wa_ref[...].astype(jnp.bfloat16), preferred_element_type=jnp.float32) + ba_ref[...])
    ig = jax.nn.sigmoid(jnp.dot(x16, wx_ref[...].astype(jnp.bfloat16), preferred_element_type=jnp.float32) + bx_ref[...])
    log_a = r * lam_ref[...]
    a = jnp.exp(log_a)
    a_ref[...] = a
    b_ref[...] = jnp.sqrt(1.0 - jnp.exp(2.0 * log_a)) * (ig * xc)
    g = g_ref[...]
    gg_ref[...] = 0.5 * g * (1.0 + jnp.tanh(math.sqrt(2.0 / math.pi) * (g + 0.044715 * (g * g * g))))


def _b_gates(u, b_par, bsz, t):
    conv_w, conv_b, wa, ba, wx, bx, lam = b_par
    lam_c = (-RG_C * jax.nn.softplus(-lam)).reshape(1, B_WIDTH)
    p = bsz * t
    g0 = (A_Q + 2 * A_KV) // B_BLOCK_DIM
    x0 = g0 + B_BLOCKS
    vec = lambda b, n: (0, n)
    blk = pl.BlockSpec((t, B_BLOCK_DIM), lambda b, n: (b, n))
    wspec = pl.BlockSpec((None, B_BLOCK_DIM, B_BLOCK_DIM), lambda b, n: (n, 0, 0))
    return pl.pallas_call(
        _b_gates_kernel,
        out_shape=(jax.ShapeDtypeStruct((p, B_WIDTH), jnp.float32),) * 3,
        grid=(bsz, B_BLOCKS),
        in_specs=[pl.BlockSpec((t, B_BLOCK_DIM), lambda b, n: (b, g0 + n)),
                  pl.BlockSpec((t, B_BLOCK_DIM), lambda b, n: (b, x0 + n)),
                  pl.BlockSpec((CONV_W, B_BLOCK_DIM), vec), pl.BlockSpec((1, B_BLOCK_DIM), vec),
                  wspec, pl.BlockSpec((1, B_BLOCK_DIM), vec), wspec, pl.BlockSpec((1, B_BLOCK_DIM), vec),
                  pl.BlockSpec((1, B_BLOCK_DIM), vec)],
        out_specs=(blk,) * 3,
        compiler_params=_params(("parallel", "parallel")),
        name="b_gates",
    )(u, u, conv_w, conv_b.reshape(1, B_WIDTH), wa, ba.reshape(1, B_WIDTH), wx, bx.reshape(1, B_WIDTH), lam_c)


def _b_scan_kernel(a_ref, b_ref, gg_ref, y_ref, hl_ref, h_scr):
    c = pl.program_id(0)

    @pl.when(c == 0)
    def _():
        h_scr[...] = jnp.zeros_like(h_scr)

    def body(t, h):
        h = a_ref[:, t] * h + b_ref[:, t]
        y_ref[:, t] = (h * gg_ref[:, t]).astype(y_ref.dtype)
        return h

    h = lax.fori_loop(0, a_ref.shape[1], body, h_scr[...], unroll=8)
    h_scr[...] = h
    hl_ref[...] = h


def _b_scan(a, b, gg, bsz, t, out_dtype):
    shp = (bsz, t, B_BLOCKS, B_BLOCK_DIM)
    blk = pl.BlockSpec((bsz, B_SCAN_T, B_BLOCKS, B_BLOCK_DIM), lambda c: (0, c, 0, 0))
    y, h_last = pl.pallas_call(
        _b_scan_kernel,
        out_shape=(jax.ShapeDtypeStruct(shp, out_dtype), jax.ShapeDtypeStruct((bsz, B_BLOCKS, B_BLOCK_DIM), jnp.float32)),
        grid=(t // B_SCAN_T,),
        in_specs=[blk] * 3,
        out_specs=(blk, pl.BlockSpec((bsz, B_BLOCKS, B_BLOCK_DIM), lambda c: (0, 0, 0))),
        scratch_shapes=[pltpu.VMEM((bsz, B_BLOCKS, B_BLOCK_DIM), jnp.float32)],
        compiler_params=_params(("arbitrary",)),
        name="b_scan",
    )(a.reshape(shp), b.reshape(shp), gg.reshape(shp))
    return y.reshape(bsz * t, B_WIDTH), h_last.reshape(bsz, B_WIDTH)


GDN_CHUNK = 128
GDN_GATE_ROWS = 512
GDN_HEADS_PER_STEP = 2
GDN_INV_PASSES = 3
DB_LANE = 3 * C_HEADS
DA_LANE = DB_LANE + D_HEADS


def _softplus(x):
    return jnp.maximum(x, 0.0) + jnp.log1p(jnp.exp(-jnp.abs(x)))


def _d_gates_kernel(us_ref, dtb_ref, aneg_ref, tri_ref, beta_ref, gam_ref):
    tm = us_ref.shape[0]
    us = us_ref[...]
    g = aneg_ref[...] * _softplus(us + dtb_ref[...])
    gam = jnp.dot(tri_ref[...], g, preferred_element_type=jnp.float32, precision=HI)
    beta = jax.nn.sigmoid(us)
    for h in range(D_HEADS):
        sl = slice(h * HEAD_DIM, (h + 1) * HEAD_DIM)
        beta_ref[:, sl] = jnp.broadcast_to(beta[:, DB_LANE + h:DB_LANE + h + 1], (tm, HEAD_DIM))
        gam_ref[:, sl] = jnp.broadcast_to(gam[:, DA_LANE + h:DA_LANE + h + 1], (tm, HEAD_DIM))


def _d_gates(u_s, a_log, dt_bias):
    p = u_s.shape[0]
    tm = GDN_GATE_ROWS
    pad = lambda x: jnp.zeros((1, LANES), jnp.float32).at[0, DA_LANE:DA_LANE + D_HEADS].set(x)
    r = np.arange(tm)
    tri = ((r[:, None] >= r[None, :]) & (r[:, None] // GDN_CHUNK == r[None, :] // GDN_CHUNK)).astype(np.float32)
    fixed = lambda i: (0, 0)
    return pl.pallas_call(
        _d_gates_kernel,
        out_shape=(jax.ShapeDtypeStruct((p, D_V), jnp.float32),) * 2,
        grid=(p // tm,),
        in_specs=[pl.BlockSpec((tm, LANES), lambda i: (i, 0)), pl.BlockSpec((1, LANES), fixed),
                  pl.BlockSpec((1, LANES), fixed), pl.BlockSpec((tm, tm), fixed)],
        out_specs=(pl.BlockSpec((tm, D_V), lambda i: (i, 0)),) * 2,
        compiler_params=_params(("parallel",)),
        name="d_gates",
    )(u_s, pad(dt_bias), pad(-jnp.exp(a_log)), jnp.asarray(tri))


def _mm(a, b, passes=1):
    ah, bh = a.astype(jnp.bfloat16), b.astype(jnp.bfloat16)
    out = jnp.dot(ah, bh, preferred_element_type=jnp.float32)
    if passes == 3:
        al = (a - ah.astype(jnp.float32)).astype(jnp.bfloat16)
        bl = (b - bh.astype(jnp.float32)).astype(jnp.bfloat16)
        out = out + jnp.dot(ah, bl, preferred_element_type=jnp.float32) + jnp.dot(al, bh, preferred_element_type=jnp.float32)
    return out


def _mm_nt(a, b):
    return lax.dot_general(a.astype(jnp.bfloat16), b.astype(jnp.bfloat16), (((1,), (1,)), ((), ())),
                           preferred_element_type=jnp.float32)


def _d_delta_kernel(q_in, k_in, v_in, z_ref, beta_ref, gam_ref, cwq_ref, cwk_ref, cwv_ref, og_ref,
                    y_ref, s_ref, q_scr, k_scr, v_scr):
    t = q_in.shape[0]
    c = GDN_CHUNK

    def conv_silu(x_ref, cw_ref):
        x = x_ref[...]
        xx = jnp.concatenate([jnp.zeros((8, x.shape[1]), x.dtype), x], axis=0)
        acc = x * cw_ref[CONV_W - 1:CONV_W, :]
        for j in range(CONV_W - 1):
            acc = acc + pltpu.roll(xx, CONV_W - 1 - j, axis=0)[8:8 + t] * cw_ref[j:j + 1, :]
        return acc * jax.nn.sigmoid(acc)

    qa = conv_silu(q_in, cwq_ref)
    ka = conv_silu(k_in, cwk_ref)
    v_scr[...] = conv_silu(v_in, cwv_ref)
    for hh in range(GDN_HEADS_PER_STEP):
        sl = slice(hh * HEAD_DIM, (hh + 1) * HEAD_DIM)
        qh, kh = qa[:, sl], ka[:, sl]
        q_scr[:, sl] = qh * lax.rsqrt(jnp.sum(qh * qh, axis=-1, keepdims=True) + EPS) * D_DK ** -0.5
        k_scr[:, sl] = kh * lax.rsqrt(jnp.sum(kh * kh, axis=-1, keepdims=True) + EPS)
    s_ref[...] = jnp.zeros_like(s_ref)

    ri = lax.broadcasted_iota(jnp.int32, (c, c), 0)
    ci = lax.broadcasted_iota(jnp.int32, (c, c), 1)
    lower, strict = ri >= ci, ri > ci
    eye = (ri == ci).astype(jnp.float32)

    def chunk(n, carry):
        r0 = pl.multiple_of(n * c, c)
        rows = pl.ds(r0, c)
        for hh in range(GDN_HEADS_PER_STEP):
            sl = slice(hh * HEAD_DIM, (hh + 1) * HEAD_DIM)
            q, k, v = q_scr[rows, sl], k_scr[rows, sl], v_scr[rows, sl]
            beta, gam = beta_ref[rows, sl], gam_ref[rows, sl]
            gam_t = jnp.transpose(gam)
            decay = jnp.where(lower, jnp.exp(jnp.minimum(gam - gam_t, 0.0)), 0.0)
            lmat = jnp.where(strict, beta * _mm_nt(k, k) * decay, 0.0)
            npow = -lmat
            inv = eye + npow
            for _ in range(c.bit_length() - 2):
                npow = _mm(npow, npow, GDN_INV_PASSES)
                inv = inv + _mm(inv, npow, GDN_INV_PASSES)
            e_gam = jnp.exp(gam)
            u = _mm(inv, v * beta, GDN_INV_PASSES)
            w = _mm(inv, k * (beta * e_gam), GDN_INV_PASSES)
            qk = jnp.where(lower, _mm_nt(q, k) * decay, 0.0)
            gam_last = gam[c - 1:c, :]
            state = s_ref[0, hh]
            v_new = u - _mm(w, state)
            o = _mm(q * e_gam, state) + _mm(qk, v_new)
            k_dec_t = jnp.transpose(k * jnp.exp(gam_last - gam))
            s_ref[0, hh] = jnp.exp(gam_last) * state + _mm(k_dec_t, v_new)
            z = z_ref[rows, sl]
            o = o * lax.rsqrt(jnp.mean(o * o, axis=-1, keepdims=True) + EPS) * og_ref[...]
            y_ref[rows, sl] = (o * (z * jax.nn.sigmoid(z))).astype(y_ref.dtype)
        return carry

    lax.fori_loop(0, t // c, chunk, 0)


def _d_delta(u_d, beta_r, gam_r, conv_w, out_gain, bsz, t, out_dtype):
    hp = GDN_HEADS_PER_STEP
    w = hp * HEAD_DIM
    nb = D_V // w
    blk = lambda sec: pl.BlockSpec((t, w), lambda b, h: (b, sec * nb + h))
    cw = lambda sec: pl.BlockSpec((CONV_W, w), lambda b, h: (0, sec * nb + h))
    return pl.pallas_call(
        _d_delta_kernel,
        out_shape=(jax.ShapeDtypeStruct((bsz * t, D_V), out_dtype),
                   jax.ShapeDtypeStruct((bsz, D_HEADS, D_DK, D_DV), jnp.float32)),
        grid=(bsz, D_HEADS // hp),
        in_specs=[blk(0), blk(1), blk(2), blk(3), blk(0), blk(0), cw(0), cw(1), cw(2),
                  pl.BlockSpec((1, HEAD_DIM), lambda b, h: (0, 0))],
        out_specs=(blk(0), pl.BlockSpec((1, hp, D_DK, D_DV), lambda b, h: (b, h, 0, 0))),
        scratch_shapes=[pltpu.VMEM((t, w), jnp.float32)] * 3,
        compiler_params=_params(("parallel", "parallel")),
        name="d_delta",
    )(u_d, u_d, u_d, u_d, beta_r, gam_r, conv_w, conv_w, conv_w, out_gain.reshape(1, HEAD_DIM))


def _dot_hi(a, b):
    return jnp.dot(a, b, preferred_element_type=jnp.float32, precision=HI)


def _dot_hi_nt(a, b):
    return lax.dot_general(a, b, (((1,), (1,)), ((), ())), preferred_element_type=jnp.float32, precision=HI)


def _gelu_tanh(g):
    return 0.5 * g * (1.0 + jnp.tanh(math.sqrt(2.0 / math.pi) * (g + 0.044715 * (g * g * g))))


def _rows_to_block(rows, n_rows):
    ri = lax.broadcasted_iota(jnp.int32, (n_rows, 1), 0)
    out = jnp.zeros((n_rows, rows[0].shape[1]), jnp.float32)
    for i, r in enumerate(rows):
        out = jnp.where(ri == i, r, out)
    return out


def _a_sample_kernel(q0_ref, q1_ref, q2_ref, kn_ref, vn_ref, kc_ref, vc_ref, qg_ref, kg_ref, o_ref, ko_ref):
    h = pl.program_id(0)
    n_ctx = kc_ref.shape[0]
    qs = [_chunk_rms(r[...], qg_ref[...]) for r in (q0_ref, q1_ref, q2_ref)]
    k_new = _chunk_rms(kn_ref[...], kg_ref[...])
    v_new = vn_ref[...]
    ko_ref[...] = k_new
    q = _rows_to_block(qs, 8)
    gi = lax.broadcasted_iota(jnp.int32, (8, 1), 0)
    live = gi < A_N_GROUPS
    dil = jnp.where(gi == 0, A_PATTERNS[0][1], jnp.where(gi == 1, A_PATTERNS[1][1], A_PATTERNS[2][1]))
    win = jnp.where(gi == 0, A_PATTERNS[0][0], jnp.where(gi == 1, A_PATTERNS[1][0], A_PATTERNS[2][0]))
    slope = jnp.exp2((-8.0 / (A_N_GROUPS * A_HEADS)) * (gi * A_HEADS + h + 1).astype(jnp.float32))
    dist = n_ctx - lax.broadcasted_iota(jnp.int32, (1, n_ctx), 1)
    ok = live & ((dist & (dil - 1)) == 0) & (dist <= win)
    scale = HEAD_DIM ** -0.5
    s = jnp.where(ok, _dot_hi_nt(q, kc_ref[...]) * scale - slope * dist.astype(jnp.float32), NEG)
    s_new = jnp.where(live, jnp.sum(q * k_new, axis=-1, keepdims=True) * scale, NEG)
    m = jnp.max(jnp.maximum(jnp.max(s, axis=-1, keepdims=True), s_new), axis=0, keepdims=True)
    p = jnp.where(ok, jnp.exp(s - m), 0.0)
    p_new = jnp.sum(jnp.where(live, jnp.exp(s_new - m), 0.0), axis=0, keepdims=True)
    l = jnp.sum(jnp.sum(p, axis=-1, keepdims=True), axis=0, keepdims=True) + p_new
    o = _dot_hi(jnp.sum(p, axis=0, keepdims=True), vc_ref[...]) + p_new * v_new
    o_ref[...] = o / l


def _a_sample(us, cache, layer, q_gain, k_gain):
    bd = us.shape[0]
    n_ctx = cache.shape[2]
    cview = cache.reshape(cache.shape[0], bd, n_ctx, 2 * A_KV)
    us3 = us.reshape(bd, 1, -1)
    col = lambda c: pl.BlockSpec((None, 1, HEAD_DIM), lambda h, b, c=c: (b, 0, c + h))
    fixed = pl.BlockSpec((1, HEAD_DIM), lambda h, b: (0, 0))
    o, k_new = pl.pallas_call(
        _a_sample_kernel,
        out_shape=(jax.ShapeDtypeStruct((bd, 1, A_OUT), jnp.float32), jax.ShapeDtypeStruct((bd, 1, A_KV), jnp.float32)),
        grid=(A_HEADS, bd),
        in_specs=[col(0), col(A_HEADS), col(2 * A_HEADS), col(3 * A_HEADS), col(4 * A_HEADS),
                  pl.BlockSpec((None, None, n_ctx, HEAD_DIM), lambda h, b: (layer, b, 0, h)),
                  pl.BlockSpec((None, None, n_ctx, HEAD_DIM), lambda h, b: (layer, b, 0, A_HEADS + h)),
                  fixed, fixed],
        out_specs=(col(0), col(0)),
        compiler_params=_params(("parallel", "parallel")),
        name="a_sample",
    )(us3, us3, us3, us3, us3, cview, cview, q_gain.reshape(1, HEAD_DIM), k_gain.reshape(1, HEAD_DIM))
    return o.reshape(bd, A_OUT), k_new.reshape(bd, A_KV)


def _b_sample_kernel(g_ref, x_ref, buf_ref, h0_ref, cw_ref, cb_ref, wa_ref, ba_ref, wx_ref, bx_ref, lam_ref, y_ref, h_ref):
    x = x_ref[...]
    xc = x * cw_ref[CONV_W - 1:CONV_W, :] + cb_ref[...]
    for j in range(CONV_W - 1):
        xc = xc + buf_ref[j] * cw_ref[j:j + 1, :]
    r = jax.nn.sigmoid(_dot_hi(xc, wa_ref[...]) + ba_ref[...])
    ig = jax.nn.sigmoid(_dot_hi(xc, wx_ref[...]) + bx_ref[...])
    log_a = r * lam_ref[...]
    h = jnp.exp(log_a) * h0_ref[...] + jnp.sqrt(1.0 - jnp.exp(2.0 * log_a)) * (ig * xc)
    h_ref[...] = h
    y_ref[...] = h * _gelu_tanh(g_ref[...])


def _b_sample(us, conv_buf, h0, b_par):
    conv_w, conv_b, wa, ba, wx, bx, lam = b_par
    bd = us.shape[0]
    lam_c = (-RG_C * jax.nn.softplus(-lam)).reshape(1, B_WIDTH)
    g0 = (A_Q + 2 * A_KV) // B_BLOCK_DIM
    vec = lambda n: (0, n)
    blk = pl.BlockSpec((bd, B_BLOCK_DIM), vec)
    one = pl.BlockSpec((1, B_BLOCK_DIM), vec)
    wspec = pl.BlockSpec((None, B_BLOCK_DIM, B_BLOCK_DIM), lambda n: (n, 0, 0))
    return pl.pallas_call(
        _b_sample_kernel,
        out_shape=(jax.ShapeDtypeStruct((bd, B_WIDTH), jnp.float32),) * 2,
        grid=(B_BLOCKS,),
        in_specs=[pl.BlockSpec((bd, B_BLOCK_DIM), lambda n: (0, g0 + n)),
                  pl.BlockSpec((bd, B_BLOCK_DIM), lambda n: (0, g0 + B_BLOCKS + n)),
                  pl.BlockSpec((CONV_W - 1, bd, B_BLOCK_DIM), lambda n: (0, 0, n)), blk,
                  pl.BlockSpec((CONV_W, B_BLOCK_DIM), vec), one, wspec, one, wspec, one, one],
        out_specs=(blk, blk),
        compiler_params=_params(("parallel",)),
        name="b_sample",
    )(us, us, jnp.swapaxes(conv_buf, 0, 1), h0, conv_w, conv_b.reshape(1, B_WIDTH), wa, ba.reshape(1, B_WIDTH),
      wx, bx.reshape(1, B_WIDTH), lam_c)


def _d_sample_kernel(q_in, k_in, v_in, z_ref, bq_ref, bk_ref, bv_ref, us_ref, dtb_ref, aneg_ref,
                     cwq_ref, cwk_ref, cwv_ref, og_ref, s0_ref, y_ref, s_ref, o_scr):
    h = pl.program_id(0)
    bd = q_in.shape[0]

    def conv_silu(x_ref, b_ref, cw_ref):
        acc = x_ref[...] * cw_ref[CONV_W - 1:CONV_W, :]
        for j in range(CONV_W - 1):
            acc = acc + b_ref[j] * cw_ref[j:j + 1, :]
        return acc * jax.nn.sigmoid(acc)

    q = conv_silu(q_in, bq_ref, cwq_ref)
    k = conv_silu(k_in, bk_ref, cwk_ref)
    v = conv_silu(v_in, bv_ref, cwv_ref)
    q = q * lax.rsqrt(jnp.sum(q * q, axis=-1, keepdims=True) + EPS) * D_DK ** -0.5
    k = k * lax.rsqrt(jnp.sum(k * k, axis=-1, keepdims=True) + EPS)
    us = us_ref[...]
    lane = lax.broadcasted_iota(jnp.int32, (1, LANES), 1)
    beta = jnp.sum(jnp.where(lane == DB_LANE + h, jax.nn.sigmoid(us), 0.0), axis=-1, keepdims=True)
    g = jnp.sum(jnp.where(lane == DA_LANE + h, aneg_ref[...] * _softplus(us + dtb_ref[...]), 0.0), axis=-1, keepdims=True)
    a = jnp.exp(g)
    k_t = jnp.transpose(k)
    q_t = jnp.transpose(q)
    for b in range(bd):
        rb = slice(b, b + 1)
        state = s0_ref[b]
        kcol, qcol = k_t[:, rb], q_t[:, rb]
        k_s = jnp.sum(kcol * state, axis=0, keepdims=True)
        v_new = beta[rb] * (v[rb] - a[rb] * k_s)
        state = a[rb] * state + kcol * v_new
        s_ref[b] = state
        o_scr[rb, :] = jnp.sum(qcol * state, axis=0, keepdims=True)
    o = o_scr[...]
    o = o * lax.rsqrt(jnp.mean(o * o, axis=-1, keepdims=True) + EPS) * og_ref[...]
    z = z_ref[...]
    y_ref[...] = o * (z * jax.nn.sigmoid(z))


def _d_sample(u_d, u_s, conv_buf, s0, layer, d_par):
    conv_w, a_log, dt_bias, out_gain = d_par
    bd = u_d.shape[0]
    pad = lambda x: jnp.zeros((1, LANES), jnp.float32).at[0, DA_LANE:DA_LANE + D_HEADS].set(x)
    sec = lambda s: pl.BlockSpec((bd, HEAD_DIM), lambda h, s=s: (0, s * D_HEADS + h))
    bsec = lambda s: pl.BlockSpec((CONV_W - 1, bd, HEAD_DIM), lambda h, s=s: (0, 0, s * D_HEADS + h))
    cw = lambda s: pl.BlockSpec((CONV_W, HEAD_DIM), lambda h, s=s: (0, s * D_HEADS + h))
    one = pl.BlockSpec((1, LANES), lambda h: (0, 0))
    buf_t = jnp.swapaxes(conv_buf, 0, 1)
    return pl.pallas_call(
        _d_sample_kernel,
        out_shape=(jax.ShapeDtypeStruct((bd, D_V), jnp.float32),
                   jax.ShapeDtypeStruct((bd, D_HEADS, D_DK, D_DV), jnp.float32)),
        grid=(D_HEADS,),
        in_specs=[sec(0), sec(1), sec(2), sec(3), bsec(0), bsec(1), bsec(2),
                  pl.BlockSpec((bd, LANES), lambda h: (0, 0)), one, one, cw(0), cw(1), cw(2), one,
                  pl.BlockSpec((None, bd, None, D_DK, D_DV), lambda h: (layer, 0, h, 0, 0))],
        out_specs=(sec(0), pl.BlockSpec((bd, None, D_DK, D_DV), lambda h: (0, h, 0, 0))),
        scratch_shapes=[pltpu.VMEM((bd, HEAD_DIM), jnp.float32)],
        compiler_params=_params(("parallel",)),
        name="d_sample",
    )(u_d, u_d, u_d, u_d, buf_t, buf_t, buf_t, u_s, pad(dt_bias), pad(-jnp.exp(a_log)),
      conv_w, conv_w, conv_w, out_gain.reshape(1, HEAD_DIM), s0)


NSA_S_PAGES = 4
NSA_S_LANES = 384


def _nsa_s_sum_kernel(pt_ref, *refs):
    pages, w_ref, o_ref = refs[:NSA_S_PAGES], refs[NSA_S_PAGES], refs[NSA_S_PAGES + 1]
    per_page = PAGE_SIZE // C_BLOCK
    for i, page in enumerate(pages):
        x = page[...] * w_ref[...]
        for j in range(per_page):
            row = i * per_page + j
            r = jnp.sum(x[j * C_BLOCK:(j + 1) * C_BLOCK], axis=0)
            for kv in range(2):
                for hh in range(C_KV_HEADS):
                    c0 = (kv * C_KV_HEADS + hh) * HEAD_DIM
                    o_ref[row:row + 1, c0:c0 + HEAD_DIM] = r[kv, hh:hh + 1, :]


def _nsa_s_summaries(pool, layer, page_table, phi_k, phi_v):
    bd, n_pages = page_table.shape
    reps = PAGE_SIZE // C_BLOCK
    page_shape = (PAGE_SIZE, 2, C_KV_HEADS, HEAD_DIM)
    w = jnp.broadcast_to(jnp.stack([jnp.tile(phi_k, reps), jnp.tile(phi_v, reps)], axis=1)[:, :, None, None], page_shape)
    steps = n_pages // NSA_S_PAGES
    rows = NSA_S_PAGES * reps
    page_spec = lambda i: pl.BlockSpec((None, None) + page_shape,
                                       lambda b, s, pt, i=i: (layer, pt[b, s * NSA_S_PAGES + i], 0, 0, 0, 0))
    return pl.pallas_call(
        _nsa_s_sum_kernel,
        out_shape=jax.ShapeDtypeStruct((bd, steps * rows, 2 * C_KV), jnp.float32),
        grid_spec=pltpu.PrefetchScalarGridSpec(
            num_scalar_prefetch=1, grid=(bd, steps),
            in_specs=[page_spec(i) for i in range(NSA_S_PAGES)]
            + [pl.BlockSpec(page_shape, lambda b, s, pt: (0, 0, 0, 0))],
            out_specs=pl.BlockSpec((None, rows, 2 * C_KV), lambda b, s, pt: (b, s, 0))),
        compiler_params=_params(("parallel", "arbitrary")),
        name="nsa_s_summaries",
    )(page_table, *([pool] * NSA_S_PAGES), w)


def _nsa_s_query(q_ref, h):
    q = _rows_to_block([q_ref[:, g * HEAD_DIM:(g + 1) * HEAD_DIM] for g in range(C_GROUP)], 8)
    gi = lax.broadcasted_iota(jnp.int32, (8, 1), 0)
    slope = jnp.exp2(-0.5 * (h * C_GROUP + gi + 1).astype(jnp.float32))
    return q, gi, slope


def _nsa_s_gate(gate_ref, gi, br):
    gt = gate_ref[...]
    out = jnp.zeros((8, 1), jnp.float32)
    for g in range(C_GROUP):
        out = jnp.where(gi == g, gt[:, br * C_GROUP + g: br * C_GROUP + g + 1], out)
    return out


def _nsa_s_cmpwin_kernel(q_ref, kc_ref, vc_ref, wk_ref, wv_ref, wkn_ref, wvn_ref, gate_ref, part_ref, sel_ref, *, past):
    h = pl.program_id(1)
    q, gi, slope = _nsa_s_query(q_ref, h)
    live = gi < C_GROUP
    scale = HEAD_DIM ** -0.5
    nbc = kc_ref.shape[0]
    nidx = lax.broadcasted_iota(jnp.int32, (1, nbc), 1)
    centre = nidx.astype(jnp.float32) * C_BLOCK + (C_BLOCK - 1) / 2.0
    cmask = (nidx + 1) * C_BLOCK - 1 <= past
    s = jnp.where(cmask, _dot_hi_nt(q, kc_ref[...]) * scale - slope * jnp.abs(past - centre), NEG)
    m = jnp.max(s, axis=-1, keepdims=True)
    p = jnp.where(cmask, jnp.exp(s - m), 0.0)
    p = p / jnp.maximum(jnp.sum(p, axis=-1, keepdims=True), 1e-30)
    o_cmp = _dot_hi(p, vc_ref[...])
    imp = jnp.sum(jnp.where(live, p, 0.0), axis=0, keepdims=True)
    n_lanes = NSA_S_LANES
    n_blocks = -(-(past + 1) // C_BLOCK)
    cur = past // C_BLOCK
    lane = lax.broadcasted_iota(jnp.int32, (1, n_lanes), 1)
    imp = jnp.concatenate([imp, jnp.zeros((1, n_lanes - nbc), jnp.float32)], axis=1)
    forced = (lane == 0) | (lane == cur) | (lane == cur - 1)
    causal = (lane <= cur) & (lane < n_blocks)
    score = jnp.where(causal, imp + jnp.where(forced, FORCE, 0.0), NEG)
    score_col = jnp.transpose(jnp.broadcast_to(score, (8, n_lanes)))[:, 0:1]
    mi = lax.broadcasted_iota(jnp.int32, (n_lanes, 1), 0)
    ahead = (score_col > score) | ((score_col == score) & (mi < lane))
    rank = jnp.sum(ahead.astype(jnp.float32), axis=0, keepdims=True)
    sel = (rank < min(C_N_SEL, n_blocks)) & causal
    sel_ref[...] = jnp.broadcast_to(sel.astype(jnp.float32), (8, n_lanes))
    n_win = wk_ref.shape[0]
    dist = n_win - lax.broadcasted_iota(jnp.int32, (1, n_win), 1)
    wmask = dist <= C_WIN
    sw = jnp.where(wmask, _dot_hi_nt(q, wk_ref[...]) * scale - slope * dist.astype(jnp.float32), NEG)
    sw_new = jnp.sum(q * wkn_ref[...], axis=-1, keepdims=True) * scale
    mw = jnp.maximum(jnp.max(sw, axis=-1, keepdims=True), sw_new)
    pw = jnp.where(wmask, jnp.exp(sw - mw), 0.0)
    pw_new = jnp.exp(sw_new - mw)
    o_win = (_dot_hi(pw, wv_ref[...]) + pw_new * wvn_ref[...]) / (jnp.sum(pw, axis=-1, keepdims=True) + pw_new)
    part_ref[...] = _nsa_s_gate(gate_ref, gi, 0) * o_cmp + _nsa_s_gate(gate_ref, gi, 2) * o_win


def _nsa_s_cmpwin(qn, kvc, win_buf, layer, win_new, gates, past):
    bd = qn.shape[0]
    nbc = kvc.shape[1]
    n_win = win_buf.shape[2]
    wview = win_buf.reshape(win_buf.shape[0], bd, n_win, 2 * C_KV)
    rows = lambda c: pl.BlockSpec((None, 1, HEAD_DIM), lambda b, h, c=c: (b, 0, c + h))
    qn, win_new, gates = qn.reshape(bd, 1, -1), win_new.reshape(bd, 1, -1), gates.reshape(bd, 1, -1)
    return pl.pallas_call(
        functools.partial(_nsa_s_cmpwin_kernel, past=past),
        out_shape=(jax.ShapeDtypeStruct((bd, C_KV_HEADS, 8, HEAD_DIM), jnp.float32),
                   jax.ShapeDtypeStruct((bd, C_KV_HEADS, 8, NSA_S_LANES), jnp.float32)),
        grid=(bd, C_KV_HEADS),
        in_specs=[pl.BlockSpec((None, 1, C_GROUP * HEAD_DIM), lambda b, h: (b, 0, h)),
                  pl.BlockSpec((None, nbc, HEAD_DIM), lambda b, h: (b, 0, h)),
                  pl.BlockSpec((None, nbc, HEAD_DIM), lambda b, h: (b, 0, C_KV_HEADS + h)),
                  pl.BlockSpec((None, None, n_win, HEAD_DIM), lambda b, h: (layer, b, 0, h)),
                  pl.BlockSpec((None, None, n_win, HEAD_DIM), lambda b, h: (layer, b, 0, C_KV_HEADS + h)),
                  rows(0), rows(C_KV_HEADS), rows(0)],
        out_specs=(pl.BlockSpec((None, None, 8, HEAD_DIM), lambda b, h: (b, h, 0, 0)),
                   pl.BlockSpec((None, None, 8, NSA_S_LANES), lambda b, h: (b, h, 0, 0))),
        compiler_params=_params(("parallel", "parallel")),
        name="nsa_s_cmpwin",
    )(qn, kvc, kvc, wview, wview, win_new, win_new, gates)


def _nsa_s_sel_kernel(hp_ref, blk_ref, q_ref, kv_ref, kn_ref, vn_ref, gate_ref, part_ref, o_ref,
                      m_scr, l_scr, acc_scr, *, past):
    b = pl.program_id(0)
    h = pl.program_id(1)
    j = pl.program_id(2)
    scale = HEAD_DIM ** -0.5
    qs = [q_ref[:, g * HEAD_DIM:(g + 1) * HEAD_DIM] for g in range(C_GROUP)]

    @pl.when(j == 0)
    def _():
        m_scr[...] = jnp.zeros_like(m_scr)
        l_scr[...] = jnp.ones_like(l_scr)
        acc_scr[...] = jnp.zeros_like(acc_scr)
        for g in range(C_GROUP):
            s_new = jnp.sum(qs[g] * kn_ref[...], axis=-1, keepdims=True) * scale
            m_scr[g:g + 1, :] = jnp.broadcast_to(s_new, (1, HEAD_DIM))
            acc_scr[g:g + 1, :] = vn_ref[...]

    blk = blk_ref[(b * C_KV_HEADS + h) * pl.num_programs(2) + j]
    mine = lax.broadcasted_iota(jnp.int32, (1, C_KV_HEADS, HEAD_DIM), 1) == h
    k3 = jnp.where(mine, kv_ref[:, 0], 0.0)
    v3 = jnp.where(mine, kv_ref[:, 1], 0.0)
    dist = (past - blk * C_BLOCK - lax.broadcasted_iota(jnp.int32, (C_BLOCK, 1, HEAD_DIM), 0)).astype(jnp.float32)
    for g in range(C_GROUP):
        slope = jnp.exp2(-0.5 * (jnp.zeros((1, 1, HEAD_DIM), jnp.float32) + (h * C_GROUP + g + 1).astype(jnp.float32)))
        qk = jnp.sum(k3 * qs[g][None], axis=1, keepdims=True)
        s = jnp.sum(qk, axis=-1, keepdims=True) * scale - slope * dist
        m_old = m_scr[g:g + 1, :][None]
        m_new = jnp.maximum(m_old, jnp.max(s, axis=0, keepdims=True))
        p = jnp.exp(s - m_new)
        alpha = jnp.exp(m_old - m_new)
        l_scr[g:g + 1, :] = (alpha * l_scr[g:g + 1, :][None] + jnp.sum(p, axis=0, keepdims=True))[0]
        pv = jnp.sum(jnp.sum(p * v3, axis=0), axis=0, keepdims=True)
        acc_scr[g:g + 1, :] = alpha[0] * acc_scr[g:g + 1, :] + pv
        m_scr[g:g + 1, :] = m_new[0]

    @pl.when(j == pl.num_programs(2) - 1)
    def _():
        gi = lax.broadcasted_iota(jnp.int32, (8, 1), 0)
        o_ref[...] = part_ref[...] + _nsa_s_gate(gate_ref, gi, 1) * (acc_scr[...] / l_scr[...])


def _nsa_s_select(qn, pool, layer, half_pages, blocks, sel_new, gates, part, past):
    bd = qn.shape[0]
    n_sel = blocks.shape[-1]
    n_phys = pool.shape[1]
    halves = PAGE_SIZE // C_BLOCK
    half_shape = (C_BLOCK, 2, C_KV_HEADS, HEAD_DIM)
    pview = pool.reshape((pool.shape[0], n_phys * halves) + half_shape)
    rows = lambda c: pl.BlockSpec((None, 1, HEAD_DIM), lambda b, h, j, hp, bl, c=c: (b, 0, c + h))
    qn, sel_new, gates = qn.reshape(bd, 1, -1), sel_new.reshape(bd, 1, -1), gates.reshape(bd, 1, -1)
    flat = lambda b, h, j: (b * C_KV_HEADS + h) * n_sel + j
    blk4 = pl.BlockSpec((None, None, 8, HEAD_DIM), lambda b, h, j, hp, bl: (b, h, 0, 0))
    return pl.pallas_call(
        functools.partial(_nsa_s_sel_kernel, past=past),
        out_shape=jax.ShapeDtypeStruct((bd, C_KV_HEADS, 8, HEAD_DIM), jnp.float32),
        grid_spec=pltpu.PrefetchScalarGridSpec(
            num_scalar_prefetch=2, grid=(bd, C_KV_HEADS, n_sel),
            in_specs=[pl.BlockSpec((None, 1, C_GROUP * HEAD_DIM), lambda b, h, j, hp, bl: (b, 0, h)),
                      pl.BlockSpec((None, None) + half_shape,
                                   lambda b, h, j, hp, bl: (layer, hp[flat(b, h, j)], 0, 0, 0, 0)),
                      rows(0), rows(C_KV_HEADS), rows(0), blk4],
            out_specs=blk4,
            scratch_shapes=[pltpu.VMEM((8, HEAD_DIM), jnp.float32), pltpu.VMEM((8, HEAD_DIM), jnp.float32),
                            pltpu.VMEM((8, HEAD_DIM), jnp.float32)]),
        compiler_params=_params(("parallel", "parallel", "arbitrary")),
        name="nsa_s_select",
    )(half_pages.reshape(-1), blocks.reshape(-1), qn, pview, sel_new, sel_new, gates, part)


def _nsa_sample(u_c, u_s, q_gain, k_gain, phi_k, phi_v, cmp_pool, sel_pool, win_buf, layer, page_table):
    bd = u_c.shape[0]
    past = page_table.shape[1] * PAGE_SIZE
    qn, cmp_new, sel_new, win_new, gates = _nsa_prep(u_c, u_s, q_gain, k_gain)
    kvc = _nsa_s_summaries(cmp_pool, layer, page_table, phi_k, phi_v)
    part, sel = _nsa_s_cmpwin(qn, kvc, win_buf, layer, win_new, gates, past)
    n_past_blocks = past // C_BLOCK
    n_sel = min(C_N_SEL, n_past_blocks + 1) - 1
    _, blocks = lax.top_k(sel[:, :, 0, :n_past_blocks], n_sel)
    blocks = blocks.astype(jnp.int32)
    halves = PAGE_SIZE // C_BLOCK
    pages = jnp.take_along_axis(page_table[:, None, :], blocks // halves, axis=2)
    half_pages = (pages * halves + blocks % halves).astype(jnp.int32)
    out = _nsa_s_select(qn, sel_pool, layer, half_pages, blocks, sel_new, gates, part, past)
    return out[:, :, :C_GROUP, :].reshape(bd, C_Q), cmp_new, sel_new, win_new


def _even_mixer(u2d, w_out3d, layer, res2d, q_gain, k_gain, b_par, kv_cache, conv_buf, h0, bsz, t):
    u_x = u2d.reshape(bsz, t, -1)[:, :, -B_WIDTH:]
    if kv_cache is None:
        qn, akv = _a_prep(u2d, q_gain, k_gain)
        a_out = _a_attn_prompt(qn, akv, bsz, t, jnp.bfloat16)
        a_state = akv.reshape(bsz, t, 2, A_HEADS, HEAD_DIM)[:, -min(A_WIN_MAX, t):]
        a_dec, b_in, gg = _b_gates(u2d, b_par, bsz, t)
        b_out, h_new = _b_scan(a_dec, b_in, gg, bsz, t, jnp.float32)
        conv_new = u_x[:, -(CONV_W - 1):]
    else:
        a_out, k_new = _a_sample(u2d, kv_cache, layer, q_gain, k_gain)
        a_state = jnp.stack([k_new, u2d[:, A_Q + A_KV:A_Q + 2 * A_KV]], axis=1).reshape(bsz, t, 2, A_HEADS, HEAD_DIM)
        b_out, h_new = _b_sample(u2d, conv_buf, h0, b_par)
        conv_new = jnp.concatenate([conv_buf[:, 1:], u_x], axis=1)
    out = _matmul([a_out, b_out], w_out3d, layer, 0, D_MODEL, kv_cache is not None, res=res2d)
    return out, (a_state, conv_new, h_new)


def _odd_mixer(u_parts, w_out3d, layer, res2d, q_gain, k_gain, phi_k, phi_v, d_par,
               cmp_pool, sel_pool, page_table, win_buf, d_conv_buf, d_s0, bsz, t):
    u_c2d, u_d2d, u_s2d = u_parts
    kv_shape = (bsz, t, 2, C_KV_HEADS, HEAD_DIM)
    d_in = u_d2d.reshape(bsz, t, -1)[:, :, :D_CONV]
    conv_w, a_log, dt_bias, out_gain = d_par
    if page_table is None:
        qn, cmp2d, sel2d, win2d, gates = _nsa_prep(u_c2d, u_s2d, q_gain, k_gain)
        kc, vc = _nsa_summaries(cmp2d, phi_k, phi_v)
        o_c = _nsa_prompt(qn, kc, vc, sel2d, win2d, gates, bsz, t, jnp.bfloat16)
        win_state = win2d.reshape(kv_shape)[:, -min(C_WIN, t):]
        beta_r, gam_r = _d_gates(u_s2d, a_log, dt_bias)
        d_out, d_s = _d_delta(u_d2d, beta_r, gam_r, conv_w, out_gain, bsz, t, jnp.bfloat16)
        d_conv_new = d_in[:, -(CONV_W - 1):]
    else:
        o_c, cmp2d, sel2d, win2d = _nsa_sample(u_c2d, u_s2d, q_gain, k_gain, phi_k, phi_v,
                                                cmp_pool, sel_pool, win_buf, layer, page_table)
        win_state = win2d.reshape(kv_shape)
        d_out, d_s = _d_sample(u_d2d, u_s2d, d_conv_buf, d_s0, layer, d_par)
        d_conv_new = jnp.concatenate([d_conv_buf[:, 1:], d_in], axis=1)
    out = _matmul([o_c, d_out], w_out3d, layer, 0, D_MODEL, page_table is not None, res=res2d)
    return out, (cmp2d.reshape(kv_shape), sel2d.reshape(kv_shape), win_state, d_conv_new, d_s)


ODD_C = C_Q + 6 * C_KV
ODD_G0 = ODD_C
ODD_D0 = ODD_G0 + 3 * C_HEADS
ODD_D = 2 * D_QK + 2 * D_V
ODD_S0 = ODD_D0 + ODD_D
ODD_IN = ODD_S0 + 2 * D_HEADS


def kernel(x_prompt, x_sample, cache_a_kv, state_b_conv, state_b_h, cache_c_cmp_kv, cache_c_sel_kv, cache_c_win_kv, state_d_conv, state_d_S, page_table, norm_mix, norm_ffn, even_w_in, even_w_out, a_q_norm, a_k_norm, b_conv_w, b_conv_b, b_gate_a_w, b_gate_a_b, b_gate_x_w, b_gate_x_b, b_lambda, odd_w_in, odd_w_out, c_q_norm, c_k_norm, c_phi_k, c_phi_v, d_conv_w, d_a_log, d_dt_bias, d_out_norm, moe_group_w, moe_group_b, moe_expert_w, moe_expert_b, moe_w1, moe_w3, moe_w2):
    bp, sp, d = x_prompt.shape
    bs, ss, _ = x_sample.shape
    depth = norm_mix.shape[0]
    hp = x_prompt.reshape(bp * sp, d)
    hs = x_sample.reshape(bs * ss, d)
    outs = {k: [] for k in ("ak", "bc", "bh", "cc", "cs", "cw", "dc", "ds")}
    outs_s = {k: [] for k in outs}
    for l in range(depth):
        i = l // 2
        xp = _rmsnorm(hp, norm_mix[l], jnp.bfloat16)
        xs = _rmsnorm(hs, norm_mix[l], jnp.float32)
        if l % 2 == 0:
            b_par = (b_conv_w[i], b_conv_b[i], b_gate_a_w[i], b_gate_a_b[i], b_gate_x_w[i], b_gate_x_b[i], b_lambda[i])
            n_in = even_w_in.shape[-1]
            up = _matmul(xp, even_w_in, i, 0, n_in, False)
            us = _matmul(xs, even_w_in, i, 0, n_in, True)
            hp, st_p = _even_mixer(up, even_w_out, i, hp, a_q_norm[i], a_k_norm[i], b_par, None, None, None, bp, sp)
            hs, st_s = _even_mixer(us, even_w_out, i, hs, a_q_norm[i], a_k_norm[i], b_par,
                                   cache_a_kv, state_b_conv[i], state_b_h[i], bs, ss)
            for dst, st in ((outs, st_p), (outs_s, st_s)):
                dst["ak"].append(st[0]); dst["bc"].append(st[1]); dst["bh"].append(st[2])
        else:
            d_par = (d_conv_w[i], d_a_log[i], d_dt_bias[i], d_out_norm[i])
            w_d = odd_w_in[:, :, ODD_D0:ODD_S0]
            w_s = jnp.concatenate([odd_w_in[:, :, ODD_G0:ODD_D0], odd_w_in[:, :, ODD_S0:],
                                   jnp.zeros((odd_w_in.shape[0], d, LANES - 3 * C_HEADS - 2 * D_HEADS), jnp.float32)], axis=-1)
            ups = (_matmul(xp, odd_w_in, i, 0, ODD_C, False), _matmul(xp, w_d, i, 0, ODD_D, False), _matmul(xp, w_s, i, 0, LANES, False))
            uss = (_matmul(xs, odd_w_in, i, 0, ODD_C, True), _matmul(xs, w_d, i, 0, ODD_D, True), _matmul(xs, w_s, i, 0, LANES, True))
            hp, st_p = _odd_mixer(ups, odd_w_out, i, hp, c_q_norm[i], c_k_norm[i], c_phi_k[i], c_phi_v[i], d_par,
                                  None, None, None, None, None, None, bp, sp)
            hs, st_s = _odd_mixer(uss, odd_w_out, i, hs, c_q_norm[i], c_k_norm[i], c_phi_k[i], c_phi_v[i], d_par,
                                  cache_c_cmp_kv, cache_c_sel_kv, page_table, cache_c_win_kv,
                                  state_d_conv[i], state_d_S, bs, ss)
            for dst, st in ((outs, st_p), (outs_s, st_s)):
                dst["cc"].append(st[0]); dst["cs"].append(st[1]); dst["cw"].append(st[2])
                dst["dc"].append(st[3]); dst["ds"].append(st[4])
        w_router = jnp.concatenate([moe_group_w[l], moe_expert_w[l],
                                    jnp.zeros((d, ROUTER_PAD - N_GROUPS - N_EXPERTS), jnp.float32)], axis=-1)
        b_router = jnp.concatenate([moe_group_b[l], moe_expert_b[l],
                                    jnp.zeros((ROUTER_PAD - N_GROUPS - N_EXPERTS,), jnp.float32)])[None, :]
        hp = _hier_moe(hp, norm_ffn[l], w_router, b_router, moe_w1, moe_w3, moe_w2, l, 256, jnp.bfloat16)
        hs = _hier_moe(hs, norm_ffn[l], w_router, b_router, moe_w1, moe_w3, moe_w2, l, 8, jnp.float32)
    res = [hp.reshape(bp, sp, d), hs.reshape(bs, ss, d)]
    for key in ("ak", "bc", "bh", "cc", "cs", "cw", "dc", "ds"):
        res.append(jnp.stack(outs[key]))
        res.append(jnp.stack(outs_s[key]))
    return tuple(res)
```

```python
import functools
import math

import jax
import jax.numpy as jnp
import numpy as np
from jax import lax
from jax.experimental import pallas as pl
from jax.experimental.pallas import tpu as pltpu

D_MODEL = 4096
HEAD_DIM = 128
CONV_W = 4
BAND_BLOCK = 128
A_PATTERNS = ((128, 1), (512, 4), (2048, 16))
A_N_GROUPS = len(A_PATTERNS)
A_HEADS = D_MODEL // 512
A_WIN_MAX = max(w for w, _ in A_PATTERNS)
A_Q = A_N_GROUPS * A_HEADS * HEAD_DIM
A_KV = A_HEADS * HEAD_DIM
A_OUT = A_HEADS * HEAD_DIM
B_WIDTH = 3 * D_MODEL // 4
B_BLOCKS = B_WIDTH // HEAD_DIM
B_BLOCK_DIM = B_WIDTH // B_BLOCKS
RG_C = 8.0
C_HEADS = D_MODEL // 256
C_KV_HEADS = C_HEADS // 4
C_GROUP = C_HEADS // C_KV_HEADS
C_BLOCK = 64
C_N_SEL = 16
C_WIN = 512
C_Q = C_HEADS * HEAD_DIM
C_KV = C_KV_HEADS * HEAD_DIM
D_HEADS = D_MODEL // 256
D_DK = HEAD_DIM
D_DV = HEAD_DIM
D_QK = D_HEADS * D_DK
D_V = D_HEADS * D_DV
D_CONV = 2 * D_QK + D_V
N_GROUPS = 8
EXPERTS_PER_GROUP = 8
N_EXPERTS = N_GROUPS * EXPERTS_PER_GROUP
TOP_K = 2
D_EXPERT = D_MODEL // 8
PAGE_SIZE = 128
EPS = 1e-6
NEG = -1e30
FORCE = 1e4

LANES = 128
VMEM_LIMIT = 56 * 1024 * 1024
ROUTER_PAD = LANES

HI = lax.Precision.HIGHEST


def _params(sem):
    return pltpu.CompilerParams(dimension_semantics=sem, vmem_limit_bytes=VMEM_LIMIT)


def _rmsnorm_kernel(x_ref, g_ref, o_ref):
    x = x_ref[...]
    y = x * lax.rsqrt(jnp.mean(x * x, axis=-1, keepdims=True) + EPS)
    o_ref[...] = (y * g_ref[...]).astype(o_ref.dtype)


def _rmsnorm(x2d, gain, out_dtype):
    m, d = x2d.shape
    tm = min(m, 512)
    return pl.pallas_call(
        _rmsnorm_kernel,
        out_shape=jax.ShapeDtypeStruct((m, d), out_dtype),
        grid=(m // tm,),
        in_specs=[pl.BlockSpec((tm, d), lambda i: (i, 0)), pl.BlockSpec((1, d), lambda i: (0, 0))],
        out_specs=pl.BlockSpec((tm, d), lambda i: (i, 0)),
        compiler_params=_params(("parallel",)),
        name="rmsnorm",
    )(x2d, gain.reshape(1, d))


def _matmul_kernel(*refs, exact, has_res, k_bounds):
    n_x = len(k_bounds) - 1
    x_refs, w_ref = refs[:n_x], refs[n_x]
    if has_res:
        r_ref, o_ref, acc_ref = refs[n_x + 1:]
    else:
        o_ref, acc_ref = refs[n_x + 1:]
    k = pl.program_id(2)

    @pl.when(k == 0)
    def _():
        acc_ref[...] = jnp.zeros_like(acc_ref)

    for p, x_ref in enumerate(x_refs):
        @pl.when((k >= k_bounds[p]) & (k < k_bounds[p + 1]))
        def _(x_ref=x_ref):
            if exact:
                acc_ref[...] += jnp.dot(x_ref[...], w_ref[...], preferred_element_type=jnp.float32, precision=HI)
            else:
                acc_ref[...] += jnp.dot(x_ref[...].astype(jnp.bfloat16), w_ref[...].astype(jnp.bfloat16),
                                        preferred_element_type=jnp.float32)

    @pl.when(k == pl.num_programs(2) - 1)
    def _():
        out = acc_ref[...]
        if has_res:
            out = out + r_ref[...]
        o_ref[...] = out


def _matmul(xs, w3d, layer, col0, n, exact, res=None, tn=1024):
    if not isinstance(xs, (list, tuple)):
        xs = [xs]
    m = xs[0].shape[0]
    tm = min(m, 1024)
    tn = min(tn, n)
    tk = 512 if exact else 1024
    assert tm * tn * 4 * 5 + tk * (tm + tn) * 4 * 2 * len(xs) <= VMEM_LIMIT
    assert m % tm == 0 and n % tn == 0 and col0 % tn == 0 and all(x.shape[1] % tk == 0 for x in xs)
    jb = col0 // tn
    k_bounds = [0]
    for x in xs:
        k_bounds.append(k_bounds[-1] + x.shape[1] // tk)
    in_specs = []
    for p in range(len(xs)):
        lo, hi = k_bounds[p], k_bounds[p + 1]
        in_specs.append(pl.BlockSpec((tm, tk), lambda i, j, k, lo=lo, hi=hi: (i, jnp.clip(k, lo, hi - 1) - lo)))
    in_specs.append(pl.BlockSpec((None, tk, tn), lambda i, j, k: (layer, k, j + jb)))
    args = list(xs) + [w3d]
    if res is not None:
        in_specs.append(pl.BlockSpec((tm, tn), lambda i, j, k: (i, j)))
        args.append(res)
    return pl.pallas_call(
        functools.partial(_matmul_kernel, exact=exact, has_res=res is not None, k_bounds=tuple(k_bounds)),
        out_shape=jax.ShapeDtypeStruct((m, n), jnp.float32),
        grid=(m // tm, n // tn, k_bounds[-1]),
        in_specs=in_specs,
        out_specs=pl.BlockSpec((tm, tn), lambda i, j, k: (i, j)),
        scratch_shapes=[pltpu.VMEM((tm, tn), jnp.float32)],
        compiler_params=_params(("parallel", "parallel", "arbitrary")),
        name="proj",
    )(*args)


def _router_kernel(h_ref, g_ref, w_ref, b_ref, xn_ref, logit_ref):
    x = h_ref[...]
    y = x * lax.rsqrt(jnp.mean(x * x, axis=-1, keepdims=True) + EPS) * g_ref[...]
    xn_ref[...] = y.astype(xn_ref.dtype)
    logit_ref[...] = jnp.dot(y, w_ref[...], preferred_element_type=jnp.float32, precision=HI) + b_ref[...]


def _router(h2d, gain, w_router, b_router, xn_dtype):
    m, d = h2d.shape
    tm = min(m, 256)
    return pl.pallas_call(
        _router_kernel,
        out_shape=(jax.ShapeDtypeStruct((m, d), xn_dtype), jax.ShapeDtypeStruct((m, ROUTER_PAD), jnp.float32)),
        grid=(m // tm,),
        in_specs=[pl.BlockSpec((tm, d), lambda i: (i, 0)), pl.BlockSpec((1, d), lambda i: (0, 0)),
                  pl.BlockSpec((d, ROUTER_PAD), lambda i: (0, 0)), pl.BlockSpec((1, ROUTER_PAD), lambda i: (0, 0))],
        out_specs=(pl.BlockSpec((tm, d), lambda i: (i, 0)), pl.BlockSpec((tm, ROUTER_PAD), lambda i: (i, 0))),
        compiler_params=_params(("parallel",)),
        name="ffn_norm_router",
    )(h2d, gain.reshape(1, d), w_router, b_router)


MOE_K_CHUNK = 512


def _moe_up_kernel(be_ref, nu_ref, x_ref, w1_ref, w3_ref, o_ref, *, exact):
    tm = x_ref.shape[0]
    used = pl.program_id(0) < nu_ref[0]

    @pl.when(used)
    def _():
        a = jnp.zeros((tm, D_EXPERT), jnp.float32)
        b = jnp.zeros((tm, D_EXPERT), jnp.float32)
        for c in range(D_MODEL // MOE_K_CHUNK):
            sl = slice(c * MOE_K_CHUNK, (c + 1) * MOE_K_CHUNK)
            if exact:
                a += jnp.dot(x_ref[:, sl], w1_ref[sl, :], preferred_element_type=jnp.float32, precision=HI)
                b += jnp.dot(x_ref[:, sl], w3_ref[sl, :], preferred_element_type=jnp.float32, precision=HI)
            else:
                a += jnp.dot(x_ref[:, sl], w1_ref[sl, :].astype(jnp.bfloat16), preferred_element_type=jnp.float32)
                b += jnp.dot(x_ref[:, sl], w3_ref[sl, :].astype(jnp.bfloat16), preferred_element_type=jnp.float32)
        o_ref[...] = (a * jax.nn.sigmoid(a) * b).astype(o_ref.dtype)

    @pl.when(jnp.logical_not(used))
    def _():
        o_ref[...] = jnp.zeros_like(o_ref)


def _moe_down_kernel(be_ref, nu_ref, h_ref, w2_ref, g_ref, o_ref, *, exact):
    used = pl.program_id(0) < nu_ref[0]

    @pl.when(used)
    def _():
        if exact:
            out = jnp.dot(h_ref[...], w2_ref[...], preferred_element_type=jnp.float32, precision=HI)
        else:
            out = jnp.dot(h_ref[...], w2_ref[...].astype(jnp.bfloat16), preferred_element_type=jnp.float32)
        o_ref[...] = out * g_ref[...]

    @pl.when(jnp.logical_not(used))
    def _():
        o_ref[...] = jnp.zeros_like(o_ref)


def _moe_experts(x_rows, row_gate, blk_exp, n_used, w1, w3, w2, layer, tm):
    rows, d = x_rows.shape
    n_blk = rows // tm
    exact = x_rows.dtype == jnp.float32
    up = pl.pallas_call(
        functools.partial(_moe_up_kernel, exact=exact),
        out_shape=jax.ShapeDtypeStruct((rows, D_EXPERT), x_rows.dtype),
        grid_spec=pltpu.PrefetchScalarGridSpec(
            num_scalar_prefetch=2, grid=(n_blk,),
            in_specs=[pl.BlockSpec((tm, d), lambda i, be, nu: (jnp.minimum(i, nu[0] - 1), 0)),
                      pl.BlockSpec((None, None, d, D_EXPERT), lambda i, be, nu: (layer, be[i], 0, 0)),
                      pl.BlockSpec((None, None, d, D_EXPERT), lambda i, be, nu: (layer, be[i], 0, 0))],
            out_specs=pl.BlockSpec((tm, D_EXPERT), lambda i, be, nu: (i, 0))),
        compiler_params=_params(("arbitrary",)),
        name="moe_up",
    )(blk_exp, n_used, x_rows, w1, w3)
    return pl.pallas_call(
        functools.partial(_moe_down_kernel, exact=exact),
        out_shape=jax.ShapeDtypeStruct((rows, d), jnp.float32),
        grid_spec=pltpu.PrefetchScalarGridSpec(
            num_scalar_prefetch=2, grid=(n_blk,),
            in_specs=[pl.BlockSpec((tm, D_EXPERT), lambda i, be, nu: (jnp.minimum(i, nu[0] - 1), 0)),
                      pl.BlockSpec((None, None, D_EXPERT, d), lambda i, be, nu: (layer, be[i], 0, 0)),
                      pl.BlockSpec((tm, 1), lambda i, be, nu: (jnp.minimum(i, nu[0] - 1), 0))],
            out_specs=pl.BlockSpec((tm, d), lambda i, be, nu: (i, 0))),
        compiler_params=_params(("arbitrary",)),
        name="moe_down",
    )(blk_exp, n_used, up, w2, row_gate)


def _hier_moe(h2d, gain, w_router, b_router, w1, w3, w2, layer, tm, xn_dtype):
    n_tok, d = h2d.shape
    xn, logits = _router(h2d, gain, w_router, b_router, xn_dtype)
    g_logit = logits[:, :N_GROUPS]
    g_prob = jax.nn.softmax(g_logit, axis=-1)
    grp = jnp.argmax(g_logit, axis=-1)
    p_grp = jnp.take_along_axis(g_prob, grp[:, None], axis=1)[:, 0]
    e_logit = logits[:, N_GROUPS:N_GROUPS + N_EXPERTS].reshape(-1, N_GROUPS, EXPERTS_PER_GROUP)
    e_logit = jnp.take_along_axis(e_logit, grp[:, None, None], axis=1)[:, 0]
    e_val, e_idx = lax.top_k(e_logit, TOP_K)
    gates = p_grp[:, None] * jax.nn.softmax(e_val, axis=-1)
    experts = grp[:, None] * EXPERTS_PER_GROUP + e_idx
    n_asg = n_tok * TOP_K
    flat_e = experts.reshape(-1).astype(jnp.int32)
    order = jnp.argsort(flat_e)
    sorted_e = flat_e[order]
    counts = jnp.bincount(flat_e, length=N_EXPERTS)
    padded = (counts + tm - 1) // tm * tm
    pad_end = jnp.cumsum(padded)
    pad_start = pad_end - padded
    start = jnp.cumsum(counts) - counts
    dest = (pad_start[sorted_e] + jnp.arange(n_asg) - start[sorted_e]).astype(jnp.int32)
    n_rows = -(-(n_asg + N_EXPERTS * (tm - 1)) // tm) * tm
    n_blk = n_rows // tm
    tok = (order // TOP_K).astype(jnp.int32)
    row_tok = jnp.zeros((n_rows,), jnp.int32).at[dest].set(tok)
    row_gate = jnp.zeros((n_rows,), jnp.float32).at[dest].set(gates.reshape(-1)[order])
    blk_exp = jnp.minimum(jnp.searchsorted(pad_end, jnp.arange(n_blk) * tm, side='right'),
                          N_EXPERTS - 1).astype(jnp.int32)
    x_rows = xn[row_tok]
    n_used = (pad_end[-1:] // tm).astype(jnp.int32)
    out = _moe_experts(x_rows, row_gate[:, None], blk_exp, n_used, w1, w3, w2, layer, tm)
    asg_row = jnp.zeros((n_asg,), jnp.int32).at[order].set(dest).reshape(n_tok, TOP_K)
    y = h2d
    for j in range(TOP_K):
        y = y + out[asg_row[:, j]]
    return y


NSA_PREP_ROWS = 256
NSA_SUM_ROWS = 512
NSA_TQ = 128
NSA_GATE_LANES = LANES
TQ_SHIFT = NSA_TQ.bit_length() - 1
C_BLOCK_SHIFT = C_BLOCK.bit_length() - 1


def _chunk_rms(x, gain):
    return x * lax.rsqrt(jnp.mean(x * x, axis=-1, keepdims=True) + EPS) * gain


def _nsa_prep_kernel(uc_ref, us_ref, qg_ref, kg_ref, perm_ref, q_ref, cmp_ref, sel_ref, win_ref, gate_ref):
    for c in range(C_Q // HEAD_DIM):
        sl = slice(c * HEAD_DIM, (c + 1) * HEAD_DIM)
        q_ref[:, sl] = _chunk_rms(uc_ref[:, sl], qg_ref[...])
    for br, o_ref in enumerate((cmp_ref, sel_ref, win_ref)):
        base = C_Q + br * 2 * C_KV
        for c in range(C_KV_HEADS):
            sl = slice(c * HEAD_DIM, (c + 1) * HEAD_DIM)
            o_ref[:, sl] = _chunk_rms(uc_ref[:, base + c * HEAD_DIM: base + (c + 1) * HEAD_DIM], kg_ref[br:br + 1, :])
        o_ref[:, C_KV:] = uc_ref[:, base + C_KV: base + 2 * C_KV]
    gate_ref[...] = jnp.dot(jax.nn.sigmoid(us_ref[...]), perm_ref[...], preferred_element_type=jnp.float32, precision=HI)


def _nsa_prep(u_c, u_s, q_gain, k_gain):
    p = u_c.shape[0]
    tm = min(p, NSA_PREP_ROWS)
    perm = np.zeros((LANES, C_KV_HEADS * NSA_GATE_LANES), np.float32)
    for br in range(3):
        for h in range(C_KV_HEADS):
            for g in range(C_GROUP):
                perm[br * C_HEADS + h * C_GROUP + g, h * NSA_GATE_LANES + br * C_GROUP + g] = 1.0
    row = lambda i: (i, 0)
    fixed = lambda i: (0, 0)
    return pl.pallas_call(
        _nsa_prep_kernel,
        out_shape=(jax.ShapeDtypeStruct((p, C_Q), jnp.float32),) + (jax.ShapeDtypeStruct((p, 2 * C_KV), jnp.float32),) * 3
        + (jax.ShapeDtypeStruct((p, C_KV_HEADS * NSA_GATE_LANES), jnp.float32),),
        grid=(p // tm,),
        in_specs=[pl.BlockSpec((tm, ODD_C), row), pl.BlockSpec((tm, LANES), row), pl.BlockSpec((1, HEAD_DIM), fixed),
                  pl.BlockSpec((3, HEAD_DIM), fixed), pl.BlockSpec(perm.shape, fixed)],
        out_specs=(pl.BlockSpec((tm, C_Q), row),) + (pl.BlockSpec((tm, 2 * C_KV), row),) * 3
        + (pl.BlockSpec((tm, C_KV_HEADS * NSA_GATE_LANES), row),),
        compiler_params=_params(("parallel",)),
        name="nsa_prep",
    )(u_c, u_s, q_gain.reshape(1, HEAD_DIM), k_gain, jnp.asarray(perm))


def _nsa_sum_kernel(cmp_ref, phik_ref, phiv_ref, kc_ref, vc_ref):
    kc_ref[...] = jnp.dot(phik_ref[...], cmp_ref[:, :C_KV], preferred_element_type=jnp.float32, precision=HI)
    vc_ref[...] = jnp.dot(phiv_ref[...], cmp_ref[:, C_KV:], preferred_element_type=jnp.float32, precision=HI)


def _nsa_summaries(cmp_rows, phi_k, phi_v):
    p = cmp_rows.shape[0]
    tm = NSA_SUM_ROWS
    nb = tm // C_BLOCK
    eye = jnp.eye(nb, dtype=jnp.float32)
    big_k = jnp.kron(eye, phi_k[None, :])
    big_v = jnp.kron(eye, phi_v[None, :])
    return pl.pallas_call(
        _nsa_sum_kernel,
        out_shape=(jax.ShapeDtypeStruct((p // C_BLOCK, C_KV), jnp.float32),) * 2,
        grid=(p // tm,),
        in_specs=[pl.BlockSpec((tm, 2 * C_KV), lambda i: (i, 0)), pl.BlockSpec((nb, tm), lambda i: (0, 0)),
                  pl.BlockSpec((nb, tm), lambda i: (0, 0))],
        out_specs=(pl.BlockSpec((nb, C_KV), lambda i: (i, 0)),) * 2,
        compiler_params=_params(("parallel",)),
        name="nsa_summaries",
    )(cmp_rows, big_k, big_v)


def _nsa_prompt_kernel(q_ref, kc_ref, vc_ref, sk_ref, sv_ref, wk_ref, wv_ref, gate_ref, o_ref,
                       m_scr, l_scr, acc_scr, *, n_blocks):
    h = pl.program_id(1)
    qb = pl.program_id(2)
    tq = NSA_TQ
    rows = C_GROUP * tq
    scale = HEAD_DIM ** -0.5
    q = jnp.concatenate([q_ref[:, g * HEAD_DIM:(g + 1) * HEAD_DIM] for g in range(C_GROUP)], axis=0)
    row = lax.broadcasted_iota(jnp.int32, (rows, 1), 0)
    t_row = qb * tq + (row & (tq - 1))
    slope = jnp.exp2(-0.5 * (h * C_GROUP + (row >> TQ_SHIFT) + 1).astype(jnp.float32))

    nidx = lax.broadcasted_iota(jnp.int32, (1, n_blocks), 1)
    s = lax.dot_general(q, kc_ref[...], (((1,), (1,)), ((), ())), preferred_element_type=jnp.float32, precision=HI) * scale
    centre = nidx.astype(jnp.float32) * C_BLOCK + (C_BLOCK - 1) / 2.0
    s = s - slope * jnp.abs(t_row.astype(jnp.float32) - centre)
    cmask = (nidx + 1) * C_BLOCK - 1 <= t_row
    s = jnp.where(cmask, s, NEG)
    m = jnp.max(s, axis=-1, keepdims=True)
    p = jnp.where(cmask, jnp.exp(s - m), 0.0)
    l = jnp.sum(p, axis=-1, keepdims=True)
    p = p / jnp.maximum(l, 1e-30)
    o_cmp = jnp.dot(p, vc_ref[...], preferred_element_type=jnp.float32, precision=HI)
    imp = p[0:tq]
    for g in range(1, C_GROUP):
        imp = imp + p[g * tq:(g + 1) * tq]

    t_q = t_row[0:tq]
    cur = t_q >> C_BLOCK_SHIFT
    forced = (nidx == 0) | (nidx == cur) | (nidx == cur - 1)
    causal = nidx <= cur
    score = jnp.where(causal, imp + jnp.where(forced, FORCE, 0.0), NEG)
    rank = jnp.zeros((tq, n_blocks), jnp.int32)
    for mcol in range(n_blocks):
        cm = score[:, mcol:mcol + 1]
        ahead = (cm > score) | ((cm == score) & (mcol < nidx))
        rank = rank + ahead.astype(jnp.int32)
    selm = ((rank < min(C_N_SEL, n_blocks)) & causal).astype(jnp.bfloat16)

    q16 = q.astype(jnp.bfloat16)
    slope_b = jnp.broadcast_to(slope, (rows, tq))
    t_rel = jnp.broadcast_to(t_row, (rows, tq)) - lax.broadcasted_iota(jnp.int32, (rows, tq), 1)

    def attend(k_ref, v_ref, c, mask):
        start = pl.multiple_of(c * tq, tq)
        k = k_ref[pl.ds(start, tq), :].astype(jnp.bfloat16)
        v = v_ref[pl.ds(start, tq), :].astype(jnp.bfloat16)
        dist = t_rel - c * tq
        sc = lax.dot_general(q16, k, (((1,), (1,)), ((), ())), preferred_element_type=jnp.float32) * scale
        sc = sc - slope_b * dist.astype(jnp.float32)
        ok = mask(dist)
        sc = jnp.where(ok, sc, NEG)
        m_old = m_scr[...]
        m_new = jnp.maximum(m_old, jnp.max(sc, axis=-1, keepdims=True))
        pc = jnp.where(ok, jnp.exp(sc - m_new), 0.0)
        alpha = jnp.exp(m_old - m_new)
        l_scr[...] = alpha * l_scr[...] + jnp.sum(pc, axis=-1, keepdims=True)
        acc_scr[...] = alpha * acc_scr[...] + jnp.dot(pc.astype(jnp.bfloat16), v, preferred_element_type=jnp.float32)
        m_scr[...] = m_new

    def reset():
        m_scr[...] = jnp.full_like(m_scr, NEG)
        l_scr[...] = jnp.zeros_like(l_scr)
        acc_scr[...] = jnp.zeros_like(acc_scr)

    reset()
    brow = lax.broadcasted_iota(jnp.int32, (n_blocks, tq), 0)
    bcol = lax.broadcasted_iota(jnp.int32, (n_blocks, tq), 1)

    def sel_body(c, carry):
        expand = (brow == c * (tq // C_BLOCK) + (bcol >> C_BLOCK_SHIFT)).astype(jnp.bfloat16)
        sel_keys = jnp.dot(selm, expand, preferred_element_type=jnp.float32)
        sel_keys = jnp.concatenate([sel_keys] * C_GROUP, axis=0) > 0.5
        attend(sk_ref, sv_ref, c, lambda dist: sel_keys & (dist >= 0))
        return carry

    lax.fori_loop(0, qb + 1, sel_body, 0)
    o_sel = acc_scr[...] / l_scr[...]

    reset()

    def win_body(c, carry):
        attend(wk_ref, wv_ref, c, lambda dist: (dist >= 0) & (dist <= C_WIN))
        return carry

    lax.fori_loop(jnp.maximum(qb - C_WIN // tq, 0), qb + 1, win_body, 0)
    o_win = acc_scr[...] / l_scr[...]

    gt = gate_ref[...]
    outs = []
    for g in range(C_GROUP):
        rs = slice(g * tq, (g + 1) * tq)
        outs.append(gt[:, g:g + 1] * o_cmp[rs] + gt[:, C_GROUP + g:C_GROUP + g + 1] * o_sel[rs]
                    + gt[:, 2 * C_GROUP + g:2 * C_GROUP + g + 1] * o_win[rs])
    o_ref[...] = jnp.concatenate(outs, axis=1).astype(o_ref.dtype)


def _nsa_prompt(qn, kc, vc, sel_rows, win_rows, gates, bsz, t, out_dtype):
    tq = NSA_TQ
    assert tq == HEAD_DIM
    nq = t // tq
    n_blocks = t // C_BLOCK
    kv_k = pl.BlockSpec((t, HEAD_DIM), lambda b, h, i: (b, h))
    kv_v = pl.BlockSpec((t, HEAD_DIM), lambda b, h, i: (b, C_KV_HEADS + h))
    return pl.pallas_call(
        functools.partial(_nsa_prompt_kernel, n_blocks=n_blocks),
        out_shape=jax.ShapeDtypeStruct((bsz * t, C_Q), out_dtype),
        grid=(bsz, C_KV_HEADS, nq),
        in_specs=[pl.BlockSpec((tq, C_GROUP * HEAD_DIM), lambda b, h, i: (b * nq + i, h)),
                  pl.BlockSpec((n_blocks, HEAD_DIM), lambda b, h, i: (b, h)),
                  pl.BlockSpec((n_blocks, HEAD_DIM), lambda b, h, i: (b, h)),
                  kv_k, kv_v, kv_k, kv_v,
                  pl.BlockSpec((tq, NSA_GATE_LANES), lambda b, h, i: (b * nq + i, h))],
        out_specs=pl.BlockSpec((tq, C_GROUP * HEAD_DIM), lambda b, h, i: (b * nq + i, h)),
        scratch_shapes=[pltpu.VMEM((C_GROUP * tq, tq), jnp.float32), pltpu.VMEM((C_GROUP * tq, tq), jnp.float32),
                        pltpu.VMEM((C_GROUP * tq, HEAD_DIM), jnp.float32)],
        compiler_params=_params(("parallel", "parallel", "arbitrary")),
        name="nsa_prompt",
    )(qn, kc, vc, sel_rows, sel_rows, win_rows, win_rows, gates)


A_PREP_ROWS = 256
A_TQ = BAND_BLOCK
A_SLOPES = [[2.0 ** (-8.0 * (gi * A_HEADS + h + 1.0) / (A_N_GROUPS * A_HEADS)) for h in range(A_HEADS)]
            for gi in range(A_N_GROUPS)]


def _a_prep_kernel(q_in, k_in, v_in, qg_ref, kg_ref, q_ref, kv_ref):
    for c in range(A_Q // HEAD_DIM):
        sl = slice(c * HEAD_DIM, (c + 1) * HEAD_DIM)
        q_ref[:, sl] = _chunk_rms(q_in[:, sl], qg_ref[...])
    for c in range(A_HEADS):
        sl = slice(c * HEAD_DIM, (c + 1) * HEAD_DIM)
        kv_ref[:, sl] = _chunk_rms(k_in[:, sl], kg_ref[...])
    kv_ref[:, A_KV:] = v_in[...]


def _a_prep(u, q_gain, k_gain):
    p = u.shape[0]
    tm = A_PREP_ROWS
    fixed = lambda i: (0, 0)
    return pl.pallas_call(
        _a_prep_kernel,
        out_shape=(jax.ShapeDtypeStruct((p, A_Q), jnp.float32), jax.ShapeDtypeStruct((p, 2 * A_KV), jnp.float32)),
        grid=(p // tm,),
        in_specs=[pl.BlockSpec((tm, A_Q), lambda i: (i, 0)),
                  pl.BlockSpec((tm, A_KV), lambda i: (i, A_Q // A_KV)),
                  pl.BlockSpec((tm, A_KV), lambda i: (i, A_Q // A_KV + 1)),
                  pl.BlockSpec((1, HEAD_DIM), fixed), pl.BlockSpec((1, HEAD_DIM), fixed)],
        out_specs=(pl.BlockSpec((tm, A_Q), lambda i: (i, 0)), pl.BlockSpec((tm, 2 * A_KV), lambda i: (i, 0))),
        compiler_params=_params(("parallel",)),
        name="a_prep",
    )(u, u, u, q_gain.reshape(1, HEAD_DIM), k_gain.reshape(1, HEAD_DIM))


def _a_band_kernel(*refs, gi, dil, first, last):
    if first:
        q_ref, kp_ref, kc_ref, vp_ref, vc_ref = refs[:5]
        outs = refs[5:]
    else:
        q_ref, kp_ref, kc_ref, vp_ref, vc_ref, m_in, l_in, acc_in = refs[:8]
        outs = refs[8:]
    qb = pl.program_id(2)
    tq = A_TQ
    row = lax.broadcasted_iota(jnp.int32, (tq, 2 * tq), 0)
    col = lax.broadcasted_iota(jnp.int32, (tq, 2 * tq), 1)
    dist = tq + row - col
    ok = (dist >= 0) & (dist <= tq) & ((col >= tq) | (qb > 0))
    dist_f = (dist * dil).astype(jnp.float32)
    scale = HEAD_DIM ** -0.5
    for h in range(A_HEADS):
        sl = slice(h * HEAD_DIM, (h + 1) * HEAD_DIM)
        q = q_ref[:, sl].astype(jnp.bfloat16)
        k = jnp.concatenate([kp_ref[:, sl], kc_ref[:, sl]], axis=0).astype(jnp.bfloat16)
        v = jnp.concatenate([vp_ref[:, sl], vc_ref[:, sl]], axis=0).astype(jnp.bfloat16)
        s = lax.dot_general(q, k, (((1,), (1,)), ((), ())), preferred_element_type=jnp.float32) * scale
        s = jnp.where(ok, s - A_SLOPES[gi][h] * dist_f, NEG)
        m_row = jnp.max(s, axis=-1, keepdims=True)
        if first:
            m_new = jnp.broadcast_to(m_row, (tq, HEAD_DIM))
        else:
            m_old = m_in[:, sl]
            m_new = jnp.maximum(m_old, m_row)
        p = jnp.where(ok, jnp.exp(s - m_new[:, 0:1]), 0.0)
        l_new = jnp.broadcast_to(jnp.sum(p, axis=-1, keepdims=True), (tq, HEAD_DIM))
        acc = jnp.dot(p.astype(jnp.bfloat16), v, preferred_element_type=jnp.float32)
        if not first:
            alpha = jnp.exp(m_old - m_new)
            l_new = alpha * l_in[:, sl] + l_new
            acc = alpha * acc_in[:, sl] + acc
        if last:
            outs[0][:, sl] = (acc / l_new).astype(outs[0].dtype)
        else:
            outs[0][:, sl] = m_new
            outs[1][:, sl] = l_new
            outs[2][:, sl] = acc


def _a_band_group(gi, qn, akv, stats, bsz, t, out_dtype):
    w, dil = A_PATTERNS[gi]
    assert w // dil == A_TQ
    p = bsz * t
    n_res = t // dil
    nq = n_res // A_TQ
    first, last = gi == 0, gi == A_N_GROUPS - 1
    rows = p // dil
    qv = qn.reshape(rows, dil * A_Q)
    kvv = akv.reshape(rows, dil * 2 * A_KV)
    blk = (A_TQ, A_KV)
    cur = lambda b, r, i: b * nq + i
    prev = lambda b, r, i: b * nq + jnp.maximum(i - 1, 0)
    in_specs = [pl.BlockSpec(blk, lambda b, r, i: (cur(b, r, i), r * A_N_GROUPS + gi)),
                pl.BlockSpec(blk, lambda b, r, i: (prev(b, r, i), r * 2)),
                pl.BlockSpec(blk, lambda b, r, i: (cur(b, r, i), r * 2)),
                pl.BlockSpec(blk, lambda b, r, i: (prev(b, r, i), r * 2 + 1)),
                pl.BlockSpec(blk, lambda b, r, i: (cur(b, r, i), r * 2 + 1))]
    args = [qv, kvv, kvv, kvv, kvv]
    stat_spec = pl.BlockSpec(blk, lambda b, r, i: (cur(b, r, i), r))
    if not first:
        in_specs += [stat_spec] * 3
        args += [s.reshape(rows, dil * A_KV) for s in stats]
    if last:
        out_shape = jax.ShapeDtypeStruct((rows, dil * A_OUT), out_dtype)
        out_specs = stat_spec
    else:
        out_shape = (jax.ShapeDtypeStruct((rows, dil * A_KV), jnp.float32),) * 3
        out_specs = (stat_spec,) * 3
    out = pl.pallas_call(
        functools.partial(_a_band_kernel, gi=gi, dil=dil, first=first, last=last),
        out_shape=out_shape,
        grid=(bsz, dil, nq),
        in_specs=in_specs,
        out_specs=out_specs,
        compiler_params=_params(("parallel", "parallel", "arbitrary")),
        name="a_band_g%d" % gi,
    )(*args)
    if last:
        return out.reshape(p, A_OUT)
    return tuple(o.reshape(p, A_KV) for o in out)


def _a_attn_prompt(qn, akv, bsz, t, out_dtype):
    stats = None
    for gi in range(A_N_GROUPS):
        stats = _a_band_group(gi, qn, akv, stats, bsz, t, out_dtype)
    return stats


B_SCAN_T = 64


def _b_gates_kernel(g_ref, x_ref, cw_ref, cb_ref, wa_ref, ba_ref, wx_ref, bx_ref, lam_ref, a_ref, b_ref, gg_ref):
    t = x_ref.shape[0]
    x = x_ref[...]
    xx = jnp.concatenate([jnp.zeros((8, x.shape[1]), x.dtype), x], axis=0)
    xc = x * cw_ref[CONV_W - 1:CONV_W, :] + cb_ref[...]
    for j in range(CONV_W - 1):
        shift = CONV_W - 1 - j
        xc = xc + pltpu.roll(xx, shift, axis=0)[8:8 + t] * cw_ref[j:j + 1, :]
    x16 = xc.astype(jnp.bfloat16)
    r = jax.nn.sigmoid(jnp.dot(x16, wa_ref[...].astype(jnp.bfloat16), preferred_element_type=jnp.float32) + ba_ref[...])
    ig = jax.nn.sigmoid(jnp.dot(x16, wx_ref[...].astype(jnp.bfloat16), preferred_element_type=jnp.float32) + bx_ref[...])
    log_a = r * lam_ref[...]
    a = jnp.exp(log_a)
    a_ref[...] = a
    b_ref[...] = jnp.sqrt(1.0 - jnp.exp(2.0 * log_a)) * (ig * xc)
    g = g_ref[...]
    gg_ref[...] = 0.5 * g * (1.0 + jnp.tanh(math.sqrt(2.0 / math.pi) * (g + 0.044715 * (g * g * g))))


def _b_gates(u, b_par, bsz, t):
    conv_w, conv_b, wa, ba, wx, bx, lam = b_par
    lam_c = (-RG_C * jax.nn.softplus(-lam)).reshape(1, B_WIDTH)
    p = bsz * t
    g0 = (A_Q + 2 * A_KV) // B_BLOCK_DIM
    x0 = g0 + B_BLOCKS
    vec = lambda b, n: (0, n)
    blk = pl.BlockSpec((t, B_BLOCK_DIM), lambda b, n: (b, n))
    wspec = pl.BlockSpec((None, B_BLOCK_DIM, B_BLOCK_DIM), lambda b, n: (n, 0, 0))
    return pl.pallas_call(
        _b_gates_kernel,
        out_shape=(jax.ShapeDtypeStruct((p, B_WIDTH), jnp.float32),) * 3,
        grid=(bsz, B_BLOCKS),
        in_specs=[pl.BlockSpec((t, B_BLOCK_DIM), lambda b, n: (b, g0 + n)),
                  pl.BlockSpec((t, B_BLOCK_DIM), lambda b, n: (b, x0 + n)),
                  pl.BlockSpec((CONV_W, B_BLOCK_DIM), vec), pl.BlockSpec((1, B_BLOCK_DIM), vec),
                  wspec, pl.BlockSpec((1, B_BLOCK_DIM), vec), wspec, pl.BlockSpec((1, B_BLOCK_DIM), vec),
                  pl.BlockSpec((1, B_BLOCK_DIM), vec)],
        out_specs=(blk,) * 3,
        compiler_params=_params(("parallel", "parallel")),
        name="b_gates",
    )(u, u, conv_w, conv_b.reshape(1, B_WIDTH), wa, ba.reshape(1, B_WIDTH), wx, bx.reshape(1, B_WIDTH), lam_c)


def _b_scan_kernel(a_ref, b_ref, gg_ref, y_ref, hl_ref, h_scr):
    c = pl.program_id(0)

    @pl.when(c == 0)
    def _():
        h_scr[...] = jnp.zeros_like(h_scr)

    def body(t, h):
        h = a_ref[:, t] * h + b_ref[:, t]
        y_ref[:, t] = (h * gg_ref[:, t]).astype(y_ref.dtype)
        return h

    h = lax.fori_loop(0, a_ref.shape[1], body, h_scr[...], unroll=8)
    h_scr[...] = h
    hl_ref[...] = h


def _b_scan(a, b, gg, bsz, t, out_dtype):
    shp = (bsz, t, B_BLOCKS, B_BLOCK_DIM)
    blk = pl.BlockSpec((bsz, B_SCAN_T, B_BLOCKS, B_BLOCK_DIM), lambda c: (0, c, 0, 0))
    y, h_last = pl.pallas_call(
        _b_scan_kernel,
        out_shape=(jax.ShapeDtypeStruct(shp, out_dtype), jax.ShapeDtypeStruct((bsz, B_BLOCKS, B_BLOCK_DIM), jnp.float32)),
        grid=(t // B_SCAN_T,),
        in_specs=[blk] * 3,
        out_specs=(blk, pl.BlockSpec((bsz, B_BLOCKS, B_BLOCK_DIM), lambda c: (0, 0, 0))),
        scratch_shapes=[pltpu.VMEM((bsz, B_BLOCKS, B_BLOCK_DIM), jnp.float32)],
        compiler_params=_params(("arbitrary",)),
        name="b_scan",
    )(a.reshape(shp), b.reshape(shp), gg.reshape(shp))
    return y.reshape(bsz * t, B_WIDTH), h_last.reshape(bsz, B_WIDTH)


GDN_CHUNK = 128
GDN_GATE_ROWS = 512
GDN_HEADS_PER_STEP = 2
GDN_INV_PASSES = 3
DB_LANE = 3 * C_HEADS
DA_LANE = DB_LANE + D_HEADS


def _softplus(x):
    return jnp.maximum(x, 0.0) + jnp.log1p(jnp.exp(-jnp.abs(x)))


def _d_gates_kernel(us_ref, dtb_ref, aneg_ref, tri_ref, beta_ref, gam_ref):
    tm = us_ref.shape[0]
    us = us_ref[...]
    g = aneg_ref[...] * _softplus(us + dtb_ref[...])
    gam = jnp.dot(tri_ref[...], g, preferred_element_type=jnp.float32, precision=HI)
    beta = jax.nn.sigmoid(us)
    for h in range(D_HEADS):
        sl = slice(h * HEAD_DIM, (h + 1) * HEAD_DIM)
        beta_ref[:, sl] = jnp.broadcast_to(beta[:, DB_LANE + h:DB_LANE + h + 1], (tm, HEAD_DIM))
        gam_ref[:, sl] = jnp.broadcast_to(gam[:, DA_LANE + h:DA_LANE + h + 1], (tm, HEAD_DIM))


def _d_gates(u_s, a_log, dt_bias):
    p = u_s.shape[0]
    tm = GDN_GATE_ROWS
    pad = lambda x: jnp.zeros((1, LANES), jnp.float32).at[0, DA_LANE:DA_LANE + D_HEADS].set(x)
    r = np.arange(tm)
    tri = ((r[:, None] >= r[None, :]) & (r[:, None] // GDN_CHUNK == r[None, :] // GDN_CHUNK)).astype(np.float32)
    fixed = lambda i: (0, 0)
    return pl.pallas_call(
        _d_gates_kernel,
        out_shape=(jax.ShapeDtypeStruct((p, D_V), jnp.float32),) * 2,
        grid=(p // tm,),
        in_specs=[pl.BlockSpec((tm, LANES), lambda i: (i, 0)), pl.BlockSpec((1, LANES), fixed),
                  pl.BlockSpec((1, LANES), fixed), pl.BlockSpec((tm, tm), fixed)],
        out_specs=(pl.BlockSpec((tm, D_V), lambda i: (i, 0)),) * 2,
        compiler_params=_params(("parallel",)),
        name="d_gates",
    )(u_s, pad(dt_bias), pad(-jnp.exp(a_log)), jnp.asarray(tri))


def _mm(a, b, passes=1):
    ah, bh = a.astype(jnp.bfloat16), b.astype(jnp.bfloat16)
    out = jnp.dot(ah, bh, preferred_element_type=jnp.float32)
    if passes == 3:
        al = (a - ah.astype(jnp.float32)).astype(jnp.bfloat16)
        bl = (b - bh.astype(jnp.float32)).astype(jnp.bfloat16)
        out = out + jnp.dot(ah, bl, preferred_element_type=jnp.float32) + jnp.dot(al, bh, preferred_element_type=jnp.float32)
    return out


def _mm_nt(a, b):
    return lax.dot_general(a.astype(jnp.bfloat16), b.astype(jnp.bfloat16), (((1,), (1,)), ((), ())),
                           preferred_element_type=jnp.float32)


def _d_delta_kernel(q_in, k_in, v_in, z_ref, beta_ref, gam_ref, cwq_ref, cwk_ref, cwv_ref, og_ref,
                    y_ref, s_ref, q_scr, k_scr, v_scr):
    t = q_in.shape[0]
    c = GDN_CHUNK

    def conv_silu(x_ref, cw_ref):
        x = x_ref[...]
        xx = jnp.concatenate([jnp.zeros((8, x.shape[1]), x.dtype), x], axis=0)
        acc = x * cw_ref[CONV_W - 1:CONV_W, :]
        for j in range(CONV_W - 1):
            acc = acc + pltpu.roll(xx, CONV_W - 1 - j, axis=0)[8:8 + t] * cw_ref[j:j + 1, :]
        return acc * jax.nn.sigmoid(acc)

    qa = conv_silu(q_in, cwq_ref)
    ka = conv_silu(k_in, cwk_ref)
    v_scr[...] = conv_silu(v_in, cwv_ref)
    for hh in range(GDN_HEADS_PER_STEP):
        sl = slice(hh * HEAD_DIM, (hh + 1) * HEAD_DIM)
        qh, kh = qa[:, sl], ka[:, sl]
        q_scr[:, sl] = qh * lax.rsqrt(jnp.sum(qh * qh, axis=-1, keepdims=True) + EPS) * D_DK ** -0.5
        k_scr[:, sl] = kh * lax.rsqrt(jnp.sum(kh * kh, axis=-1, keepdims=True) + EPS)
    s_ref[...] = jnp.zeros_like(s_ref)

    ri = lax.broadcasted_iota(jnp.int32, (c, c), 0)
    ci = lax.broadcasted_iota(jnp.int32, (c, c), 1)
    lower, strict = ri >= ci, ri > ci
    eye = (ri == ci).astype(jnp.float32)

    def chunk(n, carry):
        r0 = pl.multiple_of(n * c, c)
        rows = pl.ds(r0, c)
        for hh in range(GDN_HEADS_PER_STEP):
            sl = slice(hh * HEAD_DIM, (hh + 1) * HEAD_DIM)
            q, k, v = q_scr[rows, sl], k_scr[rows, sl], v_scr[rows, sl]
            beta, gam = beta_ref[rows, sl], gam_ref[rows, sl]
            gam_t = jnp.transpose(gam)
            decay = jnp.where(lower, jnp.exp(jnp.minimum(gam - gam_t, 0.0)), 0.0)
            lmat = jnp.where(strict, beta * _mm_nt(k, k) * decay, 0.0)
            npow = -lmat
            inv = eye + npow
            for _ in range(c.bit_length() - 2):
                npow = _mm(npow, npow, GDN_INV_PASSES)
                inv = inv + _mm(inv, npow, GDN_INV_PASSES)
            e_gam = jnp.exp(gam)
            u = _mm(inv, v * beta, GDN_INV_PASSES)
            w = _mm(inv, k * (beta * e_gam), GDN_INV_PASSES)
            qk = jnp.where(lower, _mm_nt(q, k) * decay, 0.0)
            gam_last = gam[c - 1:c, :]
            state = s_ref[0, hh]
            v_new = u - _mm(w, state)
            o = _mm(q * e_gam, state) + _mm(qk, v_new)
            k_dec_t = jnp.transpose(k * jnp.exp(gam_last - gam))
            s_ref[0, hh] = jnp.exp(gam_last) * state + _mm(k_dec_t, v_new)
            z = z_ref[rows, sl]
            o = o * lax.rsqrt(jnp.mean(o * o, axis=-1, keepdims=True) + EPS) * og_ref[...]
            y_ref[rows, sl] = (o * (z * jax.nn.sigmoid(z))).astype(y_ref.dtype)
        return carry

    lax.fori_loop(0, t // c, chunk, 0)


def _d_delta(u_d, beta_r, gam_r, conv_w, out_gain, bsz, t, out_dtype):
    hp = GDN_HEADS_PER_STEP
    w = hp * HEAD_DIM
    nb = D_V // w
    blk = lambda sec: pl.BlockSpec((t, w), lambda b, h: (b, sec * nb + h))
    cw = lambda sec: pl.BlockSpec((CONV_W, w), lambda b, h: (0, sec * nb + h))
    return pl.pallas_call(
        _d_delta_kernel,
        out_shape=(jax.ShapeDtypeStruct((bsz * t, D_V), out_dtype),
                   jax.ShapeDtypeStruct((bsz, D_HEADS, D_DK, D_DV), jnp.float32)),
        grid=(bsz, D_HEADS // hp),
        in_specs=[blk(0), blk(1), blk(2), blk(3), blk(0), blk(0), cw(0), cw(1), cw(2),
                  pl.BlockSpec((1, HEAD_DIM), lambda b, h: (0, 0))],
        out_specs=(blk(0), pl.BlockSpec((1, hp, D_DK, D_DV), lambda b, h: (b, h, 0, 0))),
        scratch_shapes=[pltpu.VMEM((t, w), jnp.float32)] * 3,
        compiler_params=_params(("parallel", "parallel")),
        name="d_delta",
    )(u_d, u_d, u_d, u_d, beta_r, gam_r, conv_w, conv_w, conv_w, out_gain.reshape(1, HEAD_DIM))


def _dot_hi(a, b):
    return jnp.dot(a, b, preferred_element_type=jnp.float32, precision=HI)


def _dot_hi_nt(a, b):
    return lax.dot_general(a, b, (((1,), (1,)), ((), ())), preferred_element_type=jnp.float32, precision=HI)


def _gelu_tanh(g):
    return 0.5 * g * (1.0 + jnp.tanh(math.sqrt(2.0 / math.pi) * (g + 0.044715 * (g * g * g))))


def _rows_to_block(rows, n_rows):
    ri = lax.broadcasted_iota(jnp.int32, (n_rows, 1), 0)
    out = jnp.zeros((n_rows, rows[0].shape[1]), jnp.float32)
    for i, r in enumerate(rows):
        out = jnp.where(ri == i, r, out)
    return out


def _a_sample_kernel(q0_ref, q1_ref, q2_ref, kn_ref, vn_ref, kc_ref, vc_ref, qg_ref, kg_ref, o_ref, ko_ref):
    h = pl.program_id(0)
    n_ctx = kc_ref.shape[0]
    qs = [_chunk_rms(r[...], qg_ref[...]) for r in (q0_ref, q1_ref, q2_ref)]
    k_new = _chunk_rms(kn_ref[...], kg_ref[...])
    v_new = vn_ref[...]
    ko_ref[...] = k_new
    q = _rows_to_block(qs, 8)
    gi = lax.broadcasted_iota(jnp.int32, (8, 1), 0)
    live = gi < A_N_GROUPS
    dil = jnp.where(gi == 0, A_PATTERNS[0][1], jnp.where(gi == 1, A_PATTERNS[1][1], A_PATTERNS[2][1]))
    win = jnp.where(gi == 0, A_PATTERNS[0][0], jnp.where(gi == 1, A_PATTERNS[1][0], A_PATTERNS[2][0]))
    slope = jnp.exp2((-8.0 / (A_N_GROUPS * A_HEADS)) * (gi * A_HEADS + h + 1).astype(jnp.float32))
    dist = n_ctx - lax.broadcasted_iota(jnp.int32, (1, n_ctx), 1)
    ok = live & ((dist & (dil - 1)) == 0) & (dist <= win)
    scale = HEAD_DIM ** -0.5
    s = jnp.where(ok, _dot_hi_nt(q, kc_ref[...]) * scale - slope * dist.astype(jnp.float32), NEG)
    s_new = jnp.where(live, jnp.sum(q * k_new, axis=-1, keepdims=True) * scale, NEG)
    m = jnp.max(jnp.maximum(jnp.max(s, axis=-1, keepdims=True), s_new), axis=0, keepdims=True)
    p = jnp.where(ok, jnp.exp(s - m), 0.0)
    p_new = jnp.sum(jnp.where(live, jnp.exp(s_new - m), 0.0), axis=0, keepdims=True)
    l = jnp.sum(jnp.sum(p, axis=-1, keepdims=True), axis=0, keepdims=True) + p_new
    o = _dot_hi(jnp.sum(p, axis=0, keepdims=True), vc_ref[...]) + p_new * v_new
    o_ref[...] = o / l


def _a_sample(us, cache, layer, q_gain, k_gain):
    bd = us.shape[0]
    n_ctx = cache.shape[2]
    cview = cache.reshape(cache.shape[0], bd, n_ctx, 2 * A_KV)
    us3 = us.reshape(bd, 1, -1)
    col = lambda c: pl.BlockSpec((None, 1, HEAD_DIM), lambda h, b, c=c: (b, 0, c + h))
    fixed = pl.BlockSpec((1, HEAD_DIM), lambda h, b: (0, 0))
    o, k_new = pl.pallas_call(
        _a_sample_kernel,
        out_shape=(jax.ShapeDtypeStruct((bd, 1, A_OUT), jnp.float32), jax.ShapeDtypeStruct((bd, 1, A_KV), jnp.float32)),
        grid=(A_HEADS, bd),
        in_specs=[col(0), col(A_HEADS), col(2 * A_HEADS), col(3 * A_HEADS), col(4 * A_HEADS),
                  pl.BlockSpec((None, None, n_ctx, HEAD_DIM), lambda h, b: (layer, b, 0, h)),
                  pl.BlockSpec((None, None, n_ctx, HEAD_DIM), lambda h, b: (layer, b, 0, A_HEADS + h)),
                  fixed, fixed],
        out_specs=(col(0), col(0)),
        compiler_params=_params(("parallel", "parallel")),
        name="a_sample",
    )(us3, us3, us3, us3, us3, cview, cview, q_gain.reshape(1, HEAD_DIM), k_gain.reshape(1, HEAD_DIM))
    return o.reshape(bd, A_OUT), k_new.reshape(bd, A_KV)


def _b_sample_kernel(g_ref, x_ref, buf_ref, h0_ref, cw_ref, cb_ref, wa_ref, ba_ref, wx_ref, bx_ref, lam_ref, y_ref, h_ref):
    x = x_ref[...]
    xc = x * cw_ref[CONV_W - 1:CONV_W, :] + cb_ref[...]
    for j in range(CONV_W - 1):
        xc = xc + buf_ref[j] * cw_ref[j:j + 1, :]
    r = jax.nn.sigmoid(_dot_hi(xc, wa_ref[...]) + ba_ref[...])
    ig = jax.nn.sigmoid(_dot_hi(xc, wx_ref[...]) + bx_ref[...])
    log_a = r * lam_ref[...]
    h = jnp.exp(log_a) * h0_ref[...] + jnp.sqrt(1.0 - jnp.exp(2.0 * log_a)) * (ig * xc)
    h_ref[...] = h
    y_ref[...] = h * _gelu_tanh(g_ref[...])


def _b_sample(us, conv_buf, h0, b_par):
    conv_w, conv_b, wa, ba, wx, bx, lam = b_par
    bd = us.shape[0]
    lam_c = (-RG_C * jax.nn.softplus(-lam)).reshape(1, B_WIDTH)
    g0 = (A_Q + 2 * A_KV) // B_BLOCK_DIM
    vec = lambda n: (0, n)
    blk = pl.BlockSpec((bd, B_BLOCK_DIM), vec)
    one = pl.BlockSpec((1, B_BLOCK_DIM), vec)
    wspec = pl.BlockSpec((None, B_BLOCK_DIM, B_BLOCK_DIM), lambda n: (n, 0, 0))
    return pl.pallas_call(
        _b_sample_kernel,
        out_shape=(jax.ShapeDtypeStruct((bd, B_WIDTH), jnp.float32),) * 2,
        grid=(B_BLOCKS,),
        in_specs=[pl.BlockSpec((bd, B_BLOCK_DIM), lambda n: (0, g0 + n)),
                  pl.BlockSpec((bd, B_BLOCK_DIM), lambda n: (0, g0 + B_BLOCKS + n)),
                  pl.BlockSpec((CONV_W - 1, bd, B_BLOCK_DIM), lambda n: (0, 0, n)), blk,
                  pl.BlockSpec((CONV_W, B_BLOCK_DIM), vec), one, wspec, one, wspec, one, one],
        out_specs=(blk, blk),
        compiler_params=_params(("parallel",)),
        name="b_sample",
    )(us, us, jnp.swapaxes(conv_buf, 0, 1), h0, conv_w, conv_b.reshape(1, B_WIDTH), wa, ba.reshape(1, B_WIDTH),
      wx, bx.reshape(1, B_WIDTH), lam_c)


def _d_sample_kernel(q_in, k_in, v_in, z_ref, bq_ref, bk_ref, bv_ref, us_ref, dtb_ref, aneg_ref,
                     cwq_ref, cwk_ref, cwv_ref, og_ref, s0_ref, y_ref, s_ref, o_scr):
    h = pl.program_id(0)
    bd = q_in.shape[0]

    def conv_silu(x_ref, b_ref, cw_ref):
        acc = x_ref[...] * cw_ref[CONV_W - 1:CONV_W, :]
        for j in range(CONV_W - 1):
            acc = acc + b_ref[j] * cw_ref[j:j + 1, :]
        return acc * jax.nn.sigmoid(acc)

    q = conv_silu(q_in, bq_ref, cwq_ref)
    k = conv_silu(k_in, bk_ref, cwk_ref)
    v = conv_silu(v_in, bv_ref, cwv_ref)
    q = q * lax.rsqrt(jnp.sum(q * q, axis=-1, keepdims=True) + EPS) * D_DK ** -0.5
    k = k * lax.rsqrt(jnp.sum(k * k, axis=-1, keepdims=True) + EPS)
    us = us_ref[...]
    lane = lax.broadcasted_iota(jnp.int32, (1, LANES), 1)
    beta = jnp.sum(jnp.where(lane == DB_LANE + h, jax.nn.sigmoid(us), 0.0), axis=-1, keepdims=True)
    g = jnp.sum(jnp.where(lane == DA_LANE + h, aneg_ref[...] * _softplus(us + dtb_ref[...]), 0.0), axis=-1, keepdims=True)
    a = jnp.exp(g)
    k_t = jnp.transpose(k)
    q_t = jnp.transpose(q)
    for b in range(bd):
        rb = slice(b, b + 1)
        state = s0_ref[b]
        kcol, qcol = k_t[:, rb], q_t[:, rb]
        k_s = jnp.sum(kcol * state, axis=0, keepdims=True)
        v_new = beta[rb] * (v[rb] - a[rb] * k_s)
        state = a[rb] * state + kcol * v_new
        s_ref[b] = state
        o_scr[rb, :] = jnp.sum(qcol * state, axis=0, keepdims=True)
    o = o_scr[...]
    o = o * lax.rsqrt(jnp.mean(o * o, axis=-1, keepdims=True) + EPS) * og_ref[...]
    z = z_ref[...]
    y_ref[...] = o * (z * jax.nn.sigmoid(z))


def _d_sample(u_d, u_s, conv_buf, s0, layer, d_par):
    conv_w, a_log, dt_bias, out_gain = d_par
    bd = u_d.shape[0]
    pad = lambda x: jnp.zeros((1, LANES), jnp.float32).at[0, DA_LANE:DA_LANE + D_HEADS].set(x)
    sec = lambda s: pl.BlockSpec((bd, HEAD_DIM), lambda h, s=s: (0, s * D_HEADS + h))
    bsec = lambda s: pl.BlockSpec((CONV_W - 1, bd, HEAD_DIM), lambda h, s=s: (0, 0, s * D_HEADS + h))
    cw = lambda s: pl.BlockSpec((CONV_W, HEAD_DIM), lambda h, s=s: (0, s * D_HEADS + h))
    one = pl.BlockSpec((1, LANES), lambda h: (0, 0))
    buf_t = jnp.swapaxes(conv_buf, 0, 1)
    return pl.pallas_call(
        _d_sample_kernel,
        out_shape=(jax.ShapeDtypeStruct((bd, D_V), jnp.float32),
                   jax.ShapeDtypeStruct((bd, D_HEADS, D_DK, D_DV), jnp.float32)),
        grid=(D_HEADS,),
        in_specs=[sec(0), sec(1), sec(2), sec(3), bsec(0), bsec(1), bsec(2),
                  pl.BlockSpec((bd, LANES), lambda h: (0, 0)), one, one, cw(0), cw(1), cw(2), one,
                  pl.BlockSpec((None, bd, None, D_DK, D_DV), lambda h: (layer, 0, h, 0, 0))],
        out_specs=(sec(0), pl.BlockSpec((bd, None, D_DK, D_DV), lambda h: (0, h, 0, 0))),
        scratch_shapes=[pltpu.VMEM((bd, HEAD_DIM), jnp.float32)],
        compiler_params=_params(("parallel",)),
        name="d_sample",
    )(u_d, u_d, u_d, u_d, buf_t, buf_t, buf_t, u_s, pad(dt_bias), pad(-jnp.exp(a_log)),
      conv_w, conv_w, conv_w, out_gain.reshape(1, HEAD_DIM), s0)


NSA_S_PAGES = 4
NSA_S_LANES = 384


def _nsa_s_sum_kernel(pt_ref, *refs):
    pages, w_ref, o_ref = refs[:NSA_S_PAGES], refs[NSA_S_PAGES], refs[NSA_S_PAGES + 1]
    per_page = PAGE_SIZE // C_BLOCK
    for i, page in enumerate(pages):
        x = page[...] * w_ref[...]
        for j in range(per_page):
            row = i * per_page + j
            r = jnp.sum(x[j * C_BLOCK:(j + 1) * C_BLOCK], axis=0)
            for kv in range(2):
                for hh in range(C_KV_HEADS):
                    c0 = (kv * C_KV_HEADS + hh) * HEAD_DIM
                    o_ref[row:row + 1, c0:c0 + HEAD_DIM] = r[kv, hh:hh + 1, :]


def _nsa_s_summaries(pool, layer, page_table, phi_k, phi_v):
    bd, n_pages = page_table.shape
    reps = PAGE_SIZE // C_BLOCK
    page_shape = (PAGE_SIZE, 2, C_KV_HEADS, HEAD_DIM)
    w = jnp.broadcast_to(jnp.stack([jnp.tile(phi_k, reps), jnp.tile(phi_v, reps)], axis=1)[:, :, None, None], page_shape)
    steps = n_pages // NSA_S_PAGES
    rows = NSA_S_PAGES * reps
    page_spec = lambda i: pl.BlockSpec((None, None) + page_shape,
                                       lambda b, s, pt, i=i: (layer, pt[b, s * NSA_S_PAGES + i], 0, 0, 0, 0))
    return pl.pallas_call(
        _nsa_s_sum_kernel,
        out_shape=jax.ShapeDtypeStruct((bd, steps * rows, 2 * C_KV), jnp.float32),
        grid_spec=pltpu.PrefetchScalarGridSpec(
            num_scalar_prefetch=1, grid=(bd, steps),
            in_specs=[page_spec(i) for i in range(NSA_S_PAGES)]
            + [pl.BlockSpec(page_shape, lambda b, s, pt: (0, 0, 0, 0))],
            out_specs=pl.BlockSpec((None, rows, 2 * C_KV), lambda b, s, pt: (b, s, 0))),
        compiler_params=_params(("parallel", "arbitrary")),
        name="nsa_s_summaries",
    )(page_table, *([pool] * NSA_S_PAGES), w)


def _nsa_s_query(q_ref, h):
    q = _rows_to_block([q_ref[:, g * HEAD_DIM:(g + 1) * HEAD_DIM] for g in range(C_GROUP)], 8)
    gi = lax.broadcasted_iota(jnp.int32, (8, 1), 0)
    slope = jnp.exp2(-0.5 * (h * C_GROUP + gi + 1).astype(jnp.float32))
    return q, gi, slope


def _nsa_s_gate(gate_ref, gi, br):
    gt = gate_ref[...]
    out = jnp.zeros((8, 1), jnp.float32)
    for g in range(C_GROUP):
        out = jnp.where(gi == g, gt[:, br * C_GROUP + g: br * C_GROUP + g + 1], out)
    return out


def _nsa_s_cmpwin_kernel(q_ref, kc_ref, vc_ref, wk_ref, wv_ref, wkn_ref, wvn_ref, gate_ref, part_ref, sel_ref, *, past):
    h = pl.program_id(1)
    q, gi, slope = _nsa_s_query(q_ref, h)
    live = gi < C_GROUP
    scale = HEAD_DIM ** -0.5
    nbc = kc_ref.shape[0]
    nidx = lax.broadcasted_iota(jnp.int32, (1, nbc), 1)
    centre = nidx.astype(jnp.float32) * C_BLOCK + (C_BLOCK - 1) / 2.0
    cmask = (nidx + 1) * C_BLOCK - 1 <= past
    s = jnp.where(cmask, _dot_hi_nt(q, kc_ref[...]) * scale - slope * jnp.abs(past - centre), NEG)
    m = jnp.max(s, axis=-1, keepdims=True)
    p = jnp.where(cmask, jnp.exp(s - m), 0.0)
    p = p / jnp.maximum(jnp.sum(p, axis=-1, keepdims=True), 1e-30)
    o_cmp = _dot_hi(p, vc_ref[...])
    imp = jnp.sum(jnp.where(live, p, 0.0), axis=0, keepdims=True)
    n_lanes = NSA_S_LANES
    n_blocks = -(-(past + 1) // C_BLOCK)
    cur = past // C_BLOCK
    lane = lax.broadcasted_iota(jnp.int32, (1, n_lanes), 1)
    imp = jnp.concatenate([imp, jnp.zeros((1, n_lanes - nbc), jnp.float32)], axis=1)
    forced = (lane == 0) | (lane == cur) | (lane == cur - 1)
    causal = (lane <= cur) & (lane < n_blocks)
    score = jnp.where(causal, imp + jnp.where(forced, FORCE, 0.0), NEG)
    score_col = jnp.transpose(jnp.broadcast_to(score, (8, n_lanes)))[:, 0:1]
    mi = lax.broadcasted_iota(jnp.int32, (n_lanes, 1), 0)
    ahead = (score_col > score) | ((score_col == score) & (mi < lane))
    rank = jnp.sum(ahead.astype(jnp.float32), axis=0, keepdims=True)
    sel = (rank < min(C_N_SEL, n_blocks)) & causal
    sel_ref[...] = jnp.broadcast_to(sel.astype(jnp.float32), (8, n_lanes))
    n_win = wk_ref.shape[0]
    dist = n_win - lax.broadcasted_iota(jnp.int32, (1, n_win), 1)
    wmask = dist <= C_WIN
    sw = jnp.where(wmask, _dot_hi_nt(q, wk_ref[...]) * scale - slope * dist.astype(jnp.float32), NEG)
    sw_new = jnp.sum(q * wkn_ref[...], axis=-1, keepdims=True) * scale
    mw = jnp.maximum(jnp.max(sw, axis=-1, keepdims=True), sw_new)
    pw = jnp.where(wmask, jnp.exp(sw - mw), 0.0)
    pw_new = jnp.exp(sw_new - mw)
    o_win = (_dot_hi(pw, wv_ref[...]) + pw_new * wvn_ref[...]) / (jnp.sum(pw, axis=-1, keepdims=True) + pw_new)
    part_ref[...] = _nsa_s_gate(gate_ref, gi, 0) * o_cmp + _nsa_s_gate(gate_ref, gi, 2) * o_win


def _nsa_s_cmpwin(qn, kvc, win_buf, layer, win_new, gates, past):
    bd = qn.shape[0]
    nbc = kvc.shape[1]
    n_win = win_buf.shape[2]
    wview = win_buf.reshape(win_buf.shape[0], bd, n_win, 2 * C_KV)
    rows = lambda c: pl.BlockSpec((None, 1, HEAD_DIM), lambda b, h, c=c: (b, 0, c + h))
    qn, win_new, gates = qn.reshape(bd, 1, -1), win_new.reshape(bd, 1, -1), gates.reshape(bd, 1, -1)
    return pl.pallas_call(
        functools.partial(_nsa_s_cmpwin_kernel, past=past),
        out_shape=(jax.ShapeDtypeStruct((bd, C_KV_HEADS, 8, HEAD_DIM), jnp.float32),
                   jax.ShapeDtypeStruct((bd, C_KV_HEADS, 8, NSA_S_LANES), jnp.float32)),
        grid=(bd, C_KV_HEADS),
        in_specs=[pl.BlockSpec((None, 1, C_GROUP * HEAD_DIM), lambda b, h: (b, 0, h)),
                  pl.BlockSpec((None, nbc, HEAD_DIM), lambda b, h: (b, 0, h)),
                  pl.BlockSpec((None, nbc, HEAD_DIM), lambda b, h: (b, 0, C_KV_HEADS + h)),
                  pl.BlockSpec((None, None, n_win, HEAD_DIM), lambda b, h: (layer, b, 0, h)),
                  pl.BlockSpec((None, None, n_win, HEAD_DIM), lambda b, h: (layer, b, 0, C_KV_HEADS + h)),
                  rows(0), rows(C_KV_HEADS), rows(0)],
        out_specs=(pl.BlockSpec((None, None, 8, HEAD_DIM), lambda b, h: (b, h, 0, 0)),
                   pl.BlockSpec((None, None, 8, NSA_S_LANES), lambda b, h: (b, h, 0, 0))),
        compiler_params=_params(("parallel", "parallel")),
        name="nsa_s_cmpwin",
    )(qn, kvc, kvc, wview, wview, win_new, win_new, gates)


def _nsa_s_sel_kernel(hp_ref, blk_ref, q_ref, kv_ref, kn_ref, vn_ref, gate_ref, part_ref, o_ref,
                      m_scr, l_scr, acc_scr, *, past):
    b = pl.program_id(0)
    h = pl.program_id(1)
    j = pl.program_id(2)
    scale = HEAD_DIM ** -0.5
    qs = [q_ref[:, g * HEAD_DIM:(g + 1) * HEAD_DIM] for g in range(C_GROUP)]

    @pl.when(j == 0)
    def _():
        m_scr[...] = jnp.zeros_like(m_scr)
        l_scr[...] = jnp.ones_like(l_scr)
        acc_scr[...] = jnp.zeros_like(acc_scr)
        for g in range(C_GROUP):
            s_new = jnp.sum(qs[g] * kn_ref[...], axis=-1, keepdims=True) * scale
            m_scr[g:g + 1, :] = jnp.broadcast_to(s_new, (1, HEAD_DIM))
            acc_scr[g:g + 1, :] = vn_ref[...]

    blk = blk_ref[(b * C_KV_HEADS + h) * pl.num_programs(2) + j]
    mine = lax.broadcasted_iota(jnp.int32, (1, C_KV_HEADS, HEAD_DIM), 1) == h
    k3 = jnp.where(mine, kv_ref[:, 0], 0.0)
    v3 = jnp.where(mine, kv_ref[:, 1], 0.0)
    dist = (past - blk * C_BLOCK - lax.broadcasted_iota(jnp.int32, (C_BLOCK, 1, HEAD_DIM), 0)).astype(jnp.float32)
    for g in range(C_GROUP):
        slope = jnp.exp2(-0.5 * (jnp.zeros((1, 1, HEAD_DIM), jnp.float32) + (h * C_GROUP + g + 1).astype(jnp.float32)))
        qk = jnp.sum(k3 * qs[g][None], axis=1, keepdims=True)
        s = jnp.sum(qk, axis=-1, keepdims=True) * scale - slope * dist
        m_old = m_scr[g:g + 1, :][None]
        m_new = jnp.maximum(m_old, jnp.max(s, axis=0, keepdims=True))
        p = jnp.exp(s - m_new)
        alpha = jnp.exp(m_old - m_new)
        l_scr[g:g + 1, :] = (alpha * l_scr[g:g + 1, :][None] + jnp.sum(p, axis=0, keepdims=True))[0]
        pv = jnp.sum(jnp.sum(p * v3, axis=0), axis=0, keepdims=True)
        acc_scr[g:g + 1, :] = alpha[0] * acc_scr[g:g + 1, :] + pv
        m_scr[g:g + 1, :] = m_new[0]

    @pl.when(j == pl.num_programs(2) - 1)
    def _():
        gi = lax.broadcasted_iota(jnp.int32, (8, 1), 0)
        o_ref[...] = part_ref[...] + _nsa_s_gate(gate_ref, gi, 1) * (acc_scr[...] / l_scr[...])


def _nsa_s_select(qn, pool, layer, half_pages, blocks, sel_new, gates, part, past):
    bd = qn.shape[0]
    n_sel = blocks.shape[-1]
    n_phys = pool.shape[1]
    halves = PAGE_SIZE // C_BLOCK
    half_shape = (C_BLOCK, 2, C_KV_HEADS, HEAD_DIM)
    pview = pool.reshape((pool.shape[0], n_phys * halves) + half_shape)
    rows = lambda c: pl.BlockSpec((None, 1, HEAD_DIM), lambda b, h, j, hp, bl, c=c: (b, 0, c + h))
    qn, sel_new, gates = qn.reshape(bd, 1, -1), sel_new.reshape(bd, 1, -1), gates.reshape(bd, 1, -1)
    flat = lambda b, h, j: (b * C_KV_HEADS + h) * n_sel + j
    blk4 = pl.BlockSpec((None, None, 8, HEAD_DIM), lambda b, h, j, hp, bl: (b, h, 0, 0))
    return pl.pallas_call(
        functools.partial(_nsa_s_sel_kernel, past=past),
        out_shape=jax.ShapeDtypeStruct((bd, C_KV_HEADS, 8, HEAD_DIM), jnp.float32),
        grid_spec=pltpu.PrefetchScalarGridSpec(
            num_scalar_prefetch=2, grid=(bd, C_KV_HEADS, n_sel),
            in_specs=[pl.BlockSpec((None, 1, C_GROUP * HEAD_DIM), lambda b, h, j, hp, bl: (b, 0, h)),
                      pl.BlockSpec((None, None) + half_shape,
                                   lambda b, h, j, hp, bl: (layer, hp[flat(b, h, j)], 0, 0, 0, 0)),
                      rows(0), rows(C_KV_HEADS), rows(0), blk4],
            out_specs=blk4,
            scratch_shapes=[pltpu.VMEM((8, HEAD_DIM), jnp.float32), pltpu.VMEM((8, HEAD_DIM), jnp.float32),
                            pltpu.VMEM((8, HEAD_DIM), jnp.float32)]),
        compiler_params=_params(("parallel", "parallel", "arbitrary")),
        name="nsa_s_select",
    )(half_pages.reshape(-1), blocks.reshape(-1), qn, pview, sel_new, sel_new, gates, part)


def _nsa_sample(u_c, u_s, q_gain, k_gain, phi_k, phi_v, cmp_pool, sel_pool, win_buf, layer, page_table):
    bd = u_c.shape[0]
    past = page_table.shape[1] * PAGE_SIZE
    qn, cmp_new, sel_new, win_new, gates = _nsa_prep(u_c, u_s, q_gain, k_gain)
    kvc = _nsa_s_summaries(cmp_pool, layer, page_table, phi_k, phi_v)
    part, sel = _nsa_s_cmpwin(qn, kvc, win_buf, layer, win_new, gates, past)
    n_past_blocks = past // C_BLOCK
    n_sel = min(C_N_SEL, n_past_blocks + 1) - 1
    _, blocks = lax.top_k(sel[:, :, 0, :n_past_blocks], n_sel)
    blocks = blocks.astype(jnp.int32)
    halves = PAGE_SIZE // C_BLOCK
    pages = jnp.take_along_axis(page_table[:, None, :], blocks // halves, axis=2)
    half_pages = (pages * halves + blocks % halves).astype(jnp.int32)
    out = _nsa_s_select(qn, sel_pool, layer, half_pages, blocks, sel_new, gates, part, past)
    return out[:, :, :C_GROUP, :].reshape(bd, C_Q), cmp_new, sel_new, win_new


def _even_mixer(u2d, w_out3d, layer, res2d, q_gain, k_gain, b_par, kv_cache, conv_buf, h0, bsz, t):
    u_x = u2d.reshape(bsz, t, -1)[:, :, -B_WIDTH:]
    if kv_cache is None:
        qn, akv = _a_prep(u2d, q_gain, k_gain)
        a_out = _a_attn_prompt(qn, akv, bsz, t, jnp.bfloat16)
        a_state = akv.reshape(bsz, t, 2, A_HEADS, HEAD_DIM)[:, -min(A_WIN_MAX, t):]
        a_dec, b_in, gg = _b_gates(u2d, b_par, bsz, t)
        b_out, h_new = _b_scan(a_dec, b_in, gg, bsz, t, jnp.float32)
        conv_new = u_x[:, -(CONV_W - 1):]
    else:
        a_out, k_new = _a_sample(u2d, kv_cache, layer, q_gain, k_gain)
        a_state = jnp.stack([k_new, u2d[:, A_Q + A_KV:A_Q + 2 * A_KV]], axis=1).reshape(bsz, t, 2, A_HEADS, HEAD_DIM)
        b_out, h_new = _b_sample(u2d, conv_buf, h0, b_par)
        conv_new = jnp.concatenate([conv_buf[:, 1:], u_x], axis=1)
    out = _matmul([a_out, b_out], w_out3d, layer, 0, D_MODEL, kv_cache is not None, res=res2d)
    return out, (a_state, conv_new, h_new)


def _odd_mixer(u_parts, w_out3d, layer, res2d, q_gain, k_gain, phi_k, phi_v, d_par,
               cmp_pool, sel_pool, page_table, win_buf, d_conv_buf, d_s0, bsz, t):
    u_c2d, u_d2d, u_s2d = u_parts
    kv_shape = (bsz, t, 2, C_KV_HEADS, HEAD_DIM)
    d_in = u_d2d.reshape(bsz, t, -1)[:, :, :D_CONV]
    conv_w, a_log, dt_bias, out_gain = d_par
    if page_table is None:
        qn, cmp2d, sel2d, win2d, gates = _nsa_prep(u_c2d, u_s2d, q_gain, k_gain)
        kc, vc = _nsa_summaries(cmp2d, phi_k, phi_v)
        o_c = _nsa_prompt(qn, kc, vc, sel2d, win2d, gates, bsz, t, jnp.bfloat16)
        win_state = win2d.reshape(kv_shape)[:, -min(C_WIN, t):]
        beta_r, gam_r = _d_gates(u_s2d, a_log, dt_bias)
        d_out, d_s = _d_delta(u_d2d, beta_r, gam_r, conv_w, out_gain, bsz, t, jnp.bfloat16)
        d_conv_new = d_in[:, -(CONV_W - 1):]
    else:
        o_c, cmp2d, sel2d, win2d = _nsa_sample(u_c2d, u_s2d, q_gain, k_gain, phi_k, phi_v,
                                                cmp_pool, sel_pool, win_buf, layer, page_table)
        win_state = win2d.reshape(kv_shape)
        d_out, d_s = _d_sample(u_d2d, u_s2d, d_conv_buf, d_s0, layer, d_par)
        d_conv_new = jnp.concatenate([d_conv_buf[:, 1:], d_in], axis=1)
    out = _matmul([o_c, d_out], w_out3d, layer, 0, D_MODEL, page_table is not None, res=res2d)
    return out, (cmp2d.reshape(kv_shape), sel2d.reshape(kv_shape), win_state, d_conv_new, d_s)


ODD_C = C_Q + 6 * C_KV
ODD_G0 = ODD_C
ODD_D0 = ODD_G0 + 3 * C_HEADS
ODD_D = 2 * D_QK + 2 * D_V
ODD_S0 = ODD_D0 + ODD_D
ODD_IN = ODD_S0 + 2 * D_HEADS


def kernel(x_prompt, x_sample, cache_a_kv, state_b_conv, state_b_h, cache_c_cmp_kv, cache_c_sel_kv, cache_c_win_kv, state_d_conv, state_d_S, page_table, norm_mix, norm_ffn, even_w_in, even_w_out, a_q_norm, a_k_norm, b_conv_w, b_conv_b, b_gate_a_w, b_gate_a_b, b_gate_x_w, b_gate_x_b, b_lambda, odd_w_in, odd_w_out, c_q_norm, c_k_norm, c_phi_k, c_phi_v, d_conv_w, d_a_log, d_dt_bias, d_out_norm, moe_group_w, moe_group_b, moe_expert_w, moe_expert_b, moe_w1, moe_w3, moe_w2):
    bp, sp, d = x_prompt.shape
    bs, ss, _ = x_sample.shape
    depth = norm_mix.shape[0]
    hp = x_prompt.reshape(bp * sp, d)
    hs = x_sample.reshape(bs * ss, d)
    outs = {k: [] for k in ("ak", "bc", "bh", "cc", "cs", "cw", "dc", "ds")}
    outs_s = {k: [] for k in outs}
    for l in range(depth):
        i = l // 2
        xp = _rmsnorm(hp, norm_mix[l], jnp.bfloat16)
        xs = _rmsnorm(hs, norm_mix[l], jnp.float32)
        if l % 2 == 0:
            b_par = (b_conv_w[i], b_conv_b[i], b_gate_a_w[i], b_gate_a_b[i], b_gate_x_w[i], b_gate_x_b[i], b_lambda[i])
            n_in = even_w_in.shape[-1]
            up = _matmul(xp, even_w_in, i, 0, n_in, False)
            us = _matmul(xs, even_w_in, i, 0, n_in, True)
            hp, st_p = _even_mixer(up, even_w_out, i, hp, a_q_norm[i], a_k_norm[i], b_par, None, None, None, bp, sp)
            hs, st_s = _even_mixer(us, even_w_out, i, hs, a_q_norm[i], a_k_norm[i], b_par,
                                   cache_a_kv, state_b_conv[i], state_b_h[i], bs, ss)
            for dst, st in ((outs, st_p), (outs_s, st_s)):
                dst["ak"].append(st[0]); dst["bc"].append(st[1]); dst["bh"].append(st[2])
        else:
            d_par = (d_conv_w[i], d_a_log[i], d_dt_bias[i], d_out_norm[i])
            w_d = odd_w_in[:, :, ODD_D0:ODD_S0]
            w_s = jnp.concatenate([odd_w_in[:, :, ODD_G0:ODD_D0], odd_w_in[:, :, ODD_S0:],
                                   jnp.zeros((odd_w_in.shape[0], d, LANES - 3 * C_HEADS - 2 * D_HEADS), jnp.float32)], axis=-1)
            ups = (_matmul(xp, odd_w_in, i, 0, ODD_C, False), _matmul(xp, w_d, i, 0, ODD_D, False), _matmul(xp, w_s, i, 0, LANES, False))
            uss = (_matmul(xs, odd_w_in, i, 0, ODD_C, True), _matmul(xs, w_d, i, 0, ODD_D, True), _matmul(xs, w_s, i, 0, LANES, True))
            hp, st_p = _odd_mixer(ups, odd_w_out, i, hp, c_q_norm[i], c_k_norm[i], c_phi_k[i], c_phi_v[i], d_par,
                                  None, None, None, None, None, None, bp, sp)
            hs, st_s = _odd_mixer(uss, odd_w_out, i, hs, c_q_norm[i], c_k_norm[i], c_phi_k[i], c_phi_v[i], d_par,
                                  cache_c_cmp_kv, cache_c_sel_kv, page_table, cache_c_win_kv,
                                  state_d_conv[i], state_d_S, bs, ss)
            for dst, st in ((outs, st_p), (outs_s, st_s)):
                dst["cc"].append(st[0]); dst["cs"].append(st[1]); dst["cw"].append(st[2])
                dst["dc"].append(st[3]); dst["ds"].append(st[4])
        w_router = jnp.concatenate([moe_group_w[l], moe_expert_w[l],
                                    jnp.zeros((d, ROUTER_PAD - N_GROUPS - N_EXPERTS), jnp.float32)], axis=-1)
        b_router = jnp.concatenate([moe_group_b[l], moe_expert_b[l],
                                    jnp.zeros((ROUTER_PAD - N_GROUPS - N_EXPERTS,), jnp.float32)])[None, :]
        hp = _hier_moe(hp, norm_ffn[l], w_router, b_router, moe_w1, moe_w3, moe_w2, l, 256, jnp.bfloat16)
        hs = _hier_moe(hs, norm_ffn[l], w_router, b_router, moe_w1, moe_w3, moe_w2, l, 8, jnp.float32)
    res = [hp.reshape(bp, sp, d), hs.reshape(bs, ss, d)]
    for key in ("ak", "bc", "bh", "cc", "cs", "cw", "dc", "ds"):
        res.append(jnp.stack(outs[key]))
        res.append(jnp.stack(outs_s[key]))
    return tuple(res)
```

```python
import functools
import math

import jax
import jax.numpy as jnp
import numpy as np
from jax import lax
from jax.experimental import pallas as pl
from jax.experimental.pallas import tpu as pltpu

D_MODEL = 4096
HEAD_DIM = 128
CONV_W = 4
BAND_BLOCK = 128
A_PATTERNS = ((128, 1), (512, 4), (2048, 16))
A_N_GROUPS = len(A_PATTERNS)
A_HEADS = D_MODEL // 512
A_WIN_MAX = max(w for w, _ in A_PATTERNS)
A_Q = A_N_GROUPS * A_HEADS * HEAD_DIM
A_KV = A_HEADS * HEAD_DIM
A_OUT = A_HEADS * HEAD_DIM
B_WIDTH = 3 * D_MODEL // 4
B_BLOCKS = B_WIDTH // HEAD_DIM
B_BLOCK_DIM = B_WIDTH // B_BLOCKS
RG_C = 8.0
C_HEADS = D_MODEL // 256
C_KV_HEADS = C_HEADS // 4
C_GROUP = C_HEADS // C_KV_HEADS
C_BLOCK = 64
C_N_SEL = 16
C_WIN = 512
C_Q = C_HEADS * HEAD_DIM
C_KV = C_KV_HEADS * HEAD_DIM
D_HEADS = D_MODEL // 256
D_DK = HEAD_DIM
D_DV = HEAD_DIM
D_QK = D_HEADS * D_DK
D_V = D_HEADS * D_DV
D_CONV = 2 * D_QK + D_V
N_GROUPS = 8
EXPERTS_PER_GROUP = 8
N_EXPERTS = N_GROUPS * EXPERTS_PER_GROUP
TOP_K = 2
D_EXPERT = D_MODEL // 8
PAGE_SIZE = 128
EPS = 1e-6
NEG = -1e30
FORCE = 1e4

LANES = 128
VMEM_LIMIT = 56 * 1024 * 1024
ROUTER_PAD = LANES

HI = lax.Precision.HIGHEST


def _params(sem):
    return pltpu.CompilerParams(dimension_semantics=sem, vmem_limit_bytes=VMEM_LIMIT)


def _rmsnorm_kernel(x_ref, g_ref, o_ref):
    x = x_ref[...]
    y = x * lax.rsqrt(jnp.mean(x * x, axis=-1, keepdims=True) + EPS)
    o_ref[...] = (y * g_ref[...]).astype(o_ref.dtype)


def _rmsnorm(x2d, gain, out_dtype):
    m, d = x2d.shape
    tm = min(m, 512)
    return pl.pallas_call(
        _rmsnorm_kernel,
        out_shape=jax.ShapeDtypeStruct((m, d), out_dtype),
        grid=(m // tm,),
        in_specs=[pl.BlockSpec((tm, d), lambda i: (i, 0)), pl.BlockSpec((1, d), lambda i: (0, 0))],
        out_specs=pl.BlockSpec((tm, d), lambda i: (i, 0)),
        compiler_params=_params(("parallel",)),
        name="rmsnorm",
    )(x2d, gain.reshape(1, d))


def _matmul_kernel(*refs, exact, has_res, k_bounds):
    n_x = len(k_bounds) - 1
    x_refs, w_ref = refs[:n_x], refs[n_x]
    if has_res:
        r_ref, o_ref, acc_ref = refs[n_x + 1:]
    else:
        o_ref, acc_ref = refs[n_x + 1:]
    k = pl.program_id(2)

    @pl.when(k == 0)
    def _():
        acc_ref[...] = jnp.zeros_like(acc_ref)

    for p, x_ref in enumerate(x_refs):
        @pl.when((k >= k_bounds[p]) & (k < k_bounds[p + 1]))
        def _(x_ref=x_ref):
            if exact:
                acc_ref[...] += jnp.dot(x_ref[...], w_ref[...], preferred_element_type=jnp.float32, precision=HI)
            else:
                acc_ref[...] += jnp.dot(x_ref[...].astype(jnp.bfloat16), w_ref[...].astype(jnp.bfloat16),
                                        preferred_element_type=jnp.float32)

    @pl.when(k == pl.num_programs(2) - 1)
    def _():
        out = acc_ref[...]
        if has_res:
            out = out + r_ref[...]
        o_ref[...] = out


def _matmul(xs, w3d, layer, col0, n, exact, res=None, tn=1024):
    if not isinstance(xs, (list, tuple)):
        xs = [xs]
    m = xs[0].shape[0]
    tm = min(m, 1024)
    tn = min(tn, n)
    tk = 512 if exact else 1024
    assert tm * tn * 4 * 5 + tk * (tm + tn) * 4 * 2 * len(xs) <= VMEM_LIMIT
    assert m % tm == 0 and n % tn == 0 and col0 % tn == 0 and all(x.shape[1] % tk == 0 for x in xs)
    jb = col0 // tn
    k_bounds = [0]
    for x in xs:
        k_bounds.append(k_bounds[-1] + x.shape[1] // tk)
    in_specs = []
    for p in range(len(xs)):
        lo, hi = k_bounds[p], k_bounds[p + 1]
        in_specs.append(pl.BlockSpec((tm, tk), lambda i, j, k, lo=lo, hi=hi: (i, jnp.clip(k, lo, hi - 1) - lo)))
    in_specs.append(pl.BlockSpec((None, tk, tn), lambda i, j, k: (layer, k, j + jb)))
    args = list(xs) + [w3d]
    if res is not None:
        in_specs.append(pl.BlockSpec((tm, tn), lambda i, j, k: (i, j)))
        args.append(res)
    return pl.pallas_call(
        functools.partial(_matmul_kernel, exact=exact, has_res=res is not None, k_bounds=tuple(k_bounds)),
        out_shape=jax.ShapeDtypeStruct((m, n), jnp.float32),
        grid=(m // tm, n // tn, k_bounds[-1]),
        in_specs=in_specs,
        out_specs=pl.BlockSpec((tm, tn), lambda i, j, k: (i, j)),
        scratch_shapes=[pltpu.VMEM((tm, tn), jnp.float32)],
        compiler_params=_params(("parallel", "parallel", "arbitrary")),
        name="proj",
    )(*args)


def _router_kernel(h_ref, g_ref, w_ref, b_ref, xn_ref, logit_ref):
    x = h_ref[...]
    y = x * lax.rsqrt(jnp.mean(x * x, axis=-1, keepdims=True) + EPS) * g_ref[...]
    xn_ref[...] = y.astype(xn_ref.dtype)
    logit_ref[...] = jnp.dot(y, w_ref[...], preferred_element_type=jnp.float32, precision=HI) + b_ref[...]


def _router(h2d, gain, w_router, b_router, xn_dtype):
    m, d = h2d.shape
    tm = min(m, 256)
    return pl.pallas_call(
        _router_kernel,
        out_shape=(jax.ShapeDtypeStruct((m, d), xn_dtype), jax.ShapeDtypeStruct((m, ROUTER_PAD), jnp.float32)),
        grid=(m // tm,),
        in_specs=[pl.BlockSpec((tm, d), lambda i: (i, 0)), pl.BlockSpec((1, d), lambda i: (0, 0)),
                  pl.BlockSpec((d, ROUTER_PAD), lambda i: (0, 0)), pl.BlockSpec((1, ROUTER_PAD), lambda i: (0, 0))],
        out_specs=(pl.BlockSpec((tm, d), lambda i: (i, 0)), pl.BlockSpec((tm, ROUTER_PAD), lambda i: (i, 0))),
        compiler_params=_params(("parallel",)),
        name="ffn_norm_router",
    )(h2d, gain.reshape(1, d), w_router, b_router)


MOE_K_CHUNK = 512


def _moe_up_kernel(be_ref, nu_ref, x_ref, w1_ref, w3_ref, o_ref, *, exact):
    tm = x_ref.shape[0]
    used = pl.program_id(0) < nu_ref[0]

    @pl.when(used)
    def _():
        a = jnp.zeros((tm, D_EXPERT), jnp.float32)
        b = jnp.zeros((tm, D_EXPERT), jnp.float32)
        for c in range(D_MODEL // MOE_K_CHUNK):
            sl = slice(c * MOE_K_CHUNK, (c + 1) * MOE_K_CHUNK)
            if exact:
                a += jnp.dot(x_ref[:, sl], w1_ref[sl, :], preferred_element_type=jnp.float32, precision=HI)
                b += jnp.dot(x_ref[:, sl], w3_ref[sl, :], preferred_element_type=jnp.float32, precision=HI)
            else:
                a += jnp.dot(x_ref[:, sl], w1_ref[sl, :].astype(jnp.bfloat16), preferred_element_type=jnp.float32)
                b += jnp.dot(x_ref[:, sl], w3_ref[sl, :].astype(jnp.bfloat16), preferred_element_type=jnp.float32)
        o_ref[...] = (a * jax.nn.sigmoid(a) * b).astype(o_ref.dtype)

    @pl.when(jnp.logical_not(used))
    def _():
        o_ref[...] = jnp.zeros_like(o_ref)


def _moe_down_kernel(be_ref, nu_ref, h_ref, w2_ref, g_ref, o_ref, *, exact):
    used = pl.program_id(0) < nu_ref[0]

    @pl.when(used)
    def _():
        if exact:
            out = jnp.dot(h_ref[...], w2_ref[...], preferred_element_type=jnp.float32, precision=HI)
        else:
            out = jnp.dot(h_ref[...], w2_ref[...].astype(jnp.bfloat16), preferred_element_type=jnp.float32)
        o_ref[...] = out * g_ref[...]

    @pl.when(jnp.logical_not(used))
    def _():
        o_ref[...] = jnp.zeros_like(o_ref)


def _moe_experts(x_rows, row_gate, blk_exp, n_used, w1, w3, w2, layer, tm):
    rows, d = x_rows.shape
    n_blk = rows // tm
    exact = x_rows.dtype == jnp.float32
    up = pl.pallas_call(
        functools.partial(_moe_up_kernel, exact=exact),
        out_shape=jax.ShapeDtypeStruct((rows, D_EXPERT), x_rows.dtype),
        grid_spec=pltpu.PrefetchScalarGridSpec(
            num_scalar_prefetch=2, grid=(n_blk,),
            in_specs=[pl.BlockSpec((tm, d), lambda i, be, nu: (jnp.minimum(i, nu[0] - 1), 0)),
                      pl.BlockSpec((None, None, d, D_EXPERT), lambda i, be, nu: (layer, be[i], 0, 0)),
                      pl.BlockSpec((None, None, d, D_EXPERT), lambda i, be, nu: (layer, be[i], 0, 0))],
            out_specs=pl.BlockSpec((tm, D_EXPERT), lambda i, be, nu: (i, 0))),
        compiler_params=_params(("arbitrary",)),
        name="moe_up",
    )(blk_exp, n_used, x_rows, w1, w3)
    return pl.pallas_call(
        functools.partial(_moe_down_kernel, exact=exact),
        out_shape=jax.ShapeDtypeStruct((rows, d), jnp.float32),
        grid_spec=pltpu.PrefetchScalarGridSpec(
            num_scalar_prefetch=2, grid=(n_blk,),
            in_specs=[pl.BlockSpec((tm, D_EXPERT), lambda i, be, nu: (jnp.minimum(i, nu[0] - 1), 0)),
                      pl.BlockSpec((None, None, D_EXPERT, d), lambda i, be, nu: (layer, be[i], 0, 0)),
                      pl.BlockSpec((tm, 1), lambda i, be, nu: (jnp.minimum(i, nu[0] - 1), 0))],
            out_specs=pl.BlockSpec((tm, d), lambda i, be, nu: (i, 0))),
        compiler_params=_params(("arbitrary",)),
        name="moe_down",
    )(blk_exp, n_used, up, w2, row_gate)


def _hier_moe(h2d, gain, w_router, b_router, w1, w3, w2, layer, tm, xn_dtype):
    n_tok, d = h2d.shape
    xn, logits = _router(h2d, gain, w_router, b_router, xn_dtype)
    g_logit = logits[:, :N_GROUPS]
    g_prob = jax.nn.softmax(g_logit, axis=-1)
    grp = jnp.argmax(g_logit, axis=-1)
    p_grp = jnp.take_along_axis(g_prob, grp[:, None], axis=1)[:, 0]
    e_logit = logits[:, N_GROUPS:N_GROUPS + N_EXPERTS].reshape(-1, N_GROUPS, EXPERTS_PER_GROUP)
    e_logit = jnp.take_along_axis(e_logit, grp[:, None, None], axis=1)[:, 0]
    e_val, e_idx = lax.top_k(e_logit, TOP_K)
    gates = p_grp[:, None] * jax.nn.softmax(e_val, axis=-1)
    experts = grp[:, None] * EXPERTS_PER_GROUP + e_idx
    n_asg = n_tok * TOP_K
    flat_e = experts.reshape(-1).astype(jnp.int32)
    order = jnp.argsort(flat_e)
    sorted_e = flat_e[order]
    counts = jnp.bincount(flat_e, length=N_EXPERTS)
    padded = (counts + tm - 1) // tm * tm
    pad_end = jnp.cumsum(padded)
    pad_start = pad_end - padded
    start = jnp.cumsum(counts) - counts
    dest = (pad_start[sorted_e] + jnp.arange(n_asg) - start[sorted_e]).astype(jnp.int32)
    n_rows = -(-(n_asg + N_EXPERTS * (tm - 1)) // tm) * tm
    n_blk = n_rows // tm
    tok = (order // TOP_K).astype(jnp.int32)
    row_tok = jnp.zeros((n_rows,), jnp.int32).at[dest].set(tok)
    row_gate = jnp.zeros((n_rows,), jnp.float32).at[dest].set(gates.reshape(-1)[order])
    blk_exp = jnp.minimum(jnp.searchsorted(pad_end, jnp.arange(n_blk) * tm, side='right'),
                          N_EXPERTS - 1).astype(jnp.int32)
    x_rows = xn[row_tok]
    n_used = (pad_end[-1:] // tm).astype(jnp.int32)
    out = _moe_experts(x_rows, row_gate[:, None], blk_exp, n_used, w1, w3, w2, layer, tm)
    asg_row = jnp.zeros((n_asg,), jnp.int32).at[order].set(dest).reshape(n_tok, TOP_K)
    y = h2d
    for j in range(TOP_K):
        y = y + out[asg_row[:, j]]
    return y


NSA_PREP_ROWS = 256
NSA_SUM_ROWS = 512
NSA_TQ = 128
NSA_GATE_LANES = LANES
TQ_SHIFT = NSA_TQ.bit_length() - 1
C_BLOCK_SHIFT = C_BLOCK.bit_length() - 1


def _chunk_rms(x, gain):
    return x * lax.rsqrt(jnp.mean(x * x, axis=-1, keepdims=True) + EPS) * gain


def _nsa_prep_kernel(uc_ref, us_ref, qg_ref, kg_ref, perm_ref, q_ref, cmp_ref, sel_ref, win_ref, gate_ref):
    for c in range(C_Q // HEAD_DIM):
        sl = slice(c * HEAD_DIM, (c + 1) * HEAD_DIM)
        q_ref[:, sl] = _chunk_rms(uc_ref[:, sl], qg_ref[...])
    for br, o_ref in enumerate((cmp_ref, sel_ref, win_ref)):
        base = C_Q + br * 2 * C_KV
        for c in range(C_KV_HEADS):
            sl = slice(c * HEAD_DIM, (c + 1) * HEAD_DIM)
            o_ref[:, sl] = _chunk_rms(uc_ref[:, base + c * HEAD_DIM: base + (c + 1) * HEAD_DIM], kg_ref[br:br + 1, :])
        o_ref[:, C_KV:] = uc_ref[:, base + C_KV: base + 2 * C_KV]
    gate_ref[...] = jnp.dot(jax.nn.sigmoid(us_ref[...]), perm_ref[...], preferred_element_type=jnp.float32, precision=HI)


def _nsa_prep(u_c, u_s, q_gain, k_gain):
    p = u_c.shape[0]
    tm = min(p, NSA_PREP_ROWS)
    perm = np.zeros((LANES, C_KV_HEADS * NSA_GATE_LANES), np.float32)
    for br in range(3):
        for h in range(C_KV_HEADS):
            for g in range(C_GROUP):
                perm[br * C_HEADS + h * C_GROUP + g, h * NSA_GATE_LANES + br * C_GROUP + g] = 1.0
    row = lambda i: (i, 0)
    fixed = lambda i: (0, 0)
    return pl.pallas_call(
        _nsa_prep_kernel,
        out_shape=(jax.ShapeDtypeStruct((p, C_Q), jnp.float32),) + (jax.ShapeDtypeStruct((p, 2 * C_KV), jnp.float32),) * 3
        + (jax.ShapeDtypeStruct((p, C_KV_HEADS * NSA_GATE_LANES), jnp.float32),),
        grid=(p // tm,),
        in_specs=[pl.BlockSpec((tm, ODD_C), row), pl.BlockSpec((tm, LANES), row), pl.BlockSpec((1, HEAD_DIM), fixed),
                  pl.BlockSpec((3, HEAD_DIM), fixed), pl.BlockSpec(perm.shape, fixed)],
        out_specs=(pl.BlockSpec((tm, C_Q), row),) + (pl.BlockSpec((tm, 2 * C_KV), row),) * 3
        + (pl.BlockSpec((tm, C_KV_HEADS * NSA_GATE_LANES), row),),
        compiler_params=_params(("parallel",)),
        name="nsa_prep",
    )(u_c, u_s, q_gain.reshape(1, HEAD_DIM), k_gain, jnp.asarray(perm))


def _nsa_sum_kernel(cmp_ref, phik_ref, phiv_ref, kc_ref, vc_ref):
    kc_ref[...] = jnp.dot(phik_ref[...], cmp_ref[:, :C_KV], preferred_element_type=jnp.float32, precision=HI)
    vc_ref[...] = jnp.dot(phiv_ref[...], cmp_ref[:, C_KV:], preferred_element_type=jnp.float32, precision=HI)


def _nsa_summaries(cmp_rows, phi_k, phi_v):
    p = cmp_rows.shape[0]
    tm = NSA_SUM_ROWS
    nb = tm // C_BLOCK
    eye = jnp.eye(nb, dtype=jnp.float32)
    big_k = jnp.kron(eye, phi_k[None, :])
    big_v = jnp.kron(eye, phi_v[None, :])
    return pl.pallas_call(
        _nsa_sum_kernel,
        out_shape=(jax.ShapeDtypeStruct((p // C_BLOCK, C_KV), jnp.float32),) * 2,
        grid=(p // tm,),
        in_specs=[pl.BlockSpec((tm, 2 * C_KV), lambda i: (i, 0)), pl.BlockSpec((nb, tm), lambda i: (0, 0)),
                  pl.BlockSpec((nb, tm), lambda i: (0, 0))],
        out_specs=(pl.BlockSpec((nb, C_KV), lambda i: (i, 0)),) * 2,
        compiler_params=_params(("parallel",)),
        name="nsa_summaries",
    )(cmp_rows, big_k, big_v)


def _nsa_prompt_kernel(q_ref, kc_ref, vc_ref, sk_ref, sv_ref, wk_ref, wv_ref, gate_ref, o_ref,
                       m_scr, l_scr, acc_scr, *, n_blocks):
    h = pl.program_id(1)
    qb = pl.program_id(2)
    tq = NSA_TQ
    rows = C_GROUP * tq
    scale = HEAD_DIM ** -0.5
    q = jnp.concatenate([q_ref[:, g * HEAD_DIM:(g + 1) * HEAD_DIM] for g in range(C_GROUP)], axis=0)
    row = lax.broadcasted_iota(jnp.int32, (rows, 1), 0)
    t_row = qb * tq + (row & (tq - 1))
    slope = jnp.exp2(-0.5 * (h * C_GROUP + (row >> TQ_SHIFT) + 1).astype(jnp.float32))

    nidx = lax.broadcasted_iota(jnp.int32, (1, n_blocks), 1)
    s = lax.dot_general(q, kc_ref[...], (((1,), (1,)), ((), ())), preferred_element_type=jnp.float32, precision=HI) * scale
    centre = nidx.astype(jnp.float32) * C_BLOCK + (C_BLOCK - 1) / 2.0
    s = s - slope * jnp.abs(t_row.astype(jnp.float32) - centre)
    cmask = (nidx + 1) * C_BLOCK - 1 <= t_row
    s = jnp.where(cmask, s, NEG)
    m = jnp.max(s, axis=-1, keepdims=True)
    p = jnp.where(cmask, jnp.exp(s - m), 0.0)
    l = jnp.sum(p, axis=-1, keepdims=True)
    p = p / jnp.maximum(l, 1e-30)
    o_cmp = jnp.dot(p, vc_ref[...], preferred_element_type=jnp.float32, precision=HI)
    imp = p[0:tq]
    for g in range(1, C_GROUP):
        imp = imp + p[g * tq:(g + 1) * tq]

    t_q = t_row[0:tq]
    cur = t_q >> C_BLOCK_SHIFT
    forced = (nidx == 0) | (nidx == cur) | (nidx == cur - 1)
    causal = nidx <= cur
    score = jnp.where(causal, imp + jnp.where(forced, FORCE, 0.0), NEG)
    rank = jnp.zeros((tq, n_blocks), jnp.int32)
    for mcol in range(n_blocks):
        cm = score[:, mcol:mcol + 1]
        ahead = (cm > score) | ((cm == score) & (mcol < nidx))
        rank = rank + ahead.astype(jnp.int32)
    selm = ((rank < min(C_N_SEL, n_blocks)) & causal).astype(jnp.bfloat16)

    q16 = q.astype(jnp.bfloat16)
    slope_b = jnp.broadcast_to(slope, (rows, tq))
    t_rel = jnp.broadcast_to(t_row, (rows, tq)) - lax.broadcasted_iota(jnp.int32, (rows, tq), 1)

    def attend(k_ref, v_ref, c, mask):
        start = pl.multiple_of(c * tq, tq)
        k = k_ref[pl.ds(start, tq), :].astype(jnp.bfloat16)
        v = v_ref[pl.ds(start, tq), :].astype(jnp.bfloat16)
        dist = t_rel - c * tq
        sc = lax.dot_general(q16, k, (((1,), (1,)), ((), ())), preferred_element_type=jnp.float32) * scale
        sc = sc - slope_b * dist.astype(jnp.float32)
        ok = mask(dist)
        sc = jnp.where(ok, sc, NEG)
        m_old = m_scr[...]
        m_new = jnp.maximum(m_old, jnp.max(sc, axis=-1, keepdims=True))
        pc = jnp.where(ok, jnp.exp(sc - m_new), 0.0)
        alpha = jnp.exp(m_old - m_new)
        l_scr[...] = alpha * l_scr[...] + jnp.sum(pc, axis=-1, keepdims=True)
        acc_scr[...] = alpha * acc_scr[...] + jnp.dot(pc.astype(jnp.bfloat16), v, preferred_element_type=jnp.float32)
        m_scr[...] = m_new

    def reset():
        m_scr[...] = jnp.full_like(m_scr, NEG)
        l_scr[...] = jnp.zeros_like(l_scr)
        acc_scr[...] = jnp.zeros_like(acc_scr)

    reset()
    brow = lax.broadcasted_iota(jnp.int32, (n_blocks, tq), 0)
    bcol = lax.broadcasted_iota(jnp.int32, (n_blocks, tq), 1)

    def sel_body(c, carry):
        expand = (brow == c * (tq // C_BLOCK) + (bcol >> C_BLOCK_SHIFT)).astype(jnp.bfloat16)
        sel_keys = jnp.dot(selm, expand, preferred_element_type=jnp.float32)
        sel_keys = jnp.concatenate([sel_keys] * C_GROUP, axis=0) > 0.5
        attend(sk_ref, sv_ref, c, lambda dist: sel_keys & (dist >= 0))
        return carry

    lax.fori_loop(0, qb + 1, sel_body, 0)
    o_sel = acc_scr[...] / l_scr[...]

    reset()

    def win_body(c, carry):
        attend(wk_ref, wv_ref, c, lambda dist: (dist >= 0) & (dist <= C_WIN))
        return carry

    lax.fori_loop(jnp.maximum(qb - C_WIN // tq, 0), qb + 1, win_body, 0)
    o_win = acc_scr[...] / l_scr[...]

    gt = gate_ref[...]
    outs = []
    for g in range(C_GROUP):
        rs = slice(g * tq, (g + 1) * tq)
        outs.append(gt[:, g:g + 1] * o_cmp[rs] + gt[:, C_GROUP + g:C_GROUP + g + 1] * o_sel[rs]
                    + gt[:, 2 * C_GROUP + g:2 * C_GROUP + g + 1] * o_win[rs])
    o_ref[...] = jnp.concatenate(outs, axis=1).astype(o_ref.dtype)


def _nsa_prompt(qn, kc, vc, sel_rows, win_rows, gates, bsz, t, out_dtype):
    tq = NSA_TQ
    assert tq == HEAD_DIM
    nq = t // tq
    n_blocks = t // C_BLOCK
    kv_k = pl.BlockSpec((t, HEAD_DIM), lambda b, h, i: (b, h))
    kv_v = pl.BlockSpec((t, HEAD_DIM), lambda b, h, i: (b, C_KV_HEADS + h))
    return pl.pallas_call(
        functools.partial(_nsa_prompt_kernel, n_blocks=n_blocks),
        out_shape=jax.ShapeDtypeStruct((bsz * t, C_Q), out_dtype),
        grid=(bsz, C_KV_HEADS, nq),
        in_specs=[pl.BlockSpec((tq, C_GROUP * HEAD_DIM), lambda b, h, i: (b * nq + i, h)),
                  pl.BlockSpec((n_blocks, HEAD_DIM), lambda b, h, i: (b, h)),
                  pl.BlockSpec((n_blocks, HEAD_DIM), lambda b, h, i: (b, h)),
                  kv_k, kv_v, kv_k, kv_v,
                  pl.BlockSpec((tq, NSA_GATE_LANES), lambda b, h, i: (b * nq + i, h))],
        out_specs=pl.BlockSpec((tq, C_GROUP * HEAD_DIM), lambda b, h, i: (b * nq + i, h)),
        scratch_shapes=[pltpu.VMEM((C_GROUP * tq, tq), jnp.float32), pltpu.VMEM((C_GROUP * tq, tq), jnp.float32),
                        pltpu.VMEM((C_GROUP * tq, HEAD_DIM), jnp.float32)],
        compiler_params=_params(("parallel", "parallel", "arbitrary")),
        name="nsa_prompt",
    )(qn, kc, vc, sel_rows, sel_rows, win_rows, win_rows, gates)


A_PREP_ROWS = 256
A_TQ = BAND_BLOCK
A_SLOPES = [[2.0 ** (-8.0 * (gi * A_HEADS + h + 1.0) / (A_N_GROUPS * A_HEADS)) for h in range(A_HEADS)]
            for gi in range(A_N_GROUPS)]


def _a_prep_kernel(q_in, k_in, v_in, qg_ref, kg_ref, q_ref, kv_ref):
    for c in range(A_Q // HEAD_DIM):
        sl = slice(c * HEAD_DIM, (c + 1) * HEAD_DIM)
        q_ref[:, sl] = _chunk_rms(q_in[:, sl], qg_ref[...])
    for c in range(A_HEADS):
        sl = slice(c * HEAD_DIM, (c + 1) * HEAD_DIM)
        kv_ref[:, sl] = _chunk_rms(k_in[:, sl], kg_ref[...])
    kv_ref[:, A_KV:] = v_in[...]


def _a_prep(u, q_gain, k_gain):
    p = u.shape[0]
    tm = A_PREP_ROWS
    fixed = lambda i: (0, 0)
    return pl.pallas_call(
        _a_prep_kernel,
        out_shape=(jax.ShapeDtypeStruct((p, A_Q), jnp.float32), jax.ShapeDtypeStruct((p, 2 * A_KV), jnp.float32)),
        grid=(p // tm,),
        in_specs=[pl.BlockSpec((tm, A_Q), lambda i: (i, 0)),
                  pl.BlockSpec((tm, A_KV), lambda i: (i, A_Q // A_KV)),
                  pl.BlockSpec((tm, A_KV), lambda i: (i, A_Q // A_KV + 1)),
                  pl.BlockSpec((1, HEAD_DIM), fixed), pl.BlockSpec((1, HEAD_DIM), fixed)],
        out_specs=(pl.BlockSpec((tm, A_Q), lambda i: (i, 0)), pl.BlockSpec((tm, 2 * A_KV), lambda i: (i, 0))),
        compiler_params=_params(("parallel",)),
        name="a_prep",
    )(u, u, u, q_gain.reshape(1, HEAD_DIM), k_gain.reshape(1, HEAD_DIM))


def _a_band_kernel(*refs, gi, dil, first, last):
    if first:
        q_ref, kp_ref, kc_ref, vp_ref, vc_ref = refs[:5]
        outs = refs[5:]
    else:
        q_ref, kp_ref, kc_ref, vp_ref, vc_ref, m_in, l_in, acc_in = refs[:8]
        outs = refs[8:]
    qb = pl.program_id(2)
    tq = A_TQ
    row = lax.broadcasted_iota(jnp.int32, (tq, 2 * tq), 0)
    col = lax.broadcasted_iota(jnp.int32, (tq, 2 * tq), 1)
    dist = tq + row - col
    ok = (dist >= 0) & (dist <= tq) & ((col >= tq) | (qb > 0))
    dist_f = (dist * dil).astype(jnp.float32)
    scale = HEAD_DIM ** -0.5
    for h in range(A_HEADS):
        sl = slice(h * HEAD_DIM, (h + 1) * HEAD_DIM)
        q = q_ref[:, sl].astype(jnp.bfloat16)
        k = jnp.concatenate([kp_ref[:, sl], kc_ref[:, sl]], axis=0).astype(jnp.bfloat16)
        v = jnp.concatenate([vp_ref[:, sl], vc_ref[:, sl]], axis=0).astype(jnp.bfloat16)
        s = lax.dot_general(q, k, (((1,), (1,)), ((), ())), preferred_element_type=jnp.float32) * scale
        s = jnp.where(ok, s - A_SLOPES[gi][h] * dist_f, NEG)
        m_row = jnp.max(s, axis=-1, keepdims=True)
        if first:
            m_new = jnp.broadcast_to(m_row, (tq, HEAD_DIM))
        else:
            m_old = m_in[:, sl]
            m_new = jnp.maximum(m_old, m_row)
        p = jnp.where(ok, jnp.exp(s - m_new[:, 0:1]), 0.0)
        l_new = jnp.broadcast_to(jnp.sum(p, axis=-1, keepdims=True), (tq, HEAD_DIM))
        acc = jnp.dot(p.astype(jnp.bfloat16), v, preferred_element_type=jnp.float32)
        if not first:
            alpha = jnp.exp(m_old - m_new)
            l_new = alpha * l_in[:, sl] + l_new
            acc = alpha * acc_in[:, sl] + acc
        if last:
            outs[0][:, sl] = (acc / l_new).astype(outs[0].dtype)
        else:
            outs[0][:, sl] = m_new
            outs[1][:, sl] = l_new
            outs[2][:, sl] = acc


def _a_band_group(gi, qn, akv, stats, bsz, t, out_dtype):
    w, dil = A_PATTERNS[gi]
    assert w // dil == A_TQ
    p = bsz * t
    n_res = t // dil
    nq = n_res // A_TQ
    first, last = gi == 0, gi == A_N_GROUPS - 1
    rows = p // dil
    qv = qn.reshape(rows, dil * A_Q)
    kvv = akv.reshape(rows, dil * 2 * A_KV)
    blk = (A_TQ, A_KV)
    cur = lambda b, r, i: b * nq + i
    prev = lambda b, r, i: b * nq + jnp.maximum(i - 1, 0)
    in_specs = [pl.BlockSpec(blk, lambda b, r, i: (cur(b, r, i), r * A_N_GROUPS + gi)),
                pl.BlockSpec(blk, lambda b, r, i: (prev(b, r, i), r * 2)),
                pl.BlockSpec(blk, lambda b, r, i: (cur(b, r, i), r * 2)),
                pl.BlockSpec(blk, lambda b, r, i: (prev(b, r, i), r * 2 + 1)),
                pl.BlockSpec(blk, lambda b, r, i: (cur(b, r, i), r * 2 + 1))]
    args = [qv, kvv, kvv, kvv, kvv]
    stat_spec = pl.BlockSpec(blk, lambda b, r, i: (cur(b, r, i), r))
    if not first:
        in_specs += [stat_spec] * 3
        args += [s.reshape(rows, dil * A_KV) for s in stats]
    if last:
        out_shape = jax.ShapeDtypeStruct((rows, dil * A_OUT), out_dtype)
        out_specs = stat_spec
    else:
        out_shape = (jax.ShapeDtypeStruct((rows, dil * A_KV), jnp.float32),) * 3
        out_specs = (stat_spec,) * 3
    out = pl.pallas_call(
        functools.partial(_a_band_kernel, gi=gi, dil=dil, first=first, last=last),
        out_shape=out_shape,
        grid=(bsz, dil, nq),
        in_specs=in_specs,
        out_specs=out_specs,
        compiler_params=_params(("parallel", "parallel", "arbitrary")),
        name="a_band_g%d" % gi,
    )(*args)
    if last:
        return out.reshape(p, A_OUT)
    return tuple(o.reshape(p, A_KV) for o in out)


def _a_attn_prompt(qn, akv, bsz, t, out_dtype):
    stats = None
    for gi in range(A_N_GROUPS):
        stats = _a_band_group(gi, qn, akv, stats, bsz, t, out_dtype)
    return stats


B_SCAN_T = 64


def _b_gates_kernel(g_ref, x_ref, cw_ref, cb_ref, wa_ref, ba_ref, wx_ref, bx_ref, lam_ref, a_ref, b_ref, gg_ref):
    t = x_ref.shape[0]
    x = x_ref[...]
    xx = jnp.concatenate([jnp.zeros((8, x.shape[1]), x.dtype), x], axis=0)
    xc = x * cw_ref[CONV_W - 1:CONV_W, :] + cb_ref[...]
    for j in range(CONV_W - 1):
        shift = CONV_W - 1 - j
        xc = xc + pltpu.roll(xx, shift, axis=0)[8:8 + t] * cw_ref[j:j + 1, :]
    x16 = xc.astype(jnp.bfloat16)
    r = jax.nn.sigmoid(jnp.dot(x16, wa_ref[...].astype(jnp.bfloat16), preferred_element_type=jnp.float32) + ba_ref[...])
    ig = jax.nn.sigmoid(jnp.dot(x16, wx_ref[...].astype(jnp.bfloat16), preferred_element_type=jnp.float32) + bx_ref[...])
    log_a = r * lam_ref[...]
    a = jnp.exp(log_a)
    a_ref[...] = a
    b_ref[...] = jnp.sqrt(1.0 - jnp.exp(2.0 * log_a)) * (ig * xc)
    g = g_ref[...]
    gg_ref[...] = 0.5 * g * (1.0 + jnp.tanh(math.sqrt(2.0 / math.pi) * (g + 0.044715 * (g * g * g))))


def _b_gates(u, b_par, bsz, t):
    conv_w, conv_b, wa, ba, wx, bx, lam = b_par
    lam_c = (-RG_C * jax.nn.softplus(-lam)).reshape(1, B_WIDTH)
    p = bsz * t
    g0 = (A_Q + 2 * A_KV) // B_BLOCK_DIM
    x0 = g0 + B_BLOCKS
    vec = lambda b, n: (0, n)
    blk = pl.BlockSpec((t, B_BLOCK_DIM), lambda b, n: (b, n))
    wspec = pl.BlockSpec((None, B_BLOCK_DIM, B_BLOCK_DIM), lambda b, n: (n, 0, 0))
    return pl.pallas_call(
        _b_gates_kernel,
        out_shape=(jax.ShapeDtypeStruct((p, B_WIDTH), jnp.float32),) * 3,
        grid=(bsz, B_BLOCKS),
        in_specs=[pl.BlockSpec((t, B_BLOCK_DIM), lambda b, n: (b, g0 + n)),
                  pl.BlockSpec((t, B_BLOCK_DIM), lambda b, n: (b, x0 + n)),
                  pl.BlockSpec((CONV_W, B_BLOCK_DIM), vec), pl.BlockSpec((1, B_BLOCK_DIM), vec),
                  wspec, pl.BlockSpec((1, B_BLOCK_DIM), vec), wspec, pl.BlockSpec((1, B_BLOCK_DIM), vec),
                  pl.BlockSpec((1, B_BLOCK_DIM), vec)],
        out_specs=(blk,) * 3,
        compiler_params=_params(("parallel", "parallel")),
        name="b_gates",
    )(u, u, conv_w, conv_b.reshape(1, B_WIDTH), wa, ba.reshape(1, B_WIDTH), wx, bx.reshape(1, B_WIDTH), lam_c)


def _b_scan_kernel(a_ref, b_ref, gg_ref, y_ref, hl_ref, h_scr):
    c = pl.program_id(0)

    @pl.when(c == 0)
    def _():
        h_scr[...] = jnp.zeros_like(h_scr)

    def body(t, h):
        h = a_ref[:, t] * h + b_ref[:, t]
        y_ref[:, t] = (h * gg_ref[:, t]).astype(y_ref.dtype)
        return h

    h = lax.fori_loop(0, a_ref.shape[1], body, h_scr[...], unroll=8)
    h_scr[...] = h
    hl_ref[...] = h


def _b_scan(a, b, gg, bsz, t, out_dtype):
    shp = (bsz, t, B_BLOCKS, B_BLOCK_DIM)
    blk = pl.BlockSpec((bsz, B_SCAN_T, B_BLOCKS, B_BLOCK_DIM), lambda c: (0, c, 0, 0))
    y, h_last = pl.pallas_call(
        _b_scan_kernel,
        out_shape=(jax.ShapeDtypeStruct(shp, out_dtype), jax.ShapeDtypeStruct((bsz, B_BLOCKS, B_BLOCK_DIM), jnp.float32)),
        grid=(t // B_SCAN_T,),
        in_specs=[blk] * 3,
        out_specs=(blk, pl.BlockSpec((bsz, B_BLOCKS, B_BLOCK_DIM), lambda c: (0, 0, 0))),
        scratch_shapes=[pltpu.VMEM((bsz, B_BLOCKS, B_BLOCK_DIM), jnp.float32)],
        compiler_params=_params(("arbitrary",)),
        name="b_scan",
    )(a.reshape(shp), b.reshape(shp), gg.reshape(shp))
    return y.reshape(bsz * t, B_WIDTH), h_last.reshape(bsz, B_WIDTH)


GDN_CHUNK = 128
GDN_GATE_ROWS = 512
GDN_HEADS_PER_STEP = 2
GDN_INV_PASSES = 3
DB_LANE = 3 * C_HEADS
DA_LANE = DB_LANE + D_HEADS


def _softplus(x):
    return jnp.maximum(x, 0.0) + jnp.log1p(jnp.exp(-jnp.abs(x)))


def _d_gates_kernel(us_ref, dtb_ref, aneg_ref, tri_ref, beta_ref, gam_ref):
    tm = us_ref.shape[0]
    us = us_ref[...]
    g = aneg_ref[...] * _softplus(us + dtb_ref[...])
    gam = jnp.dot(tri_ref[...], g, preferred_element_type=jnp.float32, precision=HI)
    beta = jax.nn.sigmoid(us)
    for h in range(D_HEADS):
        sl = slice(h * HEAD_DIM, (h + 1) * HEAD_DIM)
        beta_ref[:, sl] = jnp.broadcast_to(beta[:, DB_LANE + h:DB_LANE + h + 1], (tm, HEAD_DIM))
        gam_ref[:, sl] = jnp.broadcast_to(gam[:, DA_LANE + h:DA_LANE + h + 1], (tm, HEAD_DIM))


def _d_gates(u_s, a_log, dt_bias):
    p = u_s.shape[0]
    tm = GDN_GATE_ROWS
    pad = lambda x: jnp.zeros((1, LANES), jnp.float32).at[0, DA_LANE:DA_LANE + D_HEADS].set(x)
    r = np.arange(tm)
    tri = ((r[:, None] >= r[None, :]) & (r[:, None] // GDN_CHUNK == r[None, :] // GDN_CHUNK)).astype(np.float32)
    fixed = lambda i: (0, 0)
    return pl.pallas_call(
        _d_gates_kernel,
        out_shape=(jax.ShapeDtypeStruct((p, D_V), jnp.float32),) * 2,
        grid=(p // tm,),
        in_specs=[pl.BlockSpec((tm, LANES), lambda i: (i, 0)), pl.BlockSpec((1, LANES), fixed),
                  pl.BlockSpec((1, LANES), fixed), pl.BlockSpec((tm, tm), fixed)],
        out_specs=(pl.BlockSpec((tm, D_V), lambda i: (i, 0)),) * 2,
        compiler_params=_params(("parallel",)),
        name="d_gates",
    )(u_s, pad(dt_bias), pad(-jnp.exp(a_log)), jnp.asarray(tri))


def _mm(a, b, passes=1):
    ah, bh = a.astype(jnp.bfloat16), b.astype(jnp.bfloat16)
    out = jnp.dot(ah, bh, preferred_element_type=jnp.float32)
    if passes == 3:
        al = (a - ah.astype(jnp.float32)).astype(jnp.bfloat16)
        bl = (b - bh.astype(jnp.float32)).astype(jnp.bfloat16)
        out = out + jnp.dot(ah, bl, preferred_element_type=jnp.float32) + jnp.dot(al, bh, preferred_element_type=jnp.float32)
    return out


def _mm_nt(a, b):
    return lax.dot_general(a.astype(jnp.bfloat16), b.astype(jnp.bfloat16), (((1,), (1,)), ((), ())),
                           preferred_element_type=jnp.float32)


def _d_delta_kernel(q_in, k_in, v_in, z_ref, beta_ref, gam_ref, cwq_ref, cwk_ref, cwv_ref, og_ref,
                    y_ref, s_ref, q_scr, k_scr, v_scr):
    t = q_in.shape[0]
    c = GDN_CHUNK

    def conv_silu(x_ref, cw_ref):
        x = x_ref[...]
        xx = jnp.concatenate([jnp.zeros((8, x.shape[1]), x.dtype), x], axis=0)
        acc = x * cw_ref[CONV_W - 1:CONV_W, :]
        for j in range(CONV_W - 1):
            acc = acc + pltpu.roll(xx, CONV_W - 1 - j, axis=0)[8:8 + t] * cw_ref[j:j + 1, :]
        return acc * jax.nn.sigmoid(acc)

    qa = conv_silu(q_in, cwq_ref)
    ka = conv_silu(k_in, cwk_ref)
    v_scr[...] = conv_silu(v_in, cwv_ref)
    for hh in range(GDN_HEADS_PER_STEP):
        sl = slice(hh * HEAD_DIM, (hh + 1) * HEAD_DIM)
        qh, kh = qa[:, sl], ka[:, sl]
        q_scr[:, sl] = qh * lax.rsqrt(jnp.sum(qh * qh, axis=-1, keepdims=True) + EPS) * D_DK ** -0.5
        k_scr[:, sl] = kh * lax.rsqrt(jnp.sum(kh * kh, axis=-1, keepdims=True) + EPS)
    s_ref[...] = jnp.zeros_like(s_ref)

    ri = lax.broadcasted_iota(jnp.int32, (c, c), 0)
    ci = lax.broadcasted_iota(jnp.int32, (c, c), 1)
    lower, strict = ri >= ci, ri > ci
    eye = (ri == ci).astype(jnp.float32)

    def chunk(n, carry):
        r0 = pl.multiple_of(n * c, c)
        rows = pl.ds(r0, c)
        heads = range(GDN_HEADS_PER_STEP)
        sls = [slice(hh * HEAD_DIM, (hh + 1) * HEAD_DIM) for hh in heads]
        q = [q_scr[rows, sl] for sl in sls]
        k = [k_scr[rows, sl] for sl in sls]
        v = [v_scr[rows, sl] for sl in sls]
        beta = [beta_ref[rows, sl] for sl in sls]
        gam = [gam_ref[rows, sl] for sl in sls]
        z = [z_ref[rows, sl] for sl in sls]
        state = [s_ref[0, hh] for hh in heads]
        decay = [jnp.where(lower, jnp.exp(jnp.minimum(g - jnp.transpose(g), 0.0)), 0.0) for g in gam]
        kk = [_mm_nt(x, x) for x in k]
        npow = [-jnp.where(strict, b * a * d, 0.0) for b, a, d in zip(beta, kk, decay)]
        inv = [eye + x for x in npow]
        for _ in range(c.bit_length() - 2):
            npow = [_mm(x, x, GDN_INV_PASSES) for x in npow]
            inv = [i + _mm(i, x, GDN_INV_PASSES) for i, x in zip(inv, npow)]
        e_gam = [jnp.exp(g) for g in gam]
        u = [_mm(i, x * b, GDN_INV_PASSES) for i, x, b in zip(inv, v, beta)]
        w = [_mm(i, x * (b * e), GDN_INV_PASSES) for i, x, b, e in zip(inv, k, beta, e_gam)]
        qk = [jnp.where(lower, _mm_nt(a, b) * d, 0.0) for a, b, d in zip(q, k, decay)]
        v_new = [a - _mm(b, s) for a, b, s in zip(u, w, state)]
        o = [_mm(a * e, s) + _mm(b, x) for a, e, s, b, x in zip(q, e_gam, state, qk, v_new)]
        gam_last = [g[c - 1:c, :] for g in gam]
        k_dec_t = [jnp.transpose(x * jnp.exp(gl - g)) for x, gl, g in zip(k, gam_last, gam)]
        new_state = [jnp.exp(gl) * s + _mm(kt, x) for gl, s, kt, x in zip(gam_last, state, k_dec_t, v_new)]
        for hh in heads:
            s_ref[0, hh] = new_state[hh]
            on = o[hh] * lax.rsqrt(jnp.mean(o[hh] * o[hh], axis=-1, keepdims=True) + EPS) * og_ref[...]
            y_ref[rows, sls[hh]] = (on * (z[hh] * jax.nn.sigmoid(z[hh]))).astype(y_ref.dtype)
        return carry

    lax.fori_loop(0, t // c, chunk, 0)


def _d_delta(u_d, beta_r, gam_r, conv_w, out_gain, bsz, t, out_dtype):
    hp = GDN_HEADS_PER_STEP
    w = hp * HEAD_DIM
    nb = D_V // w
    blk = lambda sec: pl.BlockSpec((t, w), lambda b, h: (b, sec * nb + h))
    cw = lambda sec: pl.BlockSpec((CONV_W, w), lambda b, h: (0, sec * nb + h))
    return pl.pallas_call(
        _d_delta_kernel,
        out_shape=(jax.ShapeDtypeStruct((bsz * t, D_V), out_dtype),
                   jax.ShapeDtypeStruct((bsz, D_HEADS, D_DK, D_DV), jnp.float32)),
        grid=(bsz, D_HEADS // hp),
        in_specs=[blk(0), blk(1), blk(2), blk(3), blk(0), blk(0), cw(0), cw(1), cw(2),
                  pl.BlockSpec((1, HEAD_DIM), lambda b, h: (0, 0))],
        out_specs=(blk(0), pl.BlockSpec((1, hp, D_DK, D_DV), lambda b, h: (b, h, 0, 0))),
        scratch_shapes=[pltpu.VMEM((t, w), jnp.float32)] * 3,
        compiler_params=_params(("parallel", "parallel")),
        name="d_delta",
    )(u_d, u_d, u_d, u_d, beta_r, gam_r, conv_w, conv_w, conv_w, out_gain.reshape(1, HEAD_DIM))


def _dot_hi(a, b):
    return jnp.dot(a, b, preferred_element_type=jnp.float32, precision=HI)


def _dot_hi_nt(a, b):
    return lax.dot_general(a, b, (((1,), (1,)), ((), ())), preferred_element_type=jnp.float32, precision=HI)


def _gelu_tanh(g):
    return 0.5 * g * (1.0 + jnp.tanh(math.sqrt(2.0 / math.pi) * (g + 0.044715 * (g * g * g))))


def _rows_to_block(rows, n_rows):
    ri = lax.broadcasted_iota(jnp.int32, (n_rows, 1), 0)
    out = jnp.zeros((n_rows, rows[0].shape[1]), jnp.float32)
    for i, r in enumerate(rows):
        out = jnp.where(ri == i, r, out)
    return out


def _a_sample_kernel(q0_ref, q1_ref, q2_ref, kn_ref, vn_ref, kc_ref, vc_ref, qg_ref, kg_ref, o_ref, ko_ref):
    h = pl.program_id(0)
    n_ctx = kc_ref.shape[0]
    qs = [_chunk_rms(r[...], qg_ref[...]) for r in (q0_ref, q1_ref, q2_ref)]
    k_new = _chunk_rms(kn_ref[...], kg_ref[...])
    v_new = vn_ref[...]
    ko_ref[...] = k_new
    q = _rows_to_block(qs, 8)
    gi = lax.broadcasted_iota(jnp.int32, (8, 1), 0)
    live = gi < A_N_GROUPS
    dil = jnp.where(gi == 0, A_PATTERNS[0][1], jnp.where(gi == 1, A_PATTERNS[1][1], A_PATTERNS[2][1]))
    win = jnp.where(gi == 0, A_PATTERNS[0][0], jnp.where(gi == 1, A_PATTERNS[1][0], A_PATTERNS[2][0]))
    slope = jnp.exp2((-8.0 / (A_N_GROUPS * A_HEADS)) * (gi * A_HEADS + h + 1).astype(jnp.float32))
    dist = n_ctx - lax.broadcasted_iota(jnp.int32, (1, n_ctx), 1)
    ok = live & ((dist & (dil - 1)) == 0) & (dist <= win)
    scale = HEAD_DIM ** -0.5
    s = jnp.where(ok, _dot_hi_nt(q, kc_ref[...]) * scale - slope * dist.astype(jnp.float32), NEG)
    s_new = jnp.where(live, jnp.sum(q * k_new, axis=-1, keepdims=True) * scale, NEG)
    m = jnp.max(jnp.maximum(jnp.max(s, axis=-1, keepdims=True), s_new), axis=0, keepdims=True)
    p = jnp.where(ok, jnp.exp(s - m), 0.0)
    p_new = jnp.sum(jnp.where(live, jnp.exp(s_new - m), 0.0), axis=0, keepdims=True)
    l = jnp.sum(jnp.sum(p, axis=-1, keepdims=True), axis=0, keepdims=True) + p_new
    o = _dot_hi(jnp.sum(p, axis=0, keepdims=True), vc_ref[...]) + p_new * v_new
    o_ref[...] = o / l


def _a_sample(us, cache, layer, q_gain, k_gain):
    bd = us.shape[0]
    n_ctx = cache.shape[2]
    cview = cache.reshape(cache.shape[0], bd, n_ctx, 2 * A_KV)
    us3 = us.reshape(bd, 1, -1)
    col = lambda c: pl.BlockSpec((None, 1, HEAD_DIM), lambda h, b, c=c: (b, 0, c + h))
    fixed = pl.BlockSpec((1, HEAD_DIM), lambda h, b: (0, 0))
    o, k_new = pl.pallas_call(
        _a_sample_kernel,
        out_shape=(jax.ShapeDtypeStruct((bd, 1, A_OUT), jnp.float32), jax.ShapeDtypeStruct((bd, 1, A_KV), jnp.float32)),
        grid=(A_HEADS, bd),
        in_specs=[col(0), col(A_HEADS), col(2 * A_HEADS), col(3 * A_HEADS), col(4 * A_HEADS),
                  pl.BlockSpec((None, None, n_ctx, HEAD_DIM), lambda h, b: (layer, b, 0, h)),
                  pl.BlockSpec((None, None, n_ctx, HEAD_DIM), lambda h, b: (layer, b, 0, A_HEADS + h)),
                  fixed, fixed],
        out_specs=(col(0), col(0)),
        compiler_params=_params(("parallel", "parallel")),
        name="a_sample",
    )(us3, us3, us3, us3, us3, cview, cview, q_gain.reshape(1, HEAD_DIM), k_gain.reshape(1, HEAD_DIM))
    return o.reshape(bd, A_OUT), k_new.reshape(bd, A_KV)


def _b_sample_kernel(g_ref, x_ref, buf_ref, h0_ref, cw_ref, cb_ref, wa_ref, ba_ref, wx_ref, bx_ref, lam_ref, y_ref, h_ref):
    x = x_ref[...]
    xc = x * cw_ref[CONV_W - 1:CONV_W, :] + cb_ref[...]
    for j in range(CONV_W - 1):
        xc = xc + buf_ref[j] * cw_ref[j:j + 1, :]
    r = jax.nn.sigmoid(_dot_hi(xc, wa_ref[...]) + ba_ref[...])
    ig = jax.nn.sigmoid(_dot_hi(xc, wx_ref[...]) + bx_ref[...])
    log_a = r * lam_ref[...]
    h = jnp.exp(log_a) * h0_ref[...] + jnp.sqrt(1.0 - jnp.exp(2.0 * log_a)) * (ig * xc)
    h_ref[...] = h
    y_ref[...] = h * _gelu_tanh(g_ref[...])


def _b_sample(us, conv_buf, h0, b_par):
    conv_w, conv_b, wa, ba, wx, bx, lam = b_par
    bd = us.shape[0]
    lam_c = (-RG_C * jax.nn.softplus(-lam)).reshape(1, B_WIDTH)
    g0 = (A_Q + 2 * A_KV) // B_BLOCK_DIM
    vec = lambda n: (0, n)
    blk = pl.BlockSpec((bd, B_BLOCK_DIM), vec)
    one = pl.BlockSpec((1, B_BLOCK_DIM), vec)
    wspec = pl.BlockSpec((None, B_BLOCK_DIM, B_BLOCK_DIM), lambda n: (n, 0, 0))
    return pl.pallas_call(
        _b_sample_kernel,
        out_shape=(jax.ShapeDtypeStruct((bd, B_WIDTH), jnp.float32),) * 2,
        grid=(B_BLOCKS,),
        in_specs=[pl.BlockSpec((bd, B_BLOCK_DIM), lambda n: (0, g0 + n)),
                  pl.BlockSpec((bd, B_BLOCK_DIM), lambda n: (0, g0 + B_BLOCKS + n)),
                  pl.BlockSpec((CONV_W - 1, bd, B_BLOCK_DIM), lambda n: (0, 0, n)), blk,
                  pl.BlockSpec((CONV_W, B_BLOCK_DIM), vec), one, wspec, one, wspec, one, one],
        out_specs=(blk, blk),
        compiler_params=_params(("parallel",)),
        name="b_sample",
    )(us, us, jnp.swapaxes(conv_buf, 0, 1), h0, conv_w, conv_b.reshape(1, B_WIDTH), wa, ba.reshape(1, B_WIDTH),
      wx, bx.reshape(1, B_WIDTH), lam_c)


def _d_sample_kernel(q_in, k_in, v_in, z_ref, bq_ref, bk_ref, bv_ref, us_ref, dtb_ref, aneg_ref,
                     cwq_ref, cwk_ref, cwv_ref, og_ref, s0_ref, y_ref, s_ref, o_scr):
    h = pl.program_id(0)
    bd = q_in.shape[0]

    def conv_silu(x_ref, b_ref, cw_ref):
        acc = x_ref[...] * cw_ref[CONV_W - 1:CONV_W, :]
        for j in range(CONV_W - 1):
            acc = acc + b_ref[j] * cw_ref[j:j + 1, :]
        return acc * jax.nn.sigmoid(acc)

    q = conv_silu(q_in, bq_ref, cwq_ref)
    k = conv_silu(k_in, bk_ref, cwk_ref)
    v = conv_silu(v_in, bv_ref, cwv_ref)
    q = q * lax.rsqrt(jnp.sum(q * q, axis=-1, keepdims=True) + EPS) * D_DK ** -0.5
    k = k * lax.rsqrt(jnp.sum(k * k, axis=-1, keepdims=True) + EPS)
    us = us_ref[...]
    lane = lax.broadcasted_iota(jnp.int32, (1, LANES), 1)
    beta = jnp.sum(jnp.where(lane == DB_LANE + h, jax.nn.sigmoid(us), 0.0), axis=-1, keepdims=True)
    g = jnp.sum(jnp.where(lane == DA_LANE + h, aneg_ref[...] * _softplus(us + dtb_ref[...]), 0.0), axis=-1, keepdims=True)
    a = jnp.exp(g)
    k_t = jnp.transpose(k)
    q_t = jnp.transpose(q)
    for b in range(bd):
        rb = slice(b, b + 1)
        state = s0_ref[b]
        kcol, qcol = k_t[:, rb], q_t[:, rb]
        k_s = jnp.sum(kcol * state, axis=0, keepdims=True)
        v_new = beta[rb] * (v[rb] - a[rb] * k_s)
        state = a[rb] * state + kcol * v_new
        s_ref[b] = state
        o_scr[rb, :] = jnp.sum(qcol * state, axis=0, keepdims=True)
    o = o_scr[...]
    o = o * lax.rsqrt(jnp.mean(o * o, axis=-1, keepdims=True) + EPS) * og_ref[...]
    z = z_ref[...]
    y_ref[...] = o * (z * jax.nn.sigmoid(z))


def _d_sample(u_d, u_s, conv_buf, s0, layer, d_par):
    conv_w, a_log, dt_bias, out_gain = d_par
    bd = u_d.shape[0]
    pad = lambda x: jnp.zeros((1, LANES), jnp.float32).at[0, DA_LANE:DA_LANE + D_HEADS].set(x)
    sec = lambda s: pl.BlockSpec((bd, HEAD_DIM), lambda h, s=s: (0, s * D_HEADS + h))
    bsec = lambda s: pl.BlockSpec((CONV_W - 1, bd, HEAD_DIM), lambda h, s=s: (0, 0, s * D_HEADS + h))
    cw = lambda s: pl.BlockSpec((CONV_W, HEAD_DIM), lambda h, s=s: (0, s * D_HEADS + h))
    one = pl.BlockSpec((1, LANES), lambda h: (0, 0))
    buf_t = jnp.swapaxes(conv_buf, 0, 1)
    return pl.pallas_call(
        _d_sample_kernel,
        out_shape=(jax.ShapeDtypeStruct((bd, D_V), jnp.float32),
                   jax.ShapeDtypeStruct((bd, D_HEADS, D_DK, D_DV), jnp.float32)),
        grid=(D_HEADS,),
        in_specs=[sec(0), sec(1), sec(2), sec(3), bsec(0), bsec(1), bsec(2),
                  pl.BlockSpec((bd, LANES), lambda h: (0, 0)), one, one, cw(0), cw(1), cw(2), one,
                  pl.BlockSpec((None, bd, None, D_DK, D_DV), lambda h: (layer, 0, h, 0, 0))],
        out_specs=(sec(0), pl.BlockSpec((bd, None, D_DK, D_DV), lambda h: (0, h, 0, 0))),
        scratch_shapes=[pltpu.VMEM((bd, HEAD_DIM), jnp.float32)],
        compiler_params=_params(("parallel",)),
        name="d_sample",
    )(u_d, u_d, u_d, u_d, buf_t, buf_t, buf_t, u_s, pad(dt_bias), pad(-jnp.exp(a_log)),
      conv_w, conv_w, conv_w, out_gain.reshape(1, HEAD_DIM), s0)


NSA_S_PAGES = 4
NSA_S_LANES = 384


def _nsa_s_sum_kernel(pt_ref, *refs):
    pages, w_ref, o_ref = refs[:NSA_S_PAGES], refs[NSA_S_PAGES], refs[NSA_S_PAGES + 1]
    per_page = PAGE_SIZE // C_BLOCK
    for i, page in enumerate(pages):
        x = page[...] * w_ref[...]
        for j in range(per_page):
            row = i * per_page + j
            r = jnp.sum(x[j * C_BLOCK:(j + 1) * C_BLOCK], axis=0)
            for kv in range(2):
                for hh in range(C_KV_HEADS):
                    c0 = (kv * C_KV_HEADS + hh) * HEAD_DIM
                    o_ref[row:row + 1, c0:c0 + HEAD_DIM] = r[kv, hh:hh + 1, :]


def _nsa_s_summaries(pool, layer, page_table, phi_k, phi_v):
    bd, n_pages = page_table.shape
    reps = PAGE_SIZE // C_BLOCK
    page_shape = (PAGE_SIZE, 2, C_KV_HEADS, HEAD_DIM)
    w = jnp.broadcast_to(jnp.stack([jnp.tile(phi_k, reps), jnp.tile(phi_v, reps)], axis=1)[:, :, None, None], page_shape)
    steps = n_pages // NSA_S_PAGES
    rows = NSA_S_PAGES * reps
    page_spec = lambda i: pl.BlockSpec((None, None) + page_shape,
                                       lambda b, s, pt, i=i: (layer, pt[b, s * NSA_S_PAGES + i], 0, 0, 0, 0))
    return pl.pallas_call(
        _nsa_s_sum_kernel,
        out_shape=jax.ShapeDtypeStruct((bd, steps * rows, 2 * C_KV), jnp.float32),
        grid_spec=pltpu.PrefetchScalarGridSpec(
            num_scalar_prefetch=1, grid=(bd, steps),
            in_specs=[page_spec(i) for i in range(NSA_S_PAGES)]
            + [pl.BlockSpec(page_shape, lambda b, s, pt: (0, 0, 0, 0))],
            out_specs=pl.BlockSpec((None, rows, 2 * C_KV), lambda b, s, pt: (b, s, 0))),
        compiler_params=_params(("parallel", "arbitrary")),
        name="nsa_s_summaries",
    )(page_table, *([pool] * NSA_S_PAGES), w)


def _nsa_s_query(q_ref, h):
    q = _rows_to_block([q_ref[:, g * HEAD_DIM:(g + 1) * HEAD_DIM] for g in range(C_GROUP)], 8)
    gi = lax.broadcasted_iota(jnp.int32, (8, 1), 0)
    slope = jnp.exp2(-0.5 * (h * C_GROUP + gi + 1).astype(jnp.float32))
    return q, gi, slope


def _nsa_s_gate(gate_ref, gi, br):
    gt = gate_ref[...]
    out = jnp.zeros((8, 1), jnp.float32)
    for g in range(C_GROUP):
        out = jnp.where(gi == g, gt[:, br * C_GROUP + g: br * C_GROUP + g + 1], out)
    return out


def _nsa_s_cmpwin_kernel(q_ref, kc_ref, vc_ref, wk_ref, wv_ref, wkn_ref, wvn_ref, gate_ref, part_ref, sel_ref, *, past):
    h = pl.program_id(1)
    q, gi, slope = _nsa_s_query(q_ref, h)
    live = gi < C_GROUP
    scale = HEAD_DIM ** -0.5
    nbc = kc_ref.shape[0]
    nidx = lax.broadcasted_iota(jnp.int32, (1, nbc), 1)
    centre = nidx.astype(jnp.float32) * C_BLOCK + (C_BLOCK - 1) / 2.0
    cmask = (nidx + 1) * C_BLOCK - 1 <= past
    s = jnp.where(cmask, _dot_hi_nt(q, kc_ref[...]) * scale - slope * jnp.abs(past - centre), NEG)
    m = jnp.max(s, axis=-1, keepdims=True)
    p = jnp.where(cmask, jnp.exp(s - m), 0.0)
    p = p / jnp.maximum(jnp.sum(p, axis=-1, keepdims=True), 1e-30)
    o_cmp = _dot_hi(p, vc_ref[...])
    imp = jnp.sum(jnp.where(live, p, 0.0), axis=0, keepdims=True)
    n_lanes = NSA_S_LANES
    n_blocks = -(-(past + 1) // C_BLOCK)
    cur = past // C_BLOCK
    lane = lax.broadcasted_iota(jnp.int32, (1, n_lanes), 1)
    imp = jnp.concatenate([imp, jnp.zeros((1, n_lanes - nbc), jnp.float32)], axis=1)
    forced = (lane == 0) | (lane == cur) | (lane == cur - 1)
    causal = (lane <= cur) & (lane < n_blocks)
    score = jnp.where(causal, imp + jnp.where(forced, FORCE, 0.0), NEG)
    score_col = jnp.transpose(jnp.broadcast_to(score, (8, n_lanes)))[:, 0:1]
    mi = lax.broadcasted_iota(jnp.int32, (n_lanes, 1), 0)
    ahead = (score_col > score) | ((score_col == score) & (mi < lane))
    rank = jnp.sum(ahead.astype(jnp.float32), axis=0, keepdims=True)
    sel = (rank < min(C_N_SEL, n_blocks)) & causal
    sel_ref[...] = jnp.broadcast_to(sel.astype(jnp.float32), (8, n_lanes))
    n_win = wk_ref.shape[0]
    dist = n_win - lax.broadcasted_iota(jnp.int32, (1, n_win), 1)
    wmask = dist <= C_WIN
    sw = jnp.where(wmask, _dot_hi_nt(q, wk_ref[...]) * scale - slope * dist.astype(jnp.float32), NEG)
    sw_new = jnp.sum(q * wkn_ref[...], axis=-1, keepdims=True) * scale
    mw = jnp.maximum(jnp.max(sw, axis=-1, keepdims=True), sw_new)
    pw = jnp.where(wmask, jnp.exp(sw - mw), 0.0)
    pw_new = jnp.exp(sw_new - mw)
    o_win = (_dot_hi(pw, wv_ref[...]) + pw_new * wvn_ref[...]) / (jnp.sum(pw, axis=-1, keepdims=True) + pw_new)
    part_ref[...] = _nsa_s_gate(gate_ref, gi, 0) * o_cmp + _nsa_s_gate(gate_ref, gi, 2) * o_win


def _nsa_s_cmpwin(qn, kvc, win_buf, layer, win_new, gates, past):
    bd = qn.shape[0]
    nbc = kvc.shape[1]
    n_win = win_buf.shape[2]
    wview = win_buf.reshape(win_buf.shape[0], bd, n_win, 2 * C_KV)
    rows = lambda c: pl.BlockSpec((None, 1, HEAD_DIM), lambda b, h, c=c: (b, 0, c + h))
    qn, win_new, gates = qn.reshape(bd, 1, -1), win_new.reshape(bd, 1, -1), gates.reshape(bd, 1, -1)
    return pl.pallas_call(
        functools.partial(_nsa_s_cmpwin_kernel, past=past),
        out_shape=(jax.ShapeDtypeStruct((bd, C_KV_HEADS, 8, HEAD_DIM), jnp.float32),
                   jax.ShapeDtypeStruct((bd, C_KV_HEADS, 8, NSA_S_LANES), jnp.float32)),
        grid=(bd, C_KV_HEADS),
        in_specs=[pl.BlockSpec((None, 1, C_GROUP * HEAD_DIM), lambda b, h: (b, 0, h)),
                  pl.BlockSpec((None, nbc, HEAD_DIM), lambda b, h: (b, 0, h)),
                  pl.BlockSpec((None, nbc, HEAD_DIM), lambda b, h: (b, 0, C_KV_HEADS + h)),
                  pl.BlockSpec((None, None, n_win, HEAD_DIM), lambda b, h: (layer, b, 0, h)),
                  pl.BlockSpec((None, None, n_win, HEAD_DIM), lambda b, h: (layer, b, 0, C_KV_HEADS + h)),
                  rows(0), rows(C_KV_HEADS), rows(0)],
        out_specs=(pl.BlockSpec((None, None, 8, HEAD_DIM), lambda b, h: (b, h, 0, 0)),
                   pl.BlockSpec((None, None, 8, NSA_S_LANES), lambda b, h: (b, h, 0, 0))),
        compiler_params=_params(("parallel", "parallel")),
        name="nsa_s_cmpwin",
    )(qn, kvc, kvc, wview, wview, win_new, win_new, gates)


def _nsa_s_sel_kernel(hp_ref, blk_ref, q_ref, kv_ref, kn_ref, vn_ref, gate_ref, part_ref, o_ref,
                      m_scr, l_scr, acc_scr, *, past):
    b = pl.program_id(0)
    h = pl.program_id(1)
    j = pl.program_id(2)
    scale = HEAD_DIM ** -0.5
    qs = [q_ref[:, g * HEAD_DIM:(g + 1) * HEAD_DIM] for g in range(C_GROUP)]

    @pl.when(j == 0)
    def _():
        m_scr[...] = jnp.zeros_like(m_scr)
        l_scr[...] = jnp.ones_like(l_scr)
        acc_scr[...] = jnp.zeros_like(acc_scr)
        for g in range(C_GROUP):
            s_new = jnp.sum(qs[g] * kn_ref[...], axis=-1, keepdims=True) * scale
            m_scr[g:g + 1, :] = jnp.broadcast_to(s_new, (1, HEAD_DIM))
            acc_scr[g:g + 1, :] = vn_ref[...]

    blk = blk_ref[(b * C_KV_HEADS + h) * pl.num_programs(2) + j]
    mine = lax.broadcasted_iota(jnp.int32, (1, C_KV_HEADS, HEAD_DIM), 1) == h
    k3 = jnp.where(mine, kv_ref[:, 0], 0.0)
    v3 = jnp.where(mine, kv_ref[:, 1], 0.0)
    dist = (past - blk * C_BLOCK - lax.broadcasted_iota(jnp.int32, (C_BLOCK, 1, HEAD_DIM), 0)).astype(jnp.float32)
    for g in range(C_GROUP):
        slope = jnp.exp2(-0.5 * (jnp.zeros((1, 1, HEAD_DIM), jnp.float32) + (h * C_GROUP + g + 1).astype(jnp.float32)))
        qk = jnp.sum(k3 * qs[g][None], axis=1, keepdims=True)
        s = jnp.sum(qk, axis=-1, keepdims=True) * scale - slope * dist
        m_old = m_scr[g:g + 1, :][None]
        m_new = jnp.maximum(m_old, jnp.max(s, axis=0, keepdims=True))
        p = jnp.exp(s - m_new)
        alpha = jnp.exp(m_old - m_new)
        l_scr[g:g + 1, :] = (alpha * l_scr[g:g + 1, :][None] + jnp.sum(p, axis=0, keepdims=True))[0]
        pv = jnp.sum(jnp.sum(p * v3, axis=0), axis=0, keepdims=True)
        acc_scr[g:g + 1, :] = alpha[0] * acc_scr[g:g + 1, :] + pv
        m_scr[g:g + 1, :] = m_new[0]

    @pl.when(j == pl.num_programs(2) - 1)
    def _():
        gi = lax.broadcasted_iota(jnp.int32, (8, 1), 0)
        o_ref[...] = part_ref[...] + _nsa_s_gate(gate_ref, gi, 1) * (acc_scr[...] / l_scr[...])


def _nsa_s_select(qn, pool, layer, half_pages, blocks, sel_new, gates, part, past):
    bd = qn.shape[0]
    n_sel = blocks.shape[-1]
    n_phys = pool.shape[1]
    halves = PAGE_SIZE // C_BLOCK
    half_shape = (C_BLOCK, 2, C_KV_HEADS, HEAD_DIM)
    pview = pool.reshape((pool.shape[0], n_phys * halves) + half_shape)
    rows = lambda c: pl.BlockSpec((None, 1, HEAD_DIM), lambda b, h, j, hp, bl, c=c: (b, 0, c + h))
    qn, sel_new, gates = qn.reshape(bd, 1, -1), sel_new.reshape(bd, 1, -1), gates.reshape(bd, 1, -1)
    flat = lambda b, h, j: (b * C_KV_HEADS + h) * n_sel + j
    blk4 = pl.BlockSpec((None, None, 8, HEAD_DIM), lambda b, h, j, hp, bl: (b, h, 0, 0))
    return pl.pallas_call(
        functools.partial(_nsa_s_sel_kernel, past=past),
        out_shape=jax.ShapeDtypeStruct((bd, C_KV_HEADS, 8, HEAD_DIM), jnp.float32),
        grid_spec=pltpu.PrefetchScalarGridSpec(
            num_scalar_prefetch=2, grid=(bd, C_KV_HEADS, n_sel),
            in_specs=[pl.BlockSpec((None, 1, C_GROUP * HEAD_DIM), lambda b, h, j, hp, bl: (b, 0, h)),
                      pl.BlockSpec((None, None) + half_shape,
                                   lambda b, h, j, hp, bl: (layer, hp[flat(b, h, j)], 0, 0, 0, 0)),
                      rows(0), rows(C_KV_HEADS), rows(0), blk4],
            out_specs=blk4,
            scratch_shapes=[pltpu.VMEM((8, HEAD_DIM), jnp.float32), pltpu.VMEM((8, HEAD_DIM), jnp.float32),
                            pltpu.VMEM((8, HEAD_DIM), jnp.float32)]),
        compiler_params=_params(("parallel", "parallel", "arbitrary")),
        name="nsa_s_select",
    )(half_pages.reshape(-1), blocks.reshape(-1), qn, pview, sel_new, sel_new, gates, part)


def _nsa_sample(u_c, u_s, q_gain, k_gain, phi_k, phi_v, cmp_pool, sel_pool, win_buf, layer, page_table):
    bd = u_c.shape[0]
    past = page_table.shape[1] * PAGE_SIZE
    qn, cmp_new, sel_new, win_new, gates = _nsa_prep(u_c, u_s, q_gain, k_gain)
    kvc = _nsa_s_summaries(cmp_pool, layer, page_table, phi_k, phi_v)
    part, sel = _nsa_s_cmpwin(qn, kvc, win_buf, layer, win_new, gates, past)
    n_past_blocks = past // C_BLOCK
    n_sel = min(C_N_SEL, n_past_blocks + 1) - 1
    _, blocks = lax.top_k(sel[:, :, 0, :n_past_blocks], n_sel)
    blocks = blocks.astype(jnp.int32)
    halves = PAGE_SIZE // C_BLOCK
    pages = jnp.take_along_axis(page_table[:, None, :], blocks // halves, axis=2)
    half_pages = (pages * halves + blocks % halves).astype(jnp.int32)
    out = _nsa_s_select(qn, sel_pool, layer, half_pages, blocks, sel_new, gates, part, past)
    return out[:, :, :C_GROUP, :].reshape(bd, C_Q), cmp_new, sel_new, win_new


def _even_mixer(u2d, w_out3d, layer, res2d, q_gain, k_gain, b_par, kv_cache, conv_buf, h0, bsz, t):
    u_x = u2d.reshape(bsz, t, -1)[:, :, -B_WIDTH:]
    if kv_cache is None:
        qn, akv = _a_prep(u2d, q_gain, k_gain)
        a_out = _a_attn_prompt(qn, akv, bsz, t, jnp.bfloat16)
        a_state = akv.reshape(bsz, t, 2, A_HEADS, HEAD_DIM)[:, -min(A_WIN_MAX, t):]
        a_dec, b_in, gg = _b_gates(u2d, b_par, bsz, t)
        b_out, h_new = _b_scan(a_dec, b_in, gg, bsz, t, jnp.float32)
        conv_new = u_x[:, -(CONV_W - 1):]
    else:
        a_out, k_new = _a_sample(u2d, kv_cache, layer, q_gain, k_gain)
        a_state = jnp.stack([k_new, u2d[:, A_Q + A_KV:A_Q + 2 * A_KV]], axis=1).reshape(bsz, t, 2, A_HEADS, HEAD_DIM)
        b_out, h_new = _b_sample(u2d, conv_buf, h0, b_par)
        conv_new = jnp.concatenate([conv_buf[:, 1:], u_x], axis=1)
    out = _matmul([a_out, b_out], w_out3d, layer, 0, D_MODEL, kv_cache is not None, res=res2d)
    return out, (a_state, conv_new, h_new)


def _odd_mixer(u_parts, w_out3d, layer, res2d, q_gain, k_gain, phi_k, phi_v, d_par,
               cmp_pool, sel_pool, page_table, win_buf, d_conv_buf, d_s0, bsz, t):
    u_c2d, u_d2d, u_s2d = u_parts
    kv_shape = (bsz, t, 2, C_KV_HEADS, HEAD_DIM)
    d_in = u_d2d.reshape(bsz, t, -1)[:, :, :D_CONV]
    conv_w, a_log, dt_bias, out_gain = d_par
    if page_table is None:
        qn, cmp2d, sel2d, win2d, gates = _nsa_prep(u_c2d, u_s2d, q_gain, k_gain)
        kc, vc = _nsa_summaries(cmp2d, phi_k, phi_v)
        o_c = _nsa_prompt(qn, kc, vc, sel2d, win2d, gates, bsz, t, jnp.bfloat16)
        win_state = win2d.reshape(kv_shape)[:, -min(C_WIN, t):]
        beta_r, gam_r = _d_gates(u_s2d, a_log, dt_bias)
        d_out, d_s = _d_delta(u_d2d, beta_r, gam_r, conv_w, out_gain, bsz, t, jnp.bfloat16)
        d_conv_new = d_in[:, -(CONV_W - 1):]
    else:
        o_c, cmp2d, sel2d, win2d = _nsa_sample(u_c2d, u_s2d, q_gain, k_gain, phi_k, phi_v,
                                                cmp_pool, sel_pool, win_buf, layer, page_table)
        win_state = win2d.reshape(kv_shape)
        d_out, d_s = _d_sample(u_d2d, u_s2d, d_conv_buf, d_s0, layer, d_par)
        d_conv_new = jnp.concatenate([d_conv_buf[:, 1:], d_in], axis=1)
    out = _matmul([o_c, d_out], w_out3d, layer, 0, D_MODEL, page_table is not None, res=res2d)
    return out, (cmp2d.reshape(kv_shape), sel2d.reshape(kv_shape), win_state, d_conv_new, d_s)


ODD_C = C_Q + 6 * C_KV
ODD_G0 = ODD_C
ODD_D0 = ODD_G0 + 3 * C_HEADS
ODD_D = 2 * D_QK + 2 * D_V
ODD_S0 = ODD_D0 + ODD_D
ODD_IN = ODD_S0 + 2 * D_HEADS


def kernel(x_prompt, x_sample, cache_a_kv, state_b_conv, state_b_h, cache_c_cmp_kv, cache_c_sel_kv, cache_c_win_kv, state_d_conv, state_d_S, page_table, norm_mix, norm_ffn, even_w_in, even_w_out, a_q_norm, a_k_norm, b_conv_w, b_conv_b, b_gate_a_w, b_gate_a_b, b_gate_x_w, b_gate_x_b, b_lambda, odd_w_in, odd_w_out, c_q_norm, c_k_norm, c_phi_k, c_phi_v, d_conv_w, d_a_log, d_dt_bias, d_out_norm, moe_group_w, moe_group_b, moe_expert_w, moe_expert_b, moe_w1, moe_w3, moe_w2):
    bp, sp, d = x_prompt.shape
    bs, ss, _ = x_sample.shape
    depth = norm_mix.shape[0]
    hp = x_prompt.reshape(bp * sp, d)
    hs = x_sample.reshape(bs * ss, d)
    outs = {k: [] for k in ("ak", "bc", "bh", "cc", "cs", "cw", "dc", "ds")}
    outs_s = {k: [] for k in outs}
    for l in range(depth):
        i = l // 2
        xp = _rmsnorm(hp, norm_mix[l], jnp.bfloat16)
        xs = _rmsnorm(hs, norm_mix[l], jnp.float32)
        if l % 2 == 0:
            b_par = (b_conv_w[i], b_conv_b[i], b_gate_a_w[i], b_gate_a_b[i], b_gate_x_w[i], b_gate_x_b[i], b_lambda[i])
            n_in = even_w_in.shape[-1]
            up = _matmul(xp, even_w_in, i, 0, n_in, False)
            us = _matmul(xs, even_w_in, i, 0, n_in, True)
            hp, st_p = _even_mixer(up, even_w_out, i, hp, a_q_norm[i], a_k_norm[i], b_par, None, None, None, bp, sp)
            hs, st_s = _even_mixer(us, even_w_out, i, hs, a_q_norm[i], a_k_norm[i], b_par,
                                   cache_a_kv, state_b_conv[i], state_b_h[i], bs, ss)
            for dst, st in ((outs, st_p), (outs_s, st_s)):
                dst["ak"].append(st[0]); dst["bc"].append(st[1]); dst["bh"].append(st[2])
        else:
            d_par = (d_conv_w[i], d_a_log[i], d_dt_bias[i], d_out_norm[i])
            w_d = odd_w_in[:, :, ODD_D0:ODD_S0]
            w_s = jnp.concatenate([odd_w_in[:, :, ODD_G0:ODD_D0], odd_w_in[:, :, ODD_S0:],
                                   jnp.zeros((odd_w_in.shape[0], d, LANES - 3 * C_HEADS - 2 * D_HEADS), jnp.float32)], axis=-1)
            ups = (_matmul(xp, odd_w_in, i, 0, ODD_C, False), _matmul(xp, w_d, i, 0, ODD_D, False), _matmul(xp, w_s, i, 0, LANES, False))
            uss = (_matmul(xs, odd_w_in, i, 0, ODD_C, True), _matmul(xs, w_d, i, 0, ODD_D, True), _matmul(xs, w_s, i, 0, LANES, True))
            hp, st_p = _odd_mixer(ups, odd_w_out, i, hp, c_q_norm[i], c_k_norm[i], c_phi_k[i], c_phi_v[i], d_par,
                                  None, None, None, None, None, None, bp, sp)
            hs, st_s = _odd_mixer(uss, odd_w_out, i, hs, c_q_norm[i], c_k_norm[i], c_phi_k[i], c_phi_v[i], d_par,
                                  cache_c_cmp_kv, cache_c_sel_kv, page_table, cache_c_win_kv,
                                  state_d_conv[i], state_d_S, bs, ss)
            for dst, st in ((outs, st_p), (outs_s, st_s)):
                dst["cc"].append(st[0]); dst["cs"].append(st[1]); dst["cw"].append(st[2])
                dst["dc"].append(st[3]); dst["ds"].append(st[4])
        w_router = jnp.concatenate([moe_group_w[l], moe_expert_w[l],
                                    jnp.zeros((d, ROUTER_PAD - N_GROUPS - N_EXPERTS), jnp.float32)], axis=-1)
        b_router = jnp.concatenate([moe_group_b[l], moe_expert_b[l],
                                    jnp.zeros((ROUTER_PAD - N_GROUPS - N_EXPERTS,), jnp.float32)])[None, :]
        hp = _hier_moe(hp, norm_ffn[l], w_router, b_router, moe_w1, moe_w3, moe_w2, l, 256, jnp.bfloat16)
        hs = _hier_moe(hs, norm_ffn[l], w_router, b_router, moe_w1, moe_w3, moe_w2, l, 8, jnp.float32)
    res = [hp.reshape(bp, sp, d), hs.reshape(bs, ss, d)]
    for key in ("ak", "bc", "bh", "cc", "cs", "cw", "dc", "ds"):
        res.append(jnp.stack(outs[key]))
        res.append(jnp.stack(outs_s[key]))
    return tuple(res)
```

```python
import functools
import math

import jax
import jax.numpy as jnp
import numpy as np
from jax import lax
from jax.experimental import pallas as pl
from jax.experimental.pallas import tpu as pltpu

D_MODEL = 4096
HEAD_DIM = 128
CONV_W = 4
BAND_BLOCK = 128
A_PATTERNS = ((128, 1), (512, 4), (2048, 16))
A_N_GROUPS = len(A_PATTERNS)
A_HEADS = D_MODEL // 512
A_WIN_MAX = max(w for w, _ in A_PATTERNS)
A_Q = A_N_GROUPS * A_HEADS * HEAD_DIM
A_KV = A_HEADS * HEAD_DIM
A_OUT = A_HEADS * HEAD_DIM
B_WIDTH = 3 * D_MODEL // 4
B_BLOCKS = B_WIDTH // HEAD_DIM
B_BLOCK_DIM = B_WIDTH // B_BLOCKS
RG_C = 8.0
C_HEADS = D_MODEL // 256
C_KV_HEADS = C_HEADS // 4
C_GROUP = C_HEADS // C_KV_HEADS
C_BLOCK = 64
C_N_SEL = 16
C_WIN = 512
C_Q = C_HEADS * HEAD_DIM
C_KV = C_KV_HEADS * HEAD_DIM
D_HEADS = D_MODEL // 256
D_DK = HEAD_DIM
D_DV = HEAD_DIM
D_QK = D_HEADS * D_DK
D_V = D_HEADS * D_DV
D_CONV = 2 * D_QK + D_V
N_GROUPS = 8
EXPERTS_PER_GROUP = 8
N_EXPERTS = N_GROUPS * EXPERTS_PER_GROUP
TOP_K = 2
D_EXPERT = D_MODEL // 8
PAGE_SIZE = 128
EPS = 1e-6
NEG = -1e30
FORCE = 1e4

LANES = 128
VMEM_LIMIT = 56 * 1024 * 1024
ROUTER_PAD = LANES

HI = lax.Precision.HIGHEST


def _params(sem):
    return pltpu.CompilerParams(dimension_semantics=sem, vmem_limit_bytes=VMEM_LIMIT)


def _rmsnorm_kernel(x_ref, g_ref, o_ref):
    x = x_ref[...]
    y = x * lax.rsqrt(jnp.mean(x * x, axis=-1, keepdims=True) + EPS)
    o_ref[...] = (y * g_ref[...]).astype(o_ref.dtype)


def _rmsnorm(x2d, gain, out_dtype):
    m, d = x2d.shape
    tm = min(m, 512)
    return pl.pallas_call(
        _rmsnorm_kernel,
        out_shape=jax.ShapeDtypeStruct((m, d), out_dtype),
        grid=(m // tm,),
        in_specs=[pl.BlockSpec((tm, d), lambda i: (i, 0)), pl.BlockSpec((1, d), lambda i: (0, 0))],
        out_specs=pl.BlockSpec((tm, d), lambda i: (i, 0)),
        compiler_params=_params(("parallel",)),
        name="rmsnorm",
    )(x2d, gain.reshape(1, d))


def _matmul_kernel(*refs, exact, has_res, k_bounds):
    n_x = len(k_bounds) - 1
    x_refs, w_ref = refs[:n_x], refs[n_x]
    if has_res:
        r_ref, o_ref, acc_ref = refs[n_x + 1:]
    else:
        o_ref, acc_ref = refs[n_x + 1:]
    k = pl.program_id(2)

    @pl.when(k == 0)
    def _():
        acc_ref[...] = jnp.zeros_like(acc_ref)

    for p, x_ref in enumerate(x_refs):
        @pl.when((k >= k_bounds[p]) & (k < k_bounds[p + 1]))
        def _(x_ref=x_ref):
            if exact:
                acc_ref[...] += jnp.dot(x_ref[...], w_ref[...], preferred_element_type=jnp.float32, precision=HI)
            else:
                acc_ref[...] += jnp.dot(x_ref[...].astype(jnp.bfloat16), w_ref[...].astype(jnp.bfloat16),
                                        preferred_element_type=jnp.float32)

    @pl.when(k == pl.num_programs(2) - 1)
    def _():
        out = acc_ref[...]
        if has_res:
            out = out + r_ref[...]
        o_ref[...] = out


def _matmul(xs, w3d, layer, col0, n, exact, res=None, tn=1024):
    if not isinstance(xs, (list, tuple)):
        xs = [xs]
    m = xs[0].shape[0]
    tm = min(m, 1024)
    tn = min(tn, n)
    tk = 512 if exact else 1024
    assert tm * tn * 4 * 5 + tk * (tm + tn) * 4 * 2 * len(xs) <= VMEM_LIMIT
    assert m % tm == 0 and n % tn == 0 and col0 % tn == 0 and all(x.shape[1] % tk == 0 for x in xs)
    jb = col0 // tn
    k_bounds = [0]
    for x in xs:
        k_bounds.append(k_bounds[-1] + x.shape[1] // tk)
    in_specs = []
    for p in range(len(xs)):
        lo, hi = k_bounds[p], k_bounds[p + 1]
        in_specs.append(pl.BlockSpec((tm, tk), lambda i, j, k, lo=lo, hi=hi: (i, jnp.clip(k, lo, hi - 1) - lo)))
    in_specs.append(pl.BlockSpec((None, tk, tn), lambda i, j, k: (layer, k, j + jb)))
    args = list(xs) + [w3d]
    if res is not None:
        in_specs.append(pl.BlockSpec((tm, tn), lambda i, j, k: (i, j)))
        args.append(res)
    return pl.pallas_call(
        functools.partial(_matmul_kernel, exact=exact, has_res=res is not None, k_bounds=tuple(k_bounds)),
        out_shape=jax.ShapeDtypeStruct((m, n), jnp.float32),
        grid=(m // tm, n // tn, k_bounds[-1]),
        in_specs=in_specs,
        out_specs=pl.BlockSpec((tm, tn), lambda i, j, k: (i, j)),
        scratch_shapes=[pltpu.VMEM((tm, tn), jnp.float32)],
        compiler_params=_params(("parallel", "parallel", "arbitrary")),
        name="proj",
    )(*args)


def _router_kernel(h_ref, g_ref, w_ref, b_ref, xn_ref, logit_ref):
    x = h_ref[...]
    y = x * lax.rsqrt(jnp.mean(x * x, axis=-1, keepdims=True) + EPS) * g_ref[...]
    xn_ref[...] = y.astype(xn_ref.dtype)
    logit_ref[...] = jnp.dot(y, w_ref[...], preferred_element_type=jnp.float32, precision=HI) + b_ref[...]


def _router(h2d, gain, w_router, b_router, xn_dtype):
    m, d = h2d.shape
    tm = min(m, 256)
    return pl.pallas_call(
        _router_kernel,
        out_shape=(jax.ShapeDtypeStruct((m, d), xn_dtype), jax.ShapeDtypeStruct((m, ROUTER_PAD), jnp.float32)),
        grid=(m // tm,),
        in_specs=[pl.BlockSpec((tm, d), lambda i: (i, 0)), pl.BlockSpec((1, d), lambda i: (0, 0)),
                  pl.BlockSpec((d, ROUTER_PAD), lambda i: (0, 0)), pl.BlockSpec((1, ROUTER_PAD), lambda i: (0, 0))],
        out_specs=(pl.BlockSpec((tm, d), lambda i: (i, 0)), pl.BlockSpec((tm, ROUTER_PAD), lambda i: (i, 0))),
        compiler_params=_params(("parallel",)),
        name="ffn_norm_router",
    )(h2d, gain.reshape(1, d), w_router, b_router)


MOE_K_CHUNK = 512


def _moe_up_kernel(be_ref, nu_ref, x_ref, w1_ref, w3_ref, o_ref, *, exact):
    tm = x_ref.shape[0]
    used = pl.program_id(0) < nu_ref[0]

    @pl.when(used)
    def _():
        a = jnp.zeros((tm, D_EXPERT), jnp.float32)
        b = jnp.zeros((tm, D_EXPERT), jnp.float32)
        for c in range(D_MODEL // MOE_K_CHUNK):
            sl = slice(c * MOE_K_CHUNK, (c + 1) * MOE_K_CHUNK)
            if exact:
                a += jnp.dot(x_ref[:, sl], w1_ref[sl, :], preferred_element_type=jnp.float32, precision=HI)
                b += jnp.dot(x_ref[:, sl], w3_ref[sl, :], preferred_element_type=jnp.float32, precision=HI)
            else:
                a += jnp.dot(x_ref[:, sl], w1_ref[sl, :].astype(jnp.bfloat16), preferred_element_type=jnp.float32)
                b += jnp.dot(x_ref[:, sl], w3_ref[sl, :].astype(jnp.bfloat16), preferred_element_type=jnp.float32)
        o_ref[...] = (a * jax.nn.sigmoid(a) * b).astype(o_ref.dtype)

    @pl.when(jnp.logical_not(used))
    def _():
        o_ref[...] = jnp.zeros_like(o_ref)


def _moe_down_kernel(be_ref, nu_ref, h_ref, w2_ref, g_ref, o_ref, *, exact):
    used = pl.program_id(0) < nu_ref[0]

    @pl.when(used)
    def _():
        if exact:
            out = jnp.dot(h_ref[...], w2_ref[...], preferred_element_type=jnp.float32, precision=HI)
        else:
            out = jnp.dot(h_ref[...], w2_ref[...].astype(jnp.bfloat16), preferred_element_type=jnp.float32)
        o_ref[...] = out * g_ref[...]

    @pl.when(jnp.logical_not(used))
    def _():
        o_ref[...] = jnp.zeros_like(o_ref)


def _moe_experts(x_rows, row_gate, blk_exp, n_used, w1, w3, w2, layer, tm):
    rows, d = x_rows.shape
    n_blk = rows // tm
    exact = x_rows.dtype == jnp.float32
    up = pl.pallas_call(
        functools.partial(_moe_up_kernel, exact=exact),
        out_shape=jax.ShapeDtypeStruct((rows, D_EXPERT), x_rows.dtype),
        grid_spec=pltpu.PrefetchScalarGridSpec(
            num_scalar_prefetch=2, grid=(n_blk,),
            in_specs=[pl.BlockSpec((tm, d), lambda i, be, nu: (jnp.minimum(i, nu[0] - 1), 0)),
                      pl.BlockSpec((None, None, d, D_EXPERT), lambda i, be, nu: (layer, be[i], 0, 0)),
                      pl.BlockSpec((None, None, d, D_EXPERT), lambda i, be, nu: (layer, be[i], 0, 0))],
            out_specs=pl.BlockSpec((tm, D_EXPERT), lambda i, be, nu: (i, 0))),
        compiler_params=_params(("arbitrary",)),
        name="moe_up",
    )(blk_exp, n_used, x_rows, w1, w3)
    return pl.pallas_call(
        functools.partial(_moe_down_kernel, exact=exact),
        out_shape=jax.ShapeDtypeStruct((rows, d), jnp.float32),
        grid_spec=pltpu.PrefetchScalarGridSpec(
            num_scalar_prefetch=2, grid=(n_blk,),
            in_specs=[pl.BlockSpec((tm, D_EXPERT), lambda i, be, nu: (jnp.minimum(i, nu[0] - 1), 0)),
                      pl.BlockSpec((None, None, D_EXPERT, d), lambda i, be, nu: (layer, be[i], 0, 0)),
                      pl.BlockSpec((tm, 1), lambda i, be, nu: (jnp.minimum(i, nu[0] - 1), 0))],
            out_specs=pl.BlockSpec((tm, d), lambda i, be, nu: (i, 0))),
        compiler_params=_params(("arbitrary",)),
        name="moe_down",
    )(blk_exp, n_used, up, w2, row_gate)


def _hier_moe(h2d, gain, w_router, b_router, w1, w3, w2, layer, tm, xn_dtype):
    n_tok, d = h2d.shape
    xn, logits = _router(h2d, gain, w_router, b_router, xn_dtype)
    g_logit = logits[:, :N_GROUPS]
    g_prob = jax.nn.softmax(g_logit, axis=-1)
    grp = jnp.argmax(g_logit, axis=-1)
    p_grp = jnp.take_along_axis(g_prob, grp[:, None], axis=1)[:, 0]
    e_logit = logits[:, N_GROUPS:N_GROUPS + N_EXPERTS].reshape(-1, N_GROUPS, EXPERTS_PER_GROUP)
    e_logit = jnp.take_along_axis(e_logit, grp[:, None, None], axis=1)[:, 0]
    e_val, e_idx = lax.top_k(e_logit, TOP_K)
    gates = p_grp[:, None] * jax.nn.softmax(e_val, axis=-1)
    experts = grp[:, None] * EXPERTS_PER_GROUP + e_idx
    n_asg = n_tok * TOP_K
    flat_e = experts.reshape(-1).astype(jnp.int32)
    order = jnp.argsort(flat_e)
    sorted_e = flat_e[order]
    counts = jnp.bincount(flat_e, length=N_EXPERTS)
    padded = (counts + tm - 1) // tm * tm
    pad_end = jnp.cumsum(padded)
    pad_start = pad_end - padded
    start = jnp.cumsum(counts) - counts
    dest = (pad_start[sorted_e] + jnp.arange(n_asg) - start[sorted_e]).astype(jnp.int32)
    n_rows = -(-(n_asg + N_EXPERTS * (tm - 1)) // tm) * tm
    n_blk = n_rows // tm
    tok = (order // TOP_K).astype(jnp.int32)
    row_tok = jnp.zeros((n_rows,), jnp.int32).at[dest].set(tok)
    row_gate = jnp.zeros((n_rows,), jnp.float32).at[dest].set(gates.reshape(-1)[order])
    blk_exp = jnp.minimum(jnp.searchsorted(pad_end, jnp.arange(n_blk) * tm, side='right'),
                          N_EXPERTS - 1).astype(jnp.int32)
    x_rows = xn[row_tok]
    n_used = (pad_end[-1:] // tm).astype(jnp.int32)
    out = _moe_experts(x_rows, row_gate[:, None], blk_exp, n_used, w1, w3, w2, layer, tm)
    asg_row = jnp.zeros((n_asg,), jnp.int32).at[order].set(dest).reshape(n_tok, TOP_K)
    y = h2d
    for j in range(TOP_K):
        y = y + out[asg_row[:, j]]
    return y


NSA_PREP_ROWS = 256
NSA_SUM_ROWS = 512
NSA_TQ = 128
NSA_GATE_LANES = LANES
TQ_SHIFT = NSA_TQ.bit_length() - 1
C_BLOCK_SHIFT = C_BLOCK.bit_length() - 1


def _chunk_rms(x, gain):
    return x * lax.rsqrt(jnp.mean(x * x, axis=-1, keepdims=True) + EPS) * gain


def _nsa_prep_kernel(uc_ref, us_ref, qg_ref, kg_ref, perm_ref, q_ref, cmp_ref, sel_ref, win_ref, gate_ref):
    for c in range(C_Q // HEAD_DIM):
        sl = slice(c * HEAD_DIM, (c + 1) * HEAD_DIM)
        q_ref[:, sl] = _chunk_rms(uc_ref[:, sl], qg_ref[...])
    for br, o_ref in enumerate((cmp_ref, sel_ref, win_ref)):
        base = C_Q + br * 2 * C_KV
        for c in range(C_KV_HEADS):
            sl = slice(c * HEAD_DIM, (c + 1) * HEAD_DIM)
            o_ref[:, sl] = _chunk_rms(uc_ref[:, base + c * HEAD_DIM: base + (c + 1) * HEAD_DIM], kg_ref[br:br + 1, :])
        o_ref[:, C_KV:] = uc_ref[:, base + C_KV: base + 2 * C_KV]
    gate_ref[...] = jnp.dot(jax.nn.sigmoid(us_ref[...]), perm_ref[...], preferred_element_type=jnp.float32, precision=HI)


def _nsa_prep(u_c, u_s, q_gain, k_gain):
    p = u_c.shape[0]
    tm = min(p, NSA_PREP_ROWS)
    perm = np.zeros((LANES, C_KV_HEADS * NSA_GATE_LANES), np.float32)
    for br in range(3):
        for h in range(C_KV_HEADS):
            for g in range(C_GROUP):
                perm[br * C_HEADS + h * C_GROUP + g, h * NSA_GATE_LANES + br * C_GROUP + g] = 1.0
    row = lambda i: (i, 0)
    fixed = lambda i: (0, 0)
    return pl.pallas_call(
        _nsa_prep_kernel,
        out_shape=(jax.ShapeDtypeStruct((p, C_Q), jnp.float32),) + (jax.ShapeDtypeStruct((p, 2 * C_KV), jnp.float32),) * 3
        + (jax.ShapeDtypeStruct((p, C_KV_HEADS * NSA_GATE_LANES), jnp.float32),),
        grid=(p // tm,),
        in_specs=[pl.BlockSpec((tm, ODD_C), row), pl.BlockSpec((tm, LANES), row), pl.BlockSpec((1, HEAD_DIM), fixed),
                  pl.BlockSpec((3, HEAD_DIM), fixed), pl.BlockSpec(perm.shape, fixed)],
        out_specs=(pl.BlockSpec((tm, C_Q), row),) + (pl.BlockSpec((tm, 2 * C_KV), row),) * 3
        + (pl.BlockSpec((tm, C_KV_HEADS * NSA_GATE_LANES), row),),
        compiler_params=_params(("parallel",)),
        name="nsa_prep",
    )(u_c, u_s, q_gain.reshape(1, HEAD_DIM), k_gain, jnp.asarray(perm))


def _nsa_sum_kernel(cmp_ref, phik_ref, phiv_ref, kc_ref, vc_ref):
    kc_ref[...] = jnp.dot(phik_ref[...], cmp_ref[:, :C_KV], preferred_element_type=jnp.float32, precision=HI)
    vc_ref[...] = jnp.dot(phiv_ref[...], cmp_ref[:, C_KV:], preferred_element_type=jnp.float32, precision=HI)


def _nsa_summaries(cmp_rows, phi_k, phi_v):
    p = cmp_rows.shape[0]
    tm = NSA_SUM_ROWS
    nb = tm // C_BLOCK
    eye = jnp.eye(nb, dtype=jnp.float32)
    big_k = jnp.kron(eye, phi_k[None, :])
    big_v = jnp.kron(eye, phi_v[None, :])
    return pl.pallas_call(
        _nsa_sum_kernel,
        out_shape=(jax.ShapeDtypeStruct((p // C_BLOCK, C_KV), jnp.float32),) * 2,
        grid=(p // tm,),
        in_specs=[pl.BlockSpec((tm, 2 * C_KV), lambda i: (i, 0)), pl.BlockSpec((nb, tm), lambda i: (0, 0)),
                  pl.BlockSpec((nb, tm), lambda i: (0, 0))],
        out_specs=(pl.BlockSpec((nb, C_KV), lambda i: (i, 0)),) * 2,
        compiler_params=_params(("parallel",)),
        name="nsa_summaries",
    )(cmp_rows, big_k, big_v)


def _nsa_prompt_kernel(q_ref, kc_ref, vc_ref, sk_ref, sv_ref, wk_ref, wv_ref, gate_ref, o_ref,
                       m_scr, l_scr, acc_scr, *, n_blocks):
    h = pl.program_id(1)
    qb = pl.program_id(2)
    tq = NSA_TQ
    rows = C_GROUP * tq
    scale = HEAD_DIM ** -0.5
    q = jnp.concatenate([q_ref[:, g * HEAD_DIM:(g + 1) * HEAD_DIM] for g in range(C_GROUP)], axis=0)
    row = lax.broadcasted_iota(jnp.int32, (rows, 1), 0)
    t_row = qb * tq + (row & (tq - 1))
    slope = jnp.exp2(-0.5 * (h * C_GROUP + (row >> TQ_SHIFT) + 1).astype(jnp.float32))

    nidx = lax.broadcasted_iota(jnp.int32, (1, n_blocks), 1)
    s = lax.dot_general(q, kc_ref[...], (((1,), (1,)), ((), ())), preferred_element_type=jnp.float32, precision=HI) * scale
    centre = nidx.astype(jnp.float32) * C_BLOCK + (C_BLOCK - 1) / 2.0
    s = s - slope * jnp.abs(t_row.astype(jnp.float32) - centre)
    cmask = (nidx + 1) * C_BLOCK - 1 <= t_row
    s = jnp.where(cmask, s, NEG)
    m = jnp.max(s, axis=-1, keepdims=True)
    p = jnp.where(cmask, jnp.exp(s - m), 0.0)
    l = jnp.sum(p, axis=-1, keepdims=True)
    p = p / jnp.maximum(l, 1e-30)
    o_cmp = jnp.dot(p, vc_ref[...], preferred_element_type=jnp.float32, precision=HI)
    imp = p[0:tq]
    for g in range(1, C_GROUP):
        imp = imp + p[g * tq:(g + 1) * tq]

    t_q = t_row[0:tq]
    cur = t_q >> C_BLOCK_SHIFT
    forced = (nidx == 0) | (nidx == cur) | (nidx == cur - 1)
    causal = nidx <= cur
    score = jnp.where(causal, imp + jnp.where(forced, FORCE, 0.0), NEG)
    rank = jnp.zeros((tq, n_blocks), jnp.int32)
    for mcol in range(n_blocks):
        cm = score[:, mcol:mcol + 1]
        ahead = (cm > score) | ((cm == score) & (mcol < nidx))
        rank = rank + ahead.astype(jnp.int32)
    selm = ((rank < min(C_N_SEL, n_blocks)) & causal).astype(jnp.bfloat16)

    q16 = q.astype(jnp.bfloat16)
    slope_b = jnp.broadcast_to(slope, (rows, tq))
    t_rel = jnp.broadcast_to(t_row, (rows, tq)) - lax.broadcasted_iota(jnp.int32, (rows, tq), 1)

    def attend(k_ref, v_ref, c, mask):
        start = pl.multiple_of(c * tq, tq)
        k = k_ref[pl.ds(start, tq), :].astype(jnp.bfloat16)
        v = v_ref[pl.ds(start, tq), :].astype(jnp.bfloat16)
        dist = t_rel - c * tq
        sc = lax.dot_general(q16, k, (((1,), (1,)), ((), ())), preferred_element_type=jnp.float32) * scale
        sc = sc - slope_b * dist.astype(jnp.float32)
        ok = mask(dist)
        sc = jnp.where(ok, sc, NEG)
        m_old = m_scr[...]
        m_new = jnp.maximum(m_old, jnp.max(sc, axis=-1, keepdims=True))
        pc = jnp.where(ok, jnp.exp(sc - m_new), 0.0)
        alpha = jnp.exp(m_old - m_new)
        l_scr[...] = alpha * l_scr[...] + jnp.sum(pc, axis=-1, keepdims=True)
        acc_scr[...] = alpha * acc_scr[...] + jnp.dot(pc.astype(jnp.bfloat16), v, preferred_element_type=jnp.float32)
        m_scr[...] = m_new

    def reset():
        m_scr[...] = jnp.full_like(m_scr, NEG)
        l_scr[...] = jnp.zeros_like(l_scr)
        acc_scr[...] = jnp.zeros_like(acc_scr)

    reset()
    brow = lax.broadcasted_iota(jnp.int32, (n_blocks, tq), 0)
    bcol = lax.broadcasted_iota(jnp.int32, (n_blocks, tq), 1)

    def sel_chunk(c, causal_edge):
        expand = (brow == c * (tq // C_BLOCK) + (bcol >> C_BLOCK_SHIFT)).astype(jnp.bfloat16)
        sel_keys = jnp.dot(selm, expand, preferred_element_type=jnp.float32)
        sel_keys = jnp.concatenate([sel_keys] * C_GROUP, axis=0) > 0.5
        attend(sk_ref, sv_ref, c, (lambda dist: sel_keys & (dist >= 0)) if causal_edge else (lambda dist: sel_keys))

    def sel_body(c, carry):
        sel_chunk(c, False)
        return carry

    lax.fori_loop(0, qb, sel_body, 0)
    sel_chunk(qb, True)
    o_sel = acc_scr[...] / l_scr[...]

    reset()

    def win_body(c, carry):
        attend(wk_ref, wv_ref, c, lambda dist: dist <= C_WIN)
        return carry

    lax.fori_loop(jnp.maximum(qb - C_WIN // tq, 0), qb, win_body, 0)
    attend(wk_ref, wv_ref, qb, lambda dist: dist >= 0)
    o_win = acc_scr[...] / l_scr[...]

    gt = gate_ref[...]
    outs = []
    for g in range(C_GROUP):
        rs = slice(g * tq, (g + 1) * tq)
        outs.append(gt[:, g:g + 1] * o_cmp[rs] + gt[:, C_GROUP + g:C_GROUP + g + 1] * o_sel[rs]
                    + gt[:, 2 * C_GROUP + g:2 * C_GROUP + g + 1] * o_win[rs])
    o_ref[...] = jnp.concatenate(outs, axis=1).astype(o_ref.dtype)


def _nsa_prompt(qn, kc, vc, sel_rows, win_rows, gates, bsz, t, out_dtype):
    tq = NSA_TQ
    assert tq == HEAD_DIM
    nq = t // tq
    n_blocks = t // C_BLOCK
    kv_k = pl.BlockSpec((t, HEAD_DIM), lambda b, h, i: (b, h))
    kv_v = pl.BlockSpec((t, HEAD_DIM), lambda b, h, i: (b, C_KV_HEADS + h))
    return pl.pallas_call(
        functools.partial(_nsa_prompt_kernel, n_blocks=n_blocks),
        out_shape=jax.ShapeDtypeStruct((bsz * t, C_Q), out_dtype),
        grid=(bsz, C_KV_HEADS, nq),
        in_specs=[pl.BlockSpec((tq, C_GROUP * HEAD_DIM), lambda b, h, i: (b * nq + i, h)),
                  pl.BlockSpec((n_blocks, HEAD_DIM), lambda b, h, i: (b, h)),
                  pl.BlockSpec((n_blocks, HEAD_DIM), lambda b, h, i: (b, h)),
                  kv_k, kv_v, kv_k, kv_v,
                  pl.BlockSpec((tq, NSA_GATE_LANES), lambda b, h, i: (b * nq + i, h))],
        out_specs=pl.BlockSpec((tq, C_GROUP * HEAD_DIM), lambda b, h, i: (b * nq + i, h)),
        scratch_shapes=[pltpu.VMEM((C_GROUP * tq, tq), jnp.float32), pltpu.VMEM((C_GROUP * tq, tq), jnp.float32),
                        pltpu.VMEM((C_GROUP * tq, HEAD_DIM), jnp.float32)],
        compiler_params=_params(("parallel", "parallel", "arbitrary")),
        name="nsa_prompt",
    )(qn, kc, vc, sel_rows, sel_rows, win_rows, win_rows, gates)


A_PREP_ROWS = 256
A_TQ = BAND_BLOCK
A_SLOPES = [[2.0 ** (-8.0 * (gi * A_HEADS + h + 1.0) / (A_N_GROUPS * A_HEADS)) for h in range(A_HEADS)]
            for gi in range(A_N_GROUPS)]


def _a_prep_kernel(q_in, k_in, v_in, qg_ref, kg_ref, q_ref, kv_ref):
    for c in range(A_Q // HEAD_DIM):
        sl = slice(c * HEAD_DIM, (c + 1) * HEAD_DIM)
        q_ref[:, sl] = _chunk_rms(q_in[:, sl], qg_ref[...])
    for c in range(A_HEADS):
        sl = slice(c * HEAD_DIM, (c + 1) * HEAD_DIM)
        kv_ref[:, sl] = _chunk_rms(k_in[:, sl], kg_ref[...])
    kv_ref[:, A_KV:] = v_in[...]


def _a_prep(u, q_gain, k_gain):
    p = u.shape[0]
    tm = A_PREP_ROWS
    fixed = lambda i: (0, 0)
    return pl.pallas_call(
        _a_prep_kernel,
        out_shape=(jax.ShapeDtypeStruct((p, A_Q), jnp.float32), jax.ShapeDtypeStruct((p, 2 * A_KV), jnp.float32)),
        grid=(p // tm,),
        in_specs=[pl.BlockSpec((tm, A_Q), lambda i: (i, 0)),
                  pl.BlockSpec((tm, A_KV), lambda i: (i, A_Q // A_KV)),
                  pl.BlockSpec((tm, A_KV), lambda i: (i, A_Q // A_KV + 1)),
                  pl.BlockSpec((1, HEAD_DIM), fixed), pl.BlockSpec((1, HEAD_DIM), fixed)],
        out_specs=(pl.BlockSpec((tm, A_Q), lambda i: (i, 0)), pl.BlockSpec((tm, 2 * A_KV), lambda i: (i, 0))),
        compiler_params=_params(("parallel",)),
        name="a_prep",
    )(u, u, u, q_gain.reshape(1, HEAD_DIM), k_gain.reshape(1, HEAD_DIM))


def _a_band_kernel(*refs, gi, dil, first, last):
    if first:
        q_ref, kp_ref, kc_ref, vp_ref, vc_ref = refs[:5]
        outs = refs[5:]
    else:
        q_ref, kp_ref, kc_ref, vp_ref, vc_ref, m_in, l_in, acc_in = refs[:8]
        outs = refs[8:]
    qb = pl.program_id(2)
    tq = A_TQ
    row = lax.broadcasted_iota(jnp.int32, (tq, 2 * tq), 0)
    col = lax.broadcasted_iota(jnp.int32, (tq, 2 * tq), 1)
    dist = tq + row - col
    ok = (dist >= 0) & (dist <= tq) & ((col >= tq) | (qb > 0))
    dist_f = (dist * dil).astype(jnp.float32)
    scale = HEAD_DIM ** -0.5
    for h in range(A_HEADS):
        sl = slice(h * HEAD_DIM, (h + 1) * HEAD_DIM)
        q = q_ref[:, sl].astype(jnp.bfloat16)
        k = jnp.concatenate([kp_ref[:, sl], kc_ref[:, sl]], axis=0).astype(jnp.bfloat16)
        v = jnp.concatenate([vp_ref[:, sl], vc_ref[:, sl]], axis=0).astype(jnp.bfloat16)
        s = lax.dot_general(q, k, (((1,), (1,)), ((), ())), preferred_element_type=jnp.float32) * scale
        s = jnp.where(ok, s - A_SLOPES[gi][h] * dist_f, NEG)
        m_row = jnp.max(s, axis=-1, keepdims=True)
        if first:
            m_new = jnp.broadcast_to(m_row, (tq, HEAD_DIM))
        else:
            m_old = m_in[:, sl]
            m_new = jnp.maximum(m_old, m_row)
        p = jnp.where(ok, jnp.exp(s - m_new[:, 0:1]), 0.0)
        l_new = jnp.broadcast_to(jnp.sum(p, axis=-1, keepdims=True), (tq, HEAD_DIM))
        acc = jnp.dot(p.astype(jnp.bfloat16), v, preferred_element_type=jnp.float32)
        if not first:
            alpha = jnp.exp(m_old - m_new)
            l_new = alpha * l_in[:, sl] + l_new
            acc = alpha * acc_in[:, sl] + acc
        if last:
            outs[0][:, sl] = (acc / l_new).astype(outs[0].dtype)
        else:
            outs[0][:, sl] = m_new
            outs[1][:, sl] = l_new
            outs[2][:, sl] = acc


def _a_band_group(gi, qn, akv, stats, bsz, t, out_dtype):
    w, dil = A_PATTERNS[gi]
    assert w // dil == A_TQ
    p = bsz * t
    n_res = t // dil
    nq = n_res // A_TQ
    first, last = gi == 0, gi == A_N_GROUPS - 1
    rows = p // dil
    qv = qn.reshape(rows, dil * A_Q)
    kvv = akv.reshape(rows, dil * 2 * A_KV)
    blk = (A_TQ, A_KV)
    cur = lambda b, r, i: b * nq + i
    prev = lambda b, r, i: b * nq + jnp.maximum(i - 1, 0)
    in_specs = [pl.BlockSpec(blk, lambda b, r, i: (cur(b, r, i), r * A_N_GROUPS + gi)),
                pl.BlockSpec(blk, lambda b, r, i: (prev(b, r, i), r * 2)),
                pl.BlockSpec(blk, lambda b, r, i: (cur(b, r, i), r * 2)),
                pl.BlockSpec(blk, lambda b, r, i: (prev(b, r, i), r * 2 + 1)),
                pl.BlockSpec(blk, lambda b, r, i: (cur(b, r, i), r * 2 + 1))]
    args = [qv, kvv, kvv, kvv, kvv]
    stat_spec = pl.BlockSpec(blk, lambda b, r, i: (cur(b, r, i), r))
    if not first:
        in_specs += [stat_spec] * 3
        args += [s.reshape(rows, dil * A_KV) for s in stats]
    if last:
        out_shape = jax.ShapeDtypeStruct((rows, dil * A_OUT), out_dtype)
        out_specs = stat_spec
    else:
        out_shape = (jax.ShapeDtypeStruct((rows, dil * A_KV), jnp.float32),) * 3
        out_specs = (stat_spec,) * 3
    out = pl.pallas_call(
        functools.partial(_a_band_kernel, gi=gi, dil=dil, first=first, last=last),
        out_shape=out_shape,
        grid=(bsz, dil, nq),
        in_specs=in_specs,
        out_specs=out_specs,
        compiler_params=_params(("parallel", "parallel", "arbitrary")),
        name="a_band_g%d" % gi,
    )(*args)
    if last:
        return out.reshape(p, A_OUT)
    return tuple(o.reshape(p, A_KV) for o in out)


def _a_attn_prompt(qn, akv, bsz, t, out_dtype):
    stats = None
    for gi in range(A_N_GROUPS):
        stats = _a_band_group(gi, qn, akv, stats, bsz, t, out_dtype)
    return stats


B_SCAN_T = 64


def _b_gates_kernel(g_ref, x_ref, cw_ref, cb_ref, wa_ref, ba_ref, wx_ref, bx_ref, lam_ref, a_ref, b_ref, gg_ref):
    t = x_ref.shape[0]
    x = x_ref[...]
    xx = jnp.concatenate([jnp.zeros((8, x.shape[1]), x.dtype), x], axis=0)
    xc = x * cw_ref[CONV_W - 1:CONV_W, :] + cb_ref[...]
    for j in range(CONV_W - 1):
        shift = CONV_W - 1 - j
        xc = xc + pltpu.roll(xx, shift, axis=0)[8:8 + t] * cw_ref[j:j + 1, :]
    x16 = xc.astype(jnp.bfloat16)
    r = jax.nn.sigmoid(jnp.dot(x16, wa_ref[...].astype(jnp.bfloat16), preferred_element_type=jnp.float32) + ba_ref[...])
    ig = jax.nn.sigmoid(jnp.dot(x16, wx_ref[...].astype(jnp.bfloat16), preferred_element_type=jnp.float32) + bx_ref[...])
    log_a = r * lam_ref[...]
    a = jnp.exp(log_a)
    a_ref[...] = a
    b_ref[...] = jnp.sqrt(1.0 - jnp.exp(2.0 * log_a)) * (ig * xc)
    g = g_ref[...]
    gg_ref[...] = 0.5 * g * (1.0 + jnp.tanh(math.sqrt(2.0 / math.pi) * (g + 0.044715 * (g * g * g))))


def _b_gates(u, b_par, bsz, t):
    conv_w, conv_b, wa, ba, wx, bx, lam = b_par
    lam_c = (-RG_C * jax.nn.softplus(-lam)).reshape(1, B_WIDTH)
    p = bsz * t
    g0 = (A_Q + 2 * A_KV) // B_BLOCK_DIM
    x0 = g0 + B_BLOCKS
    vec = lambda b, n: (0, n)
    blk = pl.BlockSpec((t, B_BLOCK_DIM), lambda b, n: (b, n))
    wspec = pl.BlockSpec((None, B_BLOCK_DIM, B_BLOCK_DIM), lambda b, n: (n, 0, 0))
    return pl.pallas_call(
        _b_gates_kernel,
        out_shape=(jax.ShapeDtypeStruct((p, B_WIDTH), jnp.float32),) * 3,
        grid=(bsz, B_BLOCKS),
        in_specs=[pl.BlockSpec((t, B_BLOCK_DIM), lambda b, n: (b, g0 + n)),
                  pl.BlockSpec((t, B_BLOCK_DIM), lambda b, n: (b, x0 + n)),
                  pl.BlockSpec((CONV_W, B_BLOCK_DIM), vec), pl.BlockSpec((1, B_BLOCK_DIM), vec),
                  wspec, pl.BlockSpec((1, B_BLOCK_DIM), vec), wspec, pl.BlockSpec((1, B_BLOCK_DIM), vec),
                  pl.BlockSpec((1, B_BLOCK_DIM), vec)],
        out_specs=(blk,) * 3,
        compiler_params=_params(("parallel", "parallel")),
        name="b_gates",
    )(u, u, conv_w, conv_b.reshape(1, B_WIDTH), wa, ba.reshape(1, B_WIDTH), wx, bx.reshape(1, B_WIDTH), lam_c)


def _b_scan_kernel(a_ref, b_ref, gg_ref, y_ref, hl_ref, h_scr):
    c = pl.program_id(0)

    @pl.when(c == 0)
    def _():
        h_scr[...] = jnp.zeros_like(h_scr)

    def body(t, h):
        h = a_ref[:, t] * h + b_ref[:, t]
        y_ref[:, t] = (h * gg_ref[:, t]).astype(y_ref.dtype)
        return h

    h = lax.fori_loop(0, a_ref.shape[1], body, h_scr[...], unroll=8)
    h_scr[...] = h
    hl_ref[...] = h


def _b_scan(a, b, gg, bsz, t, out_dtype):
    shp = (bsz, t, B_BLOCKS, B_BLOCK_DIM)
    blk = pl.BlockSpec((bsz, B_SCAN_T, B_BLOCKS, B_BLOCK_DIM), lambda c: (0, c, 0, 0))
    y, h_last = pl.pallas_call(
        _b_scan_kernel,
        out_shape=(jax.ShapeDtypeStruct(shp, out_dtype), jax.ShapeDtypeStruct((bsz, B_BLOCKS, B_BLOCK_DIM), jnp.float32)),
        grid=(t // B_SCAN_T,),
        in_specs=[blk] * 3,
        out_specs=(blk, pl.BlockSpec((bsz, B_BLOCKS, B_BLOCK_DIM), lambda c: (0, 0, 0))),
        scratch_shapes=[pltpu.VMEM((bsz, B_BLOCKS, B_BLOCK_DIM), jnp.float32)],
        compiler_params=_params(("arbitrary",)),
        name="b_scan",
    )(a.reshape(shp), b.reshape(shp), gg.reshape(shp))
    return y.reshape(bsz * t, B_WIDTH), h_last.reshape(bsz, B_WIDTH)


GDN_CHUNK = 128
GDN_GATE_ROWS = 512
GDN_HEADS_PER_STEP = 2
GDN_INV_PASSES = 3
DB_LANE = 3 * C_HEADS
DA_LANE = DB_LANE + D_HEADS


def _softplus(x):
    return jnp.maximum(x, 0.0) + jnp.log1p(jnp.exp(-jnp.abs(x)))


def _d_gates_kernel(us_ref, dtb_ref, aneg_ref, tri_ref, beta_ref, gam_ref):
    tm = us_ref.shape[0]
    us = us_ref[...]
    g = aneg_ref[...] * _softplus(us + dtb_ref[...])
    gam = jnp.dot(tri_ref[...], g, preferred_element_type=jnp.float32, precision=HI)
    beta = jax.nn.sigmoid(us)
    for h in range(D_HEADS):
        sl = slice(h * HEAD_DIM, (h + 1) * HEAD_DIM)
        beta_ref[:, sl] = jnp.broadcast_to(beta[:, DB_LANE + h:DB_LANE + h + 1], (tm, HEAD_DIM))
        gam_ref[:, sl] = jnp.broadcast_to(gam[:, DA_LANE + h:DA_LANE + h + 1], (tm, HEAD_DIM))


def _d_gates(u_s, a_log, dt_bias):
    p = u_s.shape[0]
    tm = GDN_GATE_ROWS
    pad = lambda x: jnp.zeros((1, LANES), jnp.float32).at[0, DA_LANE:DA_LANE + D_HEADS].set(x)
    r = np.arange(tm)
    tri = ((r[:, None] >= r[None, :]) & (r[:, None] // GDN_CHUNK == r[None, :] // GDN_CHUNK)).astype(np.float32)
    fixed = lambda i: (0, 0)
    return pl.pallas_call(
        _d_gates_kernel,
        out_shape=(jax.ShapeDtypeStruct((p, D_V), jnp.float32),) * 2,
        grid=(p // tm,),
        in_specs=[pl.BlockSpec((tm, LANES), lambda i: (i, 0)), pl.BlockSpec((1, LANES), fixed),
                  pl.BlockSpec((1, LANES), fixed), pl.BlockSpec((tm, tm), fixed)],
        out_specs=(pl.BlockSpec((tm, D_V), lambda i: (i, 0)),) * 2,
        compiler_params=_params(("parallel",)),
        name="d_gates",
    )(u_s, pad(dt_bias), pad(-jnp.exp(a_log)), jnp.asarray(tri))


def _mm(a, b, passes=1):
    ah, bh = a.astype(jnp.bfloat16), b.astype(jnp.bfloat16)
    out = jnp.dot(ah, bh, preferred_element_type=jnp.float32)
    if passes == 3:
        al = (a - ah.astype(jnp.float32)).astype(jnp.bfloat16)
        bl = (b - bh.astype(jnp.float32)).astype(jnp.bfloat16)
        out = out + jnp.dot(ah, bl, preferred_element_type=jnp.float32) + jnp.dot(al, bh, preferred_element_type=jnp.float32)
    return out


def _mm_nt(a, b):
    return lax.dot_general(a.astype(jnp.bfloat16), b.astype(jnp.bfloat16), (((1,), (1,)), ((), ())),
                           preferred_element_type=jnp.float32)


def _d_delta_kernel(q_in, k_in, v_in, z_ref, beta_ref, gam_ref, cwq_ref, cwk_ref, cwv_ref, og_ref,
                    y_ref, s_ref, q_scr, k_scr, v_scr):
    t = q_in.shape[0]
    c = GDN_CHUNK

    def conv_silu(x_ref, cw_ref):
        x = x_ref[...]
        xx = jnp.concatenate([jnp.zeros((8, x.shape[1]), x.dtype), x], axis=0)
        acc = x * cw_ref[CONV_W - 1:CONV_W, :]
        for j in range(CONV_W - 1):
            acc = acc + pltpu.roll(xx, CONV_W - 1 - j, axis=0)[8:8 + t] * cw_ref[j:j + 1, :]
        return acc * jax.nn.sigmoid(acc)

    qa = conv_silu(q_in, cwq_ref)
    ka = conv_silu(k_in, cwk_ref)
    v_scr[...] = conv_silu(v_in, cwv_ref)
    for hh in range(GDN_HEADS_PER_STEP):
        sl = slice(hh * HEAD_DIM, (hh + 1) * HEAD_DIM)
        qh, kh = qa[:, sl], ka[:, sl]
        q_scr[:, sl] = qh * lax.rsqrt(jnp.sum(qh * qh, axis=-1, keepdims=True) + EPS) * D_DK ** -0.5
        k_scr[:, sl] = kh * lax.rsqrt(jnp.sum(kh * kh, axis=-1, keepdims=True) + EPS)
    s_ref[...] = jnp.zeros_like(s_ref)

    ri = lax.broadcasted_iota(jnp.int32, (c, c), 0)
    ci = lax.broadcasted_iota(jnp.int32, (c, c), 1)
    lower, strict = ri >= ci, ri > ci
    eye = (ri == ci).astype(jnp.float32)

    def chunk(n, carry):
        r0 = pl.multiple_of(n * c, c)
        rows = pl.ds(r0, c)
        heads = range(GDN_HEADS_PER_STEP)
        sls = [slice(hh * HEAD_DIM, (hh + 1) * HEAD_DIM) for hh in heads]
        q = [q_scr[rows, sl] for sl in sls]
        k = [k_scr[rows, sl] for sl in sls]
        v = [v_scr[rows, sl] for sl in sls]
        beta = [beta_ref[rows, sl] for sl in sls]
        gam = [gam_ref[rows, sl] for sl in sls]
        z = [z_ref[rows, sl] for sl in sls]
        state = [s_ref[0, hh] for hh in heads]
        decay = [jnp.where(lower, jnp.exp(jnp.minimum(g - jnp.transpose(g), 0.0)), 0.0) for g in gam]
        kk = [_mm_nt(x, x) for x in k]
        npow = [-jnp.where(strict, b * a * d, 0.0) for b, a, d in zip(beta, kk, decay)]
        inv = [eye + x for x in npow]
        for _ in range(c.bit_length() - 2):
            npow = [_mm(x, x, GDN_INV_PASSES) for x in npow]
            inv = [i + _mm(i, x, GDN_INV_PASSES) for i, x in zip(inv, npow)]
        e_gam = [jnp.exp(g) for g in gam]
        u = [_mm(i, x * b, GDN_INV_PASSES) for i, x, b in zip(inv, v, beta)]
        w = [_mm(i, x * (b * e), GDN_INV_PASSES) for i, x, b, e in zip(inv, k, beta, e_gam)]
        qk = [jnp.where(lower, _mm_nt(a, b) * d, 0.0) for a, b, d in zip(q, k, decay)]
        v_new = [a - _mm(b, s) for a, b, s in zip(u, w, state)]
        o = [_mm(a * e, s) + _mm(b, x) for a, e, s, b, x in zip(q, e_gam, state, qk, v_new)]
        gam_last = [g[c - 1:c, :] for g in gam]
        k_dec_t = [jnp.transpose(x * jnp.exp(gl - g)) for x, gl, g in zip(k, gam_last, gam)]
        new_state = [jnp.exp(gl) * s + _mm(kt, x) for gl, s, kt, x in zip(gam_last, state, k_dec_t, v_new)]
        for hh in heads:
            s_ref[0, hh] = new_state[hh]
            on = o[hh] * lax.rsqrt(jnp.mean(o[hh] * o[hh], axis=-1, keepdims=True) + EPS) * og_ref[...]
            y_ref[rows, sls[hh]] = (on * (z[hh] * jax.nn.sigmoid(z[hh]))).astype(y_ref.dtype)
        return carry

    lax.fori_loop(0, t // c, chunk, 0)


def _d_delta(u_d, beta_r, gam_r, conv_w, out_gain, bsz, t, out_dtype):
    hp = GDN_HEADS_PER_STEP
    w = hp * HEAD_DIM
    nb = D_V // w
    blk = lambda sec: pl.BlockSpec((t, w), lambda b, h: (b, sec * nb + h))
    cw = lambda sec: pl.BlockSpec((CONV_W, w), lambda b, h: (0, sec * nb + h))
    return pl.pallas_call(
        _d_delta_kernel,
        out_shape=(jax.ShapeDtypeStruct((bsz * t, D_V), out_dtype),
                   jax.ShapeDtypeStruct((bsz, D_HEADS, D_DK, D_DV), jnp.float32)),
        grid=(bsz, D_HEADS // hp),
        in_specs=[blk(0), blk(1), blk(2), blk(3), blk(0), blk(0), cw(0), cw(1), cw(2),
                  pl.BlockSpec((1, HEAD_DIM), lambda b, h: (0, 0))],
        out_specs=(blk(0), pl.BlockSpec((1, hp, D_DK, D_DV), lambda b, h: (b, h, 0, 0))),
        scratch_shapes=[pltpu.VMEM((t, w), jnp.float32)] * 3,
        compiler_params=_params(("parallel", "parallel")),
        name="d_delta",
    )(u_d, u_d, u_d, u_d, beta_r, gam_r, conv_w, conv_w, conv_w, out_gain.reshape(1, HEAD_DIM))


def _dot_hi(a, b):
    return jnp.dot(a, b, preferred_element_type=jnp.float32, precision=HI)


def _dot_hi_nt(a, b):
    return lax.dot_general(a, b, (((1,), (1,)), ((), ())), preferred_element_type=jnp.float32, precision=HI)


def _gelu_tanh(g):
    return 0.5 * g * (1.0 + jnp.tanh(math.sqrt(2.0 / math.pi) * (g + 0.044715 * (g * g * g))))


def _rows_to_block(rows, n_rows):
    ri = lax.broadcasted_iota(jnp.int32, (n_rows, 1), 0)
    out = jnp.zeros((n_rows, rows[0].shape[1]), jnp.float32)
    for i, r in enumerate(rows):
        out = jnp.where(ri == i, r, out)
    return out


def _a_sample_kernel(q0_ref, q1_ref, q2_ref, kn_ref, vn_ref, kc_ref, vc_ref, qg_ref, kg_ref, o_ref, ko_ref):
    h = pl.program_id(0)
    n_ctx = kc_ref.shape[0]
    qs = [_chunk_rms(r[...], qg_ref[...]) for r in (q0_ref, q1_ref, q2_ref)]
    k_new = _chunk_rms(kn_ref[...], kg_ref[...])
    v_new = vn_ref[...]
    ko_ref[...] = k_new
    q = _rows_to_block(qs, 8)
    gi = lax.broadcasted_iota(jnp.int32, (8, 1), 0)
    live = gi < A_N_GROUPS
    dil = jnp.where(gi == 0, A_PATTERNS[0][1], jnp.where(gi == 1, A_PATTERNS[1][1], A_PATTERNS[2][1]))
    win = jnp.where(gi == 0, A_PATTERNS[0][0], jnp.where(gi == 1, A_PATTERNS[1][0], A_PATTERNS[2][0]))
    slope = jnp.exp2((-8.0 / (A_N_GROUPS * A_HEADS)) * (gi * A_HEADS + h + 1).astype(jnp.float32))
    dist = n_ctx - lax.broadcasted_iota(jnp.int32, (1, n_ctx), 1)
    ok = live & ((dist & (dil - 1)) == 0) & (dist <= win)
    scale = HEAD_DIM ** -0.5
    s = jnp.where(ok, _dot_hi_nt(q, kc_ref[...]) * scale - slope * dist.astype(jnp.float32), NEG)
    s_new = jnp.where(live, jnp.sum(q * k_new, axis=-1, keepdims=True) * scale, NEG)
    m = jnp.max(jnp.maximum(jnp.max(s, axis=-1, keepdims=True), s_new), axis=0, keepdims=True)
    p = jnp.where(ok, jnp.exp(s - m), 0.0)
    p_new = jnp.sum(jnp.where(live, jnp.exp(s_new - m), 0.0), axis=0, keepdims=True)
    l = jnp.sum(jnp.sum(p, axis=-1, keepdims=True), axis=0, keepdims=True) + p_new
    o = _dot_hi(jnp.sum(p, axis=0, keepdims=True), vc_ref[...]) + p_new * v_new
    o_ref[...] = o / l


def _a_sample(us, cache, layer, q_gain, k_gain):
    bd = us.shape[0]
    n_ctx = cache.shape[2]
    cview = cache.reshape(cache.shape[0], bd, n_ctx, 2 * A_KV)
    us3 = us.reshape(bd, 1, -1)
    col = lambda c: pl.BlockSpec((None, 1, HEAD_DIM), lambda h, b, c=c: (b, 0, c + h))
    fixed = pl.BlockSpec((1, HEAD_DIM), lambda h, b: (0, 0))
    o, k_new = pl.pallas_call(
        _a_sample_kernel,
        out_shape=(jax.ShapeDtypeStruct((bd, 1, A_OUT), jnp.float32), jax.ShapeDtypeStruct((bd, 1, A_KV), jnp.float32)),
        grid=(A_HEADS, bd),
        in_specs=[col(0), col(A_HEADS), col(2 * A_HEADS), col(3 * A_HEADS), col(4 * A_HEADS),
                  pl.BlockSpec((None, None, n_ctx, HEAD_DIM), lambda h, b: (layer, b, 0, h)),
                  pl.BlockSpec((None, None, n_ctx, HEAD_DIM), lambda h, b: (layer, b, 0, A_HEADS + h)),
                  fixed, fixed],
        out_specs=(col(0), col(0)),
        compiler_params=_params(("parallel", "parallel")),
        name="a_sample",
    )(us3, us3, us3, us3, us3, cview, cview, q_gain.reshape(1, HEAD_DIM), k_gain.reshape(1, HEAD_DIM))
    return o.reshape(bd, A_OUT), k_new.reshape(bd, A_KV)


def _b_sample_kernel(g_ref, x_ref, buf_ref, h0_ref, cw_ref, cb_ref, wa_ref, ba_ref, wx_ref, bx_ref, lam_ref, y_ref, h_ref):
    x = x_ref[...]
    xc = x * cw_ref[CONV_W - 1:CONV_W, :] + cb_ref[...]
    for j in range(CONV_W - 1):
        xc = xc + buf_ref[j] * cw_ref[j:j + 1, :]
    r = jax.nn.sigmoid(_dot_hi(xc, wa_ref[...]) + ba_ref[...])
    ig = jax.nn.sigmoid(_dot_hi(xc, wx_ref[...]) + bx_ref[...])
    log_a = r * lam_ref[...]
    h = jnp.exp(log_a) * h0_ref[...] + jnp.sqrt(1.0 - jnp.exp(2.0 * log_a)) * (ig * xc)
    h_ref[...] = h
    y_ref[...] = h * _gelu_tanh(g_ref[...])


def _b_sample(us, conv_buf, h0, b_par):
    conv_w, conv_b, wa, ba, wx, bx, lam = b_par
    bd = us.shape[0]
    lam_c = (-RG_C * jax.nn.softplus(-lam)).reshape(1, B_WIDTH)
    g0 = (A_Q + 2 * A_KV) // B_BLOCK_DIM
    vec = lambda n: (0, n)
    blk = pl.BlockSpec((bd, B_BLOCK_DIM), vec)
    one = pl.BlockSpec((1, B_BLOCK_DIM), vec)
    wspec = pl.BlockSpec((None, B_BLOCK_DIM, B_BLOCK_DIM), lambda n: (n, 0, 0))
    return pl.pallas_call(
        _b_sample_kernel,
        out_shape=(jax.ShapeDtypeStruct((bd, B_WIDTH), jnp.float32),) * 2,
        grid=(B_BLOCKS,),
        in_specs=[pl.BlockSpec((bd, B_BLOCK_DIM), lambda n: (0, g0 + n)),
                  pl.BlockSpec((bd, B_BLOCK_DIM), lambda n: (0, g0 + B_BLOCKS + n)),
                  pl.BlockSpec((CONV_W - 1, bd, B_BLOCK_DIM), lambda n: (0, 0, n)), blk,
                  pl.BlockSpec((CONV_W, B_BLOCK_DIM), vec), one, wspec, one, wspec, one, one],
        out_specs=(blk, blk),
        compiler_params=_params(("parallel",)),
        name="b_sample",
    )(us, us, jnp.swapaxes(conv_buf, 0, 1), h0, conv_w, conv_b.reshape(1, B_WIDTH), wa, ba.reshape(1, B_WIDTH),
      wx, bx.reshape(1, B_WIDTH), lam_c)


def _d_sample_kernel(q_in, k_in, v_in, z_ref, bq_ref, bk_ref, bv_ref, us_ref, dtb_ref, aneg_ref,
                     cwq_ref, cwk_ref, cwv_ref, og_ref, s0_ref, y_ref, s_ref, o_scr):
    h = pl.program_id(0)
    bd = q_in.shape[0]

    def conv_silu(x_ref, b_ref, cw_ref):
        acc = x_ref[...] * cw_ref[CONV_W - 1:CONV_W, :]
        for j in range(CONV_W - 1):
            acc = acc + b_ref[j] * cw_ref[j:j + 1, :]
        return acc * jax.nn.sigmoid(acc)

    q = conv_silu(q_in, bq_ref, cwq_ref)
    k = conv_silu(k_in, bk_ref, cwk_ref)
    v = conv_silu(v_in, bv_ref, cwv_ref)
    q = q * lax.rsqrt(jnp.sum(q * q, axis=-1, keepdims=True) + EPS) * D_DK ** -0.5
    k = k * lax.rsqrt(jnp.sum(k * k, axis=-1, keepdims=True) + EPS)
    us = us_ref[...]
    lane = lax.broadcasted_iota(jnp.int32, (1, LANES), 1)
    beta = jnp.sum(jnp.where(lane == DB_LANE + h, jax.nn.sigmoid(us), 0.0), axis=-1, keepdims=True)
    g = jnp.sum(jnp.where(lane == DA_LANE + h, aneg_ref[...] * _softplus(us + dtb_ref[...]), 0.0), axis=-1, keepdims=True)
    a = jnp.exp(g)
    k_t = jnp.transpose(k)
    q_t = jnp.transpose(q)
    for b in range(bd):
        rb = slice(b, b + 1)
        state = s0_ref[b]
        kcol, qcol = k_t[:, rb], q_t[:, rb]
        k_s = jnp.sum(kcol * state, axis=0, keepdims=True)
        v_new = beta[rb] * (v[rb] - a[rb] * k_s)
        state = a[rb] * state + kcol * v_new
        s_ref[b] = state
        o_scr[rb, :] = jnp.sum(qcol * state, axis=0, keepdims=True)
    o = o_scr[...]
    o = o * lax.rsqrt(jnp.mean(o * o, axis=-1, keepdims=True) + EPS) * og_ref[...]
    z = z_ref[...]
    y_ref[...] = o * (z * jax.nn.sigmoid(z))


def _d_sample(u_d, u_s, conv_buf, s0, layer, d_par):
    conv_w, a_log, dt_bias, out_gain = d_par
    bd = u_d.shape[0]
    pad = lambda x: jnp.zeros((1, LANES), jnp.float32).at[0, DA_LANE:DA_LANE + D_HEADS].set(x)
    sec = lambda s: pl.BlockSpec((bd, HEAD_DIM), lambda h, s=s: (0, s * D_HEADS + h))
    bsec = lambda s: pl.BlockSpec((CONV_W - 1, bd, HEAD_DIM), lambda h, s=s: (0, 0, s * D_HEADS + h))
    cw = lambda s: pl.BlockSpec((CONV_W, HEAD_DIM), lambda h, s=s: (0, s * D_HEADS + h))
    one = pl.BlockSpec((1, LANES), lambda h: (0, 0))
    buf_t = jnp.swapaxes(conv_buf, 0, 1)
    return pl.pallas_call(
        _d_sample_kernel,
        out_shape=(jax.ShapeDtypeStruct((bd, D_V), jnp.float32),
                   jax.ShapeDtypeStruct((bd, D_HEADS, D_DK, D_DV), jnp.float32)),
        grid=(D_HEADS,),
        in_specs=[sec(0), sec(1), sec(2), sec(3), bsec(0), bsec(1), bsec(2),
                  pl.BlockSpec((bd, LANES), lambda h: (0, 0)), one, one, cw(0), cw(1), cw(2), one,
                  pl.BlockSpec((None, bd, None, D_DK, D_DV), lambda h: (layer, 0, h, 0, 0))],
        out_specs=(sec(0), pl.BlockSpec((bd, None, D_DK, D_DV), lambda h: (0, h, 0, 0))),
        scratch_shapes=[pltpu.VMEM((bd, HEAD_DIM), jnp.float32)],
        compiler_params=_params(("parallel",)),
        name="d_sample",
    )(u_d, u_d, u_d, u_d, buf_t, buf_t, buf_t, u_s, pad(dt_bias), pad(-jnp.exp(a_log)),
      conv_w, conv_w, conv_w, out_gain.reshape(1, HEAD_DIM), s0)


NSA_S_PAGES = 4
NSA_S_LANES = 384


def _nsa_s_sum_kernel(pt_ref, *refs):
    pages, w_ref, o_ref = refs[:NSA_S_PAGES], refs[NSA_S_PAGES], refs[NSA_S_PAGES + 1]
    per_page = PAGE_SIZE // C_BLOCK
    for i, page in enumerate(pages):
        x = page[...] * w_ref[...]
        for j in range(per_page):
            row = i * per_page + j
            r = jnp.sum(x[j * C_BLOCK:(j + 1) * C_BLOCK], axis=0)
            for kv in range(2):
                for hh in range(C_KV_HEADS):
                    c0 = (kv * C_KV_HEADS + hh) * HEAD_DIM
                    o_ref[row:row + 1, c0:c0 + HEAD_DIM] = r[kv, hh:hh + 1, :]


def _nsa_s_summaries(pool, layer, page_table, phi_k, phi_v):
    bd, n_pages = page_table.shape
    reps = PAGE_SIZE // C_BLOCK
    page_shape = (PAGE_SIZE, 2, C_KV_HEADS, HEAD_DIM)
    w = jnp.broadcast_to(jnp.stack([jnp.tile(phi_k, reps), jnp.tile(phi_v, reps)], axis=1)[:, :, None, None], page_shape)
    steps = n_pages // NSA_S_PAGES
    rows = NSA_S_PAGES * reps
    page_spec = lambda i: pl.BlockSpec((None, None) + page_shape,
                                       lambda b, s, pt, i=i: (layer, pt[b, s * NSA_S_PAGES + i], 0, 0, 0, 0))
    return pl.pallas_call(
        _nsa_s_sum_kernel,
        out_shape=jax.ShapeDtypeStruct((bd, steps * rows, 2 * C_KV), jnp.float32),
        grid_spec=pltpu.PrefetchScalarGridSpec(
            num_scalar_prefetch=1, grid=(bd, steps),
            in_specs=[page_spec(i) for i in range(NSA_S_PAGES)]
            + [pl.BlockSpec(page_shape, lambda b, s, pt: (0, 0, 0, 0))],
            out_specs=pl.BlockSpec((None, rows, 2 * C_KV), lambda b, s, pt: (b, s, 0))),
        compiler_params=_params(("parallel", "arbitrary")),
        name="nsa_s_summaries",
    )(page_table, *([pool] * NSA_S_PAGES), w)


def _nsa_s_query(q_ref, h):
    q = _rows_to_block([q_ref[:, g * HEAD_DIM:(g + 1) * HEAD_DIM] for g in range(C_GROUP)], 8)
    gi = lax.broadcasted_iota(jnp.int32, (8, 1), 0)
    slope = jnp.exp2(-0.5 * (h * C_GROUP + gi + 1).astype(jnp.float32))
    return q, gi, slope


def _nsa_s_gate(gate_ref, gi, br):
    gt = gate_ref[...]
    out = jnp.zeros((8, 1), jnp.float32)
    for g in range(C_GROUP):
        out = jnp.where(gi == g, gt[:, br * C_GROUP + g: br * C_GROUP + g + 1], out)
    return out


def _nsa_s_cmpwin_kernel(q_ref, kc_ref, vc_ref, wk_ref, wv_ref, wkn_ref, wvn_ref, gate_ref, part_ref, sel_ref, *, past):
    h = pl.program_id(1)
    q, gi, slope = _nsa_s_query(q_ref, h)
    live = gi < C_GROUP
    scale = HEAD_DIM ** -0.5
    nbc = kc_ref.shape[0]
    nidx = lax.broadcasted_iota(jnp.int32, (1, nbc), 1)
    centre = nidx.astype(jnp.float32) * C_BLOCK + (C_BLOCK - 1) / 2.0
    cmask = (nidx + 1) * C_BLOCK - 1 <= past
    s = jnp.where(cmask, _dot_hi_nt(q, kc_ref[...]) * scale - slope * jnp.abs(past - centre), NEG)
    m = jnp.max(s, axis=-1, keepdims=True)
    p = jnp.where(cmask, jnp.exp(s - m), 0.0)
    p = p / jnp.maximum(jnp.sum(p, axis=-1, keepdims=True), 1e-30)
    o_cmp = _dot_hi(p, vc_ref[...])
    imp = jnp.sum(jnp.where(live, p, 0.0), axis=0, keepdims=True)
    n_lanes = NSA_S_LANES
    n_blocks = -(-(past + 1) // C_BLOCK)
    cur = past // C_BLOCK
    lane = lax.broadcasted_iota(jnp.int32, (1, n_lanes), 1)
    imp = jnp.concatenate([imp, jnp.zeros((1, n_lanes - nbc), jnp.float32)], axis=1)
    forced = (lane == 0) | (lane == cur) | (lane == cur - 1)
    causal = (lane <= cur) & (lane < n_blocks)
    score = jnp.where(causal, imp + jnp.where(forced, FORCE, 0.0), NEG)
    score_col = jnp.transpose(jnp.broadcast_to(score, (8, n_lanes)))[:, 0:1]
    mi = lax.broadcasted_iota(jnp.int32, (n_lanes, 1), 0)
    ahead = (score_col > score) | ((score_col == score) & (mi < lane))
    rank = jnp.sum(ahead.astype(jnp.float32), axis=0, keepdims=True)
    sel = (rank < min(C_N_SEL, n_blocks)) & causal
    sel_ref[...] = jnp.broadcast_to(sel.astype(jnp.float32), (8, n_lanes))
    n_win = wk_ref.shape[0]
    dist = n_win - lax.broadcasted_iota(jnp.int32, (1, n_win), 1)
    wmask = dist <= C_WIN
    sw = jnp.where(wmask, _dot_hi_nt(q, wk_ref[...]) * scale - slope * dist.astype(jnp.float32), NEG)
    sw_new = jnp.sum(q * wkn_ref[...], axis=-1, keepdims=True) * scale
    mw = jnp.maximum(jnp.max(sw, axis=-1, keepdims=True), sw_new)
    pw = jnp.where(wmask, jnp.exp(sw - mw), 0.0)
    pw_new = jnp.exp(sw_new - mw)
    o_win = (_dot_hi(pw, wv_ref[...]) + pw_new * wvn_ref[...]) / (jnp.sum(pw, axis=-1, keepdims=True) + pw_new)
    part_ref[...] = _nsa_s_gate(gate_ref, gi, 0) * o_cmp + _nsa_s_gate(gate_ref, gi, 2) * o_win


def _nsa_s_cmpwin(qn, kvc, win_buf, layer, win_new, gates, past):
    bd = qn.shape[0]
    nbc = kvc.shape[1]
    n_win = win_buf.shape[2]
    wview = win_buf.reshape(win_buf.shape[0], bd, n_win, 2 * C_KV)
    rows = lambda c: pl.BlockSpec((None, 1, HEAD_DIM), lambda b, h, c=c: (b, 0, c + h))
    qn, win_new, gates = qn.reshape(bd, 1, -1), win_new.reshape(bd, 1, -1), gates.reshape(bd, 1, -1)
    return pl.pallas_call(
        functools.partial(_nsa_s_cmpwin_kernel, past=past),
        out_shape=(jax.ShapeDtypeStruct((bd, C_KV_HEADS, 8, HEAD_DIM), jnp.float32),
                   jax.ShapeDtypeStruct((bd, C_KV_HEADS, 8, NSA_S_LANES), jnp.float32)),
        grid=(bd, C_KV_HEADS),
        in_specs=[pl.BlockSpec((None, 1, C_GROUP * HEAD_DIM), lambda b, h: (b, 0, h)),
                  pl.BlockSpec((None, nbc, HEAD_DIM), lambda b, h: (b, 0, h)),
                  pl.BlockSpec((None, nbc, HEAD_DIM), lambda b, h: (b, 0, C_KV_HEADS + h)),
                  pl.BlockSpec((None, None, n_win, HEAD_DIM), lambda b, h: (layer, b, 0, h)),
                  pl.BlockSpec((None, None, n_win, HEAD_DIM), lambda b, h: (layer, b, 0, C_KV_HEADS + h)),
                  rows(0), rows(C_KV_HEADS), rows(0)],
        out_specs=(pl.BlockSpec((None, None, 8, HEAD_DIM), lambda b, h: (b, h, 0, 0)),
                   pl.BlockSpec((None, None, 8, NSA_S_LANES), lambda b, h: (b, h, 0, 0))),
        compiler_params=_params(("parallel", "parallel")),
        name="nsa_s_cmpwin",
    )(qn, kvc, kvc, wview, wview, win_new, win_new, gates)


def _nsa_s_sel_kernel(hp_ref, blk_ref, q_ref, kv_ref, kn_ref, vn_ref, gate_ref, part_ref, o_ref,
                      m_scr, l_scr, acc_scr, *, past):
    b = pl.program_id(0)
    h = pl.program_id(1)
    j = pl.program_id(2)
    scale = HEAD_DIM ** -0.5
    qs = [q_ref[:, g * HEAD_DIM:(g + 1) * HEAD_DIM] for g in range(C_GROUP)]

    @pl.when(j == 0)
    def _():
        m_scr[...] = jnp.zeros_like(m_scr)
        l_scr[...] = jnp.ones_like(l_scr)
        acc_scr[...] = jnp.zeros_like(acc_scr)
        for g in range(C_GROUP):
            s_new = jnp.sum(qs[g] * kn_ref[...], axis=-1, keepdims=True) * scale
            m_scr[g:g + 1, :] = jnp.broadcast_to(s_new, (1, HEAD_DIM))
            acc_scr[g:g + 1, :] = vn_ref[...]

    blk = blk_ref[(b * C_KV_HEADS + h) * pl.num_programs(2) + j]
    mine = lax.broadcasted_iota(jnp.int32, (1, C_KV_HEADS, HEAD_DIM), 1) == h
    k3 = jnp.where(mine, kv_ref[:, 0], 0.0)
    v3 = jnp.where(mine, kv_ref[:, 1], 0.0)
    dist = (past - blk * C_BLOCK - lax.broadcasted_iota(jnp.int32, (C_BLOCK, 1, HEAD_DIM), 0)).astype(jnp.float32)
    for g in range(C_GROUP):
        slope = jnp.exp2(-0.5 * (jnp.zeros((1, 1, HEAD_DIM), jnp.float32) + (h * C_GROUP + g + 1).astype(jnp.float32)))
        qk = jnp.sum(k3 * qs[g][None], axis=1, keepdims=True)
        s = jnp.sum(qk, axis=-1, keepdims=True) * scale - slope * dist
        m_old = m_scr[g:g + 1, :][None]
        m_new = jnp.maximum(m_old, jnp.max(s, axis=0, keepdims=True))
        p = jnp.exp(s - m_new)
        alpha = jnp.exp(m_old - m_new)
        l_scr[g:g + 1, :] = (alpha * l_scr[g:g + 1, :][None] + jnp.sum(p, axis=0, keepdims=True))[0]
        pv = jnp.sum(jnp.sum(p * v3, axis=0), axis=0, keepdims=True)
        acc_scr[g:g + 1, :] = alpha[0] * acc_scr[g:g + 1, :] + pv
        m_scr[g:g + 1, :] = m_new[0]

    @pl.when(j == pl.num_programs(2) - 1)
    def _():
        gi = lax.broadcasted_iota(jnp.int32, (8, 1), 0)
        o_ref[...] = part_ref[...] + _nsa_s_gate(gate_ref, gi, 1) * (acc_scr[...] / l_scr[...])


def _nsa_s_select(qn, pool, layer, half_pages, blocks, sel_new, gates, part, past):
    bd = qn.shape[0]
    n_sel = blocks.shape[-1]
    n_phys = pool.shape[1]
    halves = PAGE_SIZE // C_BLOCK
    half_shape = (C_BLOCK, 2, C_KV_HEADS, HEAD_DIM)
    pview = pool.reshape((pool.shape[0], n_phys * halves) + half_shape)
    rows = lambda c: pl.BlockSpec((None, 1, HEAD_DIM), lambda b, h, j, hp, bl, c=c: (b, 0, c + h))
    qn, sel_new, gates = qn.reshape(bd, 1, -1), sel_new.reshape(bd, 1, -1), gates.reshape(bd, 1, -1)
    flat = lambda b, h, j: (b * C_KV_HEADS + h) * n_sel + j
    blk4 = pl.BlockSpec((None, None, 8, HEAD_DIM), lambda b, h, j, hp, bl: (b, h, 0, 0))
    return pl.pallas_call(
        functools.partial(_nsa_s_sel_kernel, past=past),
        out_shape=jax.ShapeDtypeStruct((bd, C_KV_HEADS, 8, HEAD_DIM), jnp.float32),
        grid_spec=pltpu.PrefetchScalarGridSpec(
            num_scalar_prefetch=2, grid=(bd, C_KV_HEADS, n_sel),
            in_specs=[pl.BlockSpec((None, 1, C_GROUP * HEAD_DIM), lambda b, h, j, hp, bl: (b, 0, h)),
                      pl.BlockSpec((None, None) + half_shape,
                                   lambda b, h, j, hp, bl: (layer, hp[flat(b, h, j)], 0, 0, 0, 0)),
                      rows(0), rows(C_KV_HEADS), rows(0), blk4],
            out_specs=blk4,
            scratch_shapes=[pltpu.VMEM((8, HEAD_DIM), jnp.float32), pltpu.VMEM((8, HEAD_DIM), jnp.float32),
                            pltpu.VMEM((8, HEAD_DIM), jnp.float32)]),
        compiler_params=_params(("parallel", "parallel", "arbitrary")),
        name="nsa_s_select",
    )(half_pages.reshape(-1), blocks.reshape(-1), qn, pview, sel_new, sel_new, gates, part)


def _nsa_sample(u_c, u_s, q_gain, k_gain, phi_k, phi_v, cmp_pool, sel_pool, win_buf, layer, page_table):
    bd = u_c.shape[0]
    past = page_table.shape[1] * PAGE_SIZE
    qn, cmp_new, sel_new, win_new, gates = _nsa_prep(u_c, u_s, q_gain, k_gain)
    kvc = _nsa_s_summaries(cmp_pool, layer, page_table, phi_k, phi_v)
    part, sel = _nsa_s_cmpwin(qn, kvc, win_buf, layer, win_new, gates, past)
    n_past_blocks = past // C_BLOCK
    n_sel = min(C_N_SEL, n_past_blocks + 1) - 1
    _, blocks = lax.top_k(sel[:, :, 0, :n_past_blocks], n_sel)
    blocks = blocks.astype(jnp.int32)
    halves = PAGE_SIZE // C_BLOCK
    pages = jnp.take_along_axis(page_table[:, None, :], blocks // halves, axis=2)
    half_pages = (pages * halves + blocks % halves).astype(jnp.int32)
    out = _nsa_s_select(qn, sel_pool, layer, half_pages, blocks, sel_new, gates, part, past)
    return out[:, :, :C_GROUP, :].reshape(bd, C_Q), cmp_new, sel_new, win_new


def _even_mixer(u2d, w_out3d, layer, res2d, q_gain, k_gain, b_par, kv_cache, conv_buf, h0, bsz, t):
    u_x = u2d.reshape(bsz, t, -1)[:, :, -B_WIDTH:]
    if kv_cache is None:
        qn, akv = _a_prep(u2d, q_gain, k_gain)
        a_out = _a_attn_prompt(qn, akv, bsz, t, jnp.bfloat16)
        a_state = akv.reshape(bsz, t, 2, A_HEADS, HEAD_DIM)[:, -min(A_WIN_MAX, t):]
        a_dec, b_in, gg = _b_gates(u2d, b_par, bsz, t)
        b_out, h_new = _b_scan(a_dec, b_in, gg, bsz, t, jnp.float32)
        conv_new = u_x[:, -(CONV_W - 1):]
    else:
        a_out, k_new = _a_sample(u2d, kv_cache, layer, q_gain, k_gain)
        a_state = jnp.stack([k_new, u2d[:, A_Q + A_KV:A_Q + 2 * A_KV]], axis=1).reshape(bsz, t, 2, A_HEADS, HEAD_DIM)
        b_out, h_new = _b_sample(u2d, conv_buf, h0, b_par)
        conv_new = jnp.concatenate([conv_buf[:, 1:], u_x], axis=1)
    out = _matmul([a_out, b_out], w_out3d, layer, 0, D_MODEL, kv_cache is not None, res=res2d)
    return out, (a_state, conv_new, h_new)


def _odd_mixer(u_parts, w_out3d, layer, res2d, q_gain, k_gain, phi_k, phi_v, d_par,
               cmp_pool, sel_pool, page_table, win_buf, d_conv_buf, d_s0, bsz, t):
    u_c2d, u_d2d, u_s2d = u_parts
    kv_shape = (bsz, t, 2, C_KV_HEADS, HEAD_DIM)
    d_in = u_d2d.reshape(bsz, t, -1)[:, :, :D_CONV]
    conv_w, a_log, dt_bias, out_gain = d_par
    if page_table is None:
        qn, cmp2d, sel2d, win2d, gates = _nsa_prep(u_c2d, u_s2d, q_gain, k_gain)
        kc, vc = _nsa_summaries(cmp2d, phi_k, phi_v)
        o_c = _nsa_prompt(qn, kc, vc, sel2d, win2d, gates, bsz, t, jnp.bfloat16)
        win_state = win2d.reshape(kv_shape)[:, -min(C_WIN, t):]
        beta_r, gam_r = _d_gates(u_s2d, a_log, dt_bias)
        d_out, d_s = _d_delta(u_d2d, beta_r, gam_r, conv_w, out_gain, bsz, t, jnp.bfloat16)
        d_conv_new = d_in[:, -(CONV_W - 1):]
    else:
        o_c, cmp2d, sel2d, win2d = _nsa_sample(u_c2d, u_s2d, q_gain, k_gain, phi_k, phi_v,
                                                cmp_pool, sel_pool, win_buf, layer, page_table)
        win_state = win2d.reshape(kv_shape)
        d_out, d_s = _d_sample(u_d2d, u_s2d, d_conv_buf, d_s0, layer, d_par)
        d_conv_new = jnp.concatenate([d_conv_buf[:, 1:], d_in], axis=1)
    out = _matmul([o_c, d_out], w_out3d, layer, 0, D_MODEL, page_table is not None, res=res2d)
    return out, (cmp2d.reshape(kv_shape), sel2d.reshape(kv_shape), win_state, d_conv_new, d_s)


ODD_C = C_Q + 6 * C_KV
ODD_G0 = ODD_C
ODD_D0 = ODD_G0 + 3 * C_HEADS
ODD_D = 2 * D_QK + 2 * D_V
ODD_S0 = ODD_D0 + ODD_D
ODD_IN = ODD_S0 + 2 * D_HEADS


def kernel(x_prompt, x_sample, cache_a_kv, state_b_conv, state_b_h, cache_c_cmp_kv, cache_c_sel_kv, cache_c_win_kv, state_d_conv, state_d_S, page_table, norm_mix, norm_ffn, even_w_in, even_w_out, a_q_norm, a_k_norm, b_conv_w, b_conv_b, b_gate_a_w, b_gate_a_b, b_gate_x_w, b_gate_x_b, b_lambda, odd_w_in, odd_w_out, c_q_norm, c_k_norm, c_phi_k, c_phi_v, d_conv_w, d_a_log, d_dt_bias, d_out_norm, moe_group_w, moe_group_b, moe_expert_w, moe_expert_b, moe_w1, moe_w3, moe_w2):
    bp, sp, d = x_prompt.shape
    bs, ss, _ = x_sample.shape
    depth = norm_mix.shape[0]
    hp = x_prompt.reshape(bp * sp, d)
    hs = x_sample.reshape(bs * ss, d)
    outs = {k: [] for k in ("ak", "bc", "bh", "cc", "cs", "cw", "dc", "ds")}
    outs_s = {k: [] for k in outs}
    for l in range(depth):
        i = l // 2
        xp = _rmsnorm(hp, norm_mix[l], jnp.bfloat16)
        xs = _rmsnorm(hs, norm_mix[l], jnp.float32)
        if l % 2 == 0:
            b_par = (b_conv_w[i], b_conv_b[i], b_gate_a_w[i], b_gate_a_b[i], b_gate_x_w[i], b_gate_x_b[i], b_lambda[i])
            n_in = even_w_in.shape[-1]
            up = _matmul(xp, even_w_in, i, 0, n_in, False)
            us = _matmul(xs, even_w_in, i, 0, n_in, True)
            hp, st_p = _even_mixer(up, even_w_out, i, hp, a_q_norm[i], a_k_norm[i], b_par, None, None, None, bp, sp)
            hs, st_s = _even_mixer(us, even_w_out, i, hs, a_q_norm[i], a_k_norm[i], b_par,
                                   cache_a_kv, state_b_conv[i], state_b_h[i], bs, ss)
            for dst, st in ((outs, st_p), (outs_s, st_s)):
                dst["ak"].append(st[0]); dst["bc"].append(st[1]); dst["bh"].append(st[2])
        else:
            d_par = (d_conv_w[i], d_a_log[i], d_dt_bias[i], d_out_norm[i])
            w_d = odd_w_in[:, :, ODD_D0:ODD_S0]
            w_s = jnp.concatenate([odd_w_in[:, :, ODD_G0:ODD_D0], odd_w_in[:, :, ODD_S0:],
                                   jnp.zeros((odd_w_in.shape[0], d, LANES - 3 * C_HEADS - 2 * D_HEADS), jnp.float32)], axis=-1)
            ups = (_matmul(xp, odd_w_in, i, 0, ODD_C, False), _matmul(xp, w_d, i, 0, ODD_D, False), _matmul(xp, w_s, i, 0, LANES, False))
            uss = (_matmul(xs, odd_w_in, i, 0, ODD_C, True), _matmul(xs, w_d, i, 0, ODD_D, True), _matmul(xs, w_s, i, 0, LANES, True))
            hp, st_p = _odd_mixer(ups, odd_w_out, i, hp, c_q_norm[i], c_k_norm[i], c_phi_k[i], c_phi_v[i], d_par,
                                  None, None, None, None, None, None, bp, sp)
            hs, st_s = _odd_mixer(uss, odd_w_out, i, hs, c_q_norm[i], c_k_norm[i], c_phi_k[i], c_phi_v[i], d_par,
                                  cache_c_cmp_kv, cache_c_sel_kv, page_table, cache_c_win_kv,
                                  state_d_conv[i], state_d_S, bs, ss)
            for dst, st in ((outs, st_p), (outs_s, st_s)):
                dst["cc"].append(st[0]); dst["cs"].append(st[1]); dst["cw"].append(st[2])
                dst["dc"].append(st[3]); dst["ds"].append(st[4])
        w_router = jnp.concatenate([moe_group_w[l], moe_expert_w[l],
                                    jnp.zeros((d, ROUTER_PAD - N_GROUPS - N_EXPERTS), jnp.float32)], axis=-1)
        b_router = jnp.concatenate([moe_group_b[l], moe_expert_b[l],
                                    jnp.zeros((ROUTER_PAD - N_GROUPS - N_EXPERTS,), jnp.float32)])[None, :]
        hp = _hier_moe(hp, norm_ffn[l], w_router, b_router, moe_w1, moe_w3, moe_w2, l, 256, jnp.bfloat16)
        hs = _hier_moe(hs, norm_ffn[l], w_router, b_router, moe_w1, moe_w3, moe_w2, l, 8, jnp.float32)
    res = [hp.reshape(bp, sp, d), hs.reshape(bs, ss, d)]
    for key in ("ak", "bc", "bh", "cc", "cs", "cw", "dc", "ds"):
        res.append(jnp.stack(outs[key]))
        res.append(jnp.stack(outs_s[key]))
    return tuple(res)
```
